```python
import jax, jax.numpy as jnp
from jax import lax
import numpy as np

D_MODEL = 2048
BATCH = 4
SEQ = 4096
DEPTH = 1

GRID_W = 64
D_MIX = D_MODEL
D_RWKV = D_MIX // 2
D_NA = D_MIX - D_RWKV
HEAD_DIM = 64
H_RWKV = D_RWKV // HEAD_DIM
H_NA = D_NA // HEAD_DIM
DECAY_LORA = 64
AAA_LORA = 64
GATE_LORA = 160
TOKEN_SHIFT_TAPS = 3
RWKV_IN = 3 * D_RWKV + 2 * DECAY_LORA + 2 * AAA_LORA + GATE_LORA
D_IN = RWKV_IN + 3 * D_NA
GN_EPS = 64e-5
L2_EPS = 1e-12
WIN_R = 8
WIN_C = 16
Q_BLOCK_C = 16
K_BLOCK_C = Q_BLOCK_C + WIN_C
N_COL_BLOCKS = GRID_W // Q_BLOCK_C
N_EXPERTS = 64
TOP_K = 6
EXPERT_FF = 512
SHARED_FF = EXPERT_FF
N_GROUPS = 8
TOPK_GROUPS = 4
ROUTED_SCALE = 2.5
DISPATCH_BLOCK = 256
NORM_EPS = 1e-6

kernel_name = 'hybrid_rwkv7_natten_moe_adaln_encoder'


def rms_norm(x, w, eps=NORM_EPS):
    xf = x.astype(jnp.float32)
    y = xf * lax.rsqrt(jnp.mean(xf * xf, axis=-1, keepdims=True) + eps)
    return (y * w.astype(jnp.float32)).astype(x.dtype)


def swiglu(x, w_gate, w_up, w_down):
    return (jax.nn.silu(x @ w_gate) * (x @ w_up)) @ w_down


def centred_shift(t, w):
    tp = jnp.pad(t, ((0, 0), (1, 1), (0, 0)))
    return w[0] * tp[:, :-2] + w[1] * tp[:, 1:-1] + w[2] * tp[:, 2:]


def wkv_scan(r, w, k, v, a, b, reverse):
    bn, sn, h, n = r.shape
    xs = tuple(jnp.moveaxis(t.astype(jnp.float32), 1, 0) for t in (r, w, k, v, a, b))

    def step(state, inp):
        r_t, w_t, k_t, v_t, a_t, b_t = inp

        def update(st):
            sa = jnp.einsum('bhij,bhj->bhi', st, a_t)
            return (st * w_t[:, :, None, :] + sa[..., None] * b_t[:, :, None, :]
                    + v_t[..., None] * k_t[:, :, None, :])

        if reverse:
            y = jnp.einsum('bhij,bhj->bhi', state, r_t)
            state = update(state)
        else:
            state = update(state)
            y = jnp.einsum('bhij,bhj->bhi', state, r_t)
        return state, y

    s0 = jnp.zeros((bn, h, n, n), jnp.float32)
    _, ys = lax.scan(step, s0, xs, reverse=reverse)
    return jnp.moveaxis(ys, 0, 1).astype(r.dtype)


def rwkv7_bidirectional(z, shift_w, w0_f, w_up_f, w0_b, w_up_b, a0_f, a_up_f, a0_b, a_up_b,
                        g_up, k_k, k_a, r_k, ln_w, ln_b):
    bn, sn, _ = z.shape
    z = centred_shift(z, shift_w)
    sizes = [D_RWKV, D_RWKV, D_RWKV, DECAY_LORA, DECAY_LORA, AAA_LORA, AAA_LORA, GATE_LORA]
    r, k, v, wd_f, wd_b, ad_f, ad_b, gd = jnp.split(z, list(np.cumsum(sizes)[:-1]), axis=-1)

    def heads(t):
        return t.reshape(bn, sn, H_RWKV, HEAD_DIM)

    g = jax.nn.sigmoid(gd) @ g_up
    kk = heads(k * k_k).astype(jnp.float32)
    kk = kk * lax.rsqrt(jnp.maximum(jnp.sum(kk * kk, axis=-1, keepdims=True), L2_EPS))

    def direction(w0, w_up, wd, a0, a_up, ad, reverse):
        w_log = -jax.nn.softplus(-(w0 + jnp.tanh(wd) @ w_up)) - 0.5
        decay = jnp.exp(-jnp.exp(w_log))
        a = jax.nn.sigmoid(a0 + ad @ a_up)
        k_dir = k * (1 + (a - 1) * k_a)
        y = wkv_scan(heads(r), heads(decay), heads(k_dir), heads(v), -kk, kk * heads(a), reverse)
        return y, k_dir

    y_f, k_f = direction(w0_f, w_up_f, wd_f, a0_f, a_up_f, ad_f, False)
    y_b, _ = direction(w0_b, w_up_b, wd_b, a0_b, a_up_b, ad_b, True)
    y = (y_f + y_b).astype(jnp.float32)
    mu = jnp.mean(y, axis=-1, keepdims=True)
    var = jnp.mean(jnp.square(y - mu), axis=-1, keepdims=True)
    y = ((y - mu) * lax.rsqrt(var + GN_EPS)).reshape(bn, sn, D_RWKV) * ln_w + ln_b
    bonus = jnp.sum(heads(r) * heads(k_f) * r_k, axis=-1, keepdims=True) * heads(v)
    return (y.astype(z.dtype) + bonus.reshape(bn, sn, D_RWKV)) * g


def neighbourhood_attention_2d(q, k, v, q_norm_w, k_norm_w, rel_bias):
    bn, sn, _ = q.shape
    rows = sn // GRID_W
    kr = min(WIN_R, rows)

    def grid(t):
        return t.reshape(bn, rows, GRID_W, H_NA, HEAD_DIM)

    qg = rms_norm(grid(q), q_norm_w) * (HEAD_DIM ** -0.5)
    kg = rms_norm(grid(k), k_norm_w)
    vg = grid(v)
    q_cols = np.arange(GRID_W).reshape(N_COL_BLOCKS, Q_BLOCK_C)
    win_c0 = np.clip(q_cols - WIN_C // 2, 0, GRID_W - WIN_C)
    key_c0 = np.clip(np.arange(N_COL_BLOCKS) * Q_BLOCK_C - WIN_C // 2, 0, GRID_W - K_BLOCK_C)
    key_cols = key_c0[:, None] + np.arange(K_BLOCK_C)
    col_valid = ((key_cols[:, None, :] >= win_c0[:, :, None])
                 & (key_cols[:, None, :] < win_c0[:, :, None] + WIN_C))
    col_off = np.clip(key_cols[:, None, :] - q_cols[:, :, None] + WIN_C - 1, 0, 2 * WIN_C - 2)
    col_bias = rel_bias[:, :, col_off].astype(jnp.float32)
    k_cb = kg[:, :, key_cols]
    v_cb = vg[:, :, key_cols]

    def row_block(i):
        rs = jnp.clip(i - kr // 2, 0, rows - kr)
        kb = lax.dynamic_slice_in_dim(k_cb, rs, kr, axis=1).astype(jnp.float32)
        vb = lax.dynamic_slice_in_dim(v_cb, rs, kr, axis=1)
        qi = lax.dynamic_index_in_dim(qg, i, axis=1, keepdims=False)
        qi = qi.reshape(bn, N_COL_BLOCKS, Q_BLOCK_C, H_NA, HEAD_DIM).astype(jnp.float32)
        s = jnp.einsum('bcqhd,brckhd->bhcqrk', qi, kb)
        row_off = rs + jnp.arange(kr) - i + WIN_R - 1
        bias = jnp.transpose(col_bias[:, row_off], (0, 2, 3, 1, 4))
        s = jnp.where(col_valid[None, None, :, :, None, :], s + bias[None], -jnp.inf)
        p = jax.nn.softmax(s.reshape(s.shape[:4] + (kr * K_BLOCK_C,)), axis=-1)
        p = p.reshape(s.shape).astype(vb.dtype)
        o = jnp.einsum('bhcqrk,brckhd->bcqhd', p, vb)
        return o.reshape(bn, GRID_W, D_NA)

    out = lax.map(row_block, jnp.arange(rows))
    return jnp.transpose(out, (1, 0, 2, 3)).reshape(bn, sn, D_NA)


def routed_experts(h, w_router, router_bias, w_gate_e, w_up_e, w_down_e):
    t, d = h.shape
    scores = jax.nn.sigmoid(h.astype(jnp.float32) @ w_router.astype(jnp.float32))
    biased = scores + router_bias.astype(jnp.float32)
    grp_score = jnp.sum(lax.top_k(biased.reshape(t, N_GROUPS, -1), 2)[0], axis=-1)
    _, top_g = lax.top_k(grp_score, TOPK_GROUPS)
    gmask = jnp.any(top_g[:, :, None] == jnp.arange(N_GROUPS)[None, None, :], axis=1)
    masked = jnp.where(jnp.repeat(gmask, N_EXPERTS // N_GROUPS, axis=1), biased, -jnp.inf)
    _, idx = lax.top_k(masked, TOP_K)
    gw = jnp.take_along_axis(scores, idx, axis=1)
    gw = gw / jnp.sum(gw, axis=-1, keepdims=True) * ROUTED_SCALE

    m = DISPATCH_BLOCK
    n_assign = t * TOP_K
    flat_e = idx.reshape(-1)
    flat_tok = jnp.repeat(jnp.arange(t, dtype=jnp.int32), TOP_K)
    flat_w = gw.reshape(-1)
    order = jnp.argsort(flat_e)
    e_s, tok_s, w_s = flat_e[order], flat_tok[order], flat_w[order]
    counts = jnp.bincount(flat_e, length=N_EXPERTS)
    padded = (counts + m - 1) // m * m
    pad_end = jnp.cumsum(padded)
    pad_start = pad_end - padded
    start = jnp.cumsum(counts) - counts
    dest = pad_start[e_s] + jnp.arange(n_assign) - start[e_s]
    n_blocks = -(-(n_assign + N_EXPERTS * m) // m)
    total = n_blocks * m
    buf_tok = jnp.zeros((total,), jnp.int32).at[dest].set(tok_s)
    buf_w = jnp.zeros((total,), h.dtype).at[dest].set(w_s.astype(h.dtype))
    block_e = jnp.minimum(jnp.searchsorted(pad_end, jnp.arange(n_blocks) * m, side='right'),
                          N_EXPERTS - 1)

    def expert_block(args):
        e, tok, w = args
        y = swiglu(h[tok], w_gate_e[e], w_up_e[e], w_down_e[e])
        return y * w[:, None].astype(y.dtype)

    ys = lax.map(expert_block, (block_e, buf_tok.reshape(n_blocks, m), buf_w.reshape(n_blocks, m)))
    return jnp.zeros_like(h).at[buf_tok].add(ys.reshape(total, d).astype(h.dtype))


def setup_inputs(seed: int = 0) -> dict:
    key = jax.random.key(seed)
    ks = iter(jax.random.split(key, 40))
    L = DEPTH

    def nrm(shape, scale):
        return jax.random.normal(next(ks), shape, jnp.float32) * scale

    def gain(shape):
        return 1.0 + nrm(shape, 0.1)

    shift_base = jnp.array([0.25, 0.5, 0.25], jnp.float32)[None, :, None]
    return {
        'x': nrm((BATCH, SEQ, D_MODEL), 1.0),
        'c': nrm((BATCH, D_MODEL), 1.0),
        'w_ada': nrm((L, D_MODEL, 6 * D_MODEL), 0.5 * D_MODEL ** -0.5),
        'b_ada': nrm((L, 6 * D_MODEL), 0.02),
        'norm1_w': gain((L, D_MODEL)),
        'w_in': nrm((L, D_MODEL, D_IN), D_MODEL ** -0.5),
        'rwkv_shift': shift_base + nrm((L, TOKEN_SHIFT_TAPS, RWKV_IN), 0.1),
        'w0_f': jax.random.uniform(next(ks), (L, D_RWKV), jnp.float32, -4.0, 1.0),
        'w_up_f': nrm((L, DECAY_LORA, D_RWKV), 0.5 * DECAY_LORA ** -0.5),
        'w0_b': jax.random.uniform(next(ks), (L, D_RWKV), jnp.float32, -4.0, 1.0),
        'w_up_b': nrm((L, DECAY_LORA, D_RWKV), 0.5 * DECAY_LORA ** -0.5),
        'a0_f': nrm((L, D_RWKV), 0.5),
        'a_up_f': nrm((L, AAA_LORA, D_RWKV), 0.5 * AAA_LORA ** -0.5),
        'a0_b': nrm((L, D_RWKV), 0.5),
        'a_up_b': nrm((L, AAA_LORA, D_RWKV), 0.5 * AAA_LORA ** -0.5),
        'g_up': nrm((L, GATE_LORA, D_RWKV), GATE_LORA ** -0.5),
        'k_k': 0.85 + nrm((L, D_RWKV), 0.1),
        'k_a': gain((L, D_RWKV)),
        'r_k': nrm((L, H_RWKV, HEAD_DIM), 0.1),
        'ln_x_w': gain((L, D_RWKV)),
        'ln_x_b': nrm((L, D_RWKV), 0.02),
        'q_norm_w': gain((L, HEAD_DIM)),
        'k_norm_w': gain((L, HEAD_DIM)),
        'rel_bias': nrm((L, H_NA, 2 * WIN_R - 1, 2 * WIN_C - 1), 0.2),
        'w_out': nrm((L, D_MIX, D_MODEL), D_MIX ** -0.5),
        'norm2_w': gain((L, D_MODEL)),
        'w_router': nrm((L, D_MODEL, N_EXPERTS), D_MODEL ** -0.5),
        'router_bias': nrm((L, N_EXPERTS), 0.01),
        'w_gate_e': nrm((L, N_EXPERTS, D_MODEL, EXPERT_FF), D_MODEL ** -0.5),
        'w_up_e': nrm((L, N_EXPERTS, D_MODEL, EXPERT_FF), D_MODEL ** -0.5),
        'w_down_e': nrm((L, N_EXPERTS, EXPERT_FF, D_MODEL), EXPERT_FF ** -0.5),
        'w_gate_s': nrm((L, D_MODEL, SHARED_FF), D_MODEL ** -0.5),
        'w_up_s': nrm((L, D_MODEL, SHARED_FF), D_MODEL ** -0.5),
        'w_down_s': nrm((L, SHARED_FF, D_MODEL), SHARED_FF ** -0.5),
    }


def reference(x, c, w_ada, b_ada, norm1_w, w_in, rwkv_shift, w0_f, w_up_f, w0_b, w_up_b,
              a0_f, a_up_f, a0_b, a_up_b, g_up, k_k, k_a, r_k, ln_x_w, ln_x_b,
              q_norm_w, k_norm_w, rel_bias, w_out, norm2_w, w_router, router_bias,
              w_gate_e, w_up_e, w_down_e, w_gate_s, w_up_s, w_down_s):
    bn, sn, d = x.shape
    for l in range(DEPTH):
        mod = jax.nn.silu(c) @ w_ada[l] + b_ada[l]
        shift1, scale1, gate1, shift2, scale2, gate2 = jnp.split(mod[:, None, :], 6, axis=-1)
        h = rms_norm(x, norm1_w[l]) * (1 + scale1) + shift1
        proj = h @ w_in[l]
        rwkv_in = proj[..., :RWKV_IN]
        q, k, v = jnp.split(proj[..., RWKV_IN:], 3, axis=-1)
        y_rwkv = rwkv7_bidirectional(rwkv_in, rwkv_shift[l], w0_f[l], w_up_f[l], w0_b[l], w_up_b[l],
                                     a0_f[l], a_up_f[l], a0_b[l], a_up_b[l], g_up[l],
                                     k_k[l], k_a[l], r_k[l], ln_x_w[l], ln_x_b[l])
        y_na = neighbourhood_attention_2d(q, k, v, q_norm_w[l], k_norm_w[l], rel_bias[l])
        x = x + gate1 * (jnp.concatenate([y_rwkv, y_na], axis=-1) @ w_out[l])
        h2 = (rms_norm(x, norm2_w[l]) * (1 + scale2) + shift2).reshape(bn * sn, d)
        y = swiglu(h2, w_gate_s[l], w_up_s[l], w_down_s[l]) + routed_experts(
            h2, w_router[l], router_bias[l], w_gate_e[l], w_up_e[l], w_down_e[l])
        x = x + gate2 * y.reshape(bn, sn, d)
    return x
```

```python
import functools

import jax
import jax.numpy as jnp
import numpy as np
from jax import lax
from jax.experimental import pallas as pl
from jax.experimental.pallas import tpu as pltpu

F32 = jnp.float32
BF16 = jnp.bfloat16
I32 = jnp.int32
HI = lax.Precision.HIGHEST

LANES = 128
HEAD_DIM = 64
HEADS_PER_BLOCK = LANES // HEAD_DIM
GRID_W = 64
WIN_R = 8
WIN_C = 16
DECAY_LORA = 64
AAA_LORA = 64
GATE_LORA = 160
N_EXPERTS = 64
TOP_K = 6
SLOT_STRIDE = 8
N_GROUPS = 8
TOPK_GROUPS = 4
ROUTED_SCALE = 2.5
DISPATCH_BLOCK = 256
NORM_EPS = 1e-6
GN_EPS = 64e-5
L2_EPS = 1e-12
NEG_BIG = -1e30
WKV_CHUNK = 64
VMEM_LIMIT = 56 * 1024 * 1024

NT_DIMS = (((1,), (1,)), ((), ()))
TN_DIMS = (((0,), (0,)), ((), ()))


def _cparams(sem):
    return pltpu.CompilerParams(dimension_semantics=sem, vmem_limit_bytes=VMEM_LIMIT)


def _sigmoid(x):
    return 1.0 / (1.0 + jnp.exp(-x))


def _silu(x):
    return x * _sigmoid(x)


def _head_sum_matrix():
    a = np.arange(LANES) // HEAD_DIM
    return jnp.asarray((a[:, None] == a[None, :]).astype(np.float32))


def _mod_kernel(c_ref, w_ref, b_ref, o_ref):
    c = c_ref[...]
    o_ref[...] = jnp.dot(_silu(c), w_ref[...], preferred_element_type=F32, precision=HI) + b_ref[...]


def _mod_call(c_pad, w_ada, b_ada):
    rows, d = c_pad.shape
    n = w_ada.shape[1]
    tn = 1024
    return pl.pallas_call(
        _mod_kernel,
        grid=(n // tn,),
        in_specs=[pl.BlockSpec((rows, d), lambda j: (0, 0)),
                  pl.BlockSpec((d, tn), lambda j: (0, j)),
                  pl.BlockSpec((1, tn), lambda j: (0, j))],
        out_specs=pl.BlockSpec((rows, tn), lambda j: (0, j)),
        out_shape=jax.ShapeDtypeStruct((rows, n), F32),
        compiler_params=_cparams(("parallel",)),
        name="mod",
    )(c_pad, w_ada, b_ada)


def _proj_kernel(x_ref, mod_ref, nw_ref, w_ref, o_ref, h_sc, *, nq):
    @pl.when(pl.program_id(2) == 0)
    def _():
        x = x_ref[0]
        ms = jnp.mean(x * x, axis=-1, keepdims=True)
        y = x * lax.rsqrt(ms + NORM_EPS) * nw_ref[...]
        h_sc[...] = (y * (1.0 + mod_ref[0, 1:2, :]) + mod_ref[0, 0:1, :]).astype(BF16)

    acc = jnp.dot(h_sc[...], w_ref[...], preferred_element_type=F32)
    for q in range(nq):
        o_ref[0, q] = acc[:, q * LANES:(q + 1) * LANES].astype(o_ref.dtype)


def _proj_call(x, mod3, norm_w, w_in_r, tm=512, tn=512):
    b, s, d = x.shape
    n = w_in_r.shape[1]
    nq = tn // LANES
    return pl.pallas_call(
        functools.partial(_proj_kernel, nq=nq),
        grid=(b, s // tm, n // tn),
        in_specs=[pl.BlockSpec((1, tm, d), lambda bi, i, j: (bi, i, 0)),
                  pl.BlockSpec((1, 6, d), lambda bi, i, j: (bi, 0, 0)),
                  pl.BlockSpec((1, d), lambda bi, i, j: (0, 0)),
                  pl.BlockSpec((d, tn), lambda bi, i, j: (0, j))],
        out_specs=pl.BlockSpec((1, nq, tm, LANES), lambda bi, i, j: (bi, j, i, 0)),
        out_shape=jax.ShapeDtypeStruct((b, n // LANES, s, LANES), BF16),
        scratch_shapes=[pltpu.VMEM((tm, d), BF16)],
        compiler_params=_cparams(("parallel", "parallel", "arbitrary")),
        name="proj",
    )(x, mod3, norm_w, w_in_r)


RB_R, RB_V, RB_KK, RB_G, RB_KF, RB_KB, RB_AF, RB_AB, RB_BONUS = range(9)
VEC_W0F, VEC_W0B, VEC_A0F, VEC_A0B, VEC_KK, VEC_KA, VEC_RK = range(7)
N_RKV_BLOCKS = 24
N_SHIFT_BLOCKS = 28


def _softplus(u):
    return jnp.maximum(u, 0.0) + jnp.log(1.0 + jnp.exp(-jnp.abs(u)))


def _prep_kernel(p_ref, pp_ref, pn_ref, sw_ref, wupf_ref, wupb_ref, aupf_ref, aupb_ref, gup_ref,
                 vec_ref, mbd_ref, rb_ref, lw_ref, *, tb, n_t, nhp):
    i = pl.program_id(1)
    row = lax.broadcasted_iota(I32, (tb, LANES), 0)
    has_prev = i > 0
    has_next = i < n_t - 1
    halo = pp_ref.shape[2]

    def shifted(q):
        cur = p_ref[0, q].astype(F32)
        prev_row = jnp.where(has_prev, pp_ref[0, q, halo - 1:halo, :].astype(F32), 0.0)
        next_row = jnp.where(has_next, pn_ref[0, q, 0:1, :].astype(F32), 0.0)
        up = jnp.where(row == 0, prev_row, pltpu.roll(cur, 1, 0))
        dn = jnp.where(row == tb - 1, next_row, pltpu.roll(cur, tb - 1, 0))
        return sw_ref[0, q:q + 1, :] * up + sw_ref[1, q:q + 1, :] * cur + sw_ref[2, q:q + 1, :] * dn

    mbd = mbd_ref[...]
    t_wd = jnp.tanh(shifted(N_RKV_BLOCKS)).astype(BF16)
    z_ad = shifted(N_RKV_BLOCKS + 1).astype(BF16)
    s_gd = jnp.concatenate([_sigmoid(shifted(N_RKV_BLOCKS + 2)),
                            _sigmoid(shifted(N_RKV_BLOCKS + 3))], axis=1).astype(BF16)

    def vec(v, hp):
        return vec_ref[v, hp:hp + 1, :]

    for hp in range(nhp):
        sl = slice(hp * LANES, (hp + 1) * LANES)
        r = shifted(hp)
        k = shifted(nhp + hp)
        v = shifted(2 * nhp + hp)
        g = jnp.dot(s_gd, gup_ref[:, sl], preferred_element_type=F32)
        kk0 = k * vec(VEC_KK, hp)
        ss = jnp.dot(kk0 * kk0, mbd, preferred_element_type=F32, precision=HI)
        kk = kk0 * lax.rsqrt(jnp.maximum(ss, L2_EPS))
        kdirs = []
        for wup_ref, aup_ref, v_w0, v_a0, q_k, q_a, lw_slot in (
                (wupf_ref, aupf_ref, VEC_W0F, VEC_A0F, RB_KF, RB_AF, 0),
                (wupb_ref, aupb_ref, VEC_W0B, VEC_A0B, RB_KB, RB_AB, 1)):
            wl = vec(v_w0, hp) + jnp.dot(t_wd, wup_ref[:, sl], preferred_element_type=F32)
            w_log = -_softplus(-wl) - 0.5
            lw_ref[0, lw_slot * nhp + hp] = -jnp.exp(w_log)
            ag = _sigmoid(vec(v_a0, hp) + jnp.dot(z_ad, aup_ref[:, sl], preferred_element_type=F32))
            kd = k * (1.0 + (ag - 1.0) * vec(VEC_KA, hp))
            rb_ref[0, q_k * nhp + hp] = kd.astype(BF16)
            rb_ref[0, q_a * nhp + hp] = ag.astype(BF16)
            kdirs.append(kd)
        bonus = jnp.dot(r * kdirs[0] * vec(VEC_RK, hp), mbd, preferred_element_type=F32, precision=HI) * v
        rb_ref[0, RB_R * nhp + hp] = r.astype(BF16)
        rb_ref[0, RB_V * nhp + hp] = v.astype(BF16)
        rb_ref[0, RB_KK * nhp + hp] = kk.astype(BF16)
        rb_ref[0, RB_G * nhp + hp] = g.astype(BF16)
        rb_ref[0, RB_BONUS * nhp + hp] = bonus.astype(BF16)


def _prep_call(p, sw, wupf, wupb, aupf, aupb, gup, vecs, mbd, nhp, tb=256):
    b, _, s, _ = p.shape
    n_t = s // tb
    halo = 16
    hb = tb // halo
    full = lambda a: pl.BlockSpec(a.shape, lambda bi, i: (0,) * a.ndim)
    return pl.pallas_call(
        functools.partial(_prep_kernel, tb=tb, n_t=n_t, nhp=nhp),
        grid=(b, n_t),
        in_specs=[pl.BlockSpec((1, N_SHIFT_BLOCKS, tb, LANES), lambda bi, i: (bi, 0, i, 0)),
                  pl.BlockSpec((1, N_SHIFT_BLOCKS, halo, LANES),
                               lambda bi, i: (bi, 0, jnp.maximum(i * hb - 1, 0), 0)),
                  pl.BlockSpec((1, N_SHIFT_BLOCKS, halo, LANES),
                               lambda bi, i: (bi, 0, jnp.minimum((i + 1) * hb, s // halo - 1), 0)),
                  full(sw), full(wupf), full(wupb), full(aupf), full(aupb), full(gup), full(vecs), full(mbd)],
        out_specs=[pl.BlockSpec((1, 9 * nhp, tb, LANES), lambda bi, i: (bi, 0, i, 0)),
                   pl.BlockSpec((1, 2 * nhp, tb, LANES), lambda bi, i: (bi, 0, i, 0))],
        out_shape=[jax.ShapeDtypeStruct((b, 9 * nhp, s, LANES), BF16),
                   jax.ShapeDtypeStruct((b, 2 * nhp, s, LANES), F32)],
        compiler_params=_cparams(("parallel", "parallel")),
        name="prep",
    )(p, p, p, sw, wupf, wupb, aupf, aupb, gup, vecs, mbd)


def _wkv_kernel(r_ref, v_ref, kk_ref, kd_ref, ag_ref, lw_ref, y_ref, s_sc, *, L, nhp, reverse):
    @pl.when(pl.program_id(1) == 0)
    def _():
        s_sc[...] = jnp.zeros(s_sc.shape, F32)

    n2 = HEADS_PER_BLOCK * L
    ti = lax.broadcasted_iota(I32, (L, L), 0)
    tj = lax.broadcasted_iota(I32, (L, L), 1)
    tri = ((tj >= ti) if reverse else (tj <= ti)).astype(F32)
    ri = lax.broadcasted_iota(I32, (n2, LANES), 0)
    ci = lax.broadcasted_iota(I32, (n2, LANES), 1)
    head_match = (ri // L) == (ci // HEAD_DIM)
    si = lax.broadcasted_iota(I32, (n2, n2), 0)
    sj = lax.broadcasted_iota(I32, (n2, n2), 1)
    st, ss = si % L, sj % L
    strict = (st < ss) if reverse else (st > ss)
    incl = (st < ss) if reverse else (st >= ss)
    eye = (si == sj).astype(F32)

    def stack(x):
        return jnp.where(head_match, jnp.concatenate([x] * HEADS_PER_BLOCK, axis=0), 0.0)

    for hp in range(nhp):
        r = r_ref[0, hp].astype(F32)
        v = v_ref[0, hp].astype(F32)
        kk = kk_ref[0, hp].astype(F32)
        kd = kd_ref[0, hp].astype(F32)
        ag = ag_ref[0, hp].astype(F32)
        lw = lw_ref[0, hp]
        c_in = jnp.dot(tri, lw, preferred_element_type=F32, precision=HI)
        c_ex = c_in - lw
        e_ex = jnp.exp(c_ex)
        e_inv = jnp.exp(-c_in)
        if reverse:
            e_r = e_ex
            p_last = jnp.exp(c_in[0:1])
        else:
            e_r = jnp.exp(c_in)
            p_last = jnp.exp(c_in[L - 1:L])
        a_s = stack(-kk * e_ex)
        b_s = stack(kk * ag * e_inv)
        k_s = stack(kd * e_inv)
        r_s = stack(r * e_r)
        v_s = stack(v).astype(BF16)
        lhs = jnp.concatenate([a_s, r_s], axis=0).astype(BF16)
        rhs = jnp.concatenate([b_s, k_s], axis=0).astype(BF16)
        g = lax.dot_general(lhs, rhs, NT_DIMS, preferred_element_type=F32)
        a_ab = jnp.where(strict, g[0:n2, 0:n2], 0.0)
        a_ak = jnp.where(strict, g[0:n2, n2:2 * n2], 0.0)
        a_rb = jnp.where(incl, g[n2:2 * n2, 0:n2], 0.0)
        a_rk = jnp.where(incl, g[n2:2 * n2, n2:2 * n2], 0.0)
        t_inv = eye + a_ab
        nb = a_ab.astype(BF16)
        for _ in range(L.bit_length() - 2):
            nb = jnp.dot(nb, nb, preferred_element_type=F32).astype(BF16)
            t_inv = t_inv + jnp.dot(nb, t_inv.astype(BF16), preferred_element_type=F32)
        s_old = s_sc[hp]
        ls = lax.dot_general(lhs, s_old.astype(BF16), NT_DIMS, preferred_element_type=F32)
        x = ls[0:n2] + jnp.dot(a_ak.astype(BF16), v_s, preferred_element_type=F32)
        u = jnp.dot(t_inv.astype(BF16), x.astype(BF16), preferred_element_type=F32)
        u = jnp.where(head_match, u, 0.0).astype(BF16)
        uv = jnp.concatenate([u, v_s], axis=0)
        y = ls[n2:2 * n2] + jnp.dot(jnp.concatenate([a_rb, a_rk], axis=1).astype(BF16), uv,
                                    preferred_element_type=F32)
        out = y[0:L]
        for h in range(1, HEADS_PER_BLOCK):
            out = out + y[h * L:(h + 1) * L]
        y_ref[0, hp] = out
        bk = jnp.concatenate([b_s * p_last, k_s * p_last], axis=0).astype(BF16)
        upd = lax.dot_general(uv, bk, TN_DIMS, preferred_element_type=F32)
        s_sc[hp] = s_old * p_last + upd


def _wkv_call(rb, lw, nhp, reverse, L=WKV_CHUNK):
    b, _, s, _ = rb.shape
    nc = s // L
    cidx = (lambda c: nc - 1 - c) if reverse else (lambda c: c)
    kq, aq, lq = (RB_KB, RB_AB, 1) if reverse else (RB_KF, RB_AF, 0)
    blk = lambda q: pl.BlockSpec((1, nhp, L, LANES), lambda bi, c: (bi, q, cidx(c), 0))
    return pl.pallas_call(
        functools.partial(_wkv_kernel, L=L, nhp=nhp, reverse=reverse),
        grid=(b, nc),
        in_specs=[blk(RB_R), blk(RB_V), blk(RB_KK), blk(kq), blk(aq), blk(lq)],
        out_specs=pl.BlockSpec((1, nhp, L, LANES), lambda bi, c: (bi, 0, cidx(c), 0)),
        out_shape=jax.ShapeDtypeStruct((b, nhp, s, LANES), F32),
        scratch_shapes=[pltpu.VMEM((nhp, LANES, LANES), F32)],
        compiler_params=_cparams(("parallel", "arbitrary")),
        name="wkv_bwd" if reverse else "wkv_fwd",
    )(rb, rb, rb, rb, rb, lw)


def _post_kernel(yf_ref, yb_ref, bonus_ref, g_ref, lnw_ref, lnb_ref, mbd_ref, o_ref, *, nhp):
    mbd = mbd_ref[...]
    inv_n = 1.0 / HEAD_DIM
    for hp in range(nhp):
        y = yf_ref[0, hp] + yb_ref[0, hp]
        mu = jnp.dot(y, mbd, preferred_element_type=F32, precision=HI) * inv_n
        d = y - mu
        var = jnp.dot(d * d, mbd, preferred_element_type=F32, precision=HI) * inv_n
        yn = d * lax.rsqrt(var + GN_EPS) * lnw_ref[hp:hp + 1, :] + lnb_ref[hp:hp + 1, :]
        out = (yn + bonus_ref[0, hp].astype(F32)) * g_ref[0, hp].astype(F32)
        o_ref[0, :, hp * LANES:(hp + 1) * LANES] = out.astype(o_ref.dtype)


def _post_call(yf, yb, rb, lnw, lnb, mbd, nhp, tb=256):
    b, _, s, _ = yf.shape
    yblk = pl.BlockSpec((1, nhp, tb, LANES), lambda bi, i: (bi, 0, i, 0))
    rblk = lambda q: pl.BlockSpec((1, nhp, tb, LANES), lambda bi, i: (bi, q, i, 0))
    full = lambda a: pl.BlockSpec(a.shape, lambda bi, i: (0,) * a.ndim)
    return pl.pallas_call(
        functools.partial(_post_kernel, nhp=nhp),
        grid=(b, s // tb),
        in_specs=[yblk, yblk, rblk(RB_BONUS), rblk(RB_G), full(lnw), full(lnb), full(mbd)],
        out_specs=pl.BlockSpec((1, tb, nhp * LANES), lambda bi, i: (bi, i, 0)),
        out_shape=jax.ShapeDtypeStruct((b, s, nhp * LANES), BF16),
        compiler_params=_cparams(("parallel", "parallel")),
        name="post",
    )(yf, yb, rb, rb, lnw, lnb, mbd)


def _na_kernel(q_ref, k_ref, v_ref, bias_ref, qw_ref, kw_ref, mbd_ref, o_ref, kn_sc, *, rq, rows):
    rbi = pl.program_id(2)
    mbd = mbd_ref[...]
    inv_n = 1.0 / HEAD_DIM
    span = WIN_R * GRID_W
    s_len = kn_sc.shape[0]

    @pl.when(rbi == 0)
    def _():
        def body(c, carry):
            off = pl.multiple_of(c * span, span)
            k = k_ref[0, 0, pl.ds(off, span), :].astype(F32)
            ms = jnp.dot(k * k, mbd, preferred_element_type=F32, precision=HI) * inv_n
            kn_sc[pl.ds(off, span), :] = (k * lax.rsqrt(ms + NORM_EPS) * kw_ref[...]).astype(BF16)
            return carry
        lax.fori_loop(0, s_len // span, body, 0)

    ri = lax.broadcasted_iota(I32, (HEADS_PER_BLOCK * GRID_W, LANES), 0)
    ci = lax.broadcasted_iota(I32, (HEADS_PER_BLOCK * GRID_W, LANES), 1)
    head_match = (ri // GRID_W) == (ci // HEAD_DIM)
    lane = lax.broadcasted_iota(I32, (GRID_W, LANES), 1)
    scale = HEAD_DIM ** -0.5
    for qi in range(rq):
        i = rbi * rq + qi
        rs = jnp.clip(i - WIN_R // 2, 0, rows - WIN_R)
        off = rs - i + WIN_R - 1
        q = q_ref[0, 0, qi * GRID_W:(qi + 1) * GRID_W, :].astype(F32)
        ms = jnp.dot(q * q, mbd, preferred_element_type=F32, precision=HI) * inv_n
        qn = q * lax.rsqrt(ms + NORM_EPS) * qw_ref[...] * scale
        qs = jnp.where(head_match, jnp.concatenate([qn] * HEADS_PER_BLOCK, axis=0), 0.0).astype(BF16)
        start = pl.multiple_of(rs * GRID_W, GRID_W)
        kb = kn_sc[pl.ds(start, span), :]
        vb = v_ref[0, 0, pl.ds(start, span), :]
        s = lax.dot_general(qs, kb, NT_DIMS, preferred_element_type=F32) + bias_ref[off, 0]
        m = jnp.max(s, axis=-1, keepdims=True)
        p = jnp.exp(s - m)
        l = jnp.sum(p, axis=-1, keepdims=True)
        o = jnp.dot(p.astype(BF16), vb, preferred_element_type=F32) / l
        out = o[0:GRID_W]
        for h in range(1, HEADS_PER_BLOCK):
            out = jnp.where(lane // HEAD_DIM == h, o[h * GRID_W:(h + 1) * GRID_W], out)
        o_ref[0, qi * GRID_W:(qi + 1) * GRID_W, :] = out.astype(o_ref.dtype)


def _na_call(p, bias, qw, kw, mbd, q_blk0, nhp, rq=8):
    b, _, s, _ = p.shape
    rows = s // GRID_W
    rq = min(rq, rows)
    full = lambda a: pl.BlockSpec(a.shape, lambda bi, hp, r: (0,) * a.ndim)
    return pl.pallas_call(
        functools.partial(_na_kernel, rq=rq, rows=rows),
        grid=(b, nhp, rows // rq),
        in_specs=[pl.BlockSpec((1, 1, rq * GRID_W, LANES), lambda bi, hp, r: (bi, q_blk0 + hp, r, 0)),
                  pl.BlockSpec((1, 1, s, LANES), lambda bi, hp, r: (bi, q_blk0 + nhp + hp, 0, 0)),
                  pl.BlockSpec((1, 1, s, LANES), lambda bi, hp, r: (bi, q_blk0 + 2 * nhp + hp, 0, 0)),
                  pl.BlockSpec((WIN_R, 1, HEADS_PER_BLOCK * GRID_W, WIN_R * GRID_W),
                               lambda bi, hp, r: (0, hp, 0, 0)),
                  full(qw), full(kw), full(mbd)],
        out_specs=pl.BlockSpec((1, rq * GRID_W, LANES), lambda bi, hp, r: (bi, r, hp)),
        out_shape=jax.ShapeDtypeStruct((b, s, nhp * LANES), BF16),
        scratch_shapes=[pltpu.VMEM((s, LANES), BF16)],
        compiler_params=_cparams(("parallel", "parallel", "arbitrary")),
        name="na",
    )(p, p, p, bias, qw, kw, mbd)


def _na_bias_table(rel_bias):
    h = rel_bias.shape[0]
    qc = np.arange(GRID_W)
    kc = np.arange(GRID_W)
    win0 = np.clip(qc - WIN_C // 2, 0, GRID_W - WIN_C)
    valid = (kc[None, :] >= win0[:, None]) & (kc[None, :] < win0[:, None] + WIN_C)
    coff = np.clip(kc[None, :] - qc[:, None] + WIN_C - 1, 0, 2 * WIN_C - 2)
    roff = np.arange(WIN_R)[:, None] + np.arange(WIN_R)[None, :]
    t = rel_bias.astype(F32)[:, :, coff]
    t = t[:, roff]
    t = jnp.where(valid[None, None, None], t, NEG_BIG)
    t = jnp.transpose(t, (1, 0, 3, 2, 4))
    return t.reshape(WIN_R, h // HEADS_PER_BLOCK, HEADS_PER_BLOCK * GRID_W, WIN_R * GRID_W)


def _outproj_kernel(yr_ref, yn_ref, x_ref, mod_ref, nw_ref, w1_ref, w2_ref, wr_ref, x1_ref, h2_ref, lg_ref):
    acc = jnp.dot(yr_ref[0], w1_ref[...], preferred_element_type=F32)
    acc = acc + jnp.dot(yn_ref[0], w2_ref[...], preferred_element_type=F32)
    x1 = x_ref[0] + mod_ref[0, 2:3, :] * acc
    x1_ref[0] = x1
    ms = jnp.mean(x1 * x1, axis=-1, keepdims=True)
    y = x1 * lax.rsqrt(ms + NORM_EPS) * nw_ref[...]
    h2 = y * (1.0 + mod_ref[0, 4:5, :]) + mod_ref[0, 3:4, :]
    h2_ref[0] = h2
    lg_ref[...] = lax.dot_general(wr_ref[...], h2, NT_DIMS, preferred_element_type=F32, precision=HI)


def _outproj_call(yr, yn, x, mod3, norm_w, w1, w2, wr_t, tm=256):
    b, s, d = x.shape
    dh = yr.shape[-1]
    ne = wr_t.shape[0]
    nt = s // tm
    full = lambda a: pl.BlockSpec(a.shape, lambda bi, i: (0,) * a.ndim)
    return pl.pallas_call(
        _outproj_kernel,
        grid=(b, nt),
        in_specs=[pl.BlockSpec((1, tm, dh), lambda bi, i: (bi, i, 0)),
                  pl.BlockSpec((1, tm, dh), lambda bi, i: (bi, i, 0)),
                  pl.BlockSpec((1, tm, d), lambda bi, i: (bi, i, 0)),
                  pl.BlockSpec((1, 6, d), lambda bi, i: (bi, 0, 0)),
                  full(norm_w), full(w1), full(w2), full(wr_t)],
        out_specs=[pl.BlockSpec((1, tm, d), lambda bi, i: (bi, i, 0)),
                   pl.BlockSpec((1, tm, d), lambda bi, i: (bi, i, 0)),
                   pl.BlockSpec((ne, tm), lambda bi, i: (0, bi * nt + i))],
        out_shape=[jax.ShapeDtypeStruct((b, s, d), F32),
                   jax.ShapeDtypeStruct((b, s, d), F32),
                   jax.ShapeDtypeStruct((ne, b * s), F32)],
        compiler_params=_cparams(("parallel", "parallel")),
        name="outproj",
    )(yr, yn, x, mod3, norm_w, w1, w2, wr_t)


def _first_argmax(x, idx, n):
    m = jnp.max(x, axis=0, keepdims=True)
    a = jnp.min(jnp.where(x == m, idx, n), axis=0, keepdims=True)
    return m, a


def _route_kernel(lg_ref, bias_ref, idx_ref, gw_ref, pos_ref, cnt_ref, *, tr):
    @pl.when(pl.program_id(0) == 0)
    def _():
        cnt_ref[...] = jnp.zeros(cnt_ref.shape, F32)

    gsz = N_EXPERTS // N_GROUPS
    scores = _sigmoid(lg_ref[...])
    biased = scores + bias_ref[:, 0:1]
    ig = lax.broadcasted_iota(I32, (gsz, tr), 0)
    grp_rows = []
    for g in range(N_GROUPS):
        blk = biased[g * gsz:(g + 1) * gsz]
        m1, a1 = _first_argmax(blk, ig, gsz)
        m2 = jnp.max(jnp.where(ig == a1, -jnp.inf, blk), axis=0, keepdims=True)
        grp_rows.append(m1 + m2)
    grp = jnp.concatenate(grp_rows, axis=0)
    ign = lax.broadcasted_iota(I32, (N_GROUPS, tr), 0)
    sel = jnp.zeros((N_GROUPS, tr), jnp.bool_)
    for _ in range(TOPK_GROUPS):
        _, a = _first_argmax(grp, ign, N_GROUPS)
        hit = ign == a
        sel = jnp.logical_or(sel, hit)
        grp = jnp.where(hit, -jnp.inf, grp)
    masked = jnp.concatenate(
        [jnp.where(sel[g:g + 1], biased[g * gsz:(g + 1) * gsz], -jnp.inf) for g in range(N_GROUPS)], axis=0)
    ie = lax.broadcasted_iota(I32, (N_EXPERTS, tr), 0)
    picks, pick_scores = [], []
    onehot = jnp.zeros((N_EXPERTS, tr), F32)
    for _ in range(TOP_K):
        _, a = _first_argmax(masked, ie, N_EXPERTS)
        hit = ie == a
        picks.append(a)
        pick_scores.append(jnp.sum(jnp.where(hit, scores, 0.0), axis=0, keepdims=True))
        onehot = onehot + hit.astype(F32)
        masked = jnp.where(hit, -jnp.inf, masked)
    total = pick_scores[0]
    for sc in pick_scores[1:]:
        total = total + sc
    t0 = lax.broadcasted_iota(I32, (tr, tr), 0)
    t1 = lax.broadcasted_iota(I32, (tr, tr), 1)
    before = (t0 < t1).astype(BF16)
    rank = jnp.dot(onehot.astype(BF16), before, preferred_element_type=F32) + cnt_ref[:, 0:1]
    zero_i = jnp.zeros((1, tr), I32)
    zero_f = jnp.zeros((1, tr), F32)
    for k in range(8):
        if k < TOP_K:
            idx_ref[k:k + 1, :] = picks[k]
            gw_ref[k:k + 1, :] = pick_scores[k] / total * ROUTED_SCALE
            pos = jnp.sum(jnp.where(ie == picks[k], rank, 0.0), axis=0, keepdims=True)
            pos_ref[k:k + 1, :] = pos.astype(I32)
        else:
            idx_ref[k:k + 1, :] = zero_i
            gw_ref[k:k + 1, :] = zero_f
            pos_ref[k:k + 1, :] = zero_i
    cnt_ref[...] = cnt_ref[...] + jnp.sum(onehot, axis=1, keepdims=True)


def _route_call(lg_t, bias2, tr=512):
    ne, t = lg_t.shape
    tr = min(tr, t)
    tok = pl.BlockSpec((8, tr), lambda i: (0, i))
    return pl.pallas_call(
        functools.partial(_route_kernel, tr=tr),
        grid=(t // tr,),
        in_specs=[pl.BlockSpec((ne, tr), lambda i: (0, i)),
                  pl.BlockSpec(bias2.shape, lambda i: (0, 0))],
        out_specs=[tok, tok, tok, pl.BlockSpec((ne, LANES), lambda i: (0, 0))],
        out_shape=[jax.ShapeDtypeStruct((8, t), I32), jax.ShapeDtypeStruct((8, t), F32),
                   jax.ShapeDtypeStruct((8, t), I32), jax.ShapeDtypeStruct((ne, LANES), F32)],
        compiler_params=_cparams(("arbitrary",)),
        name="route",
    )(lg_t, bias2)


def _dispatch_kernel(slot_ref, h_ref, xs_in_ref, xs_ref, sem, *, td):
    del xs_in_ref

    def row_copy(t, slot):
        return pltpu.make_async_copy(h_ref.at[pl.ds(t, 1)], xs_ref.at[pl.ds(slot, 1)], sem)

    def start(t, carry):
        for k in range(TOP_K):
            row_copy(t, slot_ref[t * SLOT_STRIDE + k]).start()
        return carry

    def wait(t, carry):
        for k in range(TOP_K):
            row_copy(t, 0).wait()
        return carry

    lax.fori_loop(0, td, start, 0)
    lax.fori_loop(0, td, wait, 0)


def _dispatch_call(slots_flat, h2, xs_zero, td=256):
    t, d = h2.shape
    return pl.pallas_call(
        functools.partial(_dispatch_kernel, td=td),
        grid=(t // td,),
        in_specs=[pl.BlockSpec((td * SLOT_STRIDE,), lambda i: (i,), memory_space=pltpu.SMEM),
                  pl.BlockSpec((td, d), lambda i: (i, 0)),
                  pl.BlockSpec(memory_space=pl.ANY)],
        out_specs=pl.BlockSpec(memory_space=pl.ANY),
        out_shape=jax.ShapeDtypeStruct(xs_zero.shape, xs_zero.dtype),
        scratch_shapes=[pltpu.SemaphoreType.DMA(())],
        input_output_aliases={2: 0},
        compiler_params=_cparams(("arbitrary",)),
        name="dispatch",
    )(slots_flat, h2, xs_zero)


def _experts_kernel(be_ref, nv_ref, xs_ref, wg_ref, wu_ref, wd_ref, ys_ref, wg_sc, wu_sc, wd_sc):
    b = pl.program_id(0)

    @pl.when(b < nv_ref[0])
    def _():
        prev = be_ref[jnp.maximum(b - 1, 0)]
        @pl.when(jnp.logical_or(b == 0, be_ref[b] != prev))
        def _():
            wg_sc[...] = wg_ref[0].astype(BF16)
            wu_sc[...] = wu_ref[0].astype(BF16)
            wd_sc[...] = wd_ref[0].astype(BF16)

        x = xs_ref[...].astype(BF16)
        g = jnp.dot(x, wg_sc[...], preferred_element_type=F32)
        u = jnp.dot(x, wu_sc[...], preferred_element_type=F32)
        a = (_silu(g) * u).astype(BF16)
        ys_ref[...] = jnp.dot(a, wd_sc[...], preferred_element_type=F32)

    @pl.when(b >= nv_ref[0])
    def _():
        ys_ref[...] = jnp.zeros(ys_ref.shape, ys_ref.dtype)


def _experts_call(block_e, n_valid, xs, wg, wu, wd):
    total, d = xs.shape
    m = DISPATCH_BLOCK
    ff = wg.shape[-1]
    rows = lambda b, be, nv: (jnp.minimum(b, nv[0] - 1), 0)
    out_rows = lambda b, be, nv: (b, 0)
    grid_spec = pltpu.PrefetchScalarGridSpec(
        num_scalar_prefetch=2,
        grid=(total // m,),
        in_specs=[pl.BlockSpec((m, d), rows),
                  pl.BlockSpec((1, d, ff), lambda b, be, nv: (be[b], 0, 0)),
                  pl.BlockSpec((1, d, ff), lambda b, be, nv: (be[b], 0, 0)),
                  pl.BlockSpec((1, ff, d), lambda b, be, nv: (be[b], 0, 0))],
        out_specs=pl.BlockSpec((m, d), out_rows),
        scratch_shapes=[pltpu.VMEM((d, ff), BF16), pltpu.VMEM((d, ff), BF16), pltpu.VMEM((ff, d), BF16)],
    )
    return pl.pallas_call(
        _experts_kernel,
        grid_spec=grid_spec,
        out_shape=jax.ShapeDtypeStruct((total, d), F32),
        compiler_params=_cparams(("arbitrary",)),
        name="experts",
    )(block_e, n_valid, xs, wg, wu, wd)


def _combine_kernel(slot_ref, x1_ref, h2_ref, gw_ref, mod_ref, wg_ref, wu_ref, wd_ref, ys_ref, o_ref,
                    buf, sem, *, tc):
    def row_copy(t, k, slot):
        return pltpu.make_async_copy(ys_ref.at[pl.ds(slot, 1)], buf.at[k, pl.ds(t, 1)], sem)

    def start(t, carry):
        for k in range(TOP_K):
            row_copy(t, k, slot_ref[t * SLOT_STRIDE + k]).start()
        return carry

    def wait(t, carry):
        for k in range(TOP_K):
            row_copy(t, k, 0).wait()
        return carry

    lax.fori_loop(0, tc, start, 0)
    hb = h2_ref[0].astype(BF16)
    g = jnp.dot(hb, wg_ref[...], preferred_element_type=F32)
    u = jnp.dot(hb, wu_ref[...], preferred_element_type=F32)
    acc = jnp.dot((_silu(g) * u).astype(BF16), wd_ref[...], preferred_element_type=F32)
    lax.fori_loop(0, tc, wait, 0)
    gw = gw_ref[...]
    for k in range(TOP_K):
        acc = acc + gw[:, k:k + 1] * buf[k]
    o_ref[0] = x1_ref[0] + mod_ref[0, 5:6, :] * acc


def _combine_call(slots_flat, x1, h2, gw_tok, mod3, wg, wu, wd, ys, tc=128):
    b, s, d = x1.shape
    nt = s // tc
    full = lambda a: pl.BlockSpec(a.shape, lambda bi, i: (0,) * a.ndim)
    return pl.pallas_call(
        functools.partial(_combine_kernel, tc=tc),
        grid=(b, nt),
        in_specs=[pl.BlockSpec((tc * SLOT_STRIDE,), lambda bi, i: (bi * nt + i,), memory_space=pltpu.SMEM),
                  pl.BlockSpec((1, tc, d), lambda bi, i: (bi, i, 0)),
                  pl.BlockSpec((1, tc, d), lambda bi, i: (bi, i, 0)),
                  pl.BlockSpec((tc, 8), lambda bi, i: (bi * nt + i, 0)),
                  pl.BlockSpec((1, 6, d), lambda bi, i: (bi, 0, 0)),
                  full(wg), full(wu), full(wd),
                  pl.BlockSpec(memory_space=pl.ANY)],
        out_specs=pl.BlockSpec((1, tc, d), lambda bi, i: (bi, i, 0)),
        out_shape=jax.ShapeDtypeStruct((b, s, d), F32),
        scratch_shapes=[pltpu.VMEM((TOP_K, tc, d), F32), pltpu.SemaphoreType.DMA(())],
        compiler_params=_cparams(("arbitrary", "arbitrary")),
        name="combine",
    )(slots_flat, x1, h2, gw_tok, mod3, wg, wu, wd, ys)


def _pad_rows(w, rows, at=0):
    out = jnp.zeros((rows, w.shape[1]), w.dtype)
    return out.at[at:at + w.shape[0]].set(w)


def _mixing_stage(p, rwkv_shift, w0_f, w_up_f, w0_b, w_up_b, a0_f, a_up_f, a0_b, a_up_b, g_up,
                  k_k, k_a, r_k, ln_x_w, ln_x_b, q_norm_w, k_norm_w, rel_bias):
    d_rwkv = w0_f.shape[0]
    nhp = d_rwkv // LANES
    mbd = _head_sum_matrix()
    n_shift = rwkv_shift.shape[1]
    sw = jnp.pad(rwkv_shift, ((0, 0), (0, N_SHIFT_BLOCKS * LANES - n_shift))).reshape(3, N_SHIFT_BLOCKS, LANES)
    wupf = _pad_rows(w_up_f, LANES, 0).astype(BF16)
    wupb = _pad_rows(w_up_b, LANES, DECAY_LORA).astype(BF16)
    aupf = _pad_rows(a_up_f, LANES, 0).astype(BF16)
    aupb = _pad_rows(a_up_b, LANES, AAA_LORA).astype(BF16)
    gup = _pad_rows(g_up, 2 * LANES, 0).astype(BF16)
    vecs = jnp.stack([w0_f, w0_b, a0_f, a0_b, k_k, k_a, r_k.reshape(-1)]).reshape(7, nhp, LANES)
    rb, lw = _prep_call(p, sw, wupf, wupb, aupf, aupb, gup, vecs, mbd, nhp)
    y_f = _wkv_call(rb, lw, nhp, reverse=False)
    y_b = _wkv_call(rb, lw, nhp, reverse=True)
    y_rwkv = _post_call(y_f, y_b, rb, ln_x_w.reshape(nhp, LANES), ln_x_b.reshape(nhp, LANES), mbd, nhp)
    bias = _na_bias_table(rel_bias)
    qw = jnp.tile(q_norm_w, HEADS_PER_BLOCK).reshape(1, LANES)
    kw = jnp.tile(k_norm_w, HEADS_PER_BLOCK).reshape(1, LANES)
    y_na = _na_call(p, bias, qw, kw, mbd, N_SHIFT_BLOCKS, nhp)
    return y_rwkv, y_na


def _moe_stage(x1, h2, lg_t, mod3, router_bias, w_gate_e, w_up_e, w_down_e, w_gate_s, w_up_s, w_down_s):
    b, s, d = x1.shape
    t = b * s
    m = DISPATCH_BLOCK
    bias2 = jnp.broadcast_to(router_bias.astype(F32)[:, None], (N_EXPERTS, LANES))
    idx_t, gw_t, pos_t, cnt = _route_call(lg_t, bias2)
    counts = cnt[:, 0].astype(I32)
    padded = (counts + m - 1) // m * m
    pad_end = jnp.cumsum(padded)
    pad_start = pad_end - padded
    n_blocks = -(-(t * TOP_K + N_EXPERTS * m) // m)
    n_valid = (pad_end[-1] // m).astype(I32).reshape(1)
    block_e = jnp.minimum(jnp.searchsorted(pad_end, jnp.arange(n_blocks, dtype=I32) * m, side='right'),
                          N_EXPERTS - 1).astype(I32)
    slots = (jnp.take(pad_start, idx_t) + pos_t).astype(I32)
    slots_flat = slots.T.reshape(-1)
    gw_tok = gw_t.T
    xs = _dispatch_call(slots_flat, h2.reshape(t, d), jnp.zeros((n_blocks * m, d), F32))
    ys = _experts_call(block_e, n_valid, xs, w_gate_e, w_up_e, w_down_e)
    return _combine_call(slots_flat, x1, h2, gw_tok, mod3, w_gate_s.astype(BF16), w_up_s.astype(BF16),
                         w_down_s.astype(BF16), ys)


def kernel(x, c, w_ada, b_ada, norm1_w, w_in, rwkv_shift, w0_f, w_up_f, w0_b, w_up_b, a0_f, a_up_f, a0_b,
           a_up_b, g_up, k_k, k_a, r_k, ln_x_w, ln_x_b, q_norm_w, k_norm_w, rel_bias, w_out, norm2_w,
           w_router, router_bias, w_gate_e, w_up_e, w_down_e, w_gate_s, w_up_s, w_down_s):
    bn, sn, d = x.shape
    depth = w_ada.shape[0]
    for l in range(depth):
        c_pad = jnp.pad(c, ((0, 8 - bn % 8 if bn % 8 else 0), (0, 0)))
        mod = _mod_call(c_pad, w_ada[l], b_ada[l].reshape(1, -1))[:bn]
        mod3 = mod.reshape(bn, 6, d)
        rwkv_in = rwkv_shift.shape[-1]
        pad_cols = N_SHIFT_BLOCKS * LANES - rwkv_in
        w_in_r = jnp.concatenate([w_in[l][:, :rwkv_in], jnp.zeros((d, pad_cols), w_in.dtype),
                                  w_in[l][:, rwkv_in:]], axis=1).astype(BF16)
        p = _proj_call(x, mod3, norm1_w[l].reshape(1, d), w_in_r)
        y_rwkv, y_na = _mixing_stage(p, rwkv_shift[l], w0_f[l], w_up_f[l], w0_b[l], w_up_b[l], a0_f[l],
                                     a_up_f[l], a0_b[l], a_up_b[l], g_up[l], k_k[l], k_a[l], r_k[l],
                                     ln_x_w[l], ln_x_b[l], q_norm_w[l], k_norm_w[l], rel_bias[l])
        d_rwkv = y_rwkv.shape[-1]
        w_o = w_out[l].astype(BF16)
        x1, h2, lg_t = _outproj_call(y_rwkv, y_na, x, mod3, norm2_w[l].reshape(1, d), w_o[:d_rwkv], w_o[d_rwkv:],
                                     w_router[l].T)
        x = _moe_stage(x1, h2, lg_t, mod3, router_bias[l], w_gate_e[l], w_up_e[l], w_down_e[l],
                       w_gate_s[l], w_up_s[l], w_down_s[l])
    return x
```

```python
import functools

import jax
import jax.numpy as jnp
import numpy as np
from jax import lax
from jax.experimental import pallas as pl
from jax.experimental.pallas import tpu as pltpu

F32 = jnp.float32
BF16 = jnp.bfloat16
I32 = jnp.int32
HI = lax.Precision.HIGHEST

LANES = 128
HEAD_DIM = 64
HEADS_PER_BLOCK = LANES // HEAD_DIM
GRID_W = 64
WIN_R = 8
WIN_C = 16
DECAY_LORA = 64
AAA_LORA = 64
GATE_LORA = 160
N_EXPERTS = 64
TOP_K = 6
SLOT_STRIDE = 8
N_GROUPS = 8
TOPK_GROUPS = 4
ROUTED_SCALE = 2.5
DISPATCH_BLOCK = 512
NORM_EPS = 1e-6
GN_EPS = 64e-5
L2_EPS = 1e-12
NEG_BIG = -1e30
WKV_CHUNK = 64
VMEM_LIMIT = 56 * 1024 * 1024

NT_DIMS = (((1,), (1,)), ((), ()))
TN_DIMS = (((0,), (0,)), ((), ()))


def _cparams(sem):
    return pltpu.CompilerParams(dimension_semantics=sem, vmem_limit_bytes=VMEM_LIMIT)


def _sigmoid(x):
    return 1.0 / (1.0 + jnp.exp(-x))


def _silu(x):
    return x * _sigmoid(x)


U32 = jnp.uint32
HI_HALF_MASK = 0xFFFF0000


def _pack_halves(x):
    n = x.shape[-1] // 2
    bits = lax.bitcast_convert_type(x.astype(BF16).astype(F32), U32)
    return (bits[:, :n] >> 16) | bits[:, n:]


def _unpack_halves(w):
    lo = lax.bitcast_convert_type(w << 16, F32)
    hi = lax.bitcast_convert_type(w & U32(HI_HALF_MASK), F32)
    return lo, hi


def _head_sum_matrix():
    a = np.arange(LANES) // HEAD_DIM
    return jnp.asarray((a[:, None] == a[None, :]).astype(np.float32))


def _mod_kernel(c_ref, w_ref, b_ref, o_ref):
    c = c_ref[...]
    o_ref[...] = jnp.dot(_silu(c), w_ref[...], preferred_element_type=F32, precision=HI) + b_ref[...]


def _mod_call(c_pad, w_ada, b_ada):
    rows, d = c_pad.shape
    n = w_ada.shape[1]
    tn = 1024
    return pl.pallas_call(
        _mod_kernel,
        grid=(n // tn,),
        in_specs=[pl.BlockSpec((rows, d), lambda j: (0, 0)),
                  pl.BlockSpec((d, tn), lambda j: (0, j)),
                  pl.BlockSpec((1, tn), lambda j: (0, j))],
        out_specs=pl.BlockSpec((rows, tn), lambda j: (0, j)),
        out_shape=jax.ShapeDtypeStruct((rows, n), F32),
        compiler_params=_cparams(("parallel",)),
        name="mod",
    )(c_pad, w_ada, b_ada)


def _proj_kernel(x_ref, mod_ref, nw_ref, w_ref, o_ref, h_sc, *, nq):
    @pl.when(pl.program_id(2) == 0)
    def _():
        x = x_ref[0]
        ms = jnp.mean(x * x, axis=-1, keepdims=True)
        y = x * lax.rsqrt(ms + NORM_EPS) * nw_ref[...]
        h_sc[...] = (y * (1.0 + mod_ref[0, 1:2, :]) + mod_ref[0, 0:1, :]).astype(BF16)

    acc = jnp.dot(h_sc[...], w_ref[...], preferred_element_type=F32)
    for q in range(nq):
        o_ref[0, q] = acc[:, q * LANES:(q + 1) * LANES].astype(o_ref.dtype)


def _proj_call(x, mod3, norm_w, w_in_r, tm=1024, tn=512):
    b, s, d = x.shape
    n = w_in_r.shape[1]
    nq = tn // LANES
    return pl.pallas_call(
        functools.partial(_proj_kernel, nq=nq),
        grid=(b, s // tm, n // tn),
        in_specs=[pl.BlockSpec((1, tm, d), lambda bi, i, j: (bi, i, 0)),
                  pl.BlockSpec((1, 6, d), lambda bi, i, j: (bi, 0, 0)),
                  pl.BlockSpec((1, d), lambda bi, i, j: (0, 0)),
                  pl.BlockSpec((d, tn), lambda bi, i, j: (0, j))],
        out_specs=pl.BlockSpec((1, nq, tm, LANES), lambda bi, i, j: (bi, j, i, 0)),
        out_shape=jax.ShapeDtypeStruct((b, n // LANES, s, LANES), BF16),
        scratch_shapes=[pltpu.VMEM((tm, d), BF16)],
        compiler_params=_cparams(("parallel", "parallel", "arbitrary")),
        name="proj",
    )(x, mod3, norm_w, w_in_r)


RB_R, RB_V, RB_KK, RB_G, RB_KF, RB_KB, RB_AF, RB_AB, RB_BONUS = range(9)
VEC_W0F, VEC_W0B, VEC_A0F, VEC_A0B, VEC_KK, VEC_KA, VEC_RK = range(7)
N_RKV_BLOCKS = 24
N_SHIFT_BLOCKS = 28


def _softplus(u):
    return jnp.maximum(u, 0.0) + jnp.log(1.0 + jnp.exp(-jnp.abs(u)))


def _prep_kernel(p_ref, pp_ref, pn_ref, sw_ref, wupf_ref, wupb_ref, aupf_ref, aupb_ref, gup_ref,
                 vec_ref, mbd_ref, rb_ref, lw_ref, *, tb, n_t, nhp):
    i = pl.program_id(1)
    row = lax.broadcasted_iota(I32, (tb, LANES), 0)
    has_prev = i > 0
    has_next = i < n_t - 1
    halo = pp_ref.shape[2]

    def shifted(q):
        cur = p_ref[0, q].astype(F32)
        prev_row = jnp.where(has_prev, pp_ref[0, q, halo - 1:halo, :].astype(F32), 0.0)
        next_row = jnp.where(has_next, pn_ref[0, q, 0:1, :].astype(F32), 0.0)
        up = jnp.where(row == 0, prev_row, pltpu.roll(cur, 1, 0))
        dn = jnp.where(row == tb - 1, next_row, pltpu.roll(cur, tb - 1, 0))
        return sw_ref[0, q:q + 1, :] * up + sw_ref[1, q:q + 1, :] * cur + sw_ref[2, q:q + 1, :] * dn

    mbd = mbd_ref[...]
    t_wd = jnp.tanh(shifted(N_RKV_BLOCKS)).astype(BF16)
    z_ad = shifted(N_RKV_BLOCKS + 1).astype(BF16)
    s_gd = jnp.concatenate([_sigmoid(shifted(N_RKV_BLOCKS + 2)),
                            _sigmoid(shifted(N_RKV_BLOCKS + 3))], axis=1).astype(BF16)

    def vec(v, hp):
        return vec_ref[v, hp:hp + 1, :]

    for hp in range(nhp):
        sl = slice(hp * LANES, (hp + 1) * LANES)
        r = shifted(hp)
        k = shifted(nhp + hp)
        v = shifted(2 * nhp + hp)
        g = jnp.dot(s_gd, gup_ref[:, sl], preferred_element_type=F32)
        kk0 = k * vec(VEC_KK, hp)
        ss = jnp.dot(kk0 * kk0, mbd, preferred_element_type=F32, precision=HI)
        kk = kk0 * lax.rsqrt(jnp.maximum(ss, L2_EPS))
        kdirs = []
        for wup_ref, aup_ref, v_w0, v_a0, q_k, q_a, lw_slot in (
                (wupf_ref, aupf_ref, VEC_W0F, VEC_A0F, RB_KF, RB_AF, 0),
                (wupb_ref, aupb_ref, VEC_W0B, VEC_A0B, RB_KB, RB_AB, 1)):
            wl = vec(v_w0, hp) + jnp.dot(t_wd, wup_ref[:, sl], preferred_element_type=F32)
            w_log = -_softplus(-wl) - 0.5
            lw_ref[0, lw_slot * nhp + hp] = -jnp.exp(w_log)
            ag = _sigmoid(vec(v_a0, hp) + jnp.dot(z_ad, aup_ref[:, sl], preferred_element_type=F32))
            kd = k * (1.0 + (ag - 1.0) * vec(VEC_KA, hp))
            rb_ref[0, q_k * nhp + hp] = kd.astype(BF16)
            rb_ref[0, q_a * nhp + hp] = ag.astype(BF16)
            kdirs.append(kd)
        bonus = jnp.dot(r * kdirs[0] * vec(VEC_RK, hp), mbd, preferred_element_type=F32, precision=HI) * v
        rb_ref[0, RB_R * nhp + hp] = r.astype(BF16)
        rb_ref[0, RB_V * nhp + hp] = v.astype(BF16)
        rb_ref[0, RB_KK * nhp + hp] = kk.astype(BF16)
        rb_ref[0, RB_G * nhp + hp] = g.astype(BF16)
        rb_ref[0, RB_BONUS * nhp + hp] = bonus.astype(BF16)


def _prep_call(p, sw, wupf, wupb, aupf, aupb, gup, vecs, mbd, nhp, tb=256):
    b, _, s, _ = p.shape
    n_t = s // tb
    halo = 16
    hb = tb // halo
    full = lambda a: pl.BlockSpec(a.shape, lambda bi, i: (0,) * a.ndim)
    return pl.pallas_call(
        functools.partial(_prep_kernel, tb=tb, n_t=n_t, nhp=nhp),
        grid=(b, n_t),
        in_specs=[pl.BlockSpec((1, N_SHIFT_BLOCKS, tb, LANES), lambda bi, i: (bi, 0, i, 0)),
                  pl.BlockSpec((1, N_SHIFT_BLOCKS, halo, LANES),
                               lambda bi, i: (bi, 0, jnp.maximum(i * hb - 1, 0), 0)),
                  pl.BlockSpec((1, N_SHIFT_BLOCKS, halo, LANES),
                               lambda bi, i: (bi, 0, jnp.minimum((i + 1) * hb, s // halo - 1), 0)),
                  full(sw), full(wupf), full(wupb), full(aupf), full(aupb), full(gup), full(vecs), full(mbd)],
        out_specs=[pl.BlockSpec((1, 9 * nhp, tb, LANES), lambda bi, i: (bi, 0, i, 0)),
                   pl.BlockSpec((1, 2 * nhp, tb, LANES), lambda bi, i: (bi, 0, i, 0))],
        out_shape=[jax.ShapeDtypeStruct((b, 9 * nhp, s, LANES), BF16),
                   jax.ShapeDtypeStruct((b, 2 * nhp, s, LANES), F32)],
        compiler_params=_cparams(("parallel", "parallel")),
        name="prep",
    )(p, p, p, sw, wupf, wupb, aupf, aupb, gup, vecs, mbd)


def _wkv_kernel(r_ref, v_ref, kk_ref, kf_ref, af_ref, lf_ref, rr_ref, vr_ref, kkr_ref, kb_ref, ab_ref, lb_ref,
                yf_ref, yb_ref, s_sc, *, L, nhp):
    @pl.when(pl.program_id(1) == 0)
    def _():
        s_sc[...] = jnp.zeros(s_sc.shape, F32)

    n2 = HEADS_PER_BLOCK * L
    ti = lax.broadcasted_iota(I32, (L, L), 0)
    tj = lax.broadcasted_iota(I32, (L, L), 1)
    ri = lax.broadcasted_iota(I32, (n2, LANES), 0)
    ci = lax.broadcasted_iota(I32, (n2, LANES), 1)
    head_match = (ri // L) == (ci // HEAD_DIM)
    si = lax.broadcasted_iota(I32, (n2, n2), 0)
    sj = lax.broadcasted_iota(I32, (n2, n2), 1)
    st, ss = si % L, sj % L
    eye = (si == sj).astype(F32)
    tri = {False: (tj <= ti).astype(F32), True: (tj >= ti).astype(F32)}
    strict = {False: st > ss, True: st < ss}
    incl = {False: st >= ss, True: st < ss}

    def stack(x):
        return jnp.where(head_match, jnp.concatenate([x] * HEADS_PER_BLOCK, axis=0), 0.0)

    chains = [(False, hp, r_ref, v_ref, kk_ref, kf_ref, af_ref, lf_ref) for hp in range(nhp)]
    chains += [(True, hp, rr_ref, vr_ref, kkr_ref, kb_ref, ab_ref, lb_ref) for hp in range(nhp)]
    n = len(chains)
    lws = [ch[7][0, ch[1]] for ch in chains]
    c_ins = [jnp.dot(tri[ch[0]], lw, preferred_element_type=F32, precision=HI) for ch, lw in zip(chains, lws)]
    lhs_l, rhs_l, vs_l, bk_l, pl_l = [], [], [], [], []
    for i, (rev, hp, rr, vr, kkr, kdr, agr, _) in enumerate(chains):
        r = rr[0, hp].astype(F32)
        kk = kkr[0, hp].astype(F32)
        kd = kdr[0, hp].astype(F32)
        ag = agr[0, hp].astype(F32)
        c_in = c_ins[i]
        e_ex = jnp.exp(c_in - lws[i])
        e_inv = jnp.exp(-c_in)
        if rev:
            e_r = e_ex
            p_last = jnp.exp(c_in[0:1])
        else:
            e_r = jnp.exp(c_in)
            p_last = jnp.exp(c_in[L - 1:L])
        b_s = stack(kk * ag * e_inv)
        k_s = stack(kd * e_inv)
        lhs_l.append(jnp.concatenate([stack(-kk * e_ex), stack(r * e_r)], axis=0).astype(BF16))
        rhs_l.append(jnp.concatenate([b_s, k_s], axis=0).astype(BF16))
        bk_l.append(jnp.concatenate([b_s * p_last, k_s * p_last], axis=0).astype(BF16))
        vs_l.append(stack(vr[0, hp].astype(F32)).astype(BF16))
        pl_l.append(p_last)
    g_l = [lax.dot_general(lhs_l[i], rhs_l[i], NT_DIMS, preferred_element_type=F32) for i in range(n)]
    s_old = [s_sc[i] for i in range(n)]
    ls_l = [lax.dot_general(lhs_l[i], s_old[i].astype(BF16), NT_DIMS, preferred_element_type=F32)
            for i in range(n)]
    a_ab = [jnp.where(strict[ch[0]], g[0:n2, 0:n2], 0.0) for ch, g in zip(chains, g_l)]
    a_ak = [jnp.where(strict[ch[0]], g[0:n2, n2:2 * n2], 0.0).astype(BF16) for ch, g in zip(chains, g_l)]
    a_r = [jnp.concatenate([jnp.where(incl[ch[0]], g[n2:2 * n2, 0:n2], 0.0),
                            jnp.where(incl[ch[0]], g[n2:2 * n2, n2:2 * n2], 0.0)], axis=1).astype(BF16)
           for ch, g in zip(chains, g_l)]
    x_l = [ls_l[i][0:n2] + jnp.dot(a_ak[i], vs_l[i], preferred_element_type=F32) for i in range(n)]
    t_inv = [eye + a for a in a_ab]
    nb = [a.astype(BF16) for a in a_ab]
    for _ in range(L.bit_length() - 2):
        nb = [jnp.dot(m, m, preferred_element_type=F32).astype(BF16) for m in nb]
        t_inv = [t + jnp.dot(m, t.astype(BF16), preferred_element_type=F32) for m, t in zip(nb, t_inv)]
    u_l = [jnp.dot(t_inv[i].astype(BF16), x_l[i].astype(BF16), preferred_element_type=F32) for i in range(n)]
    uv_l = [jnp.concatenate([jnp.where(head_match, u_l[i], 0.0).astype(BF16), vs_l[i]], axis=0)
            for i in range(n)]
    y_l = [ls_l[i][n2:2 * n2] + jnp.dot(a_r[i], uv_l[i], preferred_element_type=F32) for i in range(n)]
    upd_l = [lax.dot_general(uv_l[i], bk_l[i], TN_DIMS, preferred_element_type=F32) for i in range(n)]
    for i, (rev, hp, *_) in enumerate(chains):
        y = y_l[i]
        out = y[0:L]
        for h in range(1, HEADS_PER_BLOCK):
            out = out + y[h * L:(h + 1) * L]
        (yb_ref if rev else yf_ref)[0, hp] = out
        s_sc[i] = s_old[i] * pl_l[i] + upd_l[i]


def _wkv_call(rb, lw, nhp, L=WKV_CHUNK):
    b, _, s, _ = rb.shape
    nc = s // L
    fwd = lambda q: pl.BlockSpec((1, nhp, L, LANES), lambda bi, c: (bi, q, c, 0))
    bwd = lambda q: pl.BlockSpec((1, nhp, L, LANES), lambda bi, c: (bi, q, nc - 1 - c, 0))
    y_shape = jax.ShapeDtypeStruct((b, nhp, s, LANES), F32)
    return pl.pallas_call(
        functools.partial(_wkv_kernel, L=L, nhp=nhp),
        grid=(b, nc),
        in_specs=[fwd(RB_R), fwd(RB_V), fwd(RB_KK), fwd(RB_KF), fwd(RB_AF), fwd(0),
                  bwd(RB_R), bwd(RB_V), bwd(RB_KK), bwd(RB_KB), bwd(RB_AB), bwd(1)],
        out_specs=[fwd(0), bwd(0)],
        out_shape=[y_shape, y_shape],
        scratch_shapes=[pltpu.VMEM((2 * nhp, LANES, LANES), F32)],
        compiler_params=_cparams(("parallel", "arbitrary")),
        name="wkv",
    )(rb, rb, rb, rb, rb, lw, rb, rb, rb, rb, rb, lw)


def _post_kernel(yf_ref, yb_ref, bonus_ref, g_ref, lnw_ref, lnb_ref, mbd_ref, o_ref, *, nhp):
    mbd = mbd_ref[...]
    inv_n = 1.0 / HEAD_DIM
    for hp in range(nhp):
        y = yf_ref[0, hp] + yb_ref[0, hp]
        mu = jnp.dot(y, mbd, preferred_element_type=F32, precision=HI) * inv_n
        d = y - mu
        var = jnp.dot(d * d, mbd, preferred_element_type=F32, precision=HI) * inv_n
        yn = d * lax.rsqrt(var + GN_EPS) * lnw_ref[hp:hp + 1, :] + lnb_ref[hp:hp + 1, :]
        out = (yn + bonus_ref[0, hp].astype(F32)) * g_ref[0, hp].astype(F32)
        o_ref[0, :, hp * LANES:(hp + 1) * LANES] = out.astype(o_ref.dtype)


def _post_call(yf, yb, rb, lnw, lnb, mbd, nhp, tb=256):
    b, _, s, _ = yf.shape
    yblk = pl.BlockSpec((1, nhp, tb, LANES), lambda bi, i: (bi, 0, i, 0))
    rblk = lambda q: pl.BlockSpec((1, nhp, tb, LANES), lambda bi, i: (bi, q, i, 0))
    full = lambda a: pl.BlockSpec(a.shape, lambda bi, i: (0,) * a.ndim)
    return pl.pallas_call(
        functools.partial(_post_kernel, nhp=nhp),
        grid=(b, s // tb),
        in_specs=[yblk, yblk, rblk(RB_BONUS), rblk(RB_G), full(lnw), full(lnb), full(mbd)],
        out_specs=pl.BlockSpec((1, tb, nhp * LANES), lambda bi, i: (bi, i, 0)),
        out_shape=jax.ShapeDtypeStruct((b, s, nhp * LANES), BF16),
        compiler_params=_cparams(("parallel", "parallel")),
        name="post",
    )(yf, yb, rb, rb, lnw, lnb, mbd)


def _na_kernel(q_ref, k_ref, v_ref, bias_ref, qw_ref, kw_ref, mbd_ref, o_ref, kn_sc, *, rq, rows):
    rbi = pl.program_id(2)
    mbd = mbd_ref[...]
    inv_n = 1.0 / HEAD_DIM
    span = WIN_R * GRID_W
    s_len = kn_sc.shape[0]

    @pl.when(rbi == 0)
    def _():
        def body(c, carry):
            off = pl.multiple_of(c * span, span)
            k = k_ref[0, 0, pl.ds(off, span), :].astype(F32)
            ms = jnp.dot(k * k, mbd, preferred_element_type=F32, precision=HI) * inv_n
            kn_sc[pl.ds(off, span), :] = (k * lax.rsqrt(ms + NORM_EPS) * kw_ref[...]).astype(BF16)
            return carry
        lax.fori_loop(0, s_len // span, body, 0)

    ri = lax.broadcasted_iota(I32, (HEADS_PER_BLOCK * GRID_W, LANES), 0)
    ci = lax.broadcasted_iota(I32, (HEADS_PER_BLOCK * GRID_W, LANES), 1)
    head_match = (ri // GRID_W) == (ci // HEAD_DIM)
    lane = lax.broadcasted_iota(I32, (GRID_W, LANES), 1)
    scale = HEAD_DIM ** -0.5
    q_all = q_ref[0, 0].astype(F32)
    ms = jnp.dot(q_all * q_all, mbd, preferred_element_type=F32, precision=HI) * inv_n
    qn_all = q_all * lax.rsqrt(ms + NORM_EPS) * qw_ref[...] * scale
    starts, offs = [], []
    for qi in range(rq):
        i = rbi * rq + qi
        rs = jnp.clip(i - WIN_R // 2, 0, rows - WIN_R)
        offs.append(rs - i + WIN_R - 1)
        starts.append(pl.multiple_of(rs * GRID_W, GRID_W))
    qs_l = [jnp.where(head_match, jnp.concatenate([qn_all[qi * GRID_W:(qi + 1) * GRID_W]] * HEADS_PER_BLOCK,
                                                  axis=0), 0.0).astype(BF16) for qi in range(rq)]
    s_l = [lax.dot_general(qs_l[qi], kn_sc[pl.ds(starts[qi], span), :], NT_DIMS, preferred_element_type=F32)
           + bias_ref[offs[qi], 0] for qi in range(rq)]
    p_l, l_l = [], []
    for s in s_l:
        p = jnp.exp(s - jnp.max(s, axis=-1, keepdims=True))
        p_l.append(p.astype(BF16))
        l_l.append(jnp.sum(p, axis=-1, keepdims=True))
    o_l = [jnp.dot(p_l[qi], v_ref[0, 0, pl.ds(starts[qi], span), :], preferred_element_type=F32) / l_l[qi]
           for qi in range(rq)]
    for qi, o in enumerate(o_l):
        out = o[0:GRID_W]
        for h in range(1, HEADS_PER_BLOCK):
            out = jnp.where(lane // HEAD_DIM == h, o[h * GRID_W:(h + 1) * GRID_W], out)
        o_ref[0, qi * GRID_W:(qi + 1) * GRID_W, :] = out.astype(o_ref.dtype)


def _na_call(p, bias, qw, kw, mbd, q_blk0, nhp, rq=8):
    b, _, s, _ = p.shape
    rows = s // GRID_W
    rq = min(rq, rows)
    full = lambda a: pl.BlockSpec(a.shape, lambda bi, hp, r: (0,) * a.ndim)
    return pl.pallas_call(
        functools.partial(_na_kernel, rq=rq, rows=rows),
        grid=(b, nhp, rows // rq),
        in_specs=[pl.BlockSpec((1, 1, rq * GRID_W, LANES), lambda bi, hp, r: (bi, q_blk0 + hp, r, 0)),
                  pl.BlockSpec((1, 1, s, LANES), lambda bi, hp, r: (bi, q_blk0 + nhp + hp, 0, 0)),
                  pl.BlockSpec((1, 1, s, LANES), lambda bi, hp, r: (bi, q_blk0 + 2 * nhp + hp, 0, 0)),
                  pl.BlockSpec((WIN_R, 1, HEADS_PER_BLOCK * GRID_W, WIN_R * GRID_W),
                               lambda bi, hp, r: (0, hp, 0, 0)),
                  full(qw), full(kw), full(mbd)],
        out_specs=pl.BlockSpec((1, rq * GRID_W, LANES), lambda bi, hp, r: (bi, r, hp)),
        out_shape=jax.ShapeDtypeStruct((b, s, nhp * LANES), BF16),
        scratch_shapes=[pltpu.VMEM((s, LANES), BF16)],
        compiler_params=_cparams(("parallel", "parallel", "arbitrary")),
        name="na",
    )(p, p, p, bias, qw, kw, mbd)


def _na_bias_table(rel_bias):
    h = rel_bias.shape[0]
    qc = np.arange(GRID_W)
    kc = np.arange(GRID_W)
    win0 = np.clip(qc - WIN_C // 2, 0, GRID_W - WIN_C)
    valid = (kc[None, :] >= win0[:, None]) & (kc[None, :] < win0[:, None] + WIN_C)
    coff = np.clip(kc[None, :] - qc[:, None] + WIN_C - 1, 0, 2 * WIN_C - 2)
    pick = jnp.asarray((coff[None] == np.arange(2 * WIN_C - 1)[:, None, None]).astype(np.float32))
    t = jnp.einsum('hrc,cqk->hrqk', rel_bias.astype(F32), pick, precision=HI)
    t = jnp.stack([t[:, o:o + WIN_R] for o in range(WIN_R)], axis=1)
    t = jnp.where(valid[None, None, None], t, NEG_BIG)
    t = jnp.transpose(t, (1, 0, 3, 2, 4))
    return t.reshape(WIN_R, h // HEADS_PER_BLOCK, HEADS_PER_BLOCK * GRID_W, WIN_R * GRID_W)


def _outproj_kernel(yr_ref, yn_ref, x_ref, mod_ref, nw_ref, w1_ref, w2_ref, wr_ref, x1_ref, h2_ref, lg_ref):
    acc = jnp.dot(yr_ref[0], w1_ref[...], preferred_element_type=F32)
    acc = acc + jnp.dot(yn_ref[0], w2_ref[...], preferred_element_type=F32)
    x1 = x_ref[0] + mod_ref[0, 2:3, :] * acc
    x1_ref[0] = x1
    ms = jnp.mean(x1 * x1, axis=-1, keepdims=True)
    y = x1 * lax.rsqrt(ms + NORM_EPS) * nw_ref[...]
    h2 = y * (1.0 + mod_ref[0, 4:5, :]) + mod_ref[0, 3:4, :]
    h2_ref[0] = _pack_halves(h2)
    lg_ref[...] = lax.dot_general(wr_ref[...], h2, NT_DIMS, preferred_element_type=F32, precision=HI)


def _outproj_call(yr, yn, x, mod3, norm_w, w1, w2, wr_t, tm=512):
    b, s, d = x.shape
    dh = yr.shape[-1]
    ne = wr_t.shape[0]
    nt = s // tm
    full = lambda a: pl.BlockSpec(a.shape, lambda bi, i: (0,) * a.ndim)
    return pl.pallas_call(
        _outproj_kernel,
        grid=(b, nt),
        in_specs=[pl.BlockSpec((1, tm, dh), lambda bi, i: (bi, i, 0)),
                  pl.BlockSpec((1, tm, dh), lambda bi, i: (bi, i, 0)),
                  pl.BlockSpec((1, tm, d), lambda bi, i: (bi, i, 0)),
                  pl.BlockSpec((1, 6, d), lambda bi, i: (bi, 0, 0)),
                  full(norm_w), full(w1), full(w2), full(wr_t)],
        out_specs=[pl.BlockSpec((1, tm, d), lambda bi, i: (bi, i, 0)),
                   pl.BlockSpec((1, tm, d // 2), lambda bi, i: (bi, i, 0)),
                   pl.BlockSpec((ne, tm), lambda bi, i: (0, bi * nt + i))],
        out_shape=[jax.ShapeDtypeStruct((b, s, d), F32),
                   jax.ShapeDtypeStruct((b, s, d // 2), U32),
                   jax.ShapeDtypeStruct((ne, b * s), F32)],
        compiler_params=_cparams(("parallel", "parallel")),
        name="outproj",
    )(yr, yn, x, mod3, norm_w, w1, w2, wr_t)


def _first_argmax(x, idx, n):
    m = jnp.max(x, axis=0, keepdims=True)
    a = jnp.min(jnp.where(x == m, idx, n), axis=0, keepdims=True)
    return m, a


def _route_kernel(lg_ref, bias_ref, idx_ref, gw_ref, pos_ref, cnt_ref, *, tr):
    @pl.when(pl.program_id(0) == 0)
    def _():
        cnt_ref[...] = jnp.zeros(cnt_ref.shape, F32)

    gsz = N_EXPERTS // N_GROUPS
    scores = _sigmoid(lg_ref[...])
    biased = scores + bias_ref[:, 0:1]
    ig = lax.broadcasted_iota(I32, (gsz, tr), 0)
    grp_rows = []
    for g in range(N_GROUPS):
        blk = biased[g * gsz:(g + 1) * gsz]
        m1, a1 = _first_argmax(blk, ig, gsz)
        m2 = jnp.max(jnp.where(ig == a1, -jnp.inf, blk), axis=0, keepdims=True)
        grp_rows.append(m1 + m2)
    grp = jnp.concatenate(grp_rows, axis=0)
    ign = lax.broadcasted_iota(I32, (N_GROUPS, tr), 0)
    sel = jnp.zeros((N_GROUPS, tr), jnp.bool_)
    for _ in range(TOPK_GROUPS):
        _, a = _first_argmax(grp, ign, N_GROUPS)
        hit = ign == a
        sel = jnp.logical_or(sel, hit)
        grp = jnp.where(hit, -jnp.inf, grp)
    masked = jnp.concatenate(
        [jnp.where(sel[g:g + 1], biased[g * gsz:(g + 1) * gsz], -jnp.inf) for g in range(N_GROUPS)], axis=0)
    ie = lax.broadcasted_iota(I32, (N_EXPERTS, tr), 0)
    picks, pick_scores = [], []
    onehot = jnp.zeros((N_EXPERTS, tr), F32)
    for _ in range(TOP_K):
        _, a = _first_argmax(masked, ie, N_EXPERTS)
        hit = ie == a
        picks.append(a)
        pick_scores.append(jnp.sum(jnp.where(hit, scores, 0.0), axis=0, keepdims=True))
        onehot = onehot + hit.astype(F32)
        masked = jnp.where(hit, -jnp.inf, masked)
    total = pick_scores[0]
    for sc in pick_scores[1:]:
        total = total + sc
    t0 = lax.broadcasted_iota(I32, (tr, tr), 0)
    t1 = lax.broadcasted_iota(I32, (tr, tr), 1)
    before = (t0 < t1).astype(BF16)
    rank = jnp.dot(onehot.astype(BF16), before, preferred_element_type=F32) + cnt_ref[:, 0:1]
    zero_i = jnp.zeros((1, tr), I32)
    zero_f = jnp.zeros((1, tr), F32)
    for k in range(8):
        if k < TOP_K:
            idx_ref[k:k + 1, :] = picks[k]
            gw_ref[k:k + 1, :] = pick_scores[k] / total * ROUTED_SCALE
            pos = jnp.sum(jnp.where(ie == picks[k], rank, 0.0), axis=0, keepdims=True)
            pos_ref[k:k + 1, :] = pos.astype(I32)
        else:
            idx_ref[k:k + 1, :] = zero_i
            gw_ref[k:k + 1, :] = zero_f
            pos_ref[k:k + 1, :] = zero_i
    cnt_ref[...] = cnt_ref[...] + jnp.sum(onehot, axis=1, keepdims=True)


def _route_call(lg_t, bias2, tr=512):
    ne, t = lg_t.shape
    tr = min(tr, t)
    tok = pl.BlockSpec((8, tr), lambda i: (0, i))
    return pl.pallas_call(
        functools.partial(_route_kernel, tr=tr),
        grid=(t // tr,),
        in_specs=[pl.BlockSpec((ne, tr), lambda i: (0, i)),
                  pl.BlockSpec(bias2.shape, lambda i: (0, 0))],
        out_specs=[tok, tok, tok, pl.BlockSpec((ne, LANES), lambda i: (0, 0))],
        out_shape=[jax.ShapeDtypeStruct((8, t), I32), jax.ShapeDtypeStruct((8, t), F32),
                   jax.ShapeDtypeStruct((8, t), I32), jax.ShapeDtypeStruct((ne, LANES), F32)],
        compiler_params=_cparams(("arbitrary",)),
        name="route",
    )(lg_t, bias2)


ZERO_ROWS = 64


def _dispatch_kernel(slot_ref, fill_lo_ref, fill_hi_ref, nv_ref, h_ref, xs_ref, zrow, sem, *, td, n_blocks):
    def row_copy(t, slot):
        return pltpu.make_async_copy(h_ref.at[pl.ds(t, 1)], xs_ref.at[pl.ds(slot, 1)], sem)

    def start(t, carry):
        for k in range(TOP_K):
            row_copy(t, slot_ref[t * SLOT_STRIDE + k]).start()
        return carry

    def wait(t, carry):
        for k in range(TOP_K):
            row_copy(t, 0).wait()
        return carry

    lax.fori_loop(0, td, start, 0)
    lax.fori_loop(0, td, wait, 0)

    @pl.when(pl.program_id(0) == pl.num_programs(0) - 1)
    def _():
        zrow[...] = jnp.zeros(zrow.shape, zrow.dtype)

        def zero_copy(row):
            return pltpu.make_async_copy(zrow.at[pl.ds(0, 1)], xs_ref.at[pl.ds(row, 1)], sem)

        def per_expert(e, carry):
            lo, hi = fill_lo_ref[e], fill_hi_ref[e]

            def fill_start(row, c):
                zero_copy(row).start()
                return c

            def fill_wait(row, c):
                zero_copy(0).wait()
                return c

            lax.fori_loop(lo, hi, fill_start, 0)
            lax.fori_loop(lo, hi, fill_wait, 0)
            return carry

        lax.fori_loop(0, N_EXPERTS, per_expert, 0)

        pieces = DISPATCH_BLOCK // ZERO_ROWS

        def tail_copy(row):
            return pltpu.make_async_copy(zrow, xs_ref.at[pl.ds(row, ZERO_ROWS)], sem)

        def tail_start(b, c):
            for j in range(pieces):
                tail_copy(b * DISPATCH_BLOCK + j * ZERO_ROWS).start()
            return c

        def tail_wait(b, c):
            for j in range(pieces):
                tail_copy(0).wait()
            return c

        lax.fori_loop(nv_ref[0], n_blocks, tail_start, 0)
        lax.fori_loop(nv_ref[0], n_blocks, tail_wait, 0)


def _dispatch_call(slots_flat, fill_lo, fill_hi, n_valid, h2, n_blocks, td=256):
    t, d = h2.shape
    smem = pl.BlockSpec(memory_space=pltpu.SMEM)
    return pl.pallas_call(
        functools.partial(_dispatch_kernel, td=td, n_blocks=n_blocks),
        grid=(t // td,),
        in_specs=[pl.BlockSpec((td * SLOT_STRIDE,), lambda i: (i,), memory_space=pltpu.SMEM),
                  smem, smem, smem,
                  pl.BlockSpec((td, d), lambda i: (i, 0))],
        out_specs=pl.BlockSpec(memory_space=pl.ANY),
        out_shape=jax.ShapeDtypeStruct((n_blocks * DISPATCH_BLOCK, d), h2.dtype),
        scratch_shapes=[pltpu.VMEM((ZERO_ROWS, d), h2.dtype), pltpu.SemaphoreType.DMA(())],
        compiler_params=_cparams(("arbitrary",)),
        name="dispatch",
    )(slots_flat, fill_lo, fill_hi, n_valid, h2)


def _experts_kernel(be_ref, nv_ref, xs_ref, wg_ref, wu_ref, wd_ref, ys_ref, wg_sc, wu_sc, wd_sc):
    b = pl.program_id(0)

    @pl.when(b < nv_ref[0])
    def _():
        prev = be_ref[jnp.maximum(b - 1, 0)]
        @pl.when(jnp.logical_or(b == 0, be_ref[b] != prev))
        def _():
            wg_sc[...] = wg_ref[0].astype(BF16)
            wu_sc[...] = wu_ref[0].astype(BF16)
            wd_sc[...] = wd_ref[0].astype(BF16)

        half = xs_ref.shape[1]
        x_lo, x_hi = _unpack_halves(xs_ref[...])
        x_lo, x_hi = x_lo.astype(BF16), x_hi.astype(BF16)
        g = (jnp.dot(x_lo, wg_sc[0:half], preferred_element_type=F32)
             + jnp.dot(x_hi, wg_sc[half:], preferred_element_type=F32))
        u = (jnp.dot(x_lo, wu_sc[0:half], preferred_element_type=F32)
             + jnp.dot(x_hi, wu_sc[half:], preferred_element_type=F32))
        a = (_silu(g) * u).astype(BF16)
        ys_ref[...] = _pack_halves(jnp.dot(a, wd_sc[...], preferred_element_type=F32))

    @pl.when(b >= nv_ref[0])
    def _():
        ys_ref[...] = jnp.zeros(ys_ref.shape, ys_ref.dtype)


def _experts_call(block_e, n_valid, xs, wg, wu, wd):
    total, d = xs.shape[0], wg.shape[1]
    m = DISPATCH_BLOCK
    ff = wg.shape[-1]
    rows = lambda b, be, nv: (jnp.minimum(b, nv[0] - 1), 0)
    out_rows = lambda b, be, nv: (b, 0)
    grid_spec = pltpu.PrefetchScalarGridSpec(
        num_scalar_prefetch=2,
        grid=(total // m,),
        in_specs=[pl.BlockSpec((m, d // 2), rows),
                  pl.BlockSpec((1, d, ff), lambda b, be, nv: (be[b], 0, 0)),
                  pl.BlockSpec((1, d, ff), lambda b, be, nv: (be[b], 0, 0)),
                  pl.BlockSpec((1, ff, d), lambda b, be, nv: (be[b], 0, 0))],
        out_specs=pl.BlockSpec((m, d // 2), out_rows),
        scratch_shapes=[pltpu.VMEM((d, ff), BF16), pltpu.VMEM((d, ff), BF16), pltpu.VMEM((ff, d), BF16)],
    )
    return pl.pallas_call(
        _experts_kernel,
        grid_spec=grid_spec,
        out_shape=jax.ShapeDtypeStruct((total, d // 2), U32),
        compiler_params=_cparams(("arbitrary",)),
        name="experts",
    )(block_e, n_valid, xs, wg, wu, wd)


def _combine_kernel(slot_ref, x1_ref, h2_ref, gw_ref, mod_ref, wg_ref, wu_ref, wd_ref, ys_ref, o_ref,
                    buf, sem, *, tc):
    def row_copy(t, k, slot):
        return pltpu.make_async_copy(ys_ref.at[pl.ds(slot, 1)], buf.at[k, pl.ds(t, 1)], sem)

    def start(t, carry):
        for k in range(TOP_K):
            row_copy(t, k, slot_ref[t * SLOT_STRIDE + k]).start()
        return carry

    def wait(t, carry):
        for k in range(TOP_K):
            row_copy(t, k, 0).wait()
        return carry

    lax.fori_loop(0, tc, start, 0)
    half = h2_ref.shape[2]
    h_lo, h_hi = _unpack_halves(h2_ref[0])
    h_lo, h_hi = h_lo.astype(BF16), h_hi.astype(BF16)
    g = (jnp.dot(h_lo, wg_ref[0:half], preferred_element_type=F32)
         + jnp.dot(h_hi, wg_ref[half:], preferred_element_type=F32))
    u = (jnp.dot(h_lo, wu_ref[0:half], preferred_element_type=F32)
         + jnp.dot(h_hi, wu_ref[half:], preferred_element_type=F32))
    acc = jnp.dot((_silu(g) * u).astype(BF16), wd_ref[...], preferred_element_type=F32)
    lax.fori_loop(0, tc, wait, 0)
    gw = gw_ref[...]
    acc_lo, acc_hi = acc[:, :half], acc[:, half:]
    for k in range(TOP_K):
        y_lo, y_hi = _unpack_halves(buf[k])
        acc_lo = acc_lo + gw[:, k:k + 1] * y_lo
        acc_hi = acc_hi + gw[:, k:k + 1] * y_hi
    gate = mod_ref[0, 5:6, :]
    o_ref[0, :, 0:half] = x1_ref[0, :, 0:half] + gate[:, :half] * acc_lo
    o_ref[0, :, half:] = x1_ref[0, :, half:] + gate[:, half:] * acc_hi


def _combine_call(slots_flat, x1, h2, gw_tok, mod3, wg, wu, wd, ys, tc=128):
    b, s, d = x1.shape
    nt = s // tc
    full = lambda a: pl.BlockSpec(a.shape, lambda bi, i: (0,) * a.ndim)
    return pl.pallas_call(
        functools.partial(_combine_kernel, tc=tc),
        grid=(b, nt),
        in_specs=[pl.BlockSpec((tc * SLOT_STRIDE,), lambda bi, i: (bi * nt + i,), memory_space=pltpu.SMEM),
                  pl.BlockSpec((1, tc, d), lambda bi, i: (bi, i, 0)),
                  pl.BlockSpec((1, tc, d // 2), lambda bi, i: (bi, i, 0)),
                  pl.BlockSpec((tc, 8), lambda bi, i: (bi * nt + i, 0)),
                  pl.BlockSpec((1, 6, d), lambda bi, i: (bi, 0, 0)),
                  full(wg), full(wu), full(wd),
                  pl.BlockSpec(memory_space=pl.ANY)],
        out_specs=pl.BlockSpec((1, tc, d), lambda bi, i: (bi, i, 0)),
        out_shape=jax.ShapeDtypeStruct((b, s, d), F32),
        scratch_shapes=[pltpu.VMEM((TOP_K, tc, d // 2), U32), pltpu.SemaphoreType.DMA(())],
        compiler_params=_cparams(("arbitrary", "arbitrary")),
        name="combine",
    )(slots_flat, x1, h2, gw_tok, mod3, wg, wu, wd, ys)


def _pad_rows(w, rows, at=0):
    out = jnp.zeros((rows, w.shape[1]), w.dtype)
    return out.at[at:at + w.shape[0]].set(w)


def _mixing_stage(p, rwkv_shift, w0_f, w_up_f, w0_b, w_up_b, a0_f, a_up_f, a0_b, a_up_b, g_up,
                  k_k, k_a, r_k, ln_x_w, ln_x_b, q_norm_w, k_norm_w, rel_bias):
    d_rwkv = w0_f.shape[0]
    nhp = d_rwkv // LANES
    mbd = _head_sum_matrix()
    n_shift = rwkv_shift.shape[1]
    sw = jnp.pad(rwkv_shift, ((0, 0), (0, N_SHIFT_BLOCKS * LANES - n_shift))).reshape(3, N_SHIFT_BLOCKS, LANES)
    wupf = _pad_rows(w_up_f, LANES, 0).astype(BF16)
    wupb = _pad_rows(w_up_b, LANES, DECAY_LORA).astype(BF16)
    aupf = _pad_rows(a_up_f, LANES, 0).astype(BF16)
    aupb = _pad_rows(a_up_b, LANES, AAA_LORA).astype(BF16)
    gup = _pad_rows(g_up, 2 * LANES, 0).astype(BF16)
    vecs = jnp.stack([w0_f, w0_b, a0_f, a0_b, k_k, k_a, r_k.reshape(-1)]).reshape(7, nhp, LANES)
    rb, lw = _prep_call(p, sw, wupf, wupb, aupf, aupb, gup, vecs, mbd, nhp)
    y_f, y_b = _wkv_call(rb, lw, nhp)
    y_rwkv = _post_call(y_f, y_b, rb, ln_x_w.reshape(nhp, LANES), ln_x_b.reshape(nhp, LANES), mbd, nhp)
    bias = _na_bias_table(rel_bias)
    qw = jnp.tile(q_norm_w, HEADS_PER_BLOCK).reshape(1, LANES)
    kw = jnp.tile(k_norm_w, HEADS_PER_BLOCK).reshape(1, LANES)
    y_na = _na_call(p, bias, qw, kw, mbd, N_SHIFT_BLOCKS, nhp)
    return y_rwkv, y_na


def _moe_stage(x1, h2, lg_t, mod3, router_bias, w_gate_e, w_up_e, w_down_e, w_gate_s, w_up_s, w_down_s):
    b, s, d = x1.shape
    t = b * s
    m = DISPATCH_BLOCK
    bias2 = jnp.broadcast_to(router_bias.astype(F32)[:, None], (N_EXPERTS, LANES))
    idx_t, gw_t, pos_t, cnt = _route_call(lg_t, bias2)
    counts = cnt[:, 0].astype(I32)
    padded = (counts + m - 1) // m * m
    pad_end = jnp.cumsum(padded)
    pad_start = pad_end - padded
    n_blocks = -(-(t * TOP_K + N_EXPERTS * m) // m)
    n_valid = (pad_end[-1] // m).astype(I32).reshape(1)
    block_row0 = jnp.arange(n_blocks, dtype=I32) * m
    block_e = jnp.minimum(jnp.sum((pad_end[None, :] <= block_row0[:, None]).astype(I32), axis=1),
                          N_EXPERTS - 1).astype(I32)
    slots = (jnp.take(pad_start, idx_t) + pos_t).astype(I32)
    slots_flat = slots.T.reshape(-1)
    gw_tok = gw_t.T
    xs = _dispatch_call(slots_flat, (pad_start + counts).astype(I32), pad_end.astype(I32), n_valid,
                        h2.reshape(t, d // 2), n_blocks)
    ys = _experts_call(block_e, n_valid, xs, w_gate_e, w_up_e, w_down_e)
    return _combine_call(slots_flat, x1, h2, gw_tok, mod3, w_gate_s.astype(BF16), w_up_s.astype(BF16),
                         w_down_s.astype(BF16), ys)


def kernel(x, c, w_ada, b_ada, norm1_w, w_in, rwkv_shift, w0_f, w_up_f, w0_b, w_up_b, a0_f, a_up_f, a0_b,
           a_up_b, g_up, k_k, k_a, r_k, ln_x_w, ln_x_b, q_norm_w, k_norm_w, rel_bias, w_out, norm2_w,
           w_router, router_bias, w_gate_e, w_up_e, w_down_e, w_gate_s, w_up_s, w_down_s):
    bn, sn, d = x.shape
    depth = w_ada.shape[0]
    for l in range(depth):
        c_pad = jnp.pad(c, ((0, 8 - bn % 8 if bn % 8 else 0), (0, 0)))
        mod = _mod_call(c_pad, w_ada[l], b_ada[l].reshape(1, -1))[:bn]
        mod3 = mod.reshape(bn, 6, d)
        rwkv_in = rwkv_shift.shape[-1]
        pad_cols = N_SHIFT_BLOCKS * LANES - rwkv_in
        w_in_r = jnp.concatenate([w_in[l][:, :rwkv_in], jnp.zeros((d, pad_cols), w_in.dtype),
                                  w_in[l][:, rwkv_in:]], axis=1).astype(BF16)
        p = _proj_call(x, mod3, norm1_w[l].reshape(1, d), w_in_r)
        y_rwkv, y_na = _mixing_stage(p, rwkv_shift[l], w0_f[l], w_up_f[l], w0_b[l], w_up_b[l], a0_f[l],
                                     a_up_f[l], a0_b[l], a_up_b[l], g_up[l], k_k[l], k_a[l], r_k[l],
                                     ln_x_w[l], ln_x_b[l], q_norm_w[l], k_norm_w[l], rel_bias[l])
        d_rwkv = y_rwkv.shape[-1]
        w_o = w_out[l].astype(BF16)
        x1, h2, lg_t = _outproj_call(y_rwkv, y_na, x, mod3, norm2_w[l].reshape(1, d), w_o[:d_rwkv], w_o[d_rwkv:],
                                     w_router[l].T)
        x = _moe_stage(x1, h2, lg_t, mod3, router_bias[l], w_gate_e[l], w_up_e[l], w_down_e[l],
                       w_gate_s[l], w_up_s[l], w_down_s[l])
    return x
```

```python
import functools

import jax
import jax.numpy as jnp
import numpy as np
from jax import lax
from jax.experimental import pallas as pl
from jax.experimental.pallas import tpu as pltpu

F32 = jnp.float32
BF16 = jnp.bfloat16
I32 = jnp.int32
HI = lax.Precision.HIGHEST

LANES = 128
HEAD_DIM = 64
HEADS_PER_BLOCK = LANES // HEAD_DIM
GRID_W = 64
WIN_R = 8
WIN_C = 16
DECAY_LORA = 64
AAA_LORA = 64
GATE_LORA = 160
N_EXPERTS = 64
TOP_K = 6
SLOT_STRIDE = 8
N_GROUPS = 8
TOPK_GROUPS = 4
ROUTED_SCALE = 2.5
DISPATCH_BLOCK = 512
NORM_EPS = 1e-6
GN_EPS = 64e-5
L2_EPS = 1e-12
NEG_BIG = -1e30
WKV_CHUNK = 64
VMEM_LIMIT = 56 * 1024 * 1024

NT_DIMS = (((1,), (1,)), ((), ()))
TN_DIMS = (((0,), (0,)), ((), ()))


def _cparams(sem):
    return pltpu.CompilerParams(dimension_semantics=sem, vmem_limit_bytes=VMEM_LIMIT)


def _sigmoid(x):
    return 1.0 / (1.0 + jnp.exp(-x))


def _silu(x):
    return x * _sigmoid(x)


U32 = jnp.uint32
HI_HALF_MASK = 0xFFFF0000


def _pack_halves(x):
    n = x.shape[-1] // 2
    bits = lax.bitcast_convert_type(x.astype(BF16).astype(F32), U32)
    return (bits[:, :n] >> 16) | bits[:, n:]


def _unpack_halves(w):
    lo = lax.bitcast_convert_type(w << 16, F32)
    hi = lax.bitcast_convert_type(w & U32(HI_HALF_MASK), F32)
    return lo, hi


def _head_sum_matrix():
    a = np.arange(LANES) // HEAD_DIM
    return jnp.asarray((a[:, None] == a[None, :]).astype(np.float32))


def _mod_kernel(c_ref, w_ref, b_ref, o_ref):
    c = c_ref[...]
    o_ref[...] = jnp.dot(_silu(c), w_ref[...], preferred_element_type=F32, precision=HI) + b_ref[...]


def _mod_call(c_pad, w_ada, b_ada):
    rows, d = c_pad.shape
    n = w_ada.shape[1]
    tn = 1024
    return pl.pallas_call(
        _mod_kernel,
        grid=(n // tn,),
        in_specs=[pl.BlockSpec((rows, d), lambda j: (0, 0)),
                  pl.BlockSpec((d, tn), lambda j: (0, j)),
                  pl.BlockSpec((1, tn), lambda j: (0, j))],
        out_specs=pl.BlockSpec((rows, tn), lambda j: (0, j)),
        out_shape=jax.ShapeDtypeStruct((rows, n), F32),
        compiler_params=_cparams(("parallel",)),
        name="mod",
    )(c_pad, w_ada, b_ada)


def _proj_kernel(x_ref, mod_ref, nw_ref, w_ref, o_ref, h_sc, *, nq):
    @pl.when(pl.program_id(2) == 0)
    def _():
        x = x_ref[0]
        ms = jnp.mean(x * x, axis=-1, keepdims=True)
        y = x * lax.rsqrt(ms + NORM_EPS) * nw_ref[...]
        h_sc[...] = (y * (1.0 + mod_ref[0, 1:2, :]) + mod_ref[0, 0:1, :]).astype(BF16)

    acc = jnp.dot(h_sc[...], w_ref[...], preferred_element_type=F32)
    for q in range(nq):
        o_ref[0, q] = acc[:, q * LANES:(q + 1) * LANES].astype(o_ref.dtype)


def _proj_call(x, mod3, norm_w, w_in_r, tm=1024, tn=512):
    b, s, d = x.shape
    n = w_in_r.shape[1]
    nq = tn // LANES
    return pl.pallas_call(
        functools.partial(_proj_kernel, nq=nq),
        grid=(b, s // tm, n // tn),
        in_specs=[pl.BlockSpec((1, tm, d), lambda bi, i, j: (bi, i, 0)),
                  pl.BlockSpec((1, 6, d), lambda bi, i, j: (bi, 0, 0)),
                  pl.BlockSpec((1, d), lambda bi, i, j: (0, 0)),
                  pl.BlockSpec((d, tn), lambda bi, i, j: (0, j))],
        out_specs=pl.BlockSpec((1, nq, tm, LANES), lambda bi, i, j: (bi, j, i, 0)),
        out_shape=jax.ShapeDtypeStruct((b, n // LANES, s, LANES), BF16),
        scratch_shapes=[pltpu.VMEM((tm, d), BF16)],
        compiler_params=_cparams(("parallel", "parallel", "arbitrary")),
        name="proj",
    )(x, mod3, norm_w, w_in_r)


RB_R, RB_V, RB_KK, RB_G, RB_KF, RB_KB, RB_AF, RB_AB, RB_BONUS = range(9)
VEC_W0F, VEC_W0B, VEC_A0F, VEC_A0B, VEC_KK, VEC_KA, VEC_RK = range(7)
N_RKV_BLOCKS = 24
N_SHIFT_BLOCKS = 28


def _softplus(u):
    return jnp.maximum(u, 0.0) + jnp.log(1.0 + jnp.exp(-jnp.abs(u)))


def _prep_kernel(p_ref, pp_ref, pn_ref, sw_ref, wupf_ref, wupb_ref, aupf_ref, aupb_ref, gup_ref,
                 vec_ref, mbd_ref, rb_ref, lw_ref, *, tb, n_t, nhp):
    i = pl.program_id(1)
    row = lax.broadcasted_iota(I32, (tb, LANES), 0)
    has_prev = i > 0
    has_next = i < n_t - 1
    halo = pp_ref.shape[2]

    def shifted(q):
        cur = p_ref[0, q].astype(F32)
        prev_row = jnp.where(has_prev, pp_ref[0, q, halo - 1:halo, :].astype(F32), 0.0)
        next_row = jnp.where(has_next, pn_ref[0, q, 0:1, :].astype(F32), 0.0)
        up = jnp.where(row == 0, prev_row, pltpu.roll(cur, 1, 0))
        dn = jnp.where(row == tb - 1, next_row, pltpu.roll(cur, tb - 1, 0))
        return sw_ref[0, q:q + 1, :] * up + sw_ref[1, q:q + 1, :] * cur + sw_ref[2, q:q + 1, :] * dn

    mbd = mbd_ref[...]
    t_wd = jnp.tanh(shifted(N_RKV_BLOCKS)).astype(BF16)
    z_ad = shifted(N_RKV_BLOCKS + 1).astype(BF16)
    s_gd = jnp.concatenate([_sigmoid(shifted(N_RKV_BLOCKS + 2)),
                            _sigmoid(shifted(N_RKV_BLOCKS + 3))], axis=1).astype(BF16)

    def vec(v, hp):
        return vec_ref[v, hp:hp + 1, :]

    for hp in range(nhp):
        sl = slice(hp * LANES, (hp + 1) * LANES)
        r = shifted(hp)
        k = shifted(nhp + hp)
        v = shifted(2 * nhp + hp)
        g = jnp.dot(s_gd, gup_ref[:, sl], preferred_element_type=F32)
        kk0 = k * vec(VEC_KK, hp)
        ss = jnp.dot(kk0 * kk0, mbd, preferred_element_type=F32, precision=HI)
        kk = kk0 * lax.rsqrt(jnp.maximum(ss, L2_EPS))
        kdirs = []
        for wup_ref, aup_ref, v_w0, v_a0, q_k, q_a, lw_slot in (
                (wupf_ref, aupf_ref, VEC_W0F, VEC_A0F, RB_KF, RB_AF, 0),
                (wupb_ref, aupb_ref, VEC_W0B, VEC_A0B, RB_KB, RB_AB, 1)):
            wl = vec(v_w0, hp) + jnp.dot(t_wd, wup_ref[:, sl], preferred_element_type=F32)
            w_log = -_softplus(-wl) - 0.5
            lw_ref[0, lw_slot * nhp + hp] = -jnp.exp(w_log)
            ag = _sigmoid(vec(v_a0, hp) + jnp.dot(z_ad, aup_ref[:, sl], preferred_element_type=F32))
            kd = k * (1.0 + (ag - 1.0) * vec(VEC_KA, hp))
            rb_ref[0, q_k * nhp + hp] = kd.astype(BF16)
            rb_ref[0, q_a * nhp + hp] = ag.astype(BF16)
            kdirs.append(kd)
        bonus = jnp.dot(r * kdirs[0] * vec(VEC_RK, hp), mbd, preferred_element_type=F32, precision=HI) * v
        rb_ref[0, RB_R * nhp + hp] = r.astype(BF16)
        rb_ref[0, RB_V * nhp + hp] = v.astype(BF16)
        rb_ref[0, RB_KK * nhp + hp] = kk.astype(BF16)
        rb_ref[0, RB_G * nhp + hp] = g.astype(BF16)
        rb_ref[0, RB_BONUS * nhp + hp] = bonus.astype(BF16)


def _prep_call(p, sw, wupf, wupb, aupf, aupb, gup, vecs, mbd, nhp, tb=256):
    b, _, s, _ = p.shape
    n_t = s // tb
    halo = 16
    hb = tb // halo
    full = lambda a: pl.BlockSpec(a.shape, lambda bi, i: (0,) * a.ndim)
    return pl.pallas_call(
        functools.partial(_prep_kernel, tb=tb, n_t=n_t, nhp=nhp),
        grid=(b, n_t),
        in_specs=[pl.BlockSpec((1, N_SHIFT_BLOCKS, tb, LANES), lambda bi, i: (bi, 0, i, 0)),
                  pl.BlockSpec((1, N_SHIFT_BLOCKS, halo, LANES),
                               lambda bi, i: (bi, 0, jnp.maximum(i * hb - 1, 0), 0)),
                  pl.BlockSpec((1, N_SHIFT_BLOCKS, halo, LANES),
                               lambda bi, i: (bi, 0, jnp.minimum((i + 1) * hb, s // halo - 1), 0)),
                  full(sw), full(wupf), full(wupb), full(aupf), full(aupb), full(gup), full(vecs), full(mbd)],
        out_specs=[pl.BlockSpec((1, 9 * nhp, tb, LANES), lambda bi, i: (bi, 0, i, 0)),
                   pl.BlockSpec((1, 2 * nhp, tb, LANES), lambda bi, i: (bi, 0, i, 0))],
        out_shape=[jax.ShapeDtypeStruct((b, 9 * nhp, s, LANES), BF16),
                   jax.ShapeDtypeStruct((b, 2 * nhp, s, LANES), F32)],
        compiler_params=_cparams(("parallel", "parallel")),
        name="prep",
    )(p, p, p, sw, wupf, wupb, aupf, aupb, gup, vecs, mbd)


def _wkv_kernel(r_ref, v_ref, kk_ref, kf_ref, af_ref, lf_ref, rr_ref, vr_ref, kkr_ref, kb_ref, ab_ref, lb_ref,
                yf_ref, yb_ref, s_sc, *, L, nhp):
    @pl.when(pl.program_id(1) == 0)
    def _():
        s_sc[...] = jnp.zeros(s_sc.shape, F32)

    n2 = HEADS_PER_BLOCK * L
    ri = lax.broadcasted_iota(I32, (n2, LANES), 0)
    ci = lax.broadcasted_iota(I32, (n2, LANES), 1)
    head_match = (ri // L) == (ci // HEAD_DIM)
    si = lax.broadcasted_iota(I32, (n2, n2), 0)
    sj = lax.broadcasted_iota(I32, (n2, n2), 1)
    st, ss = si % L, sj % L
    eye = (si == sj).astype(F32)
    strict = {False: st > ss, True: st < ss}
    incl = {False: st >= ss, True: st < ss}

    def stack(x):
        return jnp.where(head_match, jnp.concatenate([x] * HEADS_PER_BLOCK, axis=0), 0.0)

    chains = [(False, hp, r_ref, v_ref, kk_ref, kf_ref, af_ref, lf_ref) for hp in range(nhp)]
    chains += [(True, hp, rr_ref, vr_ref, kkr_ref, kb_ref, ab_ref, lb_ref) for hp in range(nhp)]
    n = len(chains)
    lws = [ch[7][0, ch[1]] for ch in chains]
    t_row = lax.broadcasted_iota(I32, (L, LANES), 0)

    def cumsum_time(x, rev):
        sh = 1
        while sh < L:
            if rev:
                x = x + jnp.where(t_row < L - sh, pltpu.roll(x, L - sh, 0), 0.0)
            else:
                x = x + jnp.where(t_row >= sh, pltpu.roll(x, sh, 0), 0.0)
            sh *= 2
        return x

    c_ins = [cumsum_time(lw, ch[0]) for ch, lw in zip(chains, lws)]
    lhs_l, rhs_l, vs_l, bk_l, pl_l = [], [], [], [], []
    for i, (rev, hp, rr, vr, kkr, kdr, agr, _) in enumerate(chains):
        r = rr[0, hp].astype(F32)
        kk = kkr[0, hp].astype(F32)
        kd = kdr[0, hp].astype(F32)
        ag = agr[0, hp].astype(F32)
        c_in = c_ins[i]
        e_ex = jnp.exp(c_in - lws[i])
        e_inv = jnp.exp(-c_in)
        if rev:
            e_r = e_ex
            p_last = jnp.exp(c_in[0:1])
        else:
            e_r = jnp.exp(c_in)
            p_last = jnp.exp(c_in[L - 1:L])
        b_s = stack(kk * ag * e_inv)
        k_s = stack(kd * e_inv)
        lhs_l.append(jnp.concatenate([stack(-kk * e_ex), stack(r * e_r)], axis=0).astype(BF16))
        rhs_l.append(jnp.concatenate([b_s, k_s], axis=0).astype(BF16))
        bk_l.append(jnp.concatenate([b_s * p_last, k_s * p_last], axis=0).astype(BF16))
        vs_l.append(stack(vr[0, hp].astype(F32)).astype(BF16))
        pl_l.append(p_last)
    g_l = [lax.dot_general(lhs_l[i], rhs_l[i], NT_DIMS, preferred_element_type=F32) for i in range(n)]
    s_old = [s_sc[i] for i in range(n)]
    ls_l = [lax.dot_general(lhs_l[i], s_old[i].astype(BF16), NT_DIMS, preferred_element_type=F32)
            for i in range(n)]
    a_ab = [jnp.where(strict[ch[0]], g[0:n2, 0:n2], 0.0) for ch, g in zip(chains, g_l)]
    a_ak = [jnp.where(strict[ch[0]], g[0:n2, n2:2 * n2], 0.0).astype(BF16) for ch, g in zip(chains, g_l)]
    a_r = [jnp.concatenate([jnp.where(incl[ch[0]], g[n2:2 * n2, 0:n2], 0.0),
                            jnp.where(incl[ch[0]], g[n2:2 * n2, n2:2 * n2], 0.0)], axis=1).astype(BF16)
           for ch, g in zip(chains, g_l)]
    x_l = [ls_l[i][0:n2] + jnp.dot(a_ak[i], vs_l[i], preferred_element_type=F32) for i in range(n)]
    t_inv = [eye + a for a in a_ab]
    nb = [a.astype(BF16) for a in a_ab]
    for _ in range(L.bit_length() - 2):
        nb = [jnp.dot(m, m, preferred_element_type=F32).astype(BF16) for m in nb]
        t_inv = [t + jnp.dot(m, t.astype(BF16), preferred_element_type=F32) for m, t in zip(nb, t_inv)]
    u_l = [jnp.dot(t_inv[i].astype(BF16), x_l[i].astype(BF16), preferred_element_type=F32) for i in range(n)]
    uv_l = [jnp.concatenate([jnp.where(head_match, u_l[i], 0.0).astype(BF16), vs_l[i]], axis=0)
            for i in range(n)]
    y_l = [ls_l[i][n2:2 * n2] + jnp.dot(a_r[i], uv_l[i], preferred_element_type=F32) for i in range(n)]
    upd_l = [lax.dot_general(uv_l[i], bk_l[i], TN_DIMS, preferred_element_type=F32) for i in range(n)]
    for i, (rev, hp, *_) in enumerate(chains):
        y = y_l[i]
        out = y[0:L]
        for h in range(1, HEADS_PER_BLOCK):
            out = out + y[h * L:(h + 1) * L]
        (yb_ref if rev else yf_ref)[0, hp] = out
        s_sc[i] = s_old[i] * pl_l[i] + upd_l[i]


def _wkv_call(rb, lw, nhp, L=WKV_CHUNK):
    b, _, s, _ = rb.shape
    nc = s // L
    fwd = lambda q: pl.BlockSpec((1, nhp, L, LANES), lambda bi, c: (bi, q, c, 0))
    bwd = lambda q: pl.BlockSpec((1, nhp, L, LANES), lambda bi, c: (bi, q, nc - 1 - c, 0))
    y_shape = jax.ShapeDtypeStruct((b, nhp, s, LANES), F32)
    return pl.pallas_call(
        functools.partial(_wkv_kernel, L=L, nhp=nhp),
        grid=(b, nc),
        in_specs=[fwd(RB_R), fwd(RB_V), fwd(RB_KK), fwd(RB_KF), fwd(RB_AF), fwd(0),
                  bwd(RB_R), bwd(RB_V), bwd(RB_KK), bwd(RB_KB), bwd(RB_AB), bwd(1)],
        out_specs=[fwd(0), bwd(0)],
        out_shape=[y_shape, y_shape],
        scratch_shapes=[pltpu.VMEM((2 * nhp, LANES, LANES), F32)],
        compiler_params=_cparams(("parallel", "arbitrary")),
        name="wkv",
    )(rb, rb, rb, rb, rb, lw, rb, rb, rb, rb, rb, lw)


def _post_kernel(yf_ref, yb_ref, bonus_ref, g_ref, lnw_ref, lnb_ref, mbd_ref, o_ref, *, nhp):
    mbd = mbd_ref[...]
    inv_n = 1.0 / HEAD_DIM
    for hp in range(nhp):
        y = yf_ref[0, hp] + yb_ref[0, hp]
        mu = jnp.dot(y, mbd, preferred_element_type=F32, precision=HI) * inv_n
        d = y - mu
        var = jnp.dot(d * d, mbd, preferred_element_type=F32, precision=HI) * inv_n
        yn = d * lax.rsqrt(var + GN_EPS) * lnw_ref[hp:hp + 1, :] + lnb_ref[hp:hp + 1, :]
        out = (yn + bonus_ref[0, hp].astype(F32)) * g_ref[0, hp].astype(F32)
        o_ref[0, :, hp * LANES:(hp + 1) * LANES] = out.astype(o_ref.dtype)


def _post_call(yf, yb, rb, lnw, lnb, mbd, nhp, tb=256):
    b, _, s, _ = yf.shape
    yblk = pl.BlockSpec((1, nhp, tb, LANES), lambda bi, i: (bi, 0, i, 0))
    rblk = lambda q: pl.BlockSpec((1, nhp, tb, LANES), lambda bi, i: (bi, q, i, 0))
    full = lambda a: pl.BlockSpec(a.shape, lambda bi, i: (0,) * a.ndim)
    return pl.pallas_call(
        functools.partial(_post_kernel, nhp=nhp),
        grid=(b, s // tb),
        in_specs=[yblk, yblk, rblk(RB_BONUS), rblk(RB_G), full(lnw), full(lnb), full(mbd)],
        out_specs=pl.BlockSpec((1, tb, nhp * LANES), lambda bi, i: (bi, i, 0)),
        out_shape=jax.ShapeDtypeStruct((b, s, nhp * LANES), BF16),
        compiler_params=_cparams(("parallel", "parallel")),
        name="post",
    )(yf, yb, rb, rb, lnw, lnb, mbd)


def _na_kernel(q_ref, k_ref, v_ref, bias_ref, qw_ref, kw_ref, mbd_ref, o_ref, kn_sc, *, rq, rows):
    rbi = pl.program_id(2)
    mbd = mbd_ref[...]
    inv_n = 1.0 / HEAD_DIM
    span = WIN_R * GRID_W
    s_len = kn_sc.shape[0]

    @pl.when(rbi == 0)
    def _():
        def body(c, carry):
            off = pl.multiple_of(c * span, span)
            k = k_ref[0, 0, pl.ds(off, span), :].astype(F32)
            ms = jnp.dot(k * k, mbd, preferred_element_type=F32, precision=HI) * inv_n
            kn_sc[pl.ds(off, span), :] = (k * lax.rsqrt(ms + NORM_EPS) * kw_ref[...]).astype(BF16)
            return carry
        lax.fori_loop(0, s_len // span, body, 0)

    ri = lax.broadcasted_iota(I32, (HEADS_PER_BLOCK * GRID_W, LANES), 0)
    ci = lax.broadcasted_iota(I32, (HEADS_PER_BLOCK * GRID_W, LANES), 1)
    head_match = (ri // GRID_W) == (ci // HEAD_DIM)
    lane = lax.broadcasted_iota(I32, (GRID_W, LANES), 1)
    scale = HEAD_DIM ** -0.5
    q_all = q_ref[0, 0].astype(F32)
    ms = jnp.dot(q_all * q_all, mbd, preferred_element_type=F32, precision=HI) * inv_n
    qn_all = q_all * lax.rsqrt(ms + NORM_EPS) * qw_ref[...] * scale
    starts, offs = [], []
    for qi in range(rq):
        i = rbi * rq + qi
        rs = jnp.clip(i - WIN_R // 2, 0, rows - WIN_R)
        offs.append(rs - i + WIN_R - 1)
        starts.append(pl.multiple_of(rs * GRID_W, GRID_W))
    qs_l = [jnp.where(head_match, jnp.concatenate([qn_all[qi * GRID_W:(qi + 1) * GRID_W]] * HEADS_PER_BLOCK,
                                                  axis=0), 0.0).astype(BF16) for qi in range(rq)]
    s_l = [lax.dot_general(qs_l[qi], kn_sc[pl.ds(starts[qi], span), :], NT_DIMS, preferred_element_type=F32)
           + bias_ref[offs[qi], 0] for qi in range(rq)]
    p_l, l_l = [], []
    for s in s_l:
        p = jnp.exp(s - jnp.max(s, axis=-1, keepdims=True))
        p_l.append(p.astype(BF16))
        l_l.append(jnp.sum(p, axis=-1, keepdims=True))
    o_l = [jnp.dot(p_l[qi], v_ref[0, 0, pl.ds(starts[qi], span), :], preferred_element_type=F32) / l_l[qi]
           for qi in range(rq)]
    for qi, o in enumerate(o_l):
        out = o[0:GRID_W]
        for h in range(1, HEADS_PER_BLOCK):
            out = jnp.where(lane // HEAD_DIM == h, o[h * GRID_W:(h + 1) * GRID_W], out)
        o_ref[0, qi * GRID_W:(qi + 1) * GRID_W, :] = out.astype(o_ref.dtype)


def _na_call(p, bias, qw, kw, mbd, q_blk0, nhp, rq=16):
    b, _, s, _ = p.shape
    rows = s // GRID_W
    rq = min(rq, rows)
    full = lambda a: pl.BlockSpec(a.shape, lambda bi, hp, r: (0,) * a.ndim)
    return pl.pallas_call(
        functools.partial(_na_kernel, rq=rq, rows=rows),
        grid=(b, nhp, rows // rq),
        in_specs=[pl.BlockSpec((1, 1, rq * GRID_W, LANES), lambda bi, hp, r: (bi, q_blk0 + hp, r, 0)),
                  pl.BlockSpec((1, 1, s, LANES), lambda bi, hp, r: (bi, q_blk0 + nhp + hp, 0, 0)),
                  pl.BlockSpec((1, 1, s, LANES), lambda bi, hp, r: (bi, q_blk0 + 2 * nhp + hp, 0, 0)),
                  pl.BlockSpec((WIN_R, 1, HEADS_PER_BLOCK * GRID_W, WIN_R * GRID_W),
                               lambda bi, hp, r: (0, hp, 0, 0)),
                  full(qw), full(kw), full(mbd)],
        out_specs=pl.BlockSpec((1, rq * GRID_W, LANES), lambda bi, hp, r: (bi, r, hp)),
        out_shape=jax.ShapeDtypeStruct((b, s, nhp * LANES), BF16),
        scratch_shapes=[pltpu.VMEM((s, LANES), BF16)],
        compiler_params=_cparams(("parallel", "parallel", "arbitrary")),
        name="na",
    )(p, p, p, bias, qw, kw, mbd)


def _na_bias_table(rel_bias):
    h = rel_bias.shape[0]
    qc = np.arange(GRID_W)
    kc = np.arange(GRID_W)
    win0 = np.clip(qc - WIN_C // 2, 0, GRID_W - WIN_C)
    valid = (kc[None, :] >= win0[:, None]) & (kc[None, :] < win0[:, None] + WIN_C)
    coff = np.clip(kc[None, :] - qc[:, None] + WIN_C - 1, 0, 2 * WIN_C - 2)
    pick = jnp.asarray((coff[None] == np.arange(2 * WIN_C - 1)[:, None, None]).astype(np.float32))
    t = jnp.einsum('hrc,cqk->hrqk', rel_bias.astype(F32), pick, precision=HI)
    t = jnp.stack([t[:, o:o + WIN_R] for o in range(WIN_R)], axis=1)
    t = jnp.where(valid[None, None, None], t, NEG_BIG)
    t = jnp.transpose(t, (1, 0, 3, 2, 4))
    return t.reshape(WIN_R, h // HEADS_PER_BLOCK, HEADS_PER_BLOCK * GRID_W, WIN_R * GRID_W)


def _outproj_kernel(yr_ref, yn_ref, x_ref, mod_ref, nw_ref, w1_ref, w2_ref, wr_ref, x1_ref, h2_ref, lg_ref):
    acc = jnp.dot(yr_ref[0], w1_ref[...], preferred_element_type=F32)
    acc = acc + jnp.dot(yn_ref[0], w2_ref[...], preferred_element_type=F32)
    x1 = x_ref[0] + mod_ref[0, 2:3, :] * acc
    x1_ref[0] = x1
    ms = jnp.mean(x1 * x1, axis=-1, keepdims=True)
    y = x1 * lax.rsqrt(ms + NORM_EPS) * nw_ref[...]
    h2 = y * (1.0 + mod_ref[0, 4:5, :]) + mod_ref[0, 3:4, :]
    h2_ref[0] = _pack_halves(h2)
    lg_ref[...] = lax.dot_general(wr_ref[...], h2, NT_DIMS, preferred_element_type=F32, precision=HI)


def _outproj_call(yr, yn, x, mod3, norm_w, w1, w2, wr_t, tm=512):
    b, s, d = x.shape
    dh = yr.shape[-1]
    ne = wr_t.shape[0]
    nt = s // tm
    full = lambda a: pl.BlockSpec(a.shape, lambda bi, i: (0,) * a.ndim)
    return pl.pallas_call(
        _outproj_kernel,
        grid=(b, nt),
        in_specs=[pl.BlockSpec((1, tm, dh), lambda bi, i: (bi, i, 0)),
                  pl.BlockSpec((1, tm, dh), lambda bi, i: (bi, i, 0)),
                  pl.BlockSpec((1, tm, d), lambda bi, i: (bi, i, 0)),
                  pl.BlockSpec((1, 6, d), lambda bi, i: (bi, 0, 0)),
                  full(norm_w), full(w1), full(w2), full(wr_t)],
        out_specs=[pl.BlockSpec((1, tm, d), lambda bi, i: (bi, i, 0)),
                   pl.BlockSpec((1, tm, d // 2), lambda bi, i: (bi, i, 0)),
                   pl.BlockSpec((ne, tm), lambda bi, i: (0, bi * nt + i))],
        out_shape=[jax.ShapeDtypeStruct((b, s, d), F32),
                   jax.ShapeDtypeStruct((b, s, d // 2), U32),
                   jax.ShapeDtypeStruct((ne, b * s), F32)],
        compiler_params=_cparams(("parallel", "parallel")),
        name="outproj",
    )(yr, yn, x, mod3, norm_w, w1, w2, wr_t)


def _first_argmax(x, idx, n):
    m = jnp.max(x, axis=0, keepdims=True)
    a = jnp.min(jnp.where(x == m, idx, n), axis=0, keepdims=True)
    return m, a


def _route_kernel(lg_ref, bias_ref, idx_ref, gw_ref, pos_ref, cnt_ref, *, tr):
    @pl.when(pl.program_id(0) == 0)
    def _():
        cnt_ref[...] = jnp.zeros(cnt_ref.shape, F32)

    gsz = N_EXPERTS // N_GROUPS
    scores = _sigmoid(lg_ref[...])
    biased = scores + bias_ref[:, 0:1]
    ig = lax.broadcasted_iota(I32, (gsz, tr), 0)
    grp_rows = []
    for g in range(N_GROUPS):
        blk = biased[g * gsz:(g + 1) * gsz]
        m1, a1 = _first_argmax(blk, ig, gsz)
        m2 = jnp.max(jnp.where(ig == a1, -jnp.inf, blk), axis=0, keepdims=True)
        grp_rows.append(m1 + m2)
    grp = jnp.concatenate(grp_rows, axis=0)
    ign = lax.broadcasted_iota(I32, (N_GROUPS, tr), 0)
    sel = jnp.zeros((N_GROUPS, tr), jnp.bool_)
    for _ in range(TOPK_GROUPS):
        _, a = _first_argmax(grp, ign, N_GROUPS)
        hit = ign == a
        sel = jnp.logical_or(sel, hit)
        grp = jnp.where(hit, -jnp.inf, grp)
    masked = jnp.concatenate(
        [jnp.where(sel[g:g + 1], biased[g * gsz:(g + 1) * gsz], -jnp.inf) for g in range(N_GROUPS)], axis=0)
    ie = lax.broadcasted_iota(I32, (N_EXPERTS, tr), 0)
    picks, pick_scores = [], []
    onehot = jnp.zeros((N_EXPERTS, tr), F32)
    for _ in range(TOP_K):
        _, a = _first_argmax(masked, ie, N_EXPERTS)
        hit = ie == a
        picks.append(a)
        pick_scores.append(jnp.sum(jnp.where(hit, scores, 0.0), axis=0, keepdims=True))
        onehot = onehot + hit.astype(F32)
        masked = jnp.where(hit, -jnp.inf, masked)
    total = pick_scores[0]
    for sc in pick_scores[1:]:
        total = total + sc
    t0 = lax.broadcasted_iota(I32, (tr, tr), 0)
    t1 = lax.broadcasted_iota(I32, (tr, tr), 1)
    before = (t0 < t1).astype(BF16)
    rank = jnp.dot(onehot.astype(BF16), before, preferred_element_type=F32) + cnt_ref[:, 0:1]
    zero_i = jnp.zeros((1, tr), I32)
    zero_f = jnp.zeros((1, tr), F32)
    for k in range(8):
        if k < TOP_K:
            idx_ref[k:k + 1, :] = picks[k]
            gw_ref[k:k + 1, :] = pick_scores[k] / total * ROUTED_SCALE
            pos = jnp.sum(jnp.where(ie == picks[k], rank, 0.0), axis=0, keepdims=True)
            pos_ref[k:k + 1, :] = pos.astype(I32)
        else:
            idx_ref[k:k + 1, :] = zero_i
            gw_ref[k:k + 1, :] = zero_f
            pos_ref[k:k + 1, :] = zero_i
    cnt_ref[...] = cnt_ref[...] + jnp.sum(onehot, axis=1, keepdims=True)


def _route_call(lg_t, bias2, tr=512):
    ne, t = lg_t.shape
    tr = min(tr, t)
    tok = pl.BlockSpec((8, tr), lambda i: (0, i))
    return pl.pallas_call(
        functools.partial(_route_kernel, tr=tr),
        grid=(t // tr,),
        in_specs=[pl.BlockSpec((ne, tr), lambda i: (0, i)),
                  pl.BlockSpec(bias2.shape, lambda i: (0, 0))],
        out_specs=[tok, tok, tok, pl.BlockSpec((ne, LANES), lambda i: (0, 0))],
        out_shape=[jax.ShapeDtypeStruct((8, t), I32), jax.ShapeDtypeStruct((8, t), F32),
                   jax.ShapeDtypeStruct((8, t), I32), jax.ShapeDtypeStruct((ne, LANES), F32)],
        compiler_params=_cparams(("arbitrary",)),
        name="route",
    )(lg_t, bias2)


FILL_CHUNKS = (1, 8, 64)
ZERO_ROWS = FILL_CHUNKS[-1]


def _dispatch_kernel(slot_ref, fill_lo_ref, fill_hi_ref, h_ref, xs_ref, zrow, sem, *, td):
    def row_copy(t, slot):
        return pltpu.make_async_copy(h_ref.at[pl.ds(t, 1)], xs_ref.at[pl.ds(slot, 1)], sem)

    def start(t, carry):
        for k in range(TOP_K):
            row_copy(t, slot_ref[t * SLOT_STRIDE + k]).start()
        return carry

    def wait(t, carry):
        for k in range(TOP_K):
            row_copy(t, 0).wait()
        return carry

    lax.fori_loop(0, td, start, 0)
    lax.fori_loop(0, td, wait, 0)

    @pl.when(pl.program_id(0) == pl.num_programs(0) - 1)
    def _():
        zrow[...] = jnp.zeros(zrow.shape, zrow.dtype)

        def zero_copy(row, n):
            if n > 1:
                row = pl.multiple_of(row, FILL_CHUNKS[1])
            return pltpu.make_async_copy(zrow.at[pl.ds(0, n)], xs_ref.at[pl.ds(row, n)], sem)

        def for_each_chunk(r, fn):
            lo, hi = fill_lo_ref[r], fill_hi_ref[r]
            for ci, n in enumerate(FILL_CHUNKS):
                if ci + 1 < len(FILL_CHUNKS):
                    count = jnp.minimum(((-lo) % FILL_CHUNKS[ci + 1]) // n, (hi - lo) // n)
                else:
                    count = (hi - lo) // n
                fn(lo, n, count)
                lo = lo + count * n

        def start_range(r, carry):
            def go(lo, n, count):
                lax.fori_loop(0, count, lambda i, c: (zero_copy(lo + i * n, n).start(), c)[1], 0)
            for_each_chunk(r, go)
            return carry

        def wait_range(r, carry):
            def go(lo, n, count):
                lax.fori_loop(0, count, lambda i, c: (zero_copy(0, n).wait(), c)[1], 0)
            for_each_chunk(r, go)
            return carry

        n_ranges = fill_lo_ref.shape[0]
        lax.fori_loop(0, n_ranges, start_range, 0)
        lax.fori_loop(0, n_ranges, wait_range, 0)


def _dispatch_call(slots_flat, fill_lo, fill_hi, h2, n_rows, td=256):
    t, dh = h2.shape
    smem = pl.BlockSpec(memory_space=pltpu.SMEM)
    return pl.pallas_call(
        functools.partial(_dispatch_kernel, td=td),
        grid=(t // td,),
        in_specs=[pl.BlockSpec((td * SLOT_STRIDE,), lambda i: (i,), memory_space=pltpu.SMEM),
                  smem, smem,
                  pl.BlockSpec((td, dh), lambda i: (i, 0))],
        out_specs=pl.BlockSpec(memory_space=pl.ANY),
        out_shape=jax.ShapeDtypeStruct((n_rows, dh), h2.dtype),
        scratch_shapes=[pltpu.VMEM((ZERO_ROWS, dh), h2.dtype), pltpu.SemaphoreType.DMA(())],
        compiler_params=_cparams(("arbitrary",)),
        name="dispatch",
    )(slots_flat, fill_lo, fill_hi, h2)


def _experts_kernel(be_ref, nv_ref, xs_ref, wg_ref, wu_ref, wd_ref, ys_ref, wg_sc, wu_sc, wd_sc):
    b = pl.program_id(0)

    @pl.when(b < nv_ref[0])
    def _():
        prev = be_ref[jnp.maximum(b - 1, 0)]
        @pl.when(jnp.logical_or(b == 0, be_ref[b] != prev))
        def _():
            wg_sc[...] = wg_ref[0].astype(BF16)
            wu_sc[...] = wu_ref[0].astype(BF16)
            wd_sc[...] = wd_ref[0].astype(BF16)

        half = wg_sc.shape[0] // 2
        x_lo, x_hi = _unpack_halves(xs_ref[...])
        x_lo, x_hi = x_lo.astype(BF16), x_hi.astype(BF16)
        g = (jnp.dot(x_lo, wg_sc[0:half], preferred_element_type=F32)
             + jnp.dot(x_hi, wg_sc[half:], preferred_element_type=F32))
        u = (jnp.dot(x_lo, wu_sc[0:half], preferred_element_type=F32)
             + jnp.dot(x_hi, wu_sc[half:], preferred_element_type=F32))
        a = (_silu(g) * u).astype(BF16)
        ys_ref[...] = _pack_halves(jnp.dot(a, wd_sc[...], preferred_element_type=F32))

    @pl.when(b >= nv_ref[0])
    def _():
        ys_ref[...] = jnp.zeros(ys_ref.shape, ys_ref.dtype)


def _experts_call(block_e, n_valid, xs, wg, wu, wd):
    total, d = xs.shape[0], wg.shape[1]
    m = DISPATCH_BLOCK
    ff = wg.shape[-1]
    rows = lambda b, be, nv: (jnp.minimum(b, nv[0] - 1), 0)
    out_rows = lambda b, be, nv: (b, 0)
    grid_spec = pltpu.PrefetchScalarGridSpec(
        num_scalar_prefetch=2,
        grid=(total // m,),
        in_specs=[pl.BlockSpec((m, d // 2), rows),
                  pl.BlockSpec((1, d, ff), lambda b, be, nv: (be[b], 0, 0)),
                  pl.BlockSpec((1, d, ff), lambda b, be, nv: (be[b], 0, 0)),
                  pl.BlockSpec((1, ff, d), lambda b, be, nv: (be[b], 0, 0))],
        out_specs=pl.BlockSpec((m, d // 2), out_rows),
        scratch_shapes=[pltpu.VMEM((d, ff), BF16), pltpu.VMEM((d, ff), BF16), pltpu.VMEM((ff, d), BF16)],
    )
    return pl.pallas_call(
        _experts_kernel,
        grid_spec=grid_spec,
        out_shape=jax.ShapeDtypeStruct((total, d // 2), U32),
        compiler_params=_cparams(("arbitrary",)),
        name="experts",
    )(block_e, n_valid, xs, wg, wu, wd)


def _combine_kernel(slot_ref, x1_ref, h2_ref, gw_ref, mod_ref, wg_ref, wu_ref, wd_ref, ys_ref, o_ref,
                    buf, sem, *, tc):
    def row_copy(t, k, slot):
        return pltpu.make_async_copy(ys_ref.at[pl.ds(slot, 1)], buf.at[k, pl.ds(t, 1)], sem)

    def start(t, carry):
        for k in range(TOP_K):
            row_copy(t, k, slot_ref[t * SLOT_STRIDE + k]).start()
        return carry

    def wait(t, carry):
        for k in range(TOP_K):
            row_copy(t, k, 0).wait()
        return carry

    lax.fori_loop(0, tc, start, 0)
    half = wg_ref.shape[0] // 2
    h_lo, h_hi = _unpack_halves(h2_ref[0])
    h_lo, h_hi = h_lo.astype(BF16), h_hi.astype(BF16)
    g = (jnp.dot(h_lo, wg_ref[0:half], preferred_element_type=F32)
         + jnp.dot(h_hi, wg_ref[half:], preferred_element_type=F32))
    u = (jnp.dot(h_lo, wu_ref[0:half], preferred_element_type=F32)
         + jnp.dot(h_hi, wu_ref[half:], preferred_element_type=F32))
    acc = jnp.dot((_silu(g) * u).astype(BF16), wd_ref[...], preferred_element_type=F32)
    lax.fori_loop(0, tc, wait, 0)
    gw = gw_ref[...]
    acc_lo, acc_hi = acc[:, :half], acc[:, half:]
    for k in range(TOP_K):
        y_lo, y_hi = _unpack_halves(buf[k])
        acc_lo = acc_lo + gw[:, k:k + 1] * y_lo
        acc_hi = acc_hi + gw[:, k:k + 1] * y_hi
    gate = mod_ref[0, 5:6, :]
    o_ref[0, :, 0:half] = x1_ref[0, :, 0:half] + gate[:, :half] * acc_lo
    o_ref[0, :, half:] = x1_ref[0, :, half:] + gate[:, half:] * acc_hi


def _combine_call(slots_flat, x1, h2, gw_tok, mod3, wg, wu, wd, ys, tc=128):
    b, s, d = x1.shape
    nt = s // tc
    full = lambda a: pl.BlockSpec(a.shape, lambda bi, i: (0,) * a.ndim)
    return pl.pallas_call(
        functools.partial(_combine_kernel, tc=tc),
        grid=(b, nt),
        in_specs=[pl.BlockSpec((tc * SLOT_STRIDE,), lambda bi, i: (bi * nt + i,), memory_space=pltpu.SMEM),
                  pl.BlockSpec((1, tc, d), lambda bi, i: (bi, i, 0)),
                  pl.BlockSpec((1, tc, d // 2), lambda bi, i: (bi, i, 0)),
                  pl.BlockSpec((tc, 8), lambda bi, i: (bi * nt + i, 0)),
                  pl.BlockSpec((1, 6, d), lambda bi, i: (bi, 0, 0)),
                  full(wg), full(wu), full(wd),
                  pl.BlockSpec(memory_space=pl.ANY)],
        out_specs=pl.BlockSpec((1, tc, d), lambda bi, i: (bi, i, 0)),
        out_shape=jax.ShapeDtypeStruct((b, s, d), F32),
        scratch_shapes=[pltpu.VMEM((TOP_K, tc, d // 2), U32), pltpu.SemaphoreType.DMA(())],
        compiler_params=_cparams(("arbitrary", "arbitrary")),
        name="combine",
    )(slots_flat, x1, h2, gw_tok, mod3, wg, wu, wd, ys)


def _pad_rows(w, rows, at=0):
    out = jnp.zeros((rows, w.shape[1]), w.dtype)
    return out.at[at:at + w.shape[0]].set(w)


def _mixing_stage(p, rwkv_shift, w0_f, w_up_f, w0_b, w_up_b, a0_f, a_up_f, a0_b, a_up_b, g_up,
                  k_k, k_a, r_k, ln_x_w, ln_x_b, q_norm_w, k_norm_w, rel_bias):
    d_rwkv = w0_f.shape[0]
    nhp = d_rwkv // LANES
    mbd = _head_sum_matrix()
    n_shift = rwkv_shift.shape[1]
    sw = jnp.pad(rwkv_shift, ((0, 0), (0, N_SHIFT_BLOCKS * LANES - n_shift))).reshape(3, N_SHIFT_BLOCKS, LANES)
    wupf = _pad_rows(w_up_f, LANES, 0).astype(BF16)
    wupb = _pad_rows(w_up_b, LANES, DECAY_LORA).astype(BF16)
    aupf = _pad_rows(a_up_f, LANES, 0).astype(BF16)
    aupb = _pad_rows(a_up_b, LANES, AAA_LORA).astype(BF16)
    gup = _pad_rows(g_up, 2 * LANES, 0).astype(BF16)
    vecs = jnp.stack([w0_f, w0_b, a0_f, a0_b, k_k, k_a, r_k.reshape(-1)]).reshape(7, nhp, LANES)
    rb, lw = _prep_call(p, sw, wupf, wupb, aupf, aupb, gup, vecs, mbd, nhp)
    y_f, y_b = _wkv_call(rb, lw, nhp)
    y_rwkv = _post_call(y_f, y_b, rb, ln_x_w.reshape(nhp, LANES), ln_x_b.reshape(nhp, LANES), mbd, nhp)
    bias = _na_bias_table(rel_bias)
    qw = jnp.tile(q_norm_w, HEADS_PER_BLOCK).reshape(1, LANES)
    kw = jnp.tile(k_norm_w, HEADS_PER_BLOCK).reshape(1, LANES)
    y_na = _na_call(p, bias, qw, kw, mbd, N_SHIFT_BLOCKS, nhp)
    return y_rwkv, y_na


def _moe_stage(x1, h2, lg_t, mod3, router_bias, w_gate_e, w_up_e, w_down_e, w_gate_s, w_up_s, w_down_s):
    b, s, d = x1.shape
    t = b * s
    m = DISPATCH_BLOCK
    bias2 = jnp.broadcast_to(router_bias.astype(F32)[:, None], (N_EXPERTS, LANES))
    idx_t, gw_t, pos_t, cnt = _route_call(lg_t, bias2)
    counts = cnt[:, 0].astype(I32)
    padded = (counts + m - 1) // m * m
    pad_end = jnp.cumsum(padded)
    pad_start = pad_end - padded
    n_blocks = -(-(t * TOP_K + N_EXPERTS * m) // m)
    n_valid = (pad_end[-1] // m).astype(I32).reshape(1)
    block_row0 = jnp.arange(n_blocks, dtype=I32) * m
    block_e = jnp.minimum(jnp.sum((pad_end[None, :] <= block_row0[:, None]).astype(I32), axis=1),
                          N_EXPERTS - 1).astype(I32)
    assert m % ZERO_ROWS == 0
    expert_ids = jnp.arange(N_EXPERTS, dtype=I32)
    start_of = jnp.sum(jnp.where(idx_t[:, :, None] == expert_ids, pad_start, 0), axis=-1)
    slots = (start_of + pos_t).astype(I32)
    slots_flat = slots.T.reshape(-1)
    gw_tok = gw_t.T
    fill_lo = jnp.concatenate([pad_start + counts, pad_end[-1:]]).astype(I32)
    fill_hi = jnp.concatenate([pad_end, jnp.full((1,), n_blocks * m, I32)]).astype(I32)
    xs = _dispatch_call(slots_flat, fill_lo, fill_hi, h2.reshape(t, d // 2), n_blocks * m)
    ys = _experts_call(block_e, n_valid, xs, w_gate_e, w_up_e, w_down_e)
    return _combine_call(slots_flat, x1, h2, gw_tok, mod3, w_gate_s.astype(BF16), w_up_s.astype(BF16),
                         w_down_s.astype(BF16), ys)


def kernel(x, c, w_ada, b_ada, norm1_w, w_in, rwkv_shift, w0_f, w_up_f, w0_b, w_up_b, a0_f, a_up_f, a0_b,
           a_up_b, g_up, k_k, k_a, r_k, ln_x_w, ln_x_b, q_norm_w, k_norm_w, rel_bias, w_out, norm2_w,
           w_router, router_bias, w_gate_e, w_up_e, w_down_e, w_gate_s, w_up_s, w_down_s):
    bn, sn, d = x.shape
    depth = w_ada.shape[0]
    for l in range(depth):
        c_pad = jnp.pad(c, ((0, 8 - bn % 8 if bn % 8 else 0), (0, 0)))
        mod = _mod_call(c_pad, w_ada[l], b_ada[l].reshape(1, -1))[:bn]
        mod3 = mod.reshape(bn, 6, d)
        rwkv_in = rwkv_shift.shape[-1]
        pad_cols = N_SHIFT_BLOCKS * LANES - rwkv_in
        w_in_r = jnp.concatenate([w_in[l][:, :rwkv_in], jnp.zeros((d, pad_cols), w_in.dtype),
                                  w_in[l][:, rwkv_in:]], axis=1).astype(BF16)
        p = _proj_call(x, mod3, norm1_w[l].reshape(1, d), w_in_r)
        y_rwkv, y_na = _mixing_stage(p, rwkv_shift[l], w0_f[l], w_up_f[l], w0_b[l], w_up_b[l], a0_f[l],
                                     a_up_f[l], a0_b[l], a_up_b[l], g_up[l], k_k[l], k_a[l], r_k[l],
                                     ln_x_w[l], ln_x_b[l], q_norm_w[l], k_norm_w[l], rel_bias[l])
        d_rwkv = y_rwkv.shape[-1]
        w_o = w_out[l].astype(BF16)
        x1, h2, lg_t = _outproj_call(y_rwkv, y_na, x, mod3, norm2_w[l].reshape(1, d), w_o[:d_rwkv], w_o[d_rwkv:],
                                     w_router[l].T)
        x = _moe_stage(x1, h2, lg_t, mod3, router_bias[l], w_gate_e[l], w_up_e[l], w_down_e[l],
                       w_gate_s[l], w_up_s[l], w_down_s[l])
    return x
```

```python
import functools

import jax
import jax.numpy as jnp
import numpy as np
from jax import lax
from jax.experimental import pallas as pl
from jax.experimental.pallas import tpu as pltpu

F32 = jnp.float32
BF16 = jnp.bfloat16
I32 = jnp.int32
HI = lax.Precision.HIGHEST

LANES = 128
HEAD_DIM = 64
HEADS_PER_BLOCK = LANES // HEAD_DIM
GRID_W = 64
WIN_R = 8
WIN_C = 16
DECAY_LORA = 64
AAA_LORA = 64
GATE_LORA = 160
N_EXPERTS = 64
TOP_K = 6
SLOT_STRIDE = 8
N_GROUPS = 8
TOPK_GROUPS = 4
ROUTED_SCALE = 2.5
DISPATCH_BLOCK = 512
NORM_EPS = 1e-6
GN_EPS = 64e-5
L2_EPS = 1e-12
NEG_BIG = -1e30
WKV_CHUNK = 64
VMEM_LIMIT = 56 * 1024 * 1024

NT_DIMS = (((1,), (1,)), ((), ()))
TN_DIMS = (((0,), (0,)), ((), ()))


def _cparams(sem):
    return pltpu.CompilerParams(dimension_semantics=sem, vmem_limit_bytes=VMEM_LIMIT)


def _sigmoid(x):
    return 1.0 / (1.0 + jnp.exp(-x))


def _silu(x):
    return x * _sigmoid(x)


U32 = jnp.uint32
HI_HALF_MASK = 0xFFFF0000


def _pack_halves(x):
    n = x.shape[-1] // 2
    bits = lax.bitcast_convert_type(x.astype(BF16).astype(F32), U32)
    return (bits[:, :n] >> 16) | bits[:, n:]


def _unpack_halves(w):
    lo = lax.bitcast_convert_type(w << 16, F32)
    hi = lax.bitcast_convert_type(w & U32(HI_HALF_MASK), F32)
    return lo, hi


def _head_sum_matrix():
    a = np.arange(LANES) // HEAD_DIM
    return jnp.asarray((a[:, None] == a[None, :]).astype(np.float32))


def _mod_kernel(c_ref, w_ref, b_ref, o_ref):
    c = c_ref[...]
    o_ref[...] = jnp.dot(_silu(c), w_ref[...], preferred_element_type=F32, precision=HI) + b_ref[...]


def _mod_call(c_pad, w_ada, b_ada):
    rows, d = c_pad.shape
    n = w_ada.shape[1]
    tn = 1024
    return pl.pallas_call(
        _mod_kernel,
        grid=(n // tn,),
        in_specs=[pl.BlockSpec((rows, d), lambda j: (0, 0)),
                  pl.BlockSpec((d, tn), lambda j: (0, j)),
                  pl.BlockSpec((1, tn), lambda j: (0, j))],
        out_specs=pl.BlockSpec((rows, tn), lambda j: (0, j)),
        out_shape=jax.ShapeDtypeStruct((rows, n), F32),
        compiler_params=_cparams(("parallel",)),
        name="mod",
    )(c_pad, w_ada, b_ada)


def _proj_kernel(x_ref, mod_ref, nw_ref, w_ref, o_ref, h_sc, *, nq):
    @pl.when(pl.program_id(2) == 0)
    def _():
        x = x_ref[0]
        ms = jnp.mean(x * x, axis=-1, keepdims=True)
        y = x * lax.rsqrt(ms + NORM_EPS) * nw_ref[...]
        h_sc[...] = (y * (1.0 + mod_ref[0, 1:2, :]) + mod_ref[0, 0:1, :]).astype(BF16)

    acc = jnp.dot(h_sc[...], w_ref[...], preferred_element_type=F32)
    for q in range(nq):
        o_ref[0, q] = acc[:, q * LANES:(q + 1) * LANES].astype(o_ref.dtype)


def _proj_call(x, mod3, norm_w, w_in_r, tm=1024, tn=1664):
    b, s, d = x.shape
    n = w_in_r.shape[1]
    nq = tn // LANES
    return pl.pallas_call(
        functools.partial(_proj_kernel, nq=nq),
        grid=(b, s // tm, n // tn),
        in_specs=[pl.BlockSpec((1, tm, d), lambda bi, i, j: (bi, i, 0)),
                  pl.BlockSpec((1, 6, d), lambda bi, i, j: (bi, 0, 0)),
                  pl.BlockSpec((1, d), lambda bi, i, j: (0, 0)),
                  pl.BlockSpec((d, tn), lambda bi, i, j: (0, j))],
        out_specs=pl.BlockSpec((1, nq, tm, LANES), lambda bi, i, j: (bi, j, i, 0)),
        out_shape=jax.ShapeDtypeStruct((b, n // LANES, s, LANES), BF16),
        scratch_shapes=[pltpu.VMEM((tm, d), BF16)],
        compiler_params=_cparams(("parallel", "parallel", "arbitrary")),
        name="proj",
    )(x, mod3, norm_w, w_in_r)


RB_R, RB_V, RB_KK, RB_G, RB_KF, RB_KB, RB_AF, RB_AB, RB_BONUS = range(9)
VEC_W0F, VEC_W0B, VEC_A0F, VEC_A0B, VEC_KK, VEC_KA, VEC_RK = range(7)
N_RKV_BLOCKS = 24
N_SHIFT_BLOCKS = 28


def _softplus(u):
    return jnp.maximum(u, 0.0) + jnp.log(1.0 + jnp.exp(-jnp.abs(u)))


def _prep_kernel(p_ref, pp_ref, pn_ref, sw_ref, wupf_ref, wupb_ref, aupf_ref, aupb_ref, gup_ref,
                 vec_ref, mbd_ref, rb_ref, lw_ref, *, tb, n_t, nhp):
    i = pl.program_id(1)
    row = lax.broadcasted_iota(I32, (tb, LANES), 0)
    has_prev = i > 0
    has_next = i < n_t - 1
    halo = pp_ref.shape[2]

    def shifted(q):
        cur = p_ref[0, q].astype(F32)
        prev_row = jnp.where(has_prev, pp_ref[0, q, halo - 1:halo, :].astype(F32), 0.0)
        next_row = jnp.where(has_next, pn_ref[0, q, 0:1, :].astype(F32), 0.0)
        up = jnp.where(row == 0, prev_row, pltpu.roll(cur, 1, 0))
        dn = jnp.where(row == tb - 1, next_row, pltpu.roll(cur, tb - 1, 0))
        return sw_ref[0, q:q + 1, :] * up + sw_ref[1, q:q + 1, :] * cur + sw_ref[2, q:q + 1, :] * dn

    mbd = mbd_ref[...]
    t_wd = jnp.tanh(shifted(N_RKV_BLOCKS)).astype(BF16)
    z_ad = shifted(N_RKV_BLOCKS + 1).astype(BF16)
    s_gd = jnp.concatenate([_sigmoid(shifted(N_RKV_BLOCKS + 2)),
                            _sigmoid(shifted(N_RKV_BLOCKS + 3))], axis=1).astype(BF16)

    def vec(v, hp):
        return vec_ref[v, hp:hp + 1, :]

    for hp in range(nhp):
        sl = slice(hp * LANES, (hp + 1) * LANES)
        r = shifted(hp)
        k = shifted(nhp + hp)
        v = shifted(2 * nhp + hp)
        g = jnp.dot(s_gd, gup_ref[:, sl], preferred_element_type=F32)
        kk0 = k * vec(VEC_KK, hp)
        ss = jnp.dot(kk0 * kk0, mbd, preferred_element_type=F32, precision=HI)
        kk = kk0 * lax.rsqrt(jnp.maximum(ss, L2_EPS))
        kdirs = []
        for wup_ref, aup_ref, v_w0, v_a0, q_k, q_a, lw_slot in (
                (wupf_ref, aupf_ref, VEC_W0F, VEC_A0F, RB_KF, RB_AF, 0),
                (wupb_ref, aupb_ref, VEC_W0B, VEC_A0B, RB_KB, RB_AB, 1)):
            wl = vec(v_w0, hp) + jnp.dot(t_wd, wup_ref[:, sl], preferred_element_type=F32)
            w_log = -_softplus(-wl) - 0.5
            lw_ref[0, lw_slot * nhp + hp] = -jnp.exp(w_log)
            ag = _sigmoid(vec(v_a0, hp) + jnp.dot(z_ad, aup_ref[:, sl], preferred_element_type=F32))
            kd = k * (1.0 + (ag - 1.0) * vec(VEC_KA, hp))
            rb_ref[0, q_k * nhp + hp] = kd.astype(BF16)
            rb_ref[0, q_a * nhp + hp] = ag.astype(BF16)
            kdirs.append(kd)
        bonus = jnp.dot(r * kdirs[0] * vec(VEC_RK, hp), mbd, preferred_element_type=F32, precision=HI) * v
        rb_ref[0, RB_R * nhp + hp] = r.astype(BF16)
        rb_ref[0, RB_V * nhp + hp] = v.astype(BF16)
        rb_ref[0, RB_KK * nhp + hp] = kk.astype(BF16)
        rb_ref[0, RB_G * nhp + hp] = g.astype(BF16)
        rb_ref[0, RB_BONUS * nhp + hp] = bonus.astype(BF16)


def _prep_call(p, sw, wupf, wupb, aupf, aupb, gup, vecs, mbd, nhp, tb=256):
    b, _, s, _ = p.shape
    n_t = s // tb
    halo = 16
    hb = tb // halo
    full = lambda a: pl.BlockSpec(a.shape, lambda bi, i: (0,) * a.ndim)
    return pl.pallas_call(
        functools.partial(_prep_kernel, tb=tb, n_t=n_t, nhp=nhp),
        grid=(b, n_t),
        in_specs=[pl.BlockSpec((1, N_SHIFT_BLOCKS, tb, LANES), lambda bi, i: (bi, 0, i, 0)),
                  pl.BlockSpec((1, N_SHIFT_BLOCKS, halo, LANES),
                               lambda bi, i: (bi, 0, jnp.maximum(i * hb - 1, 0), 0)),
                  pl.BlockSpec((1, N_SHIFT_BLOCKS, halo, LANES),
                               lambda bi, i: (bi, 0, jnp.minimum((i + 1) * hb, s // halo - 1), 0)),
                  full(sw), full(wupf), full(wupb), full(aupf), full(aupb), full(gup), full(vecs), full(mbd)],
        out_specs=[pl.BlockSpec((1, 9 * nhp, tb, LANES), lambda bi, i: (bi, 0, i, 0)),
                   pl.BlockSpec((1, 2 * nhp, tb, LANES), lambda bi, i: (bi, 0, i, 0))],
        out_shape=[jax.ShapeDtypeStruct((b, 9 * nhp, s, LANES), BF16),
                   jax.ShapeDtypeStruct((b, 2 * nhp, s, LANES), F32)],
        compiler_params=_cparams(("parallel", "parallel")),
        name="prep",
    )(p, p, p, sw, wupf, wupb, aupf, aupb, gup, vecs, mbd)


def _wkv_kernel(r_ref, v_ref, kk_ref, kf_ref, af_ref, lf_ref, rr_ref, vr_ref, kkr_ref, kb_ref, ab_ref, lb_ref,
                yf_ref, yb_ref, s_sc, *, L, nhp):
    @pl.when(pl.program_id(1) == 0)
    def _():
        s_sc[...] = jnp.zeros(s_sc.shape, F32)

    n2 = HEADS_PER_BLOCK * L
    ri = lax.broadcasted_iota(I32, (n2, LANES), 0)
    ci = lax.broadcasted_iota(I32, (n2, LANES), 1)
    head_match = (ri // L) == (ci // HEAD_DIM)
    si = lax.broadcasted_iota(I32, (n2, n2), 0)
    sj = lax.broadcasted_iota(I32, (n2, n2), 1)
    st, ss = si % L, sj % L
    eye = (si == sj).astype(F32)
    strict = {False: st > ss, True: st < ss}
    incl = {False: st >= ss, True: st < ss}

    def stack(x):
        return jnp.where(head_match, jnp.concatenate([x] * HEADS_PER_BLOCK, axis=0), 0.0)

    chains = [(False, hp, r_ref, v_ref, kk_ref, kf_ref, af_ref, lf_ref) for hp in range(nhp)]
    chains += [(True, hp, rr_ref, vr_ref, kkr_ref, kb_ref, ab_ref, lb_ref) for hp in range(nhp)]
    n = len(chains)
    lws = [ch[7][0, ch[1]] for ch in chains]
    t_row = lax.broadcasted_iota(I32, (L, LANES), 0)

    def cumsum_time(x, rev):
        sh = 1
        while sh < L:
            if rev:
                x = x + jnp.where(t_row < L - sh, pltpu.roll(x, L - sh, 0), 0.0)
            else:
                x = x + jnp.where(t_row >= sh, pltpu.roll(x, sh, 0), 0.0)
            sh *= 2
        return x

    c_ins = [cumsum_time(lw, ch[0]) for ch, lw in zip(chains, lws)]
    lhs_l, rhs_l, vs_l, bk_l, pl_l = [], [], [], [], []
    for i, (rev, hp, rr, vr, kkr, kdr, agr, _) in enumerate(chains):
        r = rr[0, hp].astype(F32)
        kk = kkr[0, hp].astype(F32)
        kd = kdr[0, hp].astype(F32)
        ag = agr[0, hp].astype(F32)
        c_in = c_ins[i]
        e_ex = jnp.exp(c_in - lws[i])
        e_inv = jnp.exp(-c_in)
        if rev:
            e_r = e_ex
            p_last = jnp.exp(c_in[0:1])
        else:
            e_r = jnp.exp(c_in)
            p_last = jnp.exp(c_in[L - 1:L])
        b_s = stack(kk * ag * e_inv)
        k_s = stack(kd * e_inv)
        lhs_l.append(jnp.concatenate([stack(-kk * e_ex), stack(r * e_r)], axis=0).astype(BF16))
        rhs_l.append(jnp.concatenate([b_s, k_s], axis=0).astype(BF16))
        bk_l.append(jnp.concatenate([b_s * p_last, k_s * p_last], axis=0).astype(BF16))
        vs_l.append(stack(vr[0, hp].astype(F32)).astype(BF16))
        pl_l.append(p_last)
    g_l = [lax.dot_general(lhs_l[i], rhs_l[i], NT_DIMS, preferred_element_type=F32) for i in range(n)]
    s_old = [s_sc[i] for i in range(n)]
    ls_l = [lax.dot_general(lhs_l[i], s_old[i].astype(BF16), NT_DIMS, preferred_element_type=F32)
            for i in range(n)]
    a_ab = [jnp.where(strict[ch[0]], g[0:n2, 0:n2], 0.0) for ch, g in zip(chains, g_l)]
    a_ak = [jnp.where(strict[ch[0]], g[0:n2, n2:2 * n2], 0.0).astype(BF16) for ch, g in zip(chains, g_l)]
    a_r = [jnp.concatenate([jnp.where(incl[ch[0]], g[n2:2 * n2, 0:n2], 0.0),
                            jnp.where(incl[ch[0]], g[n2:2 * n2, n2:2 * n2], 0.0)], axis=1).astype(BF16)
           for ch, g in zip(chains, g_l)]
    x_l = [ls_l[i][0:n2] + jnp.dot(a_ak[i], vs_l[i], preferred_element_type=F32) for i in range(n)]
    t_inv = [eye + a for a in a_ab]
    nb = [a.astype(BF16) for a in a_ab]
    for _ in range(L.bit_length() - 2):
        nb = [jnp.dot(m, m, preferred_element_type=F32).astype(BF16) for m in nb]
        t_inv = [t + jnp.dot(m, t.astype(BF16), preferred_element_type=F32) for m, t in zip(nb, t_inv)]
    u_l = [jnp.dot(t_inv[i].astype(BF16), x_l[i].astype(BF16), preferred_element_type=F32) for i in range(n)]
    uv_l = [jnp.concatenate([jnp.where(head_match, u_l[i], 0.0).astype(BF16), vs_l[i]], axis=0)
            for i in range(n)]
    y_l = [ls_l[i][n2:2 * n2] + jnp.dot(a_r[i], uv_l[i], preferred_element_type=F32) for i in range(n)]
    upd_l = [lax.dot_general(uv_l[i], bk_l[i], TN_DIMS, preferred_element_type=F32) for i in range(n)]
    for i, (rev, hp, *_) in enumerate(chains):
        y = y_l[i]
        out = y[0:L]
        for h in range(1, HEADS_PER_BLOCK):
            out = out + y[h * L:(h + 1) * L]
        (yb_ref if rev else yf_ref)[0, hp] = out
        s_sc[i] = s_old[i] * pl_l[i] + upd_l[i]


def _wkv_call(rb, lw, nhp, L=WKV_CHUNK):
    b, _, s, _ = rb.shape
    nc = s // L
    fwd = lambda q: pl.BlockSpec((1, nhp, L, LANES), lambda bi, c: (bi, q, c, 0))
    bwd = lambda q: pl.BlockSpec((1, nhp, L, LANES), lambda bi, c: (bi, q, nc - 1 - c, 0))
    y_shape = jax.ShapeDtypeStruct((b, nhp, s, LANES), F32)
    return pl.pallas_call(
        functools.partial(_wkv_kernel, L=L, nhp=nhp),
        grid=(b, nc),
        in_specs=[fwd(RB_R), fwd(RB_V), fwd(RB_KK), fwd(RB_KF), fwd(RB_AF), fwd(0),
                  bwd(RB_R), bwd(RB_V), bwd(RB_KK), bwd(RB_KB), bwd(RB_AB), bwd(1)],
        out_specs=[fwd(0), bwd(0)],
        out_shape=[y_shape, y_shape],
        scratch_shapes=[pltpu.VMEM((2 * nhp, LANES, LANES), F32)],
        compiler_params=_cparams(("parallel", "arbitrary")),
        name="wkv",
    )(rb, rb, rb, rb, rb, lw, rb, rb, rb, rb, rb, lw)


def _post_kernel(yf_ref, yb_ref, bonus_ref, g_ref, lnw_ref, lnb_ref, mbd_ref, o_ref, *, nhp):
    mbd = mbd_ref[...]
    inv_n = 1.0 / HEAD_DIM
    for hp in range(nhp):
        y = yf_ref[0, hp] + yb_ref[0, hp]
        mu = jnp.dot(y, mbd, preferred_element_type=F32, precision=HI) * inv_n
        d = y - mu
        var = jnp.dot(d * d, mbd, preferred_element_type=F32, precision=HI) * inv_n
        yn = d * lax.rsqrt(var + GN_EPS) * lnw_ref[hp:hp + 1, :] + lnb_ref[hp:hp + 1, :]
        out = (yn + bonus_ref[0, hp].astype(F32)) * g_ref[0, hp].astype(F32)
        o_ref[0, :, hp * LANES:(hp + 1) * LANES] = out.astype(o_ref.dtype)


def _post_call(yf, yb, rb, lnw, lnb, mbd, nhp, tb=256):
    b, _, s, _ = yf.shape
    yblk = pl.BlockSpec((1, nhp, tb, LANES), lambda bi, i: (bi, 0, i, 0))
    rblk = lambda q: pl.BlockSpec((1, nhp, tb, LANES), lambda bi, i: (bi, q, i, 0))
    full = lambda a: pl.BlockSpec(a.shape, lambda bi, i: (0,) * a.ndim)
    return pl.pallas_call(
        functools.partial(_post_kernel, nhp=nhp),
        grid=(b, s // tb),
        in_specs=[yblk, yblk, rblk(RB_BONUS), rblk(RB_G), full(lnw), full(lnb), full(mbd)],
        out_specs=pl.BlockSpec((1, tb, nhp * LANES), lambda bi, i: (bi, i, 0)),
        out_shape=jax.ShapeDtypeStruct((b, s, nhp * LANES), BF16),
        compiler_params=_cparams(("parallel", "parallel")),
        name="post",
    )(yf, yb, rb, rb, lnw, lnb, mbd)


def _na_kernel(q_ref, k_ref, v_ref, bias_ref, qw_ref, kw_ref, mbd_ref, o_ref, kn_sc, *, rq, rows):
    rbi = pl.program_id(2)
    mbd = mbd_ref[...]
    inv_n = 1.0 / HEAD_DIM
    span = WIN_R * GRID_W
    s_len = kn_sc.shape[0]

    @pl.when(rbi == 0)
    def _():
        def body(c, carry):
            off = pl.multiple_of(c * span, span)
            k = k_ref[0, 0, pl.ds(off, span), :].astype(F32)
            ms = jnp.dot(k * k, mbd, preferred_element_type=F32, precision=HI) * inv_n
            kn_sc[pl.ds(off, span), :] = (k * lax.rsqrt(ms + NORM_EPS) * kw_ref[...]).astype(BF16)
            return carry
        lax.fori_loop(0, s_len // span, body, 0)

    ri = lax.broadcasted_iota(I32, (HEADS_PER_BLOCK * GRID_W, LANES), 0)
    ci = lax.broadcasted_iota(I32, (HEADS_PER_BLOCK * GRID_W, LANES), 1)
    head_match = (ri // GRID_W) == (ci // HEAD_DIM)
    lane = lax.broadcasted_iota(I32, (GRID_W, LANES), 1)
    scale = HEAD_DIM ** -0.5
    q_all = q_ref[0, 0].astype(F32)
    ms = jnp.dot(q_all * q_all, mbd, preferred_element_type=F32, precision=HI) * inv_n
    qn_all = q_all * lax.rsqrt(ms + NORM_EPS) * qw_ref[...] * scale
    starts, offs = [], []
    for qi in range(rq):
        i = rbi * rq + qi
        rs = jnp.clip(i - WIN_R // 2, 0, rows - WIN_R)
        offs.append(rs - i + WIN_R - 1)
        starts.append(pl.multiple_of(rs * GRID_W, GRID_W))
    qs_l = [jnp.where(head_match, jnp.concatenate([qn_all[qi * GRID_W:(qi + 1) * GRID_W]] * HEADS_PER_BLOCK,
                                                  axis=0), 0.0).astype(BF16) for qi in range(rq)]
    s_l = [lax.dot_general(qs_l[qi], kn_sc[pl.ds(starts[qi], span), :], NT_DIMS, preferred_element_type=F32)
           + bias_ref[offs[qi], 0] for qi in range(rq)]
    p_l, l_l = [], []
    for s in s_l:
        p = jnp.exp(s - jnp.max(s, axis=-1, keepdims=True))
        p_l.append(p.astype(BF16))
        l_l.append(jnp.sum(p, axis=-1, keepdims=True))
    o_l = [jnp.dot(p_l[qi], v_ref[0, 0, pl.ds(starts[qi], span), :], preferred_element_type=F32) / l_l[qi]
           for qi in range(rq)]
    for qi, o in enumerate(o_l):
        out = o[0:GRID_W]
        for h in range(1, HEADS_PER_BLOCK):
            out = jnp.where(lane // HEAD_DIM == h, o[h * GRID_W:(h + 1) * GRID_W], out)
        o_ref[0, qi * GRID_W:(qi + 1) * GRID_W, :] = out.astype(o_ref.dtype)


def _na_call(p, bias, qw, kw, mbd, q_blk0, nhp, rq=16):
    b, _, s, _ = p.shape
    rows = s // GRID_W
    rq = min(rq, rows)
    full = lambda a: pl.BlockSpec(a.shape, lambda bi, hp, r: (0,) * a.ndim)
    return pl.pallas_call(
        functools.partial(_na_kernel, rq=rq, rows=rows),
        grid=(b, nhp, rows // rq),
        in_specs=[pl.BlockSpec((1, 1, rq * GRID_W, LANES), lambda bi, hp, r: (bi, q_blk0 + hp, r, 0)),
                  pl.BlockSpec((1, 1, s, LANES), lambda bi, hp, r: (bi, q_blk0 + nhp + hp, 0, 0)),
                  pl.BlockSpec((1, 1, s, LANES), lambda bi, hp, r: (bi, q_blk0 + 2 * nhp + hp, 0, 0)),
                  pl.BlockSpec((WIN_R, 1, HEADS_PER_BLOCK * GRID_W, WIN_R * GRID_W),
                               lambda bi, hp, r: (0, hp, 0, 0)),
                  full(qw), full(kw), full(mbd)],
        out_specs=pl.BlockSpec((1, rq * GRID_W, LANES), lambda bi, hp, r: (bi, r, hp)),
        out_shape=jax.ShapeDtypeStruct((b, s, nhp * LANES), BF16),
        scratch_shapes=[pltpu.VMEM((s, LANES), BF16)],
        compiler_params=_cparams(("parallel", "parallel", "arbitrary")),
        name="na",
    )(p, p, p, bias, qw, kw, mbd)


def _na_bias_table(rel_bias):
    h = rel_bias.shape[0]
    qc = np.arange(GRID_W)
    kc = np.arange(GRID_W)
    win0 = np.clip(qc - WIN_C // 2, 0, GRID_W - WIN_C)
    valid = (kc[None, :] >= win0[:, None]) & (kc[None, :] < win0[:, None] + WIN_C)
    coff = np.clip(kc[None, :] - qc[:, None] + WIN_C - 1, 0, 2 * WIN_C - 2)
    pick = jnp.asarray((coff[None] == np.arange(2 * WIN_C - 1)[:, None, None]).astype(np.float32))
    t = jnp.einsum('hrc,cqk->hrqk', rel_bias.astype(F32), pick, precision=HI)
    t = jnp.stack([t[:, o:o + WIN_R] for o in range(WIN_R)], axis=1)
    t = jnp.where(valid[None, None, None], t, NEG_BIG)
    t = jnp.transpose(t, (1, 0, 3, 2, 4))
    return t.reshape(WIN_R, h // HEADS_PER_BLOCK, HEADS_PER_BLOCK * GRID_W, WIN_R * GRID_W)


def _outproj_kernel(yr_ref, yn_ref, x_ref, mod_ref, nw_ref, w1_ref, w2_ref, wr_ref, x1_ref, h2_ref, lg_ref):
    tm = x_ref.shape[1]
    sub = LANES
    tiles = [slice(i * sub, (i + 1) * sub) for i in range(tm // sub)]
    accs = [jnp.dot(yr_ref[0, r], w1_ref[...], preferred_element_type=F32)
            + jnp.dot(yn_ref[0, r], w2_ref[...], preferred_element_type=F32) for r in tiles]
    h2s = []
    for r, acc in zip(tiles, accs):
        x1 = x_ref[0, r] + mod_ref[0, 2:3, :] * acc
        x1_ref[0, r] = x1
        ms = jnp.mean(x1 * x1, axis=-1, keepdims=True)
        y = x1 * lax.rsqrt(ms + NORM_EPS) * nw_ref[...]
        h2 = y * (1.0 + mod_ref[0, 4:5, :]) + mod_ref[0, 3:4, :]
        h2_ref[0, r] = _pack_halves(h2)
        h2s.append(h2)
    for r, h2 in zip(tiles, h2s):
        lg_ref[:, r] = lax.dot_general(wr_ref[...], h2, NT_DIMS, preferred_element_type=F32, precision=HI)


def _outproj_call(yr, yn, x, mod3, norm_w, w1, w2, wr_t, tm=512):
    b, s, d = x.shape
    dh = yr.shape[-1]
    ne = wr_t.shape[0]
    nt = s // tm
    full = lambda a: pl.BlockSpec(a.shape, lambda bi, i: (0,) * a.ndim)
    return pl.pallas_call(
        _outproj_kernel,
        grid=(b, nt),
        in_specs=[pl.BlockSpec((1, tm, dh), lambda bi, i: (bi, i, 0)),
                  pl.BlockSpec((1, tm, dh), lambda bi, i: (bi, i, 0)),
                  pl.BlockSpec((1, tm, d), lambda bi, i: (bi, i, 0)),
                  pl.BlockSpec((1, 6, d), lambda bi, i: (bi, 0, 0)),
                  full(norm_w), full(w1), full(w2), full(wr_t)],
        out_specs=[pl.BlockSpec((1, tm, d), lambda bi, i: (bi, i, 0)),
                   pl.BlockSpec((1, tm, d // 2), lambda bi, i: (bi, i, 0)),
                   pl.BlockSpec((ne, tm), lambda bi, i: (0, bi * nt + i))],
        out_shape=[jax.ShapeDtypeStruct((b, s, d), F32),
                   jax.ShapeDtypeStruct((b, s, d // 2), U32),
                   jax.ShapeDtypeStruct((ne, b * s), F32)],
        compiler_params=_cparams(("parallel", "parallel")),
        name="outproj",
    )(yr, yn, x, mod3, norm_w, w1, w2, wr_t)


def _first_argmax(x, idx, n):
    m = jnp.max(x, axis=0, keepdims=True)
    a = jnp.min(jnp.where(x == m, idx, n), axis=0, keepdims=True)
    return m, a


def _route_kernel(lg_ref, bias_ref, idx_ref, gw_ref, pos_ref, cnt_ref, *, tr):
    @pl.when(pl.program_id(0) == 0)
    def _():
        cnt_ref[...] = jnp.zeros(cnt_ref.shape, F32)

    gsz = N_EXPERTS // N_GROUPS
    scores = _sigmoid(lg_ref[...])
    biased = scores + bias_ref[:, 0:1]
    ig = lax.broadcasted_iota(I32, (gsz, tr), 0)
    grp_rows = []
    for g in range(N_GROUPS):
        blk = biased[g * gsz:(g + 1) * gsz]
        m1, a1 = _first_argmax(blk, ig, gsz)
        m2 = jnp.max(jnp.where(ig == a1, -jnp.inf, blk), axis=0, keepdims=True)
        grp_rows.append(m1 + m2)
    grp = jnp.concatenate(grp_rows, axis=0)
    ign = lax.broadcasted_iota(I32, (N_GROUPS, tr), 0)
    sel = jnp.zeros((N_GROUPS, tr), jnp.bool_)
    for _ in range(TOPK_GROUPS):
        _, a = _first_argmax(grp, ign, N_GROUPS)
        hit = ign == a
        sel = jnp.logical_or(sel, hit)
        grp = jnp.where(hit, -jnp.inf, grp)
    masked = jnp.concatenate(
        [jnp.where(sel[g:g + 1], biased[g * gsz:(g + 1) * gsz], -jnp.inf) for g in range(N_GROUPS)], axis=0)
    ie = lax.broadcasted_iota(I32, (N_EXPERTS, tr), 0)
    picks, pick_scores = [], []
    onehot = jnp.zeros((N_EXPERTS, tr), F32)
    for _ in range(TOP_K):
        _, a = _first_argmax(masked, ie, N_EXPERTS)
        hit = ie == a
        picks.append(a)
        pick_scores.append(jnp.sum(jnp.where(hit, scores, 0.0), axis=0, keepdims=True))
        onehot = onehot + hit.astype(F32)
        masked = jnp.where(hit, -jnp.inf, masked)
    total = pick_scores[0]
    for sc in pick_scores[1:]:
        total = total + sc
    t0 = lax.broadcasted_iota(I32, (tr, tr), 0)
    t1 = lax.broadcasted_iota(I32, (tr, tr), 1)
    before = (t0 < t1).astype(BF16)
    rank = jnp.dot(onehot.astype(BF16), before, preferred_element_type=F32) + cnt_ref[:, 0:1]
    zero_i = jnp.zeros((1, tr), I32)
    zero_f = jnp.zeros((1, tr), F32)
    for k in range(8):
        if k < TOP_K:
            idx_ref[k:k + 1, :] = picks[k]
            gw_ref[k:k + 1, :] = pick_scores[k] / total * ROUTED_SCALE
            pos = jnp.sum(jnp.where(ie == picks[k], rank, 0.0), axis=0, keepdims=True)
            pos_ref[k:k + 1, :] = pos.astype(I32)
        else:
            idx_ref[k:k + 1, :] = zero_i
            gw_ref[k:k + 1, :] = zero_f
            pos_ref[k:k + 1, :] = zero_i
    cnt_ref[...] = cnt_ref[...] + jnp.sum(onehot, axis=1, keepdims=True)


def _route_call(lg_t, bias2, tr=512):
    ne, t = lg_t.shape
    tr = min(tr, t)
    tok = pl.BlockSpec((8, tr), lambda i: (0, i))
    return pl.pallas_call(
        functools.partial(_route_kernel, tr=tr),
        grid=(t // tr,),
        in_specs=[pl.BlockSpec((ne, tr), lambda i: (0, i)),
                  pl.BlockSpec(bias2.shape, lambda i: (0, 0))],
        out_specs=[tok, tok, tok, pl.BlockSpec((ne, LANES), lambda i: (0, 0))],
        out_shape=[jax.ShapeDtypeStruct((8, t), I32), jax.ShapeDtypeStruct((8, t), F32),
                   jax.ShapeDtypeStruct((8, t), I32), jax.ShapeDtypeStruct((ne, LANES), F32)],
        compiler_params=_cparams(("arbitrary",)),
        name="route",
    )(lg_t, bias2)


FILL_CHUNKS = (1, 8, 64)
ZERO_ROWS = FILL_CHUNKS[-1]


def _dispatch_kernel(slot_ref, fill_lo_ref, fill_hi_ref, h_ref, xs_ref, zrow, sem, *, td):
    def row_copy(t, slot):
        return pltpu.make_async_copy(h_ref.at[pl.ds(t, 1)], xs_ref.at[pl.ds(slot, 1)], sem)

    def start(t, carry):
        for k in range(TOP_K):
            row_copy(t, slot_ref[t * SLOT_STRIDE + k]).start()
        return carry

    def wait(t, carry):
        for k in range(TOP_K):
            row_copy(t, 0).wait()
        return carry

    lax.fori_loop(0, td, start, 0)
    lax.fori_loop(0, td, wait, 0)

    @pl.when(pl.program_id(0) == pl.num_programs(0) - 1)
    def _():
        zrow[...] = jnp.zeros(zrow.shape, zrow.dtype)

        def zero_copy(row, n):
            if n > 1:
                row = pl.multiple_of(row, FILL_CHUNKS[1])
            return pltpu.make_async_copy(zrow.at[pl.ds(0, n)], xs_ref.at[pl.ds(row, n)], sem)

        def for_each_chunk(r, fn):
            lo, hi = fill_lo_ref[r], fill_hi_ref[r]
            for ci, n in enumerate(FILL_CHUNKS):
                if ci + 1 < len(FILL_CHUNKS):
                    count = jnp.minimum(((-lo) % FILL_CHUNKS[ci + 1]) // n, (hi - lo) // n)
                else:
                    count = (hi - lo) // n
                fn(lo, n, count)
                lo = lo + count * n

        def start_range(r, carry):
            def go(lo, n, count):
                lax.fori_loop(0, count, lambda i, c: (zero_copy(lo + i * n, n).start(), c)[1], 0)
            for_each_chunk(r, go)
            return carry

        def wait_range(r, carry):
            def go(lo, n, count):
                lax.fori_loop(0, count, lambda i, c: (zero_copy(0, n).wait(), c)[1], 0)
            for_each_chunk(r, go)
            return carry

        n_ranges = fill_lo_ref.shape[0]
        lax.fori_loop(0, n_ranges, start_range, 0)
        lax.fori_loop(0, n_ranges, wait_range, 0)


def _dispatch_call(slots_flat, fill_lo, fill_hi, h2, n_rows, td=256):
    t, dh = h2.shape
    smem = pl.BlockSpec(memory_space=pltpu.SMEM)
    return pl.pallas_call(
        functools.partial(_dispatch_kernel, td=td),
        grid=(t // td,),
        in_specs=[pl.BlockSpec((td * SLOT_STRIDE,), lambda i: (i,), memory_space=pltpu.SMEM),
                  smem, smem,
                  pl.BlockSpec((td, dh), lambda i: (i, 0))],
        out_specs=pl.BlockSpec(memory_space=pl.ANY),
        out_shape=jax.ShapeDtypeStruct((n_rows, dh), h2.dtype),
        scratch_shapes=[pltpu.VMEM((ZERO_ROWS, dh), h2.dtype), pltpu.SemaphoreType.DMA(())],
        compiler_params=_cparams(("arbitrary",)),
        name="dispatch",
    )(slots_flat, fill_lo, fill_hi, h2)


def _experts_kernel(be_ref, first_ref, nxt_ref, par_ref, rows_ref, nv_ref, xs_ref, wg_hbm, wu_hbm, wd_hbm, ys_ref,
                    wg_buf, wu_buf, wd_buf, wg_sc, wu_sc, wd_sc, sem):
    b = pl.program_id(0)
    m = xs_ref.shape[0]
    half = wg_sc.shape[0] // 2

    def weight_copies(e, slot):
        return [pltpu.make_async_copy(src.at[e], dst.at[slot], sem.at[slot, i])
                for i, (src, dst) in enumerate(((wg_hbm, wg_buf), (wu_hbm, wu_buf), (wd_hbm, wd_buf)))]

    def compute(nrows):
        x_lo, x_hi = _unpack_halves(xs_ref[0:nrows])
        x_lo, x_hi = x_lo.astype(BF16), x_hi.astype(BF16)
        g = (jnp.dot(x_lo, wg_sc[0:half], preferred_element_type=F32)
             + jnp.dot(x_hi, wg_sc[half:], preferred_element_type=F32))
        u = (jnp.dot(x_lo, wu_sc[0:half], preferred_element_type=F32)
             + jnp.dot(x_hi, wu_sc[half:], preferred_element_type=F32))
        a = (_silu(g) * u).astype(BF16)
        ys_ref[0:nrows] = _pack_halves(jnp.dot(a, wd_sc[...], preferred_element_type=F32))

    @pl.when(b < nv_ref[0])
    def _():
        @pl.when(b == 0)
        def _():
            for c in weight_copies(be_ref[0], par_ref[0]):
                c.start()

        @pl.when(first_ref[b] == 1)
        def _():
            slot = par_ref[b]
            for c in weight_copies(be_ref[b], slot):
                c.wait()

            @pl.when(nxt_ref[b] >= 0)
            def _():
                for c in weight_copies(nxt_ref[b], 1 - slot):
                    c.start()

            wg_sc[...] = wg_buf[slot].astype(BF16)
            wu_sc[...] = wu_buf[slot].astype(BF16)
            wd_sc[...] = wd_buf[slot].astype(BF16)

        @pl.when(rows_ref[b] > m // 2)
        def _():
            compute(m)

        @pl.when(rows_ref[b] <= m // 2)
        def _():
            compute(m // 2)
            ys_ref[m // 2:] = jnp.zeros((m - m // 2, ys_ref.shape[1]), ys_ref.dtype)

    @pl.when(b >= nv_ref[0])
    def _():
        ys_ref[...] = jnp.zeros(ys_ref.shape, ys_ref.dtype)


def _experts_call(block_e, first, nxt, par, rows_valid, n_valid, xs, wg, wu, wd):
    total, d = xs.shape[0], wg.shape[1]
    m = DISPATCH_BLOCK
    ff = wg.shape[-1]
    rows = lambda b, *s: (jnp.minimum(b, s[-1][0] - 1), 0)
    out_rows = lambda b, *s: (b, 0)
    hbm = pl.BlockSpec(memory_space=pl.ANY)
    grid_spec = pltpu.PrefetchScalarGridSpec(
        num_scalar_prefetch=6,
        grid=(total // m,),
        in_specs=[pl.BlockSpec((m, d // 2), rows), hbm, hbm, hbm],
        out_specs=pl.BlockSpec((m, d // 2), out_rows),
        scratch_shapes=[pltpu.VMEM((2, d, ff), F32), pltpu.VMEM((2, d, ff), F32), pltpu.VMEM((2, ff, d), F32),
                        pltpu.VMEM((d, ff), BF16), pltpu.VMEM((d, ff), BF16), pltpu.VMEM((ff, d), BF16),
                        pltpu.SemaphoreType.DMA((2, 3))],
    )
    return pl.pallas_call(
        _experts_kernel,
        grid_spec=grid_spec,
        out_shape=jax.ShapeDtypeStruct((total, d // 2), U32),
        compiler_params=_cparams(("arbitrary",)),
        name="experts",
    )(block_e, first, nxt, par, rows_valid, n_valid, xs, wg, wu, wd)


def _combine_kernel(slot_ref, x1_ref, h2_ref, gw_ref, mod_ref, wg_ref, wu_ref, wd_ref, ys_ref, o_ref,
                    buf, sem, *, tc):
    def row_copy(t, k, slot):
        return pltpu.make_async_copy(ys_ref.at[pl.ds(slot, 1)], buf.at[k, pl.ds(t, 1)], sem)

    def start(t, carry):
        for k in range(TOP_K):
            row_copy(t, k, slot_ref[t * SLOT_STRIDE + k]).start()
        return carry

    def wait(t, carry):
        for k in range(TOP_K):
            row_copy(t, k, 0).wait()
        return carry

    lax.fori_loop(0, tc, start, 0)
    half = wg_ref.shape[0] // 2
    h_lo, h_hi = _unpack_halves(h2_ref[0])
    h_lo, h_hi = h_lo.astype(BF16), h_hi.astype(BF16)
    g = (jnp.dot(h_lo, wg_ref[0:half], preferred_element_type=F32)
         + jnp.dot(h_hi, wg_ref[half:], preferred_element_type=F32))
    u = (jnp.dot(h_lo, wu_ref[0:half], preferred_element_type=F32)
         + jnp.dot(h_hi, wu_ref[half:], preferred_element_type=F32))
    acc = jnp.dot((_silu(g) * u).astype(BF16), wd_ref[...], preferred_element_type=F32)
    lax.fori_loop(0, tc, wait, 0)
    gw = gw_ref[...]
    acc_lo, acc_hi = acc[:, :half], acc[:, half:]
    for k in range(TOP_K):
        y_lo, y_hi = _unpack_halves(buf[k])
        acc_lo = acc_lo + gw[:, k:k + 1] * y_lo
        acc_hi = acc_hi + gw[:, k:k + 1] * y_hi
    gate = mod_ref[0, 5:6, :]
    o_ref[0, :, 0:half] = x1_ref[0, :, 0:half] + gate[:, :half] * acc_lo
    o_ref[0, :, half:] = x1_ref[0, :, half:] + gate[:, half:] * acc_hi


def _combine_call(slots_flat, x1, h2, gw_tok, mod3, wg, wu, wd, ys, tc=256):
    b, s, d = x1.shape
    nt = s // tc
    full = lambda a: pl.BlockSpec(a.shape, lambda bi, i: (0,) * a.ndim)
    return pl.pallas_call(
        functools.partial(_combine_kernel, tc=tc),
        grid=(b, nt),
        in_specs=[pl.BlockSpec((tc * SLOT_STRIDE,), lambda bi, i: (bi * nt + i,), memory_space=pltpu.SMEM),
                  pl.BlockSpec((1, tc, d), lambda bi, i: (bi, i, 0)),
                  pl.BlockSpec((1, tc, d // 2), lambda bi, i: (bi, i, 0)),
                  pl.BlockSpec((tc, 8), lambda bi, i: (bi * nt + i, 0)),
                  pl.BlockSpec((1, 6, d), lambda bi, i: (bi, 0, 0)),
                  full(wg), full(wu), full(wd),
                  pl.BlockSpec(memory_space=pl.ANY)],
        out_specs=pl.BlockSpec((1, tc, d), lambda bi, i: (bi, i, 0)),
        out_shape=jax.ShapeDtypeStruct((b, s, d), F32),
        scratch_shapes=[pltpu.VMEM((TOP_K, tc, d // 2), U32), pltpu.SemaphoreType.DMA(())],
        compiler_params=_cparams(("arbitrary", "arbitrary")),
        name="combine",
    )(slots_flat, x1, h2, gw_tok, mod3, wg, wu, wd, ys)


def _pad_rows(w, rows, at=0):
    out = jnp.zeros((rows, w.shape[1]), w.dtype)
    return out.at[at:at + w.shape[0]].set(w)


def _mixing_stage(p, rwkv_shift, w0_f, w_up_f, w0_b, w_up_b, a0_f, a_up_f, a0_b, a_up_b, g_up,
                  k_k, k_a, r_k, ln_x_w, ln_x_b, q_norm_w, k_norm_w, rel_bias):
    d_rwkv = w0_f.shape[0]
    nhp = d_rwkv // LANES
    mbd = _head_sum_matrix()
    n_shift = rwkv_shift.shape[1]
    sw = jnp.pad(rwkv_shift, ((0, 0), (0, N_SHIFT_BLOCKS * LANES - n_shift))).reshape(3, N_SHIFT_BLOCKS, LANES)
    wupf = _pad_rows(w_up_f, LANES, 0).astype(BF16)
    wupb = _pad_rows(w_up_b, LANES, DECAY_LORA).astype(BF16)
    aupf = _pad_rows(a_up_f, LANES, 0).astype(BF16)
    aupb = _pad_rows(a_up_b, LANES, AAA_LORA).astype(BF16)
    gup = _pad_rows(g_up, 2 * LANES, 0).astype(BF16)
    vecs = jnp.stack([w0_f, w0_b, a0_f, a0_b, k_k, k_a, r_k.reshape(-1)]).reshape(7, nhp, LANES)
    rb, lw = _prep_call(p, sw, wupf, wupb, aupf, aupb, gup, vecs, mbd, nhp)
    y_f, y_b = _wkv_call(rb, lw, nhp)
    y_rwkv = _post_call(y_f, y_b, rb, ln_x_w.reshape(nhp, LANES), ln_x_b.reshape(nhp, LANES), mbd, nhp)
    bias = _na_bias_table(rel_bias)
    qw = jnp.tile(q_norm_w, HEADS_PER_BLOCK).reshape(1, LANES)
    kw = jnp.tile(k_norm_w, HEADS_PER_BLOCK).reshape(1, LANES)
    y_na = _na_call(p, bias, qw, kw, mbd, N_SHIFT_BLOCKS, nhp)
    return y_rwkv, y_na


def _moe_stage(x1, h2, lg_t, mod3, router_bias, w_gate_e, w_up_e, w_down_e, w_gate_s, w_up_s, w_down_s):
    b, s, d = x1.shape
    t = b * s
    m = DISPATCH_BLOCK
    bias2 = jnp.broadcast_to(router_bias.astype(F32)[:, None], (N_EXPERTS, LANES))
    idx_t, gw_t, pos_t, cnt = _route_call(lg_t, bias2)
    counts = cnt[:, 0].astype(I32)
    padded = (counts + m - 1) // m * m
    pad_end = jnp.cumsum(padded)
    pad_start = pad_end - padded
    n_blocks = -(-(t * TOP_K + N_EXPERTS * m) // m)
    n_valid = (pad_end[-1] // m).astype(I32).reshape(1)
    block_row0 = jnp.arange(n_blocks, dtype=I32) * m
    block_e = jnp.minimum(jnp.sum((pad_end[None, :] <= block_row0[:, None]).astype(I32), axis=1),
                          N_EXPERTS - 1).astype(I32)
    assert m % ZERO_ROWS == 0
    expert_ids = jnp.arange(N_EXPERTS, dtype=I32)
    start_of = jnp.sum(jnp.where(idx_t[:, :, None] == expert_ids, pad_start, 0), axis=-1)
    slots = (start_of + pos_t).astype(I32)
    slots_flat = slots.T.reshape(-1)
    gw_tok = gw_t.T
    fill_lo = jnp.concatenate([pad_start + counts, pad_end[-1:]]).astype(I32)
    fill_hi = jnp.concatenate([pad_end, jnp.full((1,), n_blocks * m, I32)]).astype(I32)
    xs = _dispatch_call(slots_flat, fill_lo, fill_hi, h2.reshape(t, d // 2), n_blocks * m)
    row_end = (pad_start + counts).astype(I32)
    nonempty = counts > 0
    first = jnp.logical_and(block_row0 == pad_start[block_e], block_row0 < pad_end[-1]).astype(I32)
    ordinal = jnp.cumsum(nonempty.astype(I32)) - 1
    par = (ordinal[block_e] & 1).astype(I32)
    cand = jnp.where(nonempty, expert_ids, N_EXPERTS)
    later = jnp.concatenate([lax.cummin(cand[::-1])[::-1][1:], jnp.full((1,), N_EXPERTS, I32)])
    nxt = jnp.where(later < N_EXPERTS, later, -1)[block_e].astype(I32)
    rows_valid = jnp.clip(row_end[block_e] - block_row0, 0, m).astype(I32)
    ys = _experts_call(block_e, first, nxt, par, rows_valid, n_valid, xs, w_gate_e, w_up_e, w_down_e)
    return _combine_call(slots_flat, x1, h2, gw_tok, mod3, w_gate_s.astype(BF16), w_up_s.astype(BF16),
                         w_down_s.astype(BF16), ys)


def kernel(x, c, w_ada, b_ada, norm1_w, w_in, rwkv_shift, w0_f, w_up_f, w0_b, w_up_b, a0_f, a_up_f, a0_b,
           a_up_b, g_up, k_k, k_a, r_k, ln_x_w, ln_x_b, q_norm_w, k_norm_w, rel_bias, w_out, norm2_w,
           w_router, router_bias, w_gate_e, w_up_e, w_down_e, w_gate_s, w_up_s, w_down_s):
    bn, sn, d = x.shape
    depth = w_ada.shape[0]
    for l in range(depth):
        c_pad = jnp.pad(c, ((0, 8 - bn % 8 if bn % 8 else 0), (0, 0)))
        mod = _mod_call(c_pad, w_ada[l], b_ada[l].reshape(1, -1))[:bn]
        mod3 = mod.reshape(bn, 6, d)
        rwkv_in = rwkv_shift.shape[-1]
        pad_cols = N_SHIFT_BLOCKS * LANES - rwkv_in
        w_in_r = jnp.concatenate([w_in[l][:, :rwkv_in], jnp.zeros((d, pad_cols), w_in.dtype),
                                  w_in[l][:, rwkv_in:]], axis=1).astype(BF16)
        p = _proj_call(x, mod3, norm1_w[l].reshape(1, d), w_in_r)
        y_rwkv, y_na = _mixing_stage(p, rwkv_shift[l], w0_f[l], w_up_f[l], w0_b[l], w_up_b[l], a0_f[l],
                                     a_up_f[l], a0_b[l], a_up_b[l], g_up[l], k_k[l], k_a[l], r_k[l],
                                     ln_x_w[l], ln_x_b[l], q_norm_w[l], k_norm_w[l], rel_bias[l])
        d_rwkv = y_rwkv.shape[-1]
        w_o = w_out[l].astype(BF16)
        x1, h2, lg_t = _outproj_call(y_rwkv, y_na, x, mod3, norm2_w[l].reshape(1, d), w_o[:d_rwkv], w_o[d_rwkv:],
                                     w_router[l].T)
        x = _moe_stage(x1, h2, lg_t, mod3, router_bias[l], w_gate_e[l], w_up_e[l], w_down_e[l],
                       w_gate_s[l], w_up_s[l], w_down_s[l])
    return x
```

```python
import functools

import jax
import jax.numpy as jnp
import numpy as np
from jax import lax
from jax.experimental import pallas as pl
from jax.experimental.pallas import tpu as pltpu

F32 = jnp.float32
BF16 = jnp.bfloat16
I32 = jnp.int32
HI = lax.Precision.HIGHEST

LANES = 128
HEAD_DIM = 64
HEADS_PER_BLOCK = LANES // HEAD_DIM
GRID_W = 64
WIN_R = 8
WIN_C = 16
DECAY_LORA = 64
AAA_LORA = 64
GATE_LORA = 160
N_EXPERTS = 64
TOP_K = 6
SLOT_STRIDE = 8
N_GROUPS = 8
TOPK_GROUPS = 4
ROUTED_SCALE = 2.5
DISPATCH_BLOCK = 512
NORM_EPS = 1e-6
GN_EPS = 64e-5
L2_EPS = 1e-12
NEG_BIG = -1e30
WKV_CHUNK = 64
VMEM_LIMIT = 56 * 1024 * 1024

NT_DIMS = (((1,), (1,)), ((), ()))
TN_DIMS = (((0,), (0,)), ((), ()))


def _cparams(sem):
    return pltpu.CompilerParams(dimension_semantics=sem, vmem_limit_bytes=VMEM_LIMIT)


def _sigmoid(x):
    return 1.0 / (1.0 + jnp.exp(-x))


def _silu(x):
    return x * _sigmoid(x)


U32 = jnp.uint32
HI_HALF_MASK = 0xFFFF0000


def _pack_halves(x):
    n = x.shape[-1] // 2
    bits = lax.bitcast_convert_type(x.astype(BF16).astype(F32), U32)
    return (bits[:, :n] >> 16) | bits[:, n:]


def _unpack_halves(w):
    lo = lax.bitcast_convert_type(w << 16, F32)
    hi = lax.bitcast_convert_type(w & U32(HI_HALF_MASK), F32)
    return lo, hi


def _head_sum_matrix():
    a = np.arange(LANES) // HEAD_DIM
    return jnp.asarray((a[:, None] == a[None, :]).astype(np.float32))


def _head_sums_split(x, mbd):
    m16 = mbd.astype(BF16)
    hi = x.astype(BF16)
    lo = (x - hi.astype(F32)).astype(BF16)
    return jnp.dot(hi, m16, preferred_element_type=F32) + jnp.dot(lo, m16, preferred_element_type=F32)


def _mod_kernel(c_ref, w_ref, b_ref, o_ref):
    c = c_ref[...]
    o_ref[...] = jnp.dot(_silu(c), w_ref[...], preferred_element_type=F32, precision=HI) + b_ref[...]


def _mod_call(c_pad, w_ada, b_ada):
    rows, d = c_pad.shape
    n = w_ada.shape[1]
    tn = 1024
    return pl.pallas_call(
        _mod_kernel,
        grid=(n // tn,),
        in_specs=[pl.BlockSpec((rows, d), lambda j: (0, 0)),
                  pl.BlockSpec((d, tn), lambda j: (0, j)),
                  pl.BlockSpec((1, tn), lambda j: (0, j))],
        out_specs=pl.BlockSpec((rows, tn), lambda j: (0, j)),
        out_shape=jax.ShapeDtypeStruct((rows, n), F32),
        compiler_params=_cparams(("parallel",)),
        name="mod",
    )(c_pad, w_ada, b_ada)


def _proj_kernel(x_ref, mod_ref, nw_ref, w_ref, o_ref, h_sc, *, nq):
    @pl.when(pl.program_id(2) == 0)
    def _():
        x = x_ref[0]
        ms = jnp.mean(x * x, axis=-1, keepdims=True)
        y = x * lax.rsqrt(ms + NORM_EPS) * nw_ref[...]
        h_sc[...] = (y * (1.0 + mod_ref[0, 1:2, :]) + mod_ref[0, 0:1, :]).astype(BF16)

    acc = jnp.dot(h_sc[...], w_ref[...], preferred_element_type=F32)
    for q in range(nq):
        o_ref[0, q] = acc[:, q * LANES:(q + 1) * LANES].astype(o_ref.dtype)


def _proj_call(x, mod3, norm_w, w_in_r, tm=1024, tn=1664):
    b, s, d = x.shape
    n = w_in_r.shape[1]
    nq = tn // LANES
    return pl.pallas_call(
        functools.partial(_proj_kernel, nq=nq),
        grid=(b, s // tm, n // tn),
        in_specs=[pl.BlockSpec((1, tm, d), lambda bi, i, j: (bi, i, 0)),
                  pl.BlockSpec((1, 6, d), lambda bi, i, j: (bi, 0, 0)),
                  pl.BlockSpec((1, d), lambda bi, i, j: (0, 0)),
                  pl.BlockSpec((d, tn), lambda bi, i, j: (0, j))],
        out_specs=pl.BlockSpec((1, nq, tm, LANES), lambda bi, i, j: (bi, j, i, 0)),
        out_shape=jax.ShapeDtypeStruct((b, n // LANES, s, LANES), BF16),
        scratch_shapes=[pltpu.VMEM((tm, d), BF16)],
        compiler_params=_cparams(("parallel", "parallel", "arbitrary")),
        name="proj",
    )(x, mod3, norm_w, w_in_r)


RB_R, RB_V, RB_KK, RB_G, RB_KF, RB_KB, RB_AF, RB_AB, RB_BONUS = range(9)
VEC_W0F, VEC_W0B, VEC_A0F, VEC_A0B, VEC_KK, VEC_KA, VEC_RK = range(7)
N_RKV_BLOCKS = 24
N_SHIFT_BLOCKS = 28


def _softplus(u):
    return jnp.maximum(u, 0.0) + jnp.log(1.0 + jnp.exp(-jnp.abs(u)))


def _prep_kernel(p_ref, pp_ref, pn_ref, sw_ref, wupf_ref, wupb_ref, aupf_ref, aupb_ref, gup_ref,
                 vec_ref, mbd_ref, rb_ref, lw_ref, *, tb, n_t, nhp):
    i = pl.program_id(1)
    row = lax.broadcasted_iota(I32, (tb, LANES), 0)
    has_prev = i > 0
    has_next = i < n_t - 1
    halo = pp_ref.shape[2]

    def shifted(q):
        cur = p_ref[0, q].astype(F32)
        prev_row = jnp.where(has_prev, pp_ref[0, q, halo - 1:halo, :].astype(F32), 0.0)
        next_row = jnp.where(has_next, pn_ref[0, q, 0:1, :].astype(F32), 0.0)
        up = jnp.where(row == 0, prev_row, pltpu.roll(cur, 1, 0))
        dn = jnp.where(row == tb - 1, next_row, pltpu.roll(cur, tb - 1, 0))
        return sw_ref[0, q:q + 1, :] * up + sw_ref[1, q:q + 1, :] * cur + sw_ref[2, q:q + 1, :] * dn

    mbd = mbd_ref[...]
    t_wd = jnp.tanh(shifted(N_RKV_BLOCKS)).astype(BF16)
    z_ad = shifted(N_RKV_BLOCKS + 1).astype(BF16)
    s_gd = jnp.concatenate([_sigmoid(shifted(N_RKV_BLOCKS + 2)),
                            _sigmoid(shifted(N_RKV_BLOCKS + 3))], axis=1).astype(BF16)

    def vec(v, hp):
        return vec_ref[v, hp:hp + 1, :]

    for hp in range(nhp):
        sl = slice(hp * LANES, (hp + 1) * LANES)
        r = shifted(hp)
        k = shifted(nhp + hp)
        v = shifted(2 * nhp + hp)
        g = jnp.dot(s_gd, gup_ref[:, sl], preferred_element_type=F32)
        kk0 = k * vec(VEC_KK, hp)
        ss = _head_sums_split(kk0 * kk0, mbd)
        kk = kk0 * lax.rsqrt(jnp.maximum(ss, L2_EPS))
        kdirs = []
        for wup_ref, aup_ref, v_w0, v_a0, q_k, q_a, lw_slot in (
                (wupf_ref, aupf_ref, VEC_W0F, VEC_A0F, RB_KF, RB_AF, 0),
                (wupb_ref, aupb_ref, VEC_W0B, VEC_A0B, RB_KB, RB_AB, 1)):
            wl = vec(v_w0, hp) + jnp.dot(t_wd, wup_ref[:, sl], preferred_element_type=F32)
            w_log = -_softplus(-wl) - 0.5
            lw_ref[0, lw_slot * nhp + hp] = -jnp.exp(w_log)
            ag = _sigmoid(vec(v_a0, hp) + jnp.dot(z_ad, aup_ref[:, sl], preferred_element_type=F32))
            kd = k * (1.0 + (ag - 1.0) * vec(VEC_KA, hp))
            rb_ref[0, q_k * nhp + hp] = kd.astype(BF16)
            rb_ref[0, q_a * nhp + hp] = ag.astype(BF16)
            kdirs.append(kd)
        bonus = _head_sums_split(r * kdirs[0] * vec(VEC_RK, hp), mbd) * v
        rb_ref[0, RB_R * nhp + hp] = r.astype(BF16)
        rb_ref[0, RB_V * nhp + hp] = v.astype(BF16)
        rb_ref[0, RB_KK * nhp + hp] = kk.astype(BF16)
        rb_ref[0, RB_G * nhp + hp] = g.astype(BF16)
        rb_ref[0, RB_BONUS * nhp + hp] = bonus.astype(BF16)


def _prep_call(p, sw, wupf, wupb, aupf, aupb, gup, vecs, mbd, nhp, tb=256):
    b, _, s, _ = p.shape
    n_t = s // tb
    halo = 16
    hb = tb // halo
    full = lambda a: pl.BlockSpec(a.shape, lambda bi, i: (0,) * a.ndim)
    return pl.pallas_call(
        functools.partial(_prep_kernel, tb=tb, n_t=n_t, nhp=nhp),
        grid=(b, n_t),
        in_specs=[pl.BlockSpec((1, N_SHIFT_BLOCKS, tb, LANES), lambda bi, i: (bi, 0, i, 0)),
                  pl.BlockSpec((1, N_SHIFT_BLOCKS, halo, LANES),
                               lambda bi, i: (bi, 0, jnp.maximum(i * hb - 1, 0), 0)),
                  pl.BlockSpec((1, N_SHIFT_BLOCKS, halo, LANES),
                               lambda bi, i: (bi, 0, jnp.minimum((i + 1) * hb, s // halo - 1), 0)),
                  full(sw), full(wupf), full(wupb), full(aupf), full(aupb), full(gup), full(vecs), full(mbd)],
        out_specs=[pl.BlockSpec((1, 9 * nhp, tb, LANES), lambda bi, i: (bi, 0, i, 0)),
                   pl.BlockSpec((1, 2 * nhp, tb, LANES), lambda bi, i: (bi, 0, i, 0))],
        out_shape=[jax.ShapeDtypeStruct((b, 9 * nhp, s, LANES), BF16),
                   jax.ShapeDtypeStruct((b, 2 * nhp, s, LANES), F32)],
        compiler_params=_cparams(("parallel", "parallel")),
        name="prep",
    )(p, p, p, sw, wupf, wupb, aupf, aupb, gup, vecs, mbd)


def _wkv_kernel(r_ref, v_ref, kk_ref, kf_ref, af_ref, lf_ref, rr_ref, vr_ref, kkr_ref, kb_ref, ab_ref, lb_ref,
                yf_ref, yb_ref, s_sc, *, L, nhp):
    @pl.when(pl.program_id(1) == 0)
    def _():
        s_sc[...] = jnp.zeros(s_sc.shape, F32)

    n2 = HEADS_PER_BLOCK * L
    ri = lax.broadcasted_iota(I32, (n2, LANES), 0)
    ci = lax.broadcasted_iota(I32, (n2, LANES), 1)
    head_match = (ri // L) == (ci // HEAD_DIM)
    si = lax.broadcasted_iota(I32, (n2, n2), 0)
    sj = lax.broadcasted_iota(I32, (n2, n2), 1)
    st, ss = si % L, sj % L
    eye = (si == sj).astype(F32)
    strict = {False: st > ss, True: st < ss}
    incl = {False: st >= ss, True: st < ss}

    def stack(x):
        return jnp.where(head_match, jnp.concatenate([x] * HEADS_PER_BLOCK, axis=0), 0.0)

    chains = [(False, hp, r_ref, v_ref, kk_ref, kf_ref, af_ref, lf_ref) for hp in range(nhp)]
    chains += [(True, hp, rr_ref, vr_ref, kkr_ref, kb_ref, ab_ref, lb_ref) for hp in range(nhp)]
    n = len(chains)
    lws = [ch[7][0, ch[1]] for ch in chains]
    t_row = lax.broadcasted_iota(I32, (L, LANES), 0)

    def cumsum_time(x, rev):
        sh = 1
        while sh < L:
            if rev:
                x = x + jnp.where(t_row < L - sh, pltpu.roll(x, L - sh, 0), 0.0)
            else:
                x = x + jnp.where(t_row >= sh, pltpu.roll(x, sh, 0), 0.0)
            sh *= 2
        return x

    c_ins = [cumsum_time(lw, ch[0]) for ch, lw in zip(chains, lws)]
    lhs_l, rhs_l, vs_l, bk_l, pl_l = [], [], [], [], []
    for i, (rev, hp, rr, vr, kkr, kdr, agr, _) in enumerate(chains):
        r = rr[0, hp].astype(F32)
        kk = kkr[0, hp].astype(F32)
        kd = kdr[0, hp].astype(F32)
        ag = agr[0, hp].astype(F32)
        c_in = c_ins[i]
        e_ex = jnp.exp(c_in - lws[i])
        e_inv = jnp.exp(-c_in)
        if rev:
            e_r = e_ex
            p_last = jnp.exp(c_in[0:1])
        else:
            e_r = jnp.exp(c_in)
            p_last = jnp.exp(c_in[L - 1:L])
        b_s = stack(kk * ag * e_inv)
        k_s = stack(kd * e_inv)
        lhs_l.append(jnp.concatenate([stack(-kk * e_ex), stack(r * e_r)], axis=0).astype(BF16))
        rhs_l.append(jnp.concatenate([b_s, k_s], axis=0).astype(BF16))
        bk_l.append(jnp.concatenate([b_s * p_last, k_s * p_last], axis=0).astype(BF16))
        vs_l.append(stack(vr[0, hp].astype(F32)).astype(BF16))
        pl_l.append(p_last)
    g_l = [lax.dot_general(lhs_l[i], rhs_l[i], NT_DIMS, preferred_element_type=F32) for i in range(n)]
    s_old = [s_sc[i] for i in range(n)]
    ls_l = [lax.dot_general(lhs_l[i], s_old[i].astype(BF16), NT_DIMS, preferred_element_type=F32)
            for i in range(n)]
    a_ab = [jnp.where(strict[ch[0]], g[0:n2, 0:n2], 0.0) for ch, g in zip(chains, g_l)]
    a_ak = [jnp.where(strict[ch[0]], g[0:n2, n2:2 * n2], 0.0).astype(BF16) for ch, g in zip(chains, g_l)]
    a_r = [jnp.concatenate([jnp.where(incl[ch[0]], g[n2:2 * n2, 0:n2], 0.0),
                            jnp.where(incl[ch[0]], g[n2:2 * n2, n2:2 * n2], 0.0)], axis=1).astype(BF16)
           for ch, g in zip(chains, g_l)]
    x_l = [ls_l[i][0:n2] + jnp.dot(a_ak[i], vs_l[i], preferred_element_type=F32) for i in range(n)]
    rounds = L.bit_length() - 1
    pw = [jnp.dot(a.astype(BF16), a.astype(BF16), preferred_element_type=F32) for a in a_ab]
    t_inv = [eye + a for a in a_ab]
    for j in range(1, rounds):
        pb = [p.astype(BF16) for p in pw]
        if j + 1 < rounds:
            both = [jnp.dot(jnp.concatenate([pb[i], t_inv[i].astype(BF16)], axis=0), pb[i],
                            preferred_element_type=F32) for i in range(n)]
            pw = [m[0:n2] for m in both]
            t_inv = [t + m[n2:2 * n2] for t, m in zip(t_inv, both)]
        else:
            t_inv = [t + jnp.dot(t.astype(BF16), pb[i], preferred_element_type=F32)
                     for i, t in enumerate(t_inv)]
    u_l = [jnp.dot(t_inv[i].astype(BF16), x_l[i].astype(BF16), preferred_element_type=F32) for i in range(n)]
    uv_l = [jnp.concatenate([jnp.where(head_match, u_l[i], 0.0).astype(BF16), vs_l[i]], axis=0)
            for i in range(n)]
    y_l = [ls_l[i][n2:2 * n2] + jnp.dot(a_r[i], uv_l[i], preferred_element_type=F32) for i in range(n)]
    upd_l = [lax.dot_general(uv_l[i], bk_l[i], TN_DIMS, preferred_element_type=F32) for i in range(n)]
    for i, (rev, hp, *_) in enumerate(chains):
        y = y_l[i]
        out = y[0:L]
        for h in range(1, HEADS_PER_BLOCK):
            out = out + y[h * L:(h + 1) * L]
        (yb_ref if rev else yf_ref)[0, hp] = out
        s_sc[i] = s_old[i] * pl_l[i] + upd_l[i]


def _wkv_call(rb, lw, nhp, L=WKV_CHUNK):
    b, _, s, _ = rb.shape
    nc = s // L
    fwd = lambda q: pl.BlockSpec((1, nhp, L, LANES), lambda bi, c: (bi, q, c, 0))
    bwd = lambda q: pl.BlockSpec((1, nhp, L, LANES), lambda bi, c: (bi, q, nc - 1 - c, 0))
    y_shape = jax.ShapeDtypeStruct((b, nhp, s, LANES), F32)
    return pl.pallas_call(
        functools.partial(_wkv_kernel, L=L, nhp=nhp),
        grid=(b, nc),
        in_specs=[fwd(RB_R), fwd(RB_V), fwd(RB_KK), fwd(RB_KF), fwd(RB_AF), fwd(0),
                  bwd(RB_R), bwd(RB_V), bwd(RB_KK), bwd(RB_KB), bwd(RB_AB), bwd(1)],
        out_specs=[fwd(0), bwd(0)],
        out_shape=[y_shape, y_shape],
        scratch_shapes=[pltpu.VMEM((2 * nhp, LANES, LANES), F32)],
        compiler_params=_cparams(("parallel", "arbitrary")),
        name="wkv",
    )(rb, rb, rb, rb, rb, lw, rb, rb, rb, rb, rb, lw)


def _post_kernel(yf_ref, yb_ref, bonus_ref, g_ref, lnw_ref, lnb_ref, mbd_ref, o_ref, *, nhp):
    mbd = mbd_ref[...]
    inv_n = 1.0 / HEAD_DIM
    for hp in range(nhp):
        y = yf_ref[0, hp] + yb_ref[0, hp]
        mu = _head_sums_split(y, mbd) * inv_n
        d = y - mu
        var = _head_sums_split(d * d, mbd) * inv_n
        yn = d * lax.rsqrt(var + GN_EPS) * lnw_ref[hp:hp + 1, :] + lnb_ref[hp:hp + 1, :]
        out = (yn + bonus_ref[0, hp].astype(F32)) * g_ref[0, hp].astype(F32)
        o_ref[0, :, hp * LANES:(hp + 1) * LANES] = out.astype(o_ref.dtype)


def _post_call(yf, yb, rb, lnw, lnb, mbd, nhp, tb=256):
    b, _, s, _ = yf.shape
    yblk = pl.BlockSpec((1, nhp, tb, LANES), lambda bi, i: (bi, 0, i, 0))
    rblk = lambda q: pl.BlockSpec((1, nhp, tb, LANES), lambda bi, i: (bi, q, i, 0))
    full = lambda a: pl.BlockSpec(a.shape, lambda bi, i: (0,) * a.ndim)
    return pl.pallas_call(
        functools.partial(_post_kernel, nhp=nhp),
        grid=(b, s // tb),
        in_specs=[yblk, yblk, rblk(RB_BONUS), rblk(RB_G), full(lnw), full(lnb), full(mbd)],
        out_specs=pl.BlockSpec((1, tb, nhp * LANES), lambda bi, i: (bi, i, 0)),
        out_shape=jax.ShapeDtypeStruct((b, s, nhp * LANES), BF16),
        compiler_params=_cparams(("parallel", "parallel")),
        name="post",
    )(yf, yb, rb, rb, lnw, lnb, mbd)


def _na_kernel(q_ref, k_ref, v_ref, bias_ref, qw_ref, kw_ref, mbd_ref, o_ref, kn_sc, *, rq, rows):
    rbi = pl.program_id(2)
    mbd = mbd_ref[...]
    inv_n = 1.0 / HEAD_DIM
    span = WIN_R * GRID_W
    s_len = kn_sc.shape[0]

    @pl.when(rbi == 0)
    def _():
        def body(c, carry):
            off = pl.multiple_of(c * span, span)
            k = k_ref[0, 0, pl.ds(off, span), :].astype(F32)
            ms = _head_sums_split(k * k, mbd) * inv_n
            kn_sc[pl.ds(off, span), :] = (k * lax.rsqrt(ms + NORM_EPS) * kw_ref[...]).astype(BF16)
            return carry
        lax.fori_loop(0, s_len // span, body, 0)

    ri = lax.broadcasted_iota(I32, (HEADS_PER_BLOCK * GRID_W, LANES), 0)
    ci = lax.broadcasted_iota(I32, (HEADS_PER_BLOCK * GRID_W, LANES), 1)
    head_match = (ri // GRID_W) == (ci // HEAD_DIM)
    lane = lax.broadcasted_iota(I32, (GRID_W, LANES), 1)
    scale = HEAD_DIM ** -0.5
    q_all = q_ref[0, 0].astype(F32)
    ms = _head_sums_split(q_all * q_all, mbd) * inv_n
    qn_all = q_all * lax.rsqrt(ms + NORM_EPS) * qw_ref[...] * scale
    starts, offs = [], []
    for qi in range(rq):
        i = rbi * rq + qi
        rs = jnp.clip(i - WIN_R // 2, 0, rows - WIN_R)
        offs.append(rs - i + WIN_R - 1)
        starts.append(pl.multiple_of(rs * GRID_W, GRID_W))
    qs_l = [jnp.where(head_match, jnp.concatenate([qn_all[qi * GRID_W:(qi + 1) * GRID_W]] * HEADS_PER_BLOCK,
                                                  axis=0), 0.0).astype(BF16) for qi in range(rq)]
    s_l = [lax.dot_general(qs_l[qi], kn_sc[pl.ds(starts[qi], span), :], NT_DIMS, preferred_element_type=F32)
           + bias_ref[offs[qi], 0] for qi in range(rq)]
    p_l, l_l = [], []
    for s in s_l:
        p = jnp.exp(s - jnp.max(s, axis=-1, keepdims=True))
        p_l.append(p.astype(BF16))
        l_l.append(jnp.sum(p, axis=-1, keepdims=True))
    o_l = [jnp.dot(p_l[qi], v_ref[0, 0, pl.ds(starts[qi], span), :], preferred_element_type=F32) / l_l[qi]
           for qi in range(rq)]
    for qi, o in enumerate(o_l):
        out = o[0:GRID_W]
        for h in range(1, HEADS_PER_BLOCK):
            out = jnp.where(lane // HEAD_DIM == h, o[h * GRID_W:(h + 1) * GRID_W], out)
        o_ref[0, qi * GRID_W:(qi + 1) * GRID_W, :] = out.astype(o_ref.dtype)


def _na_call(p, bias, qw, kw, mbd, q_blk0, nhp, rq=16):
    b, _, s, _ = p.shape
    rows = s // GRID_W
    rq = min(rq, rows)
    full = lambda a: pl.BlockSpec(a.shape, lambda bi, hp, r: (0,) * a.ndim)
    return pl.pallas_call(
        functools.partial(_na_kernel, rq=rq, rows=rows),
        grid=(b, nhp, rows // rq),
        in_specs=[pl.BlockSpec((1, 1, rq * GRID_W, LANES), lambda bi, hp, r: (bi, q_blk0 + hp, r, 0)),
                  pl.BlockSpec((1, 1, s, LANES), lambda bi, hp, r: (bi, q_blk0 + nhp + hp, 0, 0)),
                  pl.BlockSpec((1, 1, s, LANES), lambda bi, hp, r: (bi, q_blk0 + 2 * nhp + hp, 0, 0)),
                  pl.BlockSpec((WIN_R, 1, HEADS_PER_BLOCK * GRID_W, WIN_R * GRID_W),
                               lambda bi, hp, r: (0, hp, 0, 0)),
                  full(qw), full(kw), full(mbd)],
        out_specs=pl.BlockSpec((1, rq * GRID_W, LANES), lambda bi, hp, r: (bi, r, hp)),
        out_shape=jax.ShapeDtypeStruct((b, s, nhp * LANES), BF16),
        scratch_shapes=[pltpu.VMEM((s, LANES), BF16)],
        compiler_params=_cparams(("parallel", "parallel", "arbitrary")),
        name="na",
    )(p, p, p, bias, qw, kw, mbd)


def _na_bias_table(rel_bias):
    h = rel_bias.shape[0]
    qc = np.arange(GRID_W)
    kc = np.arange(GRID_W)
    win0 = np.clip(qc - WIN_C // 2, 0, GRID_W - WIN_C)
    valid = (kc[None, :] >= win0[:, None]) & (kc[None, :] < win0[:, None] + WIN_C)
    coff = np.clip(kc[None, :] - qc[:, None] + WIN_C - 1, 0, 2 * WIN_C - 2)
    pick = jnp.asarray((coff[None] == np.arange(2 * WIN_C - 1)[:, None, None]).astype(np.float32))
    t = jnp.einsum('hrc,cqk->hrqk', rel_bias.astype(F32), pick, precision=HI)
    t = jnp.stack([t[:, o:o + WIN_R] for o in range(WIN_R)], axis=1)
    t = jnp.where(valid[None, None, None], t, NEG_BIG)
    t = jnp.transpose(t, (1, 0, 3, 2, 4))
    return t.reshape(WIN_R, h // HEADS_PER_BLOCK, HEADS_PER_BLOCK * GRID_W, WIN_R * GRID_W)


def _outproj_kernel(yr_ref, yn_ref, x_ref, mod_ref, nw_ref, w1_ref, w2_ref, wr_ref, x1_ref, h2_ref, lg_ref):
    tm = x_ref.shape[1]
    sub = LANES
    tiles = [slice(i * sub, (i + 1) * sub) for i in range(tm // sub)]
    accs = [jnp.dot(yr_ref[0, r], w1_ref[...], preferred_element_type=F32)
            + jnp.dot(yn_ref[0, r], w2_ref[...], preferred_element_type=F32) for r in tiles]
    h2s = []
    for r, acc in zip(tiles, accs):
        x1 = x_ref[0, r] + mod_ref[0, 2:3, :] * acc
        x1_ref[0, r] = x1
        ms = jnp.mean(x1 * x1, axis=-1, keepdims=True)
        y = x1 * lax.rsqrt(ms + NORM_EPS) * nw_ref[...]
        h2 = y * (1.0 + mod_ref[0, 4:5, :]) + mod_ref[0, 3:4, :]
        h2_ref[0, r] = _pack_halves(h2)
        h2s.append(h2)
    wr = wr_ref[...]
    wr_hi = wr.astype(BF16)
    wr_lo = (wr - wr_hi.astype(F32)).astype(BF16)
    for r, h2 in zip(tiles, h2s):
        h_hi = h2.astype(BF16)
        h_lo = (h2 - h_hi.astype(F32)).astype(BF16)
        lg_ref[:, r] = (lax.dot_general(wr_hi, h_hi, NT_DIMS, preferred_element_type=F32)
                        + lax.dot_general(wr_hi, h_lo, NT_DIMS, preferred_element_type=F32)
                        + lax.dot_general(wr_lo, h_hi, NT_DIMS, preferred_element_type=F32))


def _outproj_call(yr, yn, x, mod3, norm_w, w1, w2, wr_t, tm=512):
    b, s, d = x.shape
    dh = yr.shape[-1]
    ne = wr_t.shape[0]
    nt = s // tm
    full = lambda a: pl.BlockSpec(a.shape, lambda bi, i: (0,) * a.ndim)
    return pl.pallas_call(
        _outproj_kernel,
        grid=(b, nt),
        in_specs=[pl.BlockSpec((1, tm, dh), lambda bi, i: (bi, i, 0)),
                  pl.BlockSpec((1, tm, dh), lambda bi, i: (bi, i, 0)),
                  pl.BlockSpec((1, tm, d), lambda bi, i: (bi, i, 0)),
                  pl.BlockSpec((1, 6, d), lambda bi, i: (bi, 0, 0)),
                  full(norm_w), full(w1), full(w2), full(wr_t)],
        out_specs=[pl.BlockSpec((1, tm, d), lambda bi, i: (bi, i, 0)),
                   pl.BlockSpec((1, tm, d // 2), lambda bi, i: (bi, i, 0)),
                   pl.BlockSpec((ne, tm), lambda bi, i: (0, bi * nt + i))],
        out_shape=[jax.ShapeDtypeStruct((b, s, d), F32),
                   jax.ShapeDtypeStruct((b, s, d // 2), U32),
                   jax.ShapeDtypeStruct((ne, b * s), F32)],
        compiler_params=_cparams(("parallel", "parallel")),
        name="outproj",
    )(yr, yn, x, mod3, norm_w, w1, w2, wr_t)


def _first_argmax(x, idx, n):
    m = jnp.max(x, axis=0, keepdims=True)
    a = jnp.min(jnp.where(x == m, idx, n), axis=0, keepdims=True)
    return m, a


def _route_kernel(lg_ref, bias_ref, idx_ref, gw_ref, pos_ref, cnt_ref, *, tr):
    @pl.when(pl.program_id(0) == 0)
    def _():
        cnt_ref[...] = jnp.zeros(cnt_ref.shape, F32)

    gsz = N_EXPERTS // N_GROUPS
    scores = _sigmoid(lg_ref[...])
    biased = scores + bias_ref[:, 0:1]
    ig = lax.broadcasted_iota(I32, (gsz, tr), 0)
    grp_rows = []
    for g in range(N_GROUPS):
        blk = biased[g * gsz:(g + 1) * gsz]
        m1, a1 = _first_argmax(blk, ig, gsz)
        m2 = jnp.max(jnp.where(ig == a1, -jnp.inf, blk), axis=0, keepdims=True)
        grp_rows.append(m1 + m2)
    grp = jnp.concatenate(grp_rows, axis=0)
    ign = lax.broadcasted_iota(I32, (N_GROUPS, tr), 0)
    sel = jnp.zeros((N_GROUPS, tr), jnp.bool_)
    for _ in range(TOPK_GROUPS):
        _, a = _first_argmax(grp, ign, N_GROUPS)
        hit = ign == a
        sel = jnp.logical_or(sel, hit)
        grp = jnp.where(hit, -jnp.inf, grp)
    masked = jnp.concatenate(
        [jnp.where(sel[g:g + 1], biased[g * gsz:(g + 1) * gsz], -jnp.inf) for g in range(N_GROUPS)], axis=0)
    ie = lax.broadcasted_iota(I32, (N_EXPERTS, tr), 0)
    picks, pick_scores = [], []
    onehot = jnp.zeros((N_EXPERTS, tr), F32)
    for _ in range(TOP_K):
        _, a = _first_argmax(masked, ie, N_EXPERTS)
        hit = ie == a
        picks.append(a)
        pick_scores.append(jnp.sum(jnp.where(hit, scores, 0.0), axis=0, keepdims=True))
        onehot = onehot + hit.astype(F32)
        masked = jnp.where(hit, -jnp.inf, masked)
    total = pick_scores[0]
    for sc in pick_scores[1:]:
        total = total + sc
    t0 = lax.broadcasted_iota(I32, (tr, tr), 0)
    t1 = lax.broadcasted_iota(I32, (tr, tr), 1)
    before = (t0 < t1).astype(BF16)
    rank = jnp.dot(onehot.astype(BF16), before, preferred_element_type=F32) + cnt_ref[:, 0:1]
    zero_i = jnp.zeros((1, tr), I32)
    zero_f = jnp.zeros((1, tr), F32)
    for k in range(8):
        if k < TOP_K:
            idx_ref[k:k + 1, :] = picks[k]
            gw_ref[k:k + 1, :] = pick_scores[k] / total * ROUTED_SCALE
            pos = jnp.sum(jnp.where(ie == picks[k], rank, 0.0), axis=0, keepdims=True)
            pos_ref[k:k + 1, :] = pos.astype(I32)
        else:
            idx_ref[k:k + 1, :] = zero_i
            gw_ref[k:k + 1, :] = zero_f
            pos_ref[k:k + 1, :] = zero_i
    cnt_ref[...] = cnt_ref[...] + jnp.sum(onehot, axis=1, keepdims=True)


def _route_call(lg_t, bias2, tr=512):
    ne, t = lg_t.shape
    tr = min(tr, t)
    tok = pl.BlockSpec((8, tr), lambda i: (0, i))
    return pl.pallas_call(
        functools.partial(_route_kernel, tr=tr),
        grid=(t // tr,),
        in_specs=[pl.BlockSpec((ne, tr), lambda i: (0, i)),
                  pl.BlockSpec(bias2.shape, lambda i: (0, 0))],
        out_specs=[tok, tok, tok, pl.BlockSpec((ne, LANES), lambda i: (0, 0))],
        out_shape=[jax.ShapeDtypeStruct((8, t), I32), jax.ShapeDtypeStruct((8, t), F32),
                   jax.ShapeDtypeStruct((8, t), I32), jax.ShapeDtypeStruct((ne, LANES), F32)],
        compiler_params=_cparams(("arbitrary",)),
        name="route",
    )(lg_t, bias2)


FILL_CHUNKS = (1, 8, 64)
ZERO_ROWS = FILL_CHUNKS[-1]


def _dispatch_kernel(slot_ref, fill_lo_ref, fill_hi_ref, h_ref, xs_ref, zrow, sem, *, td):
    def row_copy(t, slot):
        return pltpu.make_async_copy(h_ref.at[pl.ds(t, 1)], xs_ref.at[pl.ds(slot, 1)], sem)

    def start(t, carry):
        for k in range(TOP_K):
            row_copy(t, slot_ref[t * SLOT_STRIDE + k]).start()
        return carry

    lax.fori_loop(0, td, start, 0)
    for k in range(TOP_K):
        pltpu.make_async_copy(h_ref, xs_ref.at[pl.ds(0, td)], sem).wait()

    @pl.when(pl.program_id(0) == pl.num_programs(0) - 1)
    def _():
        zrow[...] = jnp.zeros(zrow.shape, zrow.dtype)

        def zero_copy(row, n):
            if n > 1:
                row = pl.multiple_of(row, FILL_CHUNKS[1])
            return pltpu.make_async_copy(zrow.at[pl.ds(0, n)], xs_ref.at[pl.ds(row, n)], sem)

        def for_each_chunk(r, fn):
            lo, hi = fill_lo_ref[r], fill_hi_ref[r]
            for ci, n in enumerate(FILL_CHUNKS):
                if ci + 1 < len(FILL_CHUNKS):
                    count = jnp.minimum(((-lo) % FILL_CHUNKS[ci + 1]) // n, (hi - lo) // n)
                else:
                    count = (hi - lo) // n
                fn(lo, n, count)
                lo = lo + count * n

        def start_range(r, carry):
            def go(lo, n, count):
                lax.fori_loop(0, count, lambda i, c: (zero_copy(lo + i * n, n).start(), c)[1], 0)
            for_each_chunk(r, go)
            return carry

        def wait_range(r, carry):
            def go(lo, n, count):
                lax.fori_loop(0, count, lambda i, c: (zero_copy(0, n).wait(), c)[1], 0)
            for_each_chunk(r, go)
            return carry

        n_ranges = fill_lo_ref.shape[0]
        lax.fori_loop(0, n_ranges, start_range, 0)
        lax.fori_loop(0, n_ranges, wait_range, 0)


def _dispatch_call(slots_flat, fill_lo, fill_hi, h2, n_rows, td=256):
    t, dh = h2.shape
    smem = pl.BlockSpec(memory_space=pltpu.SMEM)
    return pl.pallas_call(
        functools.partial(_dispatch_kernel, td=td),
        grid=(t // td,),
        in_specs=[pl.BlockSpec((td * SLOT_STRIDE,), lambda i: (i,), memory_space=pltpu.SMEM),
                  smem, smem,
                  pl.BlockSpec((td, dh), lambda i: (i, 0))],
        out_specs=pl.BlockSpec(memory_space=pl.ANY),
        out_shape=jax.ShapeDtypeStruct((n_rows, dh), h2.dtype),
        scratch_shapes=[pltpu.VMEM((ZERO_ROWS, dh), h2.dtype), pltpu.SemaphoreType.DMA(())],
        compiler_params=_cparams(("arbitrary",)),
        name="dispatch",
    )(slots_flat, fill_lo, fill_hi, h2)


def _experts_kernel(be_ref, first_ref, nxt_ref, par_ref, rows_ref, nv_ref, xs_ref, wg_hbm, wu_hbm, wd_hbm, ys_ref,
                    wg_buf, wu_buf, wd_buf, wg_sc, wu_sc, wd_sc, sem):
    b = pl.program_id(0)
    m = xs_ref.shape[0]
    half = wg_sc.shape[0] // 2

    def weight_copies(e, slot):
        return [pltpu.make_async_copy(src.at[e], dst.at[slot], sem.at[slot, i])
                for i, (src, dst) in enumerate(((wg_hbm, wg_buf), (wu_hbm, wu_buf), (wd_hbm, wd_buf)))]

    def compute(nrows):
        x_lo, x_hi = _unpack_halves(xs_ref[0:nrows])
        x_lo, x_hi = x_lo.astype(BF16), x_hi.astype(BF16)
        g = (jnp.dot(x_lo, wg_sc[0:half], preferred_element_type=F32)
             + jnp.dot(x_hi, wg_sc[half:], preferred_element_type=F32))
        u = (jnp.dot(x_lo, wu_sc[0:half], preferred_element_type=F32)
             + jnp.dot(x_hi, wu_sc[half:], preferred_element_type=F32))
        a = (_silu(g) * u).astype(BF16)
        ys_ref[0:nrows] = _pack_halves(jnp.dot(a, wd_sc[...], preferred_element_type=F32))

    @pl.when(b < nv_ref[0])
    def _():
        @pl.when(b == 0)
        def _():
            for c in weight_copies(be_ref[0], par_ref[0]):
                c.start()

        @pl.when(first_ref[b] == 1)
        def _():
            slot = par_ref[b]
            for c in weight_copies(be_ref[b], slot):
                c.wait()

            @pl.when(nxt_ref[b] >= 0)
            def _():
                for c in weight_copies(nxt_ref[b], 1 - slot):
                    c.start()

            wg_sc[...] = wg_buf[slot].astype(BF16)
            wu_sc[...] = wu_buf[slot].astype(BF16)
            wd_sc[...] = wd_buf[slot].astype(BF16)

        @pl.when(rows_ref[b] > m // 2)
        def _():
            compute(m)

        @pl.when(rows_ref[b] <= m // 2)
        def _():
            compute(m // 2)
            ys_ref[m // 2:] = jnp.zeros((m - m // 2, ys_ref.shape[1]), ys_ref.dtype)

    @pl.when(b >= nv_ref[0])
    def _():
        ys_ref[...] = jnp.zeros(ys_ref.shape, ys_ref.dtype)


def _experts_call(block_e, first, nxt, par, rows_valid, n_valid, xs, wg, wu, wd):
    total, d = xs.shape[0], wg.shape[1]
    m = DISPATCH_BLOCK
    ff = wg.shape[-1]
    rows = lambda b, *s: (jnp.minimum(b, s[-1][0] - 1), 0)
    out_rows = lambda b, *s: (b, 0)
    hbm = pl.BlockSpec(memory_space=pl.ANY)
    grid_spec = pltpu.PrefetchScalarGridSpec(
        num_scalar_prefetch=6,
        grid=(total // m,),
        in_specs=[pl.BlockSpec((m, d // 2), rows), hbm, hbm, hbm],
        out_specs=pl.BlockSpec((m, d // 2), out_rows),
        scratch_shapes=[pltpu.VMEM((2, d, ff), F32), pltpu.VMEM((2, d, ff), F32), pltpu.VMEM((2, ff, d), F32),
                        pltpu.VMEM((d, ff), BF16), pltpu.VMEM((d, ff), BF16), pltpu.VMEM((ff, d), BF16),
                        pltpu.SemaphoreType.DMA((2, 3))],
    )
    return pl.pallas_call(
        _experts_kernel,
        grid_spec=grid_spec,
        out_shape=jax.ShapeDtypeStruct((total, d // 2), U32),
        compiler_params=_cparams(("arbitrary",)),
        name="experts",
    )(block_e, first, nxt, par, rows_valid, n_valid, xs, wg, wu, wd)


def _combine_kernel(slot_ref, x1_ref, h2_ref, gw_ref, mod_ref, wg_ref, wu_ref, wd_ref, ys_ref, o_ref,
                    buf, sem, *, tc):
    def row_copy(t, k, slot):
        return pltpu.make_async_copy(ys_ref.at[pl.ds(slot, 1)], buf.at[k, pl.ds(t, 1)], sem)

    def start(t, carry):
        for k in range(TOP_K):
            row_copy(t, k, slot_ref[t * SLOT_STRIDE + k]).start()
        return carry

    lax.fori_loop(0, tc, start, 0)
    half = wg_ref.shape[0] // 2
    h_lo, h_hi = _unpack_halves(h2_ref[0])
    h_lo, h_hi = h_lo.astype(BF16), h_hi.astype(BF16)
    g = (jnp.dot(h_lo, wg_ref[0:half], preferred_element_type=F32)
         + jnp.dot(h_hi, wg_ref[half:], preferred_element_type=F32))
    u = (jnp.dot(h_lo, wu_ref[0:half], preferred_element_type=F32)
         + jnp.dot(h_hi, wu_ref[half:], preferred_element_type=F32))
    acc = jnp.dot((_silu(g) * u).astype(BF16), wd_ref[...], preferred_element_type=F32)
    for k in range(TOP_K):
        pltpu.make_async_copy(ys_ref.at[pl.ds(0, tc)], buf.at[k], sem).wait()
    gw = gw_ref[...]
    acc_lo, acc_hi = acc[:, :half], acc[:, half:]
    for k in range(TOP_K):
        y_lo, y_hi = _unpack_halves(buf[k])
        acc_lo = acc_lo + gw[:, k:k + 1] * y_lo
        acc_hi = acc_hi + gw[:, k:k + 1] * y_hi
    gate = mod_ref[0, 5:6, :]
    o_ref[0, :, 0:half] = x1_ref[0, :, 0:half] + gate[:, :half] * acc_lo
    o_ref[0, :, half:] = x1_ref[0, :, half:] + gate[:, half:] * acc_hi


def _combine_call(slots_flat, x1, h2, gw_tok, mod3, wg, wu, wd, ys, tc=256):
    b, s, d = x1.shape
    nt = s // tc
    full = lambda a: pl.BlockSpec(a.shape, lambda bi, i: (0,) * a.ndim)
    return pl.pallas_call(
        functools.partial(_combine_kernel, tc=tc),
        grid=(b, nt),
        in_specs=[pl.BlockSpec((tc * SLOT_STRIDE,), lambda bi, i: (bi * nt + i,), memory_space=pltpu.SMEM),
                  pl.BlockSpec((1, tc, d), lambda bi, i: (bi, i, 0)),
                  pl.BlockSpec((1, tc, d // 2), lambda bi, i: (bi, i, 0)),
                  pl.BlockSpec((tc, 8), lambda bi, i: (bi * nt + i, 0)),
                  pl.BlockSpec((1, 6, d), lambda bi, i: (bi, 0, 0)),
                  full(wg), full(wu), full(wd),
                  pl.BlockSpec(memory_space=pl.ANY)],
        out_specs=pl.BlockSpec((1, tc, d), lambda bi, i: (bi, i, 0)),
        out_shape=jax.ShapeDtypeStruct((b, s, d), F32),
        scratch_shapes=[pltpu.VMEM((TOP_K, tc, d // 2), U32), pltpu.SemaphoreType.DMA(())],
        compiler_params=_cparams(("arbitrary", "arbitrary")),
        name="combine",
    )(slots_flat, x1, h2, gw_tok, mod3, wg, wu, wd, ys)


def _pad_rows(w, rows, at=0):
    out = jnp.zeros((rows, w.shape[1]), w.dtype)
    return out.at[at:at + w.shape[0]].set(w)


def _mixing_stage(p, rwkv_shift, w0_f, w_up_f, w0_b, w_up_b, a0_f, a_up_f, a0_b, a_up_b, g_up,
                  k_k, k_a, r_k, ln_x_w, ln_x_b, q_norm_w, k_norm_w, rel_bias):
    d_rwkv = w0_f.shape[0]
    nhp = d_rwkv // LANES
    mbd = _head_sum_matrix()
    n_shift = rwkv_shift.shape[1]
    sw = jnp.pad(rwkv_shift, ((0, 0), (0, N_SHIFT_BLOCKS * LANES - n_shift))).reshape(3, N_SHIFT_BLOCKS, LANES)
    wupf = _pad_rows(w_up_f, LANES, 0).astype(BF16)
    wupb = _pad_rows(w_up_b, LANES, DECAY_LORA).astype(BF16)
    aupf = _pad_rows(a_up_f, LANES, 0).astype(BF16)
    aupb = _pad_rows(a_up_b, LANES, AAA_LORA).astype(BF16)
    gup = _pad_rows(g_up, 2 * LANES, 0).astype(BF16)
    vecs = jnp.stack([w0_f, w0_b, a0_f, a0_b, k_k, k_a, r_k.reshape(-1)]).reshape(7, nhp, LANES)
    rb, lw = _prep_call(p, sw, wupf, wupb, aupf, aupb, gup, vecs, mbd, nhp)
    y_f, y_b = _wkv_call(rb, lw, nhp)
    y_rwkv = _post_call(y_f, y_b, rb, ln_x_w.reshape(nhp, LANES), ln_x_b.reshape(nhp, LANES), mbd, nhp)
    bias = _na_bias_table(rel_bias)
    qw = jnp.tile(q_norm_w, HEADS_PER_BLOCK).reshape(1, LANES)
    kw = jnp.tile(k_norm_w, HEADS_PER_BLOCK).reshape(1, LANES)
    y_na = _na_call(p, bias, qw, kw, mbd, N_SHIFT_BLOCKS, nhp)
    return y_rwkv, y_na


def _moe_stage(x1, h2, lg_t, mod3, router_bias, w_gate_e, w_up_e, w_down_e, w_gate_s, w_up_s, w_down_s):
    b, s, d = x1.shape
    t = b * s
    m = DISPATCH_BLOCK
    bias2 = jnp.broadcast_to(router_bias.astype(F32)[:, None], (N_EXPERTS, LANES))
    idx_t, gw_t, pos_t, cnt = _route_call(lg_t, bias2)
    counts = cnt[:, 0].astype(I32)
    padded = (counts + m - 1) // m * m
    pad_end = jnp.cumsum(padded)
    pad_start = pad_end - padded
    n_blocks = -(-(t * TOP_K + N_EXPERTS * m) // m)
    n_valid = (pad_end[-1] // m).astype(I32).reshape(1)
    block_row0 = jnp.arange(n_blocks, dtype=I32) * m
    block_e = jnp.minimum(jnp.sum((pad_end[None, :] <= block_row0[:, None]).astype(I32), axis=1),
                          N_EXPERTS - 1).astype(I32)
    assert m % ZERO_ROWS == 0
    expert_ids = jnp.arange(N_EXPERTS, dtype=I32)
    start_of = jnp.sum(jnp.where(idx_t[:, :, None] == expert_ids, pad_start, 0), axis=-1)
    slots = (start_of + pos_t).astype(I32)
    slots_flat = slots.T.reshape(-1)
    gw_tok = gw_t.T
    fill_lo = jnp.concatenate([pad_start + counts, pad_end[-1:]]).astype(I32)
    fill_hi = jnp.concatenate([pad_end, jnp.full((1,), n_blocks * m, I32)]).astype(I32)
    xs = _dispatch_call(slots_flat, fill_lo, fill_hi, h2.reshape(t, d // 2), n_blocks * m)
    row_end = (pad_start + counts).astype(I32)
    nonempty = counts > 0
    first = jnp.logical_and(block_row0 == pad_start[block_e], block_row0 < pad_end[-1]).astype(I32)
    ordinal = jnp.cumsum(nonempty.astype(I32)) - 1
    par = (ordinal[block_e] & 1).astype(I32)
    cand = jnp.where(nonempty, expert_ids, N_EXPERTS)
    later = jnp.concatenate([lax.cummin(cand[::-1])[::-1][1:], jnp.full((1,), N_EXPERTS, I32)])
    nxt = jnp.where(later < N_EXPERTS, later, -1)[block_e].astype(I32)
    rows_valid = jnp.clip(row_end[block_e] - block_row0, 0, m).astype(I32)
    ys = _experts_call(block_e, first, nxt, par, rows_valid, n_valid, xs, w_gate_e, w_up_e, w_down_e)
    return _combine_call(slots_flat, x1, h2, gw_tok, mod3, w_gate_s.astype(BF16), w_up_s.astype(BF16),
                         w_down_s.astype(BF16), ys)


def kernel(x, c, w_ada, b_ada, norm1_w, w_in, rwkv_shift, w0_f, w_up_f, w0_b, w_up_b, a0_f, a_up_f, a0_b,
           a_up_b, g_up, k_k, k_a, r_k, ln_x_w, ln_x_b, q_norm_w, k_norm_w, rel_bias, w_out, norm2_w,
           w_router, router_bias, w_gate_e, w_up_e, w_down_e, w_gate_s, w_up_s, w_down_s):
    bn, sn, d = x.shape
    depth = w_ada.shape[0]
    for l in range(depth):
        c_pad = jnp.pad(c, ((0, 8 - bn % 8 if bn % 8 else 0), (0, 0)))
        mod = _mod_call(c_pad, w_ada[l], b_ada[l].reshape(1, -1))[:bn]
        mod3 = mod.reshape(bn, 6, d)
        rwkv_in = rwkv_shift.shape[-1]
        pad_cols = N_SHIFT_BLOCKS * LANES - rwkv_in
        w_in_r = jnp.concatenate([w_in[l][:, :rwkv_in], jnp.zeros((d, pad_cols), w_in.dtype),
                                  w_in[l][:, rwkv_in:]], axis=1).astype(BF16)
        p = _proj_call(x, mod3, norm1_w[l].reshape(1, d), w_in_r)
        y_rwkv, y_na = _mixing_stage(p, rwkv_shift[l], w0_f[l], w_up_f[l], w0_b[l], w_up_b[l], a0_f[l],
                                     a_up_f[l], a0_b[l], a_up_b[l], g_up[l], k_k[l], k_a[l], r_k[l],
                                     ln_x_w[l], ln_x_b[l], q_norm_w[l], k_norm_w[l], rel_bias[l])
        d_rwkv = y_rwkv.shape[-1]
        w_o = w_out[l].astype(BF16)
        x1, h2, lg_t = _outproj_call(y_rwkv, y_na, x, mod3, norm2_w[l].reshape(1, d), w_o[:d_rwkv], w_o[d_rwkv:],
                                     w_router[l].T)
        x = _moe_stage(x1, h2, lg_t, mod3, router_bias[l], w_gate_e[l], w_up_e[l], w_down_e[l],
                       w_gate_s[l], w_up_s[l], w_down_s[l])
    return x
```

```python
import functools

import jax
import jax.numpy as jnp
import numpy as np
from jax import lax
from jax.experimental import pallas as pl
from jax.experimental.pallas import tpu as pltpu

F32 = jnp.float32
BF16 = jnp.bfloat16
I32 = jnp.int32
HI = lax.Precision.HIGHEST

LANES = 128
HEAD_DIM = 64
HEADS_PER_BLOCK = LANES // HEAD_DIM
GRID_W = 64
WIN_R = 8
WIN_C = 16
DECAY_LORA = 64
AAA_LORA = 64
GATE_LORA = 160
N_EXPERTS = 64
TOP_K = 6
SLOT_STRIDE = 8
N_GROUPS = 8
TOPK_GROUPS = 4
ROUTED_SCALE = 2.5
DISPATCH_BLOCK = 512
NORM_EPS = 1e-6
GN_EPS = 64e-5
L2_EPS = 1e-12
NEG_BIG = -1e30
WKV_CHUNK = 64
VMEM_LIMIT = 56 * 1024 * 1024

NT_DIMS = (((1,), (1,)), ((), ()))
TN_DIMS = (((0,), (0,)), ((), ()))


def _cparams(sem):
    return pltpu.CompilerParams(dimension_semantics=sem, vmem_limit_bytes=VMEM_LIMIT)


def _sigmoid(x):
    return 1.0 / (1.0 + jnp.exp(-x))


def _silu(x):
    return x * _sigmoid(x)


U32 = jnp.uint32
HI_HALF_MASK = 0xFFFF0000


def _pack_halves(x):
    n = x.shape[-1] // 2
    bits = lax.bitcast_convert_type(x.astype(BF16).astype(F32), U32)
    return (bits[:, :n] >> 16) | bits[:, n:]


def _unpack_halves(w):
    lo = lax.bitcast_convert_type(w << 16, F32)
    hi = lax.bitcast_convert_type(w & U32(HI_HALF_MASK), F32)
    return lo, hi


def _head_sum_matrix():
    a = np.arange(LANES) // HEAD_DIM
    return jnp.asarray((a[:, None] == a[None, :]).astype(np.float32))


def _head_sums_split(x, mbd):
    m16 = mbd.astype(BF16)
    hi = x.astype(BF16)
    lo = (x - hi.astype(F32)).astype(BF16)
    return jnp.dot(hi, m16, preferred_element_type=F32) + jnp.dot(lo, m16, preferred_element_type=F32)


def _mod_kernel(c_ref, w_ref, b_ref, o_ref):
    c = c_ref[...]
    o_ref[...] = jnp.dot(_silu(c), w_ref[...], preferred_element_type=F32, precision=HI) + b_ref[...]


def _mod_call(c_pad, w_ada, b_ada):
    rows, d = c_pad.shape
    n = w_ada.shape[1]
    tn = 1024
    return pl.pallas_call(
        _mod_kernel,
        grid=(n // tn,),
        in_specs=[pl.BlockSpec((rows, d), lambda j: (0, 0)),
                  pl.BlockSpec((d, tn), lambda j: (0, j)),
                  pl.BlockSpec((1, tn), lambda j: (0, j))],
        out_specs=pl.BlockSpec((rows, tn), lambda j: (0, j)),
        out_shape=jax.ShapeDtypeStruct((rows, n), F32),
        compiler_params=_cparams(("parallel",)),
        name="mod",
    )(c_pad, w_ada, b_ada)


def _proj_kernel(x_ref, mod_ref, nw_ref, w_ref, o_ref, h_sc, *, nq):
    @pl.when(pl.program_id(2) == 0)
    def _():
        x = x_ref[0]
        ms = jnp.mean(x * x, axis=-1, keepdims=True)
        y = x * lax.rsqrt(ms + NORM_EPS) * nw_ref[...]
        h_sc[...] = (y * (1.0 + mod_ref[0, 1:2, :]) + mod_ref[0, 0:1, :]).astype(BF16)

    acc = jnp.dot(h_sc[...], w_ref[...], preferred_element_type=F32)
    for q in range(nq):
        o_ref[0, q] = acc[:, q * LANES:(q + 1) * LANES].astype(o_ref.dtype)


def _proj_call(x, mod3, norm_w, w_in_r, tm=1024, tn=1664):
    b, s, d = x.shape
    n = w_in_r.shape[1]
    nq = tn // LANES
    return pl.pallas_call(
        functools.partial(_proj_kernel, nq=nq),
        grid=(b, s // tm, n // tn),
        in_specs=[pl.BlockSpec((1, tm, d), lambda bi, i, j: (bi, i, 0)),
                  pl.BlockSpec((1, 6, d), lambda bi, i, j: (bi, 0, 0)),
                  pl.BlockSpec((1, d), lambda bi, i, j: (0, 0)),
                  pl.BlockSpec((d, tn), lambda bi, i, j: (0, j))],
        out_specs=pl.BlockSpec((1, nq, tm, LANES), lambda bi, i, j: (bi, j, i, 0)),
        out_shape=jax.ShapeDtypeStruct((b, n // LANES, s, LANES), BF16),
        scratch_shapes=[pltpu.VMEM((tm, d), BF16)],
        compiler_params=_cparams(("parallel", "parallel", "arbitrary")),
        name="proj",
    )(x, mod3, norm_w, w_in_r)


RB_R, RB_V, RB_KK, RB_G, RB_KF, RB_KB, RB_AF, RB_AB, RB_BONUS = range(9)
VEC_W0F, VEC_W0B, VEC_A0F, VEC_A0B, VEC_KK, VEC_KA, VEC_RK = range(7)
N_RKV_BLOCKS = 24
N_SHIFT_BLOCKS = 28


def _softplus(u):
    return jnp.maximum(u, 0.0) + jnp.log(1.0 + jnp.exp(-jnp.abs(u)))


def _prep_kernel(p_ref, pp_ref, pn_ref, sw_ref, wupf_ref, wupb_ref, aupf_ref, aupb_ref, gup_ref,
                 vec_ref, mbd_ref, rb_ref, lw_ref, *, tb, n_t, nhp):
    i = pl.program_id(1)
    row = lax.broadcasted_iota(I32, (tb, LANES), 0)
    has_prev = i > 0
    has_next = i < n_t - 1
    halo = pp_ref.shape[2]

    def shifted(q):
        cur = p_ref[0, q].astype(F32)
        prev_row = jnp.where(has_prev, pp_ref[0, q, halo - 1:halo, :].astype(F32), 0.0)
        next_row = jnp.where(has_next, pn_ref[0, q, 0:1, :].astype(F32), 0.0)
        up = jnp.where(row == 0, prev_row, pltpu.roll(cur, 1, 0))
        dn = jnp.where(row == tb - 1, next_row, pltpu.roll(cur, tb - 1, 0))
        return sw_ref[0, q:q + 1, :] * up + sw_ref[1, q:q + 1, :] * cur + sw_ref[2, q:q + 1, :] * dn

    mbd = mbd_ref[...]
    t_wd = jnp.tanh(shifted(N_RKV_BLOCKS)).astype(BF16)
    z_ad = shifted(N_RKV_BLOCKS + 1).astype(BF16)
    s_gd = jnp.concatenate([_sigmoid(shifted(N_RKV_BLOCKS + 2)),
                            _sigmoid(shifted(N_RKV_BLOCKS + 3))], axis=1).astype(BF16)

    def vec(v, hp):
        return vec_ref[v, hp:hp + 1, :]

    for hp in range(nhp):
        sl = slice(hp * LANES, (hp + 1) * LANES)
        r = shifted(hp)
        k = shifted(nhp + hp)
        v = shifted(2 * nhp + hp)
        g = jnp.dot(s_gd, gup_ref[:, sl], preferred_element_type=F32)
        kk0 = k * vec(VEC_KK, hp)
        ss = _head_sums_split(kk0 * kk0, mbd)
        kk = kk0 * lax.rsqrt(jnp.maximum(ss, L2_EPS))
        kdirs = []
        for wup_ref, aup_ref, v_w0, v_a0, q_k, q_a, lw_slot in (
                (wupf_ref, aupf_ref, VEC_W0F, VEC_A0F, RB_KF, RB_AF, 0),
                (wupb_ref, aupb_ref, VEC_W0B, VEC_A0B, RB_KB, RB_AB, 1)):
            wl = vec(v_w0, hp) + jnp.dot(t_wd, wup_ref[:, sl], preferred_element_type=F32)
            w_log = -_softplus(-wl) - 0.5
            lw_ref[0, lw_slot * nhp + hp] = -jnp.exp(w_log)
            ag = _sigmoid(vec(v_a0, hp) + jnp.dot(z_ad, aup_ref[:, sl], preferred_element_type=F32))
            kd = k * (1.0 + (ag - 1.0) * vec(VEC_KA, hp))
            rb_ref[0, q_k * nhp + hp] = kd.astype(BF16)
            rb_ref[0, q_a * nhp + hp] = ag.astype(BF16)
            kdirs.append(kd)
        bonus = _head_sums_split(r * kdirs[0] * vec(VEC_RK, hp), mbd) * v
        rb_ref[0, RB_R * nhp + hp] = r.astype(BF16)
        rb_ref[0, RB_V * nhp + hp] = v.astype(BF16)
        rb_ref[0, RB_KK * nhp + hp] = kk.astype(BF16)
        rb_ref[0, RB_G * nhp + hp] = g.astype(BF16)
        rb_ref[0, RB_BONUS * nhp + hp] = bonus.astype(BF16)


def _prep_call(p, sw, wupf, wupb, aupf, aupb, gup, vecs, mbd, nhp, tb=256):
    b, _, s, _ = p.shape
    n_t = s // tb
    halo = 16
    hb = tb // halo
    full = lambda a: pl.BlockSpec(a.shape, lambda bi, i: (0,) * a.ndim)
    return pl.pallas_call(
        functools.partial(_prep_kernel, tb=tb, n_t=n_t, nhp=nhp),
        grid=(b, n_t),
        in_specs=[pl.BlockSpec((1, N_SHIFT_BLOCKS, tb, LANES), lambda bi, i: (bi, 0, i, 0)),
                  pl.BlockSpec((1, N_SHIFT_BLOCKS, halo, LANES),
                               lambda bi, i: (bi, 0, jnp.maximum(i * hb - 1, 0), 0)),
                  pl.BlockSpec((1, N_SHIFT_BLOCKS, halo, LANES),
                               lambda bi, i: (bi, 0, jnp.minimum((i + 1) * hb, s // halo - 1), 0)),
                  full(sw), full(wupf), full(wupb), full(aupf), full(aupb), full(gup), full(vecs), full(mbd)],
        out_specs=[pl.BlockSpec((1, 9 * nhp, tb, LANES), lambda bi, i: (bi, 0, i, 0)),
                   pl.BlockSpec((1, 2 * nhp, tb, LANES), lambda bi, i: (bi, 0, i, 0))],
        out_shape=[jax.ShapeDtypeStruct((b, 9 * nhp, s, LANES), BF16),
                   jax.ShapeDtypeStruct((b, 2 * nhp, s, LANES), F32)],
        compiler_params=_cparams(("parallel", "parallel")),
        name="prep",
    )(p, p, p, sw, wupf, wupb, aupf, aupb, gup, vecs, mbd)


def _wkv_kernel(r_ref, v_ref, kk_ref, kf_ref, af_ref, lf_ref, rr_ref, vr_ref, kkr_ref, kb_ref, ab_ref, lb_ref,
                yf_ref, yb_ref, s_sc, *, L, nhp):
    @pl.when(pl.program_id(1) == 0)
    def _():
        s_sc[...] = jnp.zeros(s_sc.shape, F32)

    n2 = HEADS_PER_BLOCK * L
    ri = lax.broadcasted_iota(I32, (n2, LANES), 0)
    ci = lax.broadcasted_iota(I32, (n2, LANES), 1)
    head_match = (ri // L) == (ci // HEAD_DIM)
    si = lax.broadcasted_iota(I32, (n2, n2), 0)
    sj = lax.broadcasted_iota(I32, (n2, n2), 1)
    st, ss = si % L, sj % L
    eye = (si == sj).astype(F32)
    strict = {False: st > ss, True: st < ss}
    incl = {False: st >= ss, True: st < ss}

    def stack(x):
        return jnp.where(head_match, jnp.concatenate([x] * HEADS_PER_BLOCK, axis=0), 0.0)

    chains = [(False, hp, r_ref, v_ref, kk_ref, kf_ref, af_ref, lf_ref) for hp in range(nhp)]
    chains += [(True, hp, rr_ref, vr_ref, kkr_ref, kb_ref, ab_ref, lb_ref) for hp in range(nhp)]
    n = len(chains)
    lws = [ch[7][0, ch[1]] for ch in chains]
    t_row = lax.broadcasted_iota(I32, (L, LANES), 0)

    def cumsum_time(x, rev):
        sh = 1
        while sh < L:
            if rev:
                x = x + jnp.where(t_row < L - sh, pltpu.roll(x, L - sh, 0), 0.0)
            else:
                x = x + jnp.where(t_row >= sh, pltpu.roll(x, sh, 0), 0.0)
            sh *= 2
        return x

    c_ins = [cumsum_time(lw, ch[0]) for ch, lw in zip(chains, lws)]
    lhs_l, rhs_l, vs_l, bk_l, pl_l = [], [], [], [], []
    for i, (rev, hp, rr, vr, kkr, kdr, agr, _) in enumerate(chains):
        r = rr[0, hp].astype(F32)
        kk = kkr[0, hp].astype(F32)
        kd = kdr[0, hp].astype(F32)
        ag = agr[0, hp].astype(F32)
        c_in = c_ins[i]
        e_ex = jnp.exp(c_in - lws[i])
        e_inv = jnp.exp(-c_in)
        if rev:
            e_r = e_ex
            p_last = jnp.exp(c_in[0:1])
        else:
            e_r = jnp.exp(c_in)
            p_last = jnp.exp(c_in[L - 1:L])
        b_s = stack(kk * ag * e_inv)
        k_s = stack(kd * e_inv)
        lhs_l.append(jnp.concatenate([stack(-kk * e_ex), stack(r * e_r)], axis=0).astype(BF16))
        rhs_l.append(jnp.concatenate([b_s, k_s], axis=0).astype(BF16))
        bk_l.append(jnp.concatenate([b_s * p_last, k_s * p_last], axis=0).astype(BF16))
        vs_l.append(stack(vr[0, hp].astype(F32)).astype(BF16))
        pl_l.append(p_last)
    g_l = [lax.dot_general(lhs_l[i], rhs_l[i], NT_DIMS, preferred_element_type=F32) for i in range(n)]
    s_old = [s_sc[i] for i in range(n)]
    ls_l = [lax.dot_general(lhs_l[i], s_old[i].astype(BF16), NT_DIMS, preferred_element_type=F32)
            for i in range(n)]
    a_ab = [jnp.where(strict[ch[0]], g[0:n2, 0:n2], 0.0) for ch, g in zip(chains, g_l)]
    a_ak = [jnp.where(strict[ch[0]], g[0:n2, n2:2 * n2], 0.0).astype(BF16) for ch, g in zip(chains, g_l)]
    a_r = [jnp.concatenate([jnp.where(incl[ch[0]], g[n2:2 * n2, 0:n2], 0.0),
                            jnp.where(incl[ch[0]], g[n2:2 * n2, n2:2 * n2], 0.0)], axis=1).astype(BF16)
           for ch, g in zip(chains, g_l)]
    x_l = [ls_l[i][0:n2] + jnp.dot(a_ak[i], vs_l[i], preferred_element_type=F32) for i in range(n)]
    rounds = L.bit_length() - 1
    pw = [jnp.dot(a.astype(BF16), a.astype(BF16), preferred_element_type=F32) for a in a_ab]
    t_inv = [eye + a for a in a_ab]
    for j in range(1, rounds):
        pb = [p.astype(BF16) for p in pw]
        if j + 1 < rounds:
            both = [jnp.dot(jnp.concatenate([pb[i], t_inv[i].astype(BF16)], axis=0), pb[i],
                            preferred_element_type=F32) for i in range(n)]
            pw = [m[0:n2] for m in both]
            t_inv = [t + m[n2:2 * n2] for t, m in zip(t_inv, both)]
        else:
            t_inv = [t + jnp.dot(t.astype(BF16), pb[i], preferred_element_type=F32)
                     for i, t in enumerate(t_inv)]
    u_l = [jnp.dot(t_inv[i].astype(BF16), x_l[i].astype(BF16), preferred_element_type=F32) for i in range(n)]
    uv_l = [jnp.concatenate([jnp.where(head_match, u_l[i], 0.0).astype(BF16), vs_l[i]], axis=0)
            for i in range(n)]
    y_l = [ls_l[i][n2:2 * n2] + jnp.dot(a_r[i], uv_l[i], preferred_element_type=F32) for i in range(n)]
    upd_l = [lax.dot_general(uv_l[i], bk_l[i], TN_DIMS, preferred_element_type=F32) for i in range(n)]
    for i, (rev, hp, *_) in enumerate(chains):
        y = y_l[i]
        out = y[0:L]
        for h in range(1, HEADS_PER_BLOCK):
            out = out + y[h * L:(h + 1) * L]
        (yb_ref if rev else yf_ref)[0, hp] = out
        s_sc[i] = s_old[i] * pl_l[i] + upd_l[i]


def _wkv_call(rb, lw, nhp, L=WKV_CHUNK):
    b, _, s, _ = rb.shape
    nc = s // L
    fwd = lambda q: pl.BlockSpec((1, nhp, L, LANES), lambda bi, c: (bi, q, c, 0))
    bwd = lambda q: pl.BlockSpec((1, nhp, L, LANES), lambda bi, c: (bi, q, nc - 1 - c, 0))
    y_shape = jax.ShapeDtypeStruct((b, nhp, s, LANES), F32)
    return pl.pallas_call(
        functools.partial(_wkv_kernel, L=L, nhp=nhp),
        grid=(b, nc),
        in_specs=[fwd(RB_R), fwd(RB_V), fwd(RB_KK), fwd(RB_KF), fwd(RB_AF), fwd(0),
                  bwd(RB_R), bwd(RB_V), bwd(RB_KK), bwd(RB_KB), bwd(RB_AB), bwd(1)],
        out_specs=[fwd(0), bwd(0)],
        out_shape=[y_shape, y_shape],
        scratch_shapes=[pltpu.VMEM((2 * nhp, LANES, LANES), F32)],
        compiler_params=_cparams(("parallel", "arbitrary")),
        name="wkv",
    )(rb, rb, rb, rb, rb, lw, rb, rb, rb, rb, rb, lw)


def _post_kernel(yf_ref, yb_ref, bonus_ref, g_ref, lnw_ref, lnb_ref, mbd_ref, o_ref, *, nhp):
    mbd = mbd_ref[...]
    inv_n = 1.0 / HEAD_DIM
    for hp in range(nhp):
        y = yf_ref[0, hp] + yb_ref[0, hp]
        mu = _head_sums_split(y, mbd) * inv_n
        d = y - mu
        var = _head_sums_split(d * d, mbd) * inv_n
        yn = d * lax.rsqrt(var + GN_EPS) * lnw_ref[hp:hp + 1, :] + lnb_ref[hp:hp + 1, :]
        out = (yn + bonus_ref[0, hp].astype(F32)) * g_ref[0, hp].astype(F32)
        o_ref[0, :, hp * LANES:(hp + 1) * LANES] = out.astype(o_ref.dtype)


def _post_call(yf, yb, rb, lnw, lnb, mbd, nhp, tb=256):
    b, _, s, _ = yf.shape
    yblk = pl.BlockSpec((1, nhp, tb, LANES), lambda bi, i: (bi, 0, i, 0))
    rblk = lambda q: pl.BlockSpec((1, nhp, tb, LANES), lambda bi, i: (bi, q, i, 0))
    full = lambda a: pl.BlockSpec(a.shape, lambda bi, i: (0,) * a.ndim)
    return pl.pallas_call(
        functools.partial(_post_kernel, nhp=nhp),
        grid=(b, s // tb),
        in_specs=[yblk, yblk, rblk(RB_BONUS), rblk(RB_G), full(lnw), full(lnb), full(mbd)],
        out_specs=pl.BlockSpec((1, tb, nhp * LANES), lambda bi, i: (bi, i, 0)),
        out_shape=jax.ShapeDtypeStruct((b, s, nhp * LANES), BF16),
        compiler_params=_cparams(("parallel", "parallel")),
        name="post",
    )(yf, yb, rb, rb, lnw, lnb, mbd)


def _na_kernel(q_ref, k_ref, v_ref, bias_ref, qw_ref, kw_ref, mbd_ref, o_ref, kn_sc, *, rq, rows):
    rbi = pl.program_id(2)
    mbd = mbd_ref[...]
    inv_n = 1.0 / HEAD_DIM
    span = WIN_R * GRID_W
    s_len = kn_sc.shape[0]

    @pl.when(rbi == 0)
    def _():
        def body(c, carry):
            off = pl.multiple_of(c * span, span)
            k = k_ref[0, 0, pl.ds(off, span), :].astype(F32)
            ms = _head_sums_split(k * k, mbd) * inv_n
            kn_sc[pl.ds(off, span), :] = (k * lax.rsqrt(ms + NORM_EPS) * kw_ref[...]).astype(BF16)
            return carry
        lax.fori_loop(0, s_len // span, body, 0)

    ri = lax.broadcasted_iota(I32, (HEADS_PER_BLOCK * GRID_W, LANES), 0)
    ci = lax.broadcasted_iota(I32, (HEADS_PER_BLOCK * GRID_W, LANES), 1)
    head_match = (ri // GRID_W) == (ci // HEAD_DIM)
    lane = lax.broadcasted_iota(I32, (GRID_W, LANES), 1)
    scale = HEAD_DIM ** -0.5
    q_all = q_ref[0, 0].astype(F32)
    ms = _head_sums_split(q_all * q_all, mbd) * inv_n
    qn_all = q_all * lax.rsqrt(ms + NORM_EPS) * qw_ref[...] * scale
    starts, offs = [], []
    for qi in range(rq):
        i = rbi * rq + qi
        rs = jnp.clip(i - WIN_R // 2, 0, rows - WIN_R)
        offs.append(rs - i + WIN_R - 1)
        starts.append(pl.multiple_of(rs * GRID_W, GRID_W))
    qs_l = [jnp.where(head_match, jnp.concatenate([qn_all[qi * GRID_W:(qi + 1) * GRID_W]] * HEADS_PER_BLOCK,
                                                  axis=0), 0.0).astype(BF16) for qi in range(rq)]
    s_l = [lax.dot_general(qs_l[qi], kn_sc[pl.ds(starts[qi], span), :], NT_DIMS, preferred_element_type=F32)
           + bias_ref[offs[qi], 0] for qi in range(rq)]
    p_l, l_l = [], []
    for s in s_l:
        p = jnp.exp(s - jnp.max(s, axis=-1, keepdims=True))
        p_l.append(p.astype(BF16))
        l_l.append(jnp.sum(p, axis=-1, keepdims=True))
    o_l = [jnp.dot(p_l[qi], v_ref[0, 0, pl.ds(starts[qi], span), :], preferred_element_type=F32) / l_l[qi]
           for qi in range(rq)]
    for qi, o in enumerate(o_l):
        out = o[0:GRID_W]
        for h in range(1, HEADS_PER_BLOCK):
            out = jnp.where(lane // HEAD_DIM == h, o[h * GRID_W:(h + 1) * GRID_W], out)
        o_ref[0, qi * GRID_W:(qi + 1) * GRID_W, :] = out.astype(o_ref.dtype)


def _na_call(p, bias, qw, kw, mbd, q_blk0, nhp, rq=16):
    b, _, s, _ = p.shape
    rows = s // GRID_W
    rq = min(rq, rows)
    full = lambda a: pl.BlockSpec(a.shape, lambda bi, hp, r: (0,) * a.ndim)
    return pl.pallas_call(
        functools.partial(_na_kernel, rq=rq, rows=rows),
        grid=(b, nhp, rows // rq),
        in_specs=[pl.BlockSpec((1, 1, rq * GRID_W, LANES), lambda bi, hp, r: (bi, q_blk0 + hp, r, 0)),
                  pl.BlockSpec((1, 1, s, LANES), lambda bi, hp, r: (bi, q_blk0 + nhp + hp, 0, 0)),
                  pl.BlockSpec((1, 1, s, LANES), lambda bi, hp, r: (bi, q_blk0 + 2 * nhp + hp, 0, 0)),
                  pl.BlockSpec((WIN_R, 1, HEADS_PER_BLOCK * GRID_W, WIN_R * GRID_W),
                               lambda bi, hp, r: (0, hp, 0, 0)),
                  full(qw), full(kw), full(mbd)],
        out_specs=pl.BlockSpec((1, rq * GRID_W, LANES), lambda bi, hp, r: (bi, r, hp)),
        out_shape=jax.ShapeDtypeStruct((b, s, nhp * LANES), BF16),
        scratch_shapes=[pltpu.VMEM((s, LANES), BF16)],
        compiler_params=_cparams(("parallel", "parallel", "arbitrary")),
        name="na",
    )(p, p, p, bias, qw, kw, mbd)


def _na_bias_table(rel_bias):
    h = rel_bias.shape[0]
    qc = np.arange(GRID_W)
    kc = np.arange(GRID_W)
    win0 = np.clip(qc - WIN_C // 2, 0, GRID_W - WIN_C)
    valid = (kc[None, :] >= win0[:, None]) & (kc[None, :] < win0[:, None] + WIN_C)
    coff = np.clip(kc[None, :] - qc[:, None] + WIN_C - 1, 0, 2 * WIN_C - 2)
    pick = jnp.asarray((coff[None] == np.arange(2 * WIN_C - 1)[:, None, None]).astype(np.float32))
    t = jnp.einsum('hrc,cqk->hrqk', rel_bias.astype(F32), pick, precision=HI)
    t = jnp.stack([t[:, o:o + WIN_R] for o in range(WIN_R)], axis=1)
    t = jnp.where(valid[None, None, None], t, NEG_BIG)
    t = jnp.transpose(t, (1, 0, 3, 2, 4))
    return t.reshape(WIN_R, h // HEADS_PER_BLOCK, HEADS_PER_BLOCK * GRID_W, WIN_R * GRID_W)


def _outproj_kernel(yr_ref, yn_ref, x_ref, mod_ref, nw_ref, w1_ref, w2_ref, wr_ref, x1_ref, h2_ref, h2row_ref,
                    lg_ref):
    tm = x_ref.shape[1]
    sub = LANES
    tiles = [slice(i * sub, (i + 1) * sub) for i in range(tm // sub)]
    accs = [jnp.dot(yr_ref[0, r], w1_ref[...], preferred_element_type=F32)
            + jnp.dot(yn_ref[0, r], w2_ref[...], preferred_element_type=F32) for r in tiles]
    h2s = []
    for r, acc in zip(tiles, accs):
        x1 = x_ref[0, r] + mod_ref[0, 2:3, :] * acc
        x1_ref[0, r] = x1
        ms = jnp.mean(x1 * x1, axis=-1, keepdims=True)
        y = x1 * lax.rsqrt(ms + NORM_EPS) * nw_ref[...]
        h2 = y * (1.0 + mod_ref[0, 4:5, :]) + mod_ref[0, 3:4, :]
        packed = _pack_halves(h2)
        h2row_ref[0, r] = packed
        for j in range(h2_ref.shape[2]):
            h2_ref[0, r, j, :] = packed[:, j * LANES:(j + 1) * LANES]
        h2s.append(h2)
    wr = wr_ref[...]
    wr_hi = wr.astype(BF16)
    wr_lo = (wr - wr_hi.astype(F32)).astype(BF16)
    for r, h2 in zip(tiles, h2s):
        h_hi = h2.astype(BF16)
        h_lo = (h2 - h_hi.astype(F32)).astype(BF16)
        lg_ref[:, r] = (lax.dot_general(wr_hi, h_hi, NT_DIMS, preferred_element_type=F32)
                        + lax.dot_general(wr_hi, h_lo, NT_DIMS, preferred_element_type=F32)
                        + lax.dot_general(wr_lo, h_hi, NT_DIMS, preferred_element_type=F32))


def _outproj_call(yr, yn, x, mod3, norm_w, w1, w2, wr_t, tm=512):
    b, s, d = x.shape
    dh = yr.shape[-1]
    ne = wr_t.shape[0]
    nt = s // tm
    full = lambda a: pl.BlockSpec(a.shape, lambda bi, i: (0,) * a.ndim)
    return pl.pallas_call(
        _outproj_kernel,
        grid=(b, nt),
        in_specs=[pl.BlockSpec((1, tm, dh), lambda bi, i: (bi, i, 0)),
                  pl.BlockSpec((1, tm, dh), lambda bi, i: (bi, i, 0)),
                  pl.BlockSpec((1, tm, d), lambda bi, i: (bi, i, 0)),
                  pl.BlockSpec((1, 6, d), lambda bi, i: (bi, 0, 0)),
                  full(norm_w), full(w1), full(w2), full(wr_t)],
        out_specs=[pl.BlockSpec((1, tm, d), lambda bi, i: (bi, i, 0)),
                   pl.BlockSpec((1, tm, d // 2 // LANES, LANES), lambda bi, i: (bi, i, 0, 0)),
                   pl.BlockSpec((1, tm, d // 2), lambda bi, i: (bi, i, 0)),
                   pl.BlockSpec((ne, tm), lambda bi, i: (0, bi * nt + i))],
        out_shape=[jax.ShapeDtypeStruct((b, s, d), F32),
                   jax.ShapeDtypeStruct((b, s, d // 2 // LANES, LANES), U32),
                   jax.ShapeDtypeStruct((b, s, d // 2), U32),
                   jax.ShapeDtypeStruct((ne, b * s), F32)],
        compiler_params=_cparams(("parallel", "parallel")),
        name="outproj",
    )(yr, yn, x, mod3, norm_w, w1, w2, wr_t)


def _first_argmax(x, idx, n):
    m = jnp.max(x, axis=0, keepdims=True)
    a = jnp.min(jnp.where(x == m, idx, n), axis=0, keepdims=True)
    return m, a


def _route_kernel(lg_ref, bias_ref, idx_ref, gw_ref, pos_ref, cnt_ref, *, tr):
    @pl.when(pl.program_id(0) == 0)
    def _():
        cnt_ref[...] = jnp.zeros(cnt_ref.shape, F32)

    gsz = N_EXPERTS // N_GROUPS
    scores = _sigmoid(lg_ref[...])
    biased = scores + bias_ref[:, 0:1]
    ig = lax.broadcasted_iota(I32, (gsz, tr), 0)
    grp_rows = []
    for g in range(N_GROUPS):
        blk = biased[g * gsz:(g + 1) * gsz]
        m1, a1 = _first_argmax(blk, ig, gsz)
        m2 = jnp.max(jnp.where(ig == a1, -jnp.inf, blk), axis=0, keepdims=True)
        grp_rows.append(m1 + m2)
    grp = jnp.concatenate(grp_rows, axis=0)
    ign = lax.broadcasted_iota(I32, (N_GROUPS, tr), 0)
    sel = jnp.zeros((N_GROUPS, tr), jnp.bool_)
    for _ in range(TOPK_GROUPS):
        _, a = _first_argmax(grp, ign, N_GROUPS)
        hit = ign == a
        sel = jnp.logical_or(sel, hit)
        grp = jnp.where(hit, -jnp.inf, grp)
    masked = jnp.concatenate(
        [jnp.where(sel[g:g + 1], biased[g * gsz:(g + 1) * gsz], -jnp.inf) for g in range(N_GROUPS)], axis=0)
    ie = lax.broadcasted_iota(I32, (N_EXPERTS, tr), 0)
    picks, pick_scores = [], []
    onehot = jnp.zeros((N_EXPERTS, tr), F32)
    for _ in range(TOP_K):
        _, a = _first_argmax(masked, ie, N_EXPERTS)
        hit = ie == a
        picks.append(a)
        pick_scores.append(jnp.sum(jnp.where(hit, scores, 0.0), axis=0, keepdims=True))
        onehot = onehot + hit.astype(F32)
        masked = jnp.where(hit, -jnp.inf, masked)
    total = pick_scores[0]
    for sc in pick_scores[1:]:
        total = total + sc
    t0 = lax.broadcasted_iota(I32, (tr, tr), 0)
    t1 = lax.broadcasted_iota(I32, (tr, tr), 1)
    before = (t0 < t1).astype(BF16)
    rank = jnp.dot(onehot.astype(BF16), before, preferred_element_type=F32) + cnt_ref[:, 0:1]
    zero_i = jnp.zeros((1, tr), I32)
    zero_f = jnp.zeros((1, tr), F32)
    for k in range(8):
        if k < TOP_K:
            idx_ref[k:k + 1, :] = picks[k]
            gw_ref[k:k + 1, :] = pick_scores[k] / total * ROUTED_SCALE
            pos = jnp.sum(jnp.where(ie == picks[k], rank, 0.0), axis=0, keepdims=True)
            pos_ref[k:k + 1, :] = pos.astype(I32)
        else:
            idx_ref[k:k + 1, :] = zero_i
            gw_ref[k:k + 1, :] = zero_f
            pos_ref[k:k + 1, :] = zero_i
    cnt_ref[...] = cnt_ref[...] + jnp.sum(onehot, axis=1, keepdims=True)


def _route_call(lg_t, bias2, tr=512):
    ne, t = lg_t.shape
    tr = min(tr, t)
    tok = pl.BlockSpec((8, tr), lambda i: (0, i))
    return pl.pallas_call(
        functools.partial(_route_kernel, tr=tr),
        grid=(t // tr,),
        in_specs=[pl.BlockSpec((ne, tr), lambda i: (0, i)),
                  pl.BlockSpec(bias2.shape, lambda i: (0, 0))],
        out_specs=[tok, tok, tok, pl.BlockSpec((ne, LANES), lambda i: (0, 0))],
        out_shape=[jax.ShapeDtypeStruct((8, t), I32), jax.ShapeDtypeStruct((8, t), F32),
                   jax.ShapeDtypeStruct((8, t), I32), jax.ShapeDtypeStruct((ne, LANES), F32)],
        compiler_params=_cparams(("arbitrary",)),
        name="route",
    )(lg_t, bias2)


FILL_CHUNKS = (1, 8, 64)
ZERO_ROWS = FILL_CHUNKS[-1]


def _dispatch_kernel(slot_ref, fill_lo_ref, fill_hi_ref, h_ref, xs_ref, zrow, sem, *, td):
    def row_copy(t, slot):
        return pltpu.make_async_copy(h_ref.at[t], xs_ref.at[slot], sem)

    def start(t, carry):
        for k in range(TOP_K):
            row_copy(t, slot_ref[t * SLOT_STRIDE + k]).start()
        return carry

    lax.fori_loop(0, td, start, 0)
    for k in range(TOP_K):
        pltpu.make_async_copy(h_ref, xs_ref.at[pl.ds(0, td)], sem).wait()

    @pl.when(pl.program_id(0) == pl.num_programs(0) - 1)
    def _():
        zrow[...] = jnp.zeros(zrow.shape, zrow.dtype)

        def zero_copy(row, n):
            if n > 1:
                row = pl.multiple_of(row, FILL_CHUNKS[1])
            return pltpu.make_async_copy(zrow.at[pl.ds(0, n)], xs_ref.at[pl.ds(row, n)], sem)

        def for_each_chunk(r, fn):
            lo, hi = fill_lo_ref[r], fill_hi_ref[r]
            for ci, n in enumerate(FILL_CHUNKS):
                if ci + 1 < len(FILL_CHUNKS):
                    count = jnp.minimum(((-lo) % FILL_CHUNKS[ci + 1]) // n, (hi - lo) // n)
                else:
                    count = (hi - lo) // n
                fn(lo, n, count)
                lo = lo + count * n

        def start_range(r, carry):
            def go(lo, n, count):
                lax.fori_loop(0, count, lambda i, c: (zero_copy(lo + i * n, n).start(), c)[1], 0)
            for_each_chunk(r, go)
            return carry

        def wait_range(r, carry):
            def go(lo, n, count):
                lax.fori_loop(0, count, lambda i, c: (zero_copy(0, n).wait(), c)[1], 0)
            for_each_chunk(r, go)
            return carry

        n_ranges = fill_lo_ref.shape[0]
        lax.fori_loop(0, n_ranges, start_range, 0)
        lax.fori_loop(0, n_ranges, wait_range, 0)


def _dispatch_call(slots_flat, fill_lo, fill_hi, h2, n_rows, td=256):
    t, nj, _ = h2.shape
    smem = pl.BlockSpec(memory_space=pltpu.SMEM)
    return pl.pallas_call(
        functools.partial(_dispatch_kernel, td=td),
        grid=(t // td,),
        in_specs=[pl.BlockSpec((td * SLOT_STRIDE,), lambda i: (i,), memory_space=pltpu.SMEM),
                  smem, smem,
                  pl.BlockSpec((td, nj, LANES), lambda i: (i, 0, 0))],
        out_specs=pl.BlockSpec(memory_space=pl.ANY),
        out_shape=jax.ShapeDtypeStruct((n_rows, nj, LANES), h2.dtype),
        scratch_shapes=[pltpu.VMEM((ZERO_ROWS, nj, LANES), h2.dtype), pltpu.SemaphoreType.DMA(())],
        compiler_params=_cparams(("arbitrary",)),
        name="dispatch",
    )(slots_flat, fill_lo, fill_hi, h2)


def _experts_kernel(be_ref, first_ref, nxt_ref, par_ref, rows_ref, nv_ref, xs_hbm, wg_hbm, wu_hbm, wd_hbm, ys_ref,
                    wg_buf, wu_buf, wd_buf, wg_sc, wu_sc, wd_sc, xpack, sem, isem):
    nj = xs_hbm.shape[1]
    b = pl.program_id(0)
    m = xpack.shape[1]
    half = wg_sc.shape[0] // 2
    oslot = lax.rem(b, 2)

    def in_copies(slot, row0):
        return [pltpu.make_async_copy(xs_hbm.at[pl.ds(row0, m), j],
                                      xpack.at[slot, :, pl.ds(j * LANES, LANES)], isem.at[slot]) for j in range(nj)]

    @pl.when(b == 0)
    def _():
        for c in in_copies(0, 0):
            c.start()

    @pl.when(b + 1 < nv_ref[0])
    def _():
        for c in in_copies(1 - oslot, pl.multiple_of((b + 1) * m, m)):
            c.start()

    def weight_copies(e, slot):
        return [pltpu.make_async_copy(src.at[e], dst.at[slot], sem.at[slot, i])
                for i, (src, dst) in enumerate(((wg_hbm, wg_buf), (wu_hbm, wu_buf), (wd_hbm, wd_buf)))]

    def compute(nrows):
        x_lo, x_hi = _unpack_halves(xpack[oslot, 0:nrows])
        x_lo, x_hi = x_lo.astype(BF16), x_hi.astype(BF16)
        g = (jnp.dot(x_lo, wg_sc[0:half], preferred_element_type=F32)
             + jnp.dot(x_hi, wg_sc[half:], preferred_element_type=F32))
        u = (jnp.dot(x_lo, wu_sc[0:half], preferred_element_type=F32)
             + jnp.dot(x_hi, wu_sc[half:], preferred_element_type=F32))
        a = (_silu(g) * u).astype(BF16)
        ys_ref[0:nrows] = _pack_halves(jnp.dot(a, wd_sc[...], preferred_element_type=F32))

    @pl.when(b < nv_ref[0])
    def _():
        for c in in_copies(oslot, 0):
            c.wait()

        @pl.when(b == 0)
        def _():
            for c in weight_copies(be_ref[0], par_ref[0]):
                c.start()

        @pl.when(first_ref[b] == 1)
        def _():
            slot = par_ref[b]
            for c in weight_copies(be_ref[b], slot):
                c.wait()

            @pl.when(nxt_ref[b] >= 0)
            def _():
                for c in weight_copies(nxt_ref[b], 1 - slot):
                    c.start()

            wg_sc[...] = wg_buf[slot].astype(BF16)
            wu_sc[...] = wu_buf[slot].astype(BF16)
            wd_sc[...] = wd_buf[slot].astype(BF16)

        @pl.when(rows_ref[b] > m // 2)
        def _():
            compute(m)

        @pl.when(rows_ref[b] <= m // 2)
        def _():
            compute(m // 2)
            ys_ref[m // 2:] = jnp.zeros((m - m // 2, ys_ref.shape[1]), ys_ref.dtype)

    @pl.when(b >= nv_ref[0])
    def _():
        ys_ref[...] = jnp.zeros(ys_ref.shape, ys_ref.dtype)


def _experts_call(block_e, first, nxt, par, rows_valid, n_valid, xs, wg, wu, wd):
    total, nj, _ = xs.shape
    d = wg.shape[1]
    m = DISPATCH_BLOCK
    ff = wg.shape[-1]

    hbm = pl.BlockSpec(memory_space=pl.ANY)
    grid_spec = pltpu.PrefetchScalarGridSpec(
        num_scalar_prefetch=6,
        grid=(total // m,),
        in_specs=[hbm, hbm, hbm, hbm],
        out_specs=pl.BlockSpec((m, nj * LANES), lambda b, *s: (b, 0)),
        scratch_shapes=[pltpu.VMEM((2, d, ff), F32), pltpu.VMEM((2, d, ff), F32), pltpu.VMEM((2, ff, d), F32),
                        pltpu.VMEM((d, ff), BF16), pltpu.VMEM((d, ff), BF16), pltpu.VMEM((ff, d), BF16),
                        pltpu.VMEM((2, m, nj * LANES), U32),
                        pltpu.SemaphoreType.DMA((2, 3)), pltpu.SemaphoreType.DMA((2,))],
    )
    return pl.pallas_call(
        _experts_kernel,
        grid_spec=grid_spec,
        out_shape=jax.ShapeDtypeStruct((total, nj * LANES), U32),
        compiler_params=_cparams(("arbitrary",)),
        name="experts",
    )(block_e, first, nxt, par, rows_valid, n_valid, xs, wg, wu, wd)


def _combine_kernel(slot_ref, x1_ref, h2_ref, gw_ref, mod_ref, wg_ref, wu_ref, wd_ref, ys_ref, o_ref,
                    buf, sem, *, tc):
    def row_copy(t, k, slot):
        return pltpu.make_async_copy(ys_ref.at[pl.ds(slot, 1)], buf.at[k, pl.ds(t, 1)], sem)

    def start(t, carry):
        for k in range(TOP_K):
            row_copy(t, k, slot_ref[t * SLOT_STRIDE + k]).start()
        return carry

    lax.fori_loop(0, tc, start, 0)
    half = wg_ref.shape[0] // 2
    h_lo, h_hi = _unpack_halves(h2_ref[0])
    h_lo, h_hi = h_lo.astype(BF16), h_hi.astype(BF16)
    g = (jnp.dot(h_lo, wg_ref[0:half], preferred_element_type=F32)
         + jnp.dot(h_hi, wg_ref[half:], preferred_element_type=F32))
    u = (jnp.dot(h_lo, wu_ref[0:half], preferred_element_type=F32)
         + jnp.dot(h_hi, wu_ref[half:], preferred_element_type=F32))
    acc = jnp.dot((_silu(g) * u).astype(BF16), wd_ref[...], preferred_element_type=F32)
    for k in range(TOP_K):
        pltpu.make_async_copy(ys_ref.at[pl.ds(0, tc)], buf.at[k], sem).wait()
    gw = gw_ref[...]
    acc_lo, acc_hi = acc[:, :half], acc[:, half:]
    for k in range(TOP_K):
        y_lo, y_hi = _unpack_halves(buf[k])
        acc_lo = acc_lo + gw[:, k:k + 1] * y_lo
        acc_hi = acc_hi + gw[:, k:k + 1] * y_hi
    gate = mod_ref[0, 5:6, :]
    o_ref[0, :, 0:half] = x1_ref[0, :, 0:half] + gate[:, :half] * acc_lo
    o_ref[0, :, half:] = x1_ref[0, :, half:] + gate[:, half:] * acc_hi


def _combine_call(slots_flat, x1, h2, gw_tok, mod3, wg, wu, wd, ys, tc=256):
    b, s, d = x1.shape
    nt = s // tc
    full = lambda a: pl.BlockSpec(a.shape, lambda bi, i: (0,) * a.ndim)
    return pl.pallas_call(
        functools.partial(_combine_kernel, tc=tc),
        grid=(b, nt),
        in_specs=[pl.BlockSpec((tc * SLOT_STRIDE,), lambda bi, i: (bi * nt + i,), memory_space=pltpu.SMEM),
                  pl.BlockSpec((1, tc, d), lambda bi, i: (bi, i, 0)),
                  pl.BlockSpec((1, tc, d // 2), lambda bi, i: (bi, i, 0)),
                  pl.BlockSpec((tc, 8), lambda bi, i: (bi * nt + i, 0)),
                  pl.BlockSpec((1, 6, d), lambda bi, i: (bi, 0, 0)),
                  full(wg), full(wu), full(wd),
                  pl.BlockSpec(memory_space=pl.ANY)],
        out_specs=pl.BlockSpec((1, tc, d), lambda bi, i: (bi, i, 0)),
        out_shape=jax.ShapeDtypeStruct((b, s, d), F32),
        scratch_shapes=[pltpu.VMEM((TOP_K, tc, d // 2), U32), pltpu.SemaphoreType.DMA(())],
        compiler_params=_cparams(("arbitrary", "arbitrary")),
        name="combine",
    )(slots_flat, x1, h2, gw_tok, mod3, wg, wu, wd, ys)


def _pad_rows(w, rows, at=0):
    out = jnp.zeros((rows, w.shape[1]), w.dtype)
    return out.at[at:at + w.shape[0]].set(w)


def _mixing_stage(p, rwkv_shift, w0_f, w_up_f, w0_b, w_up_b, a0_f, a_up_f, a0_b, a_up_b, g_up,
                  k_k, k_a, r_k, ln_x_w, ln_x_b, q_norm_w, k_norm_w, rel_bias):
    d_rwkv = w0_f.shape[0]
    nhp = d_rwkv // LANES
    mbd = _head_sum_matrix()
    n_shift = rwkv_shift.shape[1]
    sw = jnp.pad(rwkv_shift, ((0, 0), (0, N_SHIFT_BLOCKS * LANES - n_shift))).reshape(3, N_SHIFT_BLOCKS, LANES)
    wupf = _pad_rows(w_up_f, LANES, 0).astype(BF16)
    wupb = _pad_rows(w_up_b, LANES, DECAY_LORA).astype(BF16)
    aupf = _pad_rows(a_up_f, LANES, 0).astype(BF16)
    aupb = _pad_rows(a_up_b, LANES, AAA_LORA).astype(BF16)
    gup = _pad_rows(g_up, 2 * LANES, 0).astype(BF16)
    vecs = jnp.stack([w0_f, w0_b, a0_f, a0_b, k_k, k_a, r_k.reshape(-1)]).reshape(7, nhp, LANES)
    rb, lw = _prep_call(p, sw, wupf, wupb, aupf, aupb, gup, vecs, mbd, nhp)
    y_f, y_b = _wkv_call(rb, lw, nhp)
    y_rwkv = _post_call(y_f, y_b, rb, ln_x_w.reshape(nhp, LANES), ln_x_b.reshape(nhp, LANES), mbd, nhp)
    bias = _na_bias_table(rel_bias)
    qw = jnp.tile(q_norm_w, HEADS_PER_BLOCK).reshape(1, LANES)
    kw = jnp.tile(k_norm_w, HEADS_PER_BLOCK).reshape(1, LANES)
    y_na = _na_call(p, bias, qw, kw, mbd, N_SHIFT_BLOCKS, nhp)
    return y_rwkv, y_na


def _moe_stage(x1, h2_tiles, h2_rows, lg_t, mod3, router_bias, w_gate_e, w_up_e, w_down_e, w_gate_s, w_up_s,
               w_down_s):
    b, s, d = x1.shape
    t = b * s
    m = DISPATCH_BLOCK
    bias2 = jnp.broadcast_to(router_bias.astype(F32)[:, None], (N_EXPERTS, LANES))
    idx_t, gw_t, pos_t, cnt = _route_call(lg_t, bias2)
    counts = cnt[:, 0].astype(I32)
    padded = (counts + m - 1) // m * m
    pad_end = jnp.cumsum(padded)
    pad_start = pad_end - padded
    n_blocks = -(-(t * TOP_K + N_EXPERTS * m) // m)
    n_valid = (pad_end[-1] // m).astype(I32).reshape(1)
    block_row0 = jnp.arange(n_blocks, dtype=I32) * m
    block_e = jnp.minimum(jnp.sum((pad_end[None, :] <= block_row0[:, None]).astype(I32), axis=1),
                          N_EXPERTS - 1).astype(I32)
    assert m % ZERO_ROWS == 0
    expert_ids = jnp.arange(N_EXPERTS, dtype=I32)
    start_of = jnp.sum(jnp.where(idx_t[:, :, None] == expert_ids, pad_start, 0), axis=-1)
    slots = (start_of + pos_t).astype(I32)
    slots_flat = slots.T.reshape(-1)
    gw_tok = gw_t.T
    fill_lo = jnp.concatenate([pad_start + counts, pad_end[-1:]]).astype(I32)
    fill_hi = jnp.concatenate([pad_end, jnp.full((1,), n_blocks * m, I32)]).astype(I32)
    xs = _dispatch_call(slots_flat, fill_lo, fill_hi, h2_tiles.reshape(t, *h2_tiles.shape[2:]), n_blocks * m)
    row_end = (pad_start + counts).astype(I32)
    nonempty = counts > 0
    first = jnp.logical_and(block_row0 == pad_start[block_e], block_row0 < pad_end[-1]).astype(I32)
    ordinal = jnp.cumsum(nonempty.astype(I32)) - 1
    par = (ordinal[block_e] & 1).astype(I32)
    cand = jnp.where(nonempty, expert_ids, N_EXPERTS)
    later = jnp.concatenate([lax.cummin(cand[::-1])[::-1][1:], jnp.full((1,), N_EXPERTS, I32)])
    nxt = jnp.where(later < N_EXPERTS, later, -1)[block_e].astype(I32)
    rows_valid = jnp.clip(row_end[block_e] - block_row0, 0, m).astype(I32)
    ys = _experts_call(block_e, first, nxt, par, rows_valid, n_valid, xs, w_gate_e, w_up_e, w_down_e)
    return _combine_call(slots_flat, x1, h2_rows, gw_tok, mod3, w_gate_s.astype(BF16), w_up_s.astype(BF16),
                         w_down_s.astype(BF16), ys)


def kernel(x, c, w_ada, b_ada, norm1_w, w_in, rwkv_shift, w0_f, w_up_f, w0_b, w_up_b, a0_f, a_up_f, a0_b,
           a_up_b, g_up, k_k, k_a, r_k, ln_x_w, ln_x_b, q_norm_w, k_norm_w, rel_bias, w_out, norm2_w,
           w_router, router_bias, w_gate_e, w_up_e, w_down_e, w_gate_s, w_up_s, w_down_s):
    bn, sn, d = x.shape
    depth = w_ada.shape[0]
    for l in range(depth):
        c_pad = jnp.pad(c, ((0, 8 - bn % 8 if bn % 8 else 0), (0, 0)))
        mod = _mod_call(c_pad, w_ada[l], b_ada[l].reshape(1, -1))[:bn]
        mod3 = mod.reshape(bn, 6, d)
        rwkv_in = rwkv_shift.shape[-1]
        pad_cols = N_SHIFT_BLOCKS * LANES - rwkv_in
        w_in_r = jnp.concatenate([w_in[l][:, :rwkv_in], jnp.zeros((d, pad_cols), w_in.dtype),
                                  w_in[l][:, rwkv_in:]], axis=1).astype(BF16)
        p = _proj_call(x, mod3, norm1_w[l].reshape(1, d), w_in_r)
        y_rwkv, y_na = _mixing_stage(p, rwkv_shift[l], w0_f[l], w_up_f[l], w0_b[l], w_up_b[l], a0_f[l],
                                     a_up_f[l], a0_b[l], a_up_b[l], g_up[l], k_k[l], k_a[l], r_k[l],
                                     ln_x_w[l], ln_x_b[l], q_norm_w[l], k_norm_w[l], rel_bias[l])
        d_rwkv = y_rwkv.shape[-1]
        w_o = w_out[l].astype(BF16)
        x1, h2_tiles, h2_rows, lg_t = _outproj_call(y_rwkv, y_na, x, mod3, norm2_w[l].reshape(1, d), w_o[:d_rwkv],
                                                    w_o[d_rwkv:], w_router[l].T)
        x = _moe_stage(x1, h2_tiles, h2_rows, lg_t, mod3, router_bias[l], w_gate_e[l], w_up_e[l], w_down_e[l],
                       w_gate_s[l], w_up_s[l], w_down_s[l])
    return x
```

```python
import functools

import jax
import jax.numpy as jnp
import numpy as np
from jax import lax
from jax.experimental import pallas as pl
from jax.experimental.pallas import tpu as pltpu

F32 = jnp.float32
BF16 = jnp.bfloat16
I32 = jnp.int32
HI = lax.Precision.HIGHEST

LANES = 128
HEAD_DIM = 64
HEADS_PER_BLOCK = LANES // HEAD_DIM
GRID_W = 64
WIN_R = 8
WIN_C = 16
DECAY_LORA = 64
AAA_LORA = 64
GATE_LORA = 160
N_EXPERTS = 64
TOP_K = 6
SLOT_STRIDE = 8
N_GROUPS = 8
TOPK_GROUPS = 4
ROUTED_SCALE = 2.5
DISPATCH_BLOCK = 512
NORM_EPS = 1e-6
GN_EPS = 64e-5
L2_EPS = 1e-12
NEG_BIG = -1e30
WKV_CHUNK = 64
VMEM_LIMIT = 56 * 1024 * 1024

NT_DIMS = (((1,), (1,)), ((), ()))
TN_DIMS = (((0,), (0,)), ((), ()))


def _cparams(sem):
    return pltpu.CompilerParams(dimension_semantics=sem, vmem_limit_bytes=VMEM_LIMIT)


def _sigmoid(x):
    return 1.0 / (1.0 + jnp.exp(-x))


def _silu(x):
    return x * _sigmoid(x)


U32 = jnp.uint32
HI_HALF_MASK = 0xFFFF0000


def _pack_halves(x):
    n = x.shape[-1] // 2
    bits = lax.bitcast_convert_type(x.astype(BF16).astype(F32), U32)
    return (bits[:, :n] >> 16) | bits[:, n:]


def _unpack_halves(w):
    lo = lax.bitcast_convert_type(w << 16, F32)
    hi = lax.bitcast_convert_type(w & U32(HI_HALF_MASK), F32)
    return lo, hi


def _head_sum_matrix():
    a = np.arange(LANES) // HEAD_DIM
    return jnp.asarray((a[:, None] == a[None, :]).astype(np.float32))


def _head_sums_split(x, mbd):
    m16 = mbd.astype(BF16)
    hi = x.astype(BF16)
    lo = (x - hi.astype(F32)).astype(BF16)
    return jnp.dot(hi, m16, preferred_element_type=F32) + jnp.dot(lo, m16, preferred_element_type=F32)


def _mod_kernel(c_ref, w_ref, b_ref, o_ref):
    c = c_ref[...]
    o_ref[...] = jnp.dot(_silu(c), w_ref[...], preferred_element_type=F32, precision=HI) + b_ref[...]


def _mod_call(c_pad, w_ada, b_ada):
    rows, d = c_pad.shape
    n = w_ada.shape[1]
    tn = 1024
    return pl.pallas_call(
        _mod_kernel,
        grid=(n // tn,),
        in_specs=[pl.BlockSpec((rows, d), lambda j: (0, 0)),
                  pl.BlockSpec((d, tn), lambda j: (0, j)),
                  pl.BlockSpec((1, tn), lambda j: (0, j))],
        out_specs=pl.BlockSpec((rows, tn), lambda j: (0, j)),
        out_shape=jax.ShapeDtypeStruct((rows, n), F32),
        compiler_params=_cparams(("parallel",)),
        name="mod",
    )(c_pad, w_ada, b_ada)


def _proj_kernel(x_ref, mod_ref, nw_ref, w_ref, o_ref, h_sc, *, nq):
    @pl.when(pl.program_id(2) == 0)
    def _():
        x = x_ref[0]
        ms = jnp.mean(x * x, axis=-1, keepdims=True)
        y = x * lax.rsqrt(ms + NORM_EPS) * nw_ref[...]
        h_sc[...] = (y * (1.0 + mod_ref[0, 1:2, :]) + mod_ref[0, 0:1, :]).astype(BF16)

    acc = jnp.dot(h_sc[...], w_ref[...], preferred_element_type=F32)
    for q in range(nq):
        o_ref[0, q] = acc[:, q * LANES:(q + 1) * LANES].astype(o_ref.dtype)


def _proj_call(x, mod3, norm_w, w_in_r, tm=1024, tn=1664):
    b, s, d = x.shape
    n = w_in_r.shape[1]
    nq = tn // LANES
    return pl.pallas_call(
        functools.partial(_proj_kernel, nq=nq),
        grid=(b, s // tm, n // tn),
        in_specs=[pl.BlockSpec((1, tm, d), lambda bi, i, j: (bi, i, 0)),
                  pl.BlockSpec((1, 6, d), lambda bi, i, j: (bi, 0, 0)),
                  pl.BlockSpec((1, d), lambda bi, i, j: (0, 0)),
                  pl.BlockSpec((d, tn), lambda bi, i, j: (0, j))],
        out_specs=pl.BlockSpec((1, nq, tm, LANES), lambda bi, i, j: (bi, j, i, 0)),
        out_shape=jax.ShapeDtypeStruct((b, n // LANES, s, LANES), BF16),
        scratch_shapes=[pltpu.VMEM((tm, d), BF16)],
        compiler_params=_cparams(("parallel", "parallel", "arbitrary")),
        name="proj",
    )(x, mod3, norm_w, w_in_r)


RB_R, RB_V, RB_KK, RB_G, RB_KF, RB_KB, RB_AF, RB_AB, RB_BONUS = range(9)
VEC_W0F, VEC_W0B, VEC_A0F, VEC_A0B, VEC_KK, VEC_KA, VEC_RK = range(7)
N_RKV_BLOCKS = 24
N_SHIFT_BLOCKS = 28


def _softplus(u):
    return jnp.maximum(u, 0.0) + jnp.log(1.0 + jnp.exp(-jnp.abs(u)))


def _prep_kernel(p_ref, pp_ref, pn_ref, sw_ref, wupf_ref, wupb_ref, aupf_ref, aupb_ref, gup_ref,
                 vec_ref, mbd_ref, rb_ref, lw_ref, *, tb, n_t, nhp):
    i = pl.program_id(1)
    row = lax.broadcasted_iota(I32, (tb, LANES), 0)
    has_prev = i > 0
    has_next = i < n_t - 1
    halo = pp_ref.shape[2]

    def shifted(q):
        cur = p_ref[0, q].astype(F32)
        prev_row = jnp.where(has_prev, pp_ref[0, q, halo - 1:halo, :].astype(F32), 0.0)
        next_row = jnp.where(has_next, pn_ref[0, q, 0:1, :].astype(F32), 0.0)
        up = jnp.where(row == 0, prev_row, pltpu.roll(cur, 1, 0))
        dn = jnp.where(row == tb - 1, next_row, pltpu.roll(cur, tb - 1, 0))
        return sw_ref[0, q:q + 1, :] * up + sw_ref[1, q:q + 1, :] * cur + sw_ref[2, q:q + 1, :] * dn

    mbd = mbd_ref[...]
    t_wd = jnp.tanh(shifted(N_RKV_BLOCKS)).astype(BF16)
    z_ad = shifted(N_RKV_BLOCKS + 1).astype(BF16)
    s_gd = jnp.concatenate([_sigmoid(shifted(N_RKV_BLOCKS + 2)),
                            _sigmoid(shifted(N_RKV_BLOCKS + 3))], axis=1).astype(BF16)

    def vec(v, hp):
        return vec_ref[v, hp:hp + 1, :]

    for hp in range(nhp):
        sl = slice(hp * LANES, (hp + 1) * LANES)
        r = shifted(hp)
        k = shifted(nhp + hp)
        v = shifted(2 * nhp + hp)
        g = jnp.dot(s_gd, gup_ref[:, sl], preferred_element_type=F32)
        kk0 = k * vec(VEC_KK, hp)
        ss = _head_sums_split(kk0 * kk0, mbd)
        kk = kk0 * lax.rsqrt(jnp.maximum(ss, L2_EPS))
        kdirs = []
        for wup_ref, aup_ref, v_w0, v_a0, q_k, q_a, lw_slot in (
                (wupf_ref, aupf_ref, VEC_W0F, VEC_A0F, RB_KF, RB_AF, 0),
                (wupb_ref, aupb_ref, VEC_W0B, VEC_A0B, RB_KB, RB_AB, 1)):
            wl = vec(v_w0, hp) + jnp.dot(t_wd, wup_ref[:, sl], preferred_element_type=F32)
            w_log = -_softplus(-wl) - 0.5
            lw_ref[0, lw_slot * nhp + hp] = -jnp.exp(w_log)
            ag = _sigmoid(vec(v_a0, hp) + jnp.dot(z_ad, aup_ref[:, sl], preferred_element_type=F32))
            kd = k * (1.0 + (ag - 1.0) * vec(VEC_KA, hp))
            rb_ref[0, q_k * nhp + hp] = kd.astype(BF16)
            rb_ref[0, q_a * nhp + hp] = ag.astype(BF16)
            kdirs.append(kd)
        bonus = _head_sums_split(r * kdirs[0] * vec(VEC_RK, hp), mbd) * v
        rb_ref[0, RB_R * nhp + hp] = r.astype(BF16)
        rb_ref[0, RB_V * nhp + hp] = v.astype(BF16)
        rb_ref[0, RB_KK * nhp + hp] = kk.astype(BF16)
        rb_ref[0, RB_G * nhp + hp] = g.astype(BF16)
        rb_ref[0, RB_BONUS * nhp + hp] = bonus.astype(BF16)


def _prep_call(p, sw, wupf, wupb, aupf, aupb, gup, vecs, mbd, nhp, tb=256):
    b, _, s, _ = p.shape
    n_t = s // tb
    halo = 16
    hb = tb // halo
    full = lambda a: pl.BlockSpec(a.shape, lambda bi, i: (0,) * a.ndim)
    return pl.pallas_call(
        functools.partial(_prep_kernel, tb=tb, n_t=n_t, nhp=nhp),
        grid=(b, n_t),
        in_specs=[pl.BlockSpec((1, N_SHIFT_BLOCKS, tb, LANES), lambda bi, i: (bi, 0, i, 0)),
                  pl.BlockSpec((1, N_SHIFT_BLOCKS, halo, LANES),
                               lambda bi, i: (bi, 0, jnp.maximum(i * hb - 1, 0), 0)),
                  pl.BlockSpec((1, N_SHIFT_BLOCKS, halo, LANES),
                               lambda bi, i: (bi, 0, jnp.minimum((i + 1) * hb, s // halo - 1), 0)),
                  full(sw), full(wupf), full(wupb), full(aupf), full(aupb), full(gup), full(vecs), full(mbd)],
        out_specs=[pl.BlockSpec((1, 9 * nhp, tb, LANES), lambda bi, i: (bi, 0, i, 0)),
                   pl.BlockSpec((1, 2 * nhp, tb, LANES), lambda bi, i: (bi, 0, i, 0))],
        out_shape=[jax.ShapeDtypeStruct((b, 9 * nhp, s, LANES), BF16),
                   jax.ShapeDtypeStruct((b, 2 * nhp, s, LANES), F32)],
        compiler_params=_cparams(("parallel", "parallel")),
        name="prep",
    )(p, p, p, sw, wupf, wupb, aupf, aupb, gup, vecs, mbd)


def _wkv_kernel(r_ref, v_ref, kk_ref, kf_ref, af_ref, lf_ref, rr_ref, vr_ref, kkr_ref, kb_ref, ab_ref, lb_ref,
                yf_ref, yb_ref, s_sc, *, L, nhp):
    @pl.when(pl.program_id(1) == 0)
    def _():
        s_sc[...] = jnp.zeros(s_sc.shape, F32)

    n2 = HEADS_PER_BLOCK * L
    ri = lax.broadcasted_iota(I32, (n2, LANES), 0)
    ci = lax.broadcasted_iota(I32, (n2, LANES), 1)
    head_match = (ri // L) == (ci // HEAD_DIM)
    si = lax.broadcasted_iota(I32, (n2, n2), 0)
    sj = lax.broadcasted_iota(I32, (n2, n2), 1)
    st, ss = si % L, sj % L
    eye = (si == sj).astype(F32)
    strict = {False: st > ss, True: st < ss}
    incl = {False: st >= ss, True: st < ss}

    def stack(x):
        return jnp.where(head_match, jnp.concatenate([x] * HEADS_PER_BLOCK, axis=0), 0.0)

    chains = [(False, hp, r_ref, v_ref, kk_ref, kf_ref, af_ref, lf_ref) for hp in range(nhp)]
    chains += [(True, hp, rr_ref, vr_ref, kkr_ref, kb_ref, ab_ref, lb_ref) for hp in range(nhp)]
    n = len(chains)
    lws = [ch[7][0, ch[1]] for ch in chains]
    t_row = lax.broadcasted_iota(I32, (L, LANES), 0)

    def cumsum_time(x, rev):
        sh = 1
        while sh < L:
            if rev:
                x = x + jnp.where(t_row < L - sh, pltpu.roll(x, L - sh, 0), 0.0)
            else:
                x = x + jnp.where(t_row >= sh, pltpu.roll(x, sh, 0), 0.0)
            sh *= 2
        return x

    c_ins = [cumsum_time(lw, ch[0]) for ch, lw in zip(chains, lws)]
    lhs_l, rhs_l, vs_l, bk_l, pl_l = [], [], [], [], []
    for i, (rev, hp, rr, vr, kkr, kdr, agr, _) in enumerate(chains):
        r = rr[0, hp].astype(F32)
        kk = kkr[0, hp].astype(F32)
        kd = kdr[0, hp].astype(F32)
        ag = agr[0, hp].astype(F32)
        c_in = c_ins[i]
        e_ex = jnp.exp(c_in - lws[i])
        e_inv = jnp.exp(-c_in)
        if rev:
            e_r = e_ex
            p_last = jnp.exp(c_in[0:1])
        else:
            e_r = jnp.exp(c_in)
            p_last = jnp.exp(c_in[L - 1:L])
        b_s = stack(kk * ag * e_inv)
        k_s = stack(kd * e_inv)
        lhs_l.append(jnp.concatenate([stack(-kk * e_ex), stack(r * e_r)], axis=0).astype(BF16))
        rhs_l.append(jnp.concatenate([b_s, k_s], axis=0).astype(BF16))
        bk_l.append(jnp.concatenate([b_s * p_last, k_s * p_last], axis=0).astype(BF16))
        vs_l.append(stack(vr[0, hp].astype(F32)).astype(BF16))
        pl_l.append(p_last)
    g_l = [lax.dot_general(lhs_l[i], rhs_l[i], NT_DIMS, preferred_element_type=F32) for i in range(n)]
    s_old = [s_sc[i] for i in range(n)]
    ls_l = [lax.dot_general(lhs_l[i], s_old[i].astype(BF16), NT_DIMS, preferred_element_type=F32)
            for i in range(n)]
    a_ab = [jnp.where(strict[ch[0]], g[0:n2, 0:n2], 0.0) for ch, g in zip(chains, g_l)]
    a_ak = [jnp.where(strict[ch[0]], g[0:n2, n2:2 * n2], 0.0).astype(BF16) for ch, g in zip(chains, g_l)]
    a_r = [jnp.concatenate([jnp.where(incl[ch[0]], g[n2:2 * n2, 0:n2], 0.0),
                            jnp.where(incl[ch[0]], g[n2:2 * n2, n2:2 * n2], 0.0)], axis=1).astype(BF16)
           for ch, g in zip(chains, g_l)]
    x_l = [ls_l[i][0:n2] + jnp.dot(a_ak[i], vs_l[i], preferred_element_type=F32) for i in range(n)]
    rounds = L.bit_length() - 1
    pw = [jnp.dot(a.astype(BF16), a.astype(BF16), preferred_element_type=F32) for a in a_ab]
    t_inv = [eye + a for a in a_ab]
    for j in range(1, rounds):
        pb = [p.astype(BF16) for p in pw]
        if j + 1 < rounds:
            both = [jnp.dot(jnp.concatenate([pb[i], t_inv[i].astype(BF16)], axis=0), pb[i],
                            preferred_element_type=F32) for i in range(n)]
            pw = [m[0:n2] for m in both]
            t_inv = [t + m[n2:2 * n2] for t, m in zip(t_inv, both)]
        else:
            t_inv = [t + jnp.dot(t.astype(BF16), pb[i], preferred_element_type=F32)
                     for i, t in enumerate(t_inv)]
    u_l = [jnp.dot(t_inv[i].astype(BF16), x_l[i].astype(BF16), preferred_element_type=F32) for i in range(n)]
    uv_l = [jnp.concatenate([jnp.where(head_match, u_l[i], 0.0).astype(BF16), vs_l[i]], axis=0)
            for i in range(n)]
    y_l = [ls_l[i][n2:2 * n2] + jnp.dot(a_r[i], uv_l[i], preferred_element_type=F32) for i in range(n)]
    upd_l = [lax.dot_general(uv_l[i], bk_l[i], TN_DIMS, preferred_element_type=F32) for i in range(n)]
    for i, (rev, hp, *_) in enumerate(chains):
        y = y_l[i]
        out = y[0:L]
        for h in range(1, HEADS_PER_BLOCK):
            out = out + y[h * L:(h + 1) * L]
        (yb_ref if rev else yf_ref)[0, hp] = out
        s_sc[i] = s_old[i] * pl_l[i] + upd_l[i]


def _wkv_call(rb, lw, nhp, L=WKV_CHUNK):
    b, _, s, _ = rb.shape
    nc = s // L
    fwd = lambda q: pl.BlockSpec((1, nhp, L, LANES), lambda bi, c: (bi, q, c, 0))
    bwd = lambda q: pl.BlockSpec((1, nhp, L, LANES), lambda bi, c: (bi, q, nc - 1 - c, 0))
    y_shape = jax.ShapeDtypeStruct((b, nhp, s, LANES), F32)
    return pl.pallas_call(
        functools.partial(_wkv_kernel, L=L, nhp=nhp),
        grid=(b, nc),
        in_specs=[fwd(RB_R), fwd(RB_V), fwd(RB_KK), fwd(RB_KF), fwd(RB_AF), fwd(0),
                  bwd(RB_R), bwd(RB_V), bwd(RB_KK), bwd(RB_KB), bwd(RB_AB), bwd(1)],
        out_specs=[fwd(0), bwd(0)],
        out_shape=[y_shape, y_shape],
        scratch_shapes=[pltpu.VMEM((2 * nhp, LANES, LANES), F32)],
        compiler_params=_cparams(("parallel", "arbitrary")),
        name="wkv",
    )(rb, rb, rb, rb, rb, lw, rb, rb, rb, rb, rb, lw)


def _post_kernel(yf_ref, yb_ref, bonus_ref, g_ref, lnw_ref, lnb_ref, mbd_ref, o_ref, *, nhp):
    mbd = mbd_ref[...]
    inv_n = 1.0 / HEAD_DIM
    for hp in range(nhp):
        y = yf_ref[0, hp] + yb_ref[0, hp]
        mu = _head_sums_split(y, mbd) * inv_n
        d = y - mu
        var = _head_sums_split(d * d, mbd) * inv_n
        yn = d * lax.rsqrt(var + GN_EPS) * lnw_ref[hp:hp + 1, :] + lnb_ref[hp:hp + 1, :]
        out = (yn + bonus_ref[0, hp].astype(F32)) * g_ref[0, hp].astype(F32)
        o_ref[0, :, hp * LANES:(hp + 1) * LANES] = out.astype(o_ref.dtype)


def _post_call(yf, yb, rb, lnw, lnb, mbd, nhp, tb=256):
    b, _, s, _ = yf.shape
    yblk = pl.BlockSpec((1, nhp, tb, LANES), lambda bi, i: (bi, 0, i, 0))
    rblk = lambda q: pl.BlockSpec((1, nhp, tb, LANES), lambda bi, i: (bi, q, i, 0))
    full = lambda a: pl.BlockSpec(a.shape, lambda bi, i: (0,) * a.ndim)
    return pl.pallas_call(
        functools.partial(_post_kernel, nhp=nhp),
        grid=(b, s // tb),
        in_specs=[yblk, yblk, rblk(RB_BONUS), rblk(RB_G), full(lnw), full(lnb), full(mbd)],
        out_specs=pl.BlockSpec((1, tb, nhp * LANES), lambda bi, i: (bi, i, 0)),
        out_shape=jax.ShapeDtypeStruct((b, s, nhp * LANES), BF16),
        compiler_params=_cparams(("parallel", "parallel")),
        name="post",
    )(yf, yb, rb, rb, lnw, lnb, mbd)


def _na_kernel(q_ref, k_ref, v_ref, bias_ref, qw_ref, kw_ref, mbd_ref, o_ref, kn_sc, *, rq, rows):
    rbi = pl.program_id(2)
    mbd = mbd_ref[...]
    inv_n = 1.0 / HEAD_DIM
    span = WIN_R * GRID_W
    s_len = kn_sc.shape[0]

    @pl.when(rbi == 0)
    def _():
        def body(c, carry):
            off = pl.multiple_of(c * span, span)
            k = k_ref[0, 0, pl.ds(off, span), :].astype(F32)
            ms = _head_sums_split(k * k, mbd) * inv_n
            kn_sc[pl.ds(off, span), :] = (k * lax.rsqrt(ms + NORM_EPS) * kw_ref[...]).astype(BF16)
            return carry
        lax.fori_loop(0, s_len // span, body, 0)

    ri = lax.broadcasted_iota(I32, (HEADS_PER_BLOCK * GRID_W, LANES), 0)
    ci = lax.broadcasted_iota(I32, (HEADS_PER_BLOCK * GRID_W, LANES), 1)
    head_match = (ri // GRID_W) == (ci // HEAD_DIM)
    lane = lax.broadcasted_iota(I32, (GRID_W, LANES), 1)
    scale = HEAD_DIM ** -0.5
    q_all = q_ref[0, 0].astype(F32)
    ms = _head_sums_split(q_all * q_all, mbd) * inv_n
    qn_all = q_all * lax.rsqrt(ms + NORM_EPS) * qw_ref[...] * scale
    starts, offs = [], []
    for qi in range(rq):
        i = rbi * rq + qi
        rs = jnp.clip(i - WIN_R // 2, 0, rows - WIN_R)
        offs.append(rs - i + WIN_R - 1)
        starts.append(pl.multiple_of(rs * GRID_W, GRID_W))
    qs_l = [jnp.where(head_match, jnp.concatenate([qn_all[qi * GRID_W:(qi + 1) * GRID_W]] * HEADS_PER_BLOCK,
                                                  axis=0), 0.0).astype(BF16) for qi in range(rq)]
    s_l = [lax.dot_general(qs_l[qi], kn_sc[pl.ds(starts[qi], span), :], NT_DIMS, preferred_element_type=F32)
           + bias_ref[offs[qi], 0] for qi in range(rq)]
    p_l, l_l = [], []
    for s in s_l:
        p = jnp.exp(s - jnp.max(s, axis=-1, keepdims=True))
        p_l.append(p.astype(BF16))
        l_l.append(jnp.sum(p, axis=-1, keepdims=True))
    o_l = [jnp.dot(p_l[qi], v_ref[0, 0, pl.ds(starts[qi], span), :], preferred_element_type=F32) / l_l[qi]
           for qi in range(rq)]
    for qi, o in enumerate(o_l):
        out = o[0:GRID_W]
        for h in range(1, HEADS_PER_BLOCK):
            out = jnp.where(lane // HEAD_DIM == h, o[h * GRID_W:(h + 1) * GRID_W], out)
        o_ref[0, qi * GRID_W:(qi + 1) * GRID_W, :] = out.astype(o_ref.dtype)


def _na_call(p, bias, qw, kw, mbd, q_blk0, nhp, rq=16):
    b, _, s, _ = p.shape
    rows = s // GRID_W
    rq = min(rq, rows)
    full = lambda a: pl.BlockSpec(a.shape, lambda bi, hp, r: (0,) * a.ndim)
    return pl.pallas_call(
        functools.partial(_na_kernel, rq=rq, rows=rows),
        grid=(b, nhp, rows // rq),
        in_specs=[pl.BlockSpec((1, 1, rq * GRID_W, LANES), lambda bi, hp, r: (bi, q_blk0 + hp, r, 0)),
                  pl.BlockSpec((1, 1, s, LANES), lambda bi, hp, r: (bi, q_blk0 + nhp + hp, 0, 0)),
                  pl.BlockSpec((1, 1, s, LANES), lambda bi, hp, r: (bi, q_blk0 + 2 * nhp + hp, 0, 0)),
                  pl.BlockSpec((WIN_R, 1, HEADS_PER_BLOCK * GRID_W, WIN_R * GRID_W),
                               lambda bi, hp, r: (0, hp, 0, 0)),
                  full(qw), full(kw), full(mbd)],
        out_specs=pl.BlockSpec((1, rq * GRID_W, LANES), lambda bi, hp, r: (bi, r, hp)),
        out_shape=jax.ShapeDtypeStruct((b, s, nhp * LANES), BF16),
        scratch_shapes=[pltpu.VMEM((s, LANES), BF16)],
        compiler_params=_cparams(("parallel", "parallel", "arbitrary")),
        name="na",
    )(p, p, p, bias, qw, kw, mbd)


def _na_bias_table(rel_bias):
    h = rel_bias.shape[0]
    qc = np.arange(GRID_W)
    kc = np.arange(GRID_W)
    win0 = np.clip(qc - WIN_C // 2, 0, GRID_W - WIN_C)
    valid = (kc[None, :] >= win0[:, None]) & (kc[None, :] < win0[:, None] + WIN_C)
    coff = np.clip(kc[None, :] - qc[:, None] + WIN_C - 1, 0, 2 * WIN_C - 2)
    pick = jnp.asarray((coff[None] == np.arange(2 * WIN_C - 1)[:, None, None]).astype(np.float32))
    t = jnp.einsum('hrc,cqk->hrqk', rel_bias.astype(F32), pick, precision=HI)
    t = jnp.stack([t[:, o:o + WIN_R] for o in range(WIN_R)], axis=1)
    t = jnp.where(valid[None, None, None], t, NEG_BIG)
    t = jnp.transpose(t, (1, 0, 3, 2, 4))
    return t.reshape(WIN_R, h // HEADS_PER_BLOCK, HEADS_PER_BLOCK * GRID_W, WIN_R * GRID_W)


def _outproj_kernel(yr_ref, yn_ref, x_ref, mod_ref, nw_ref, w1_ref, w2_ref, wr_ref, x1_ref, h2_ref, h2row_ref,
                    lg_ref):
    tm = x_ref.shape[1]
    sub = LANES
    tiles = [slice(i * sub, (i + 1) * sub) for i in range(tm // sub)]
    accs = [jnp.dot(yr_ref[0, r], w1_ref[...], preferred_element_type=F32)
            + jnp.dot(yn_ref[0, r], w2_ref[...], preferred_element_type=F32) for r in tiles]
    h2s = []
    for r, acc in zip(tiles, accs):
        x1 = x_ref[0, r] + mod_ref[0, 2:3, :] * acc
        x1_ref[0, r] = x1
        ms = jnp.mean(x1 * x1, axis=-1, keepdims=True)
        y = x1 * lax.rsqrt(ms + NORM_EPS) * nw_ref[...]
        h2 = y * (1.0 + mod_ref[0, 4:5, :]) + mod_ref[0, 3:4, :]
        packed = _pack_halves(h2)
        h2row_ref[0, r] = packed
        for j in range(h2_ref.shape[2]):
            h2_ref[0, r, j, :] = packed[:, j * LANES:(j + 1) * LANES]
        h2s.append(h2)
    wr = wr_ref[...]
    wr_hi = wr.astype(BF16)
    wr_lo = (wr - wr_hi.astype(F32)).astype(BF16)
    for r, h2 in zip(tiles, h2s):
        h_hi = h2.astype(BF16)
        h_lo = (h2 - h_hi.astype(F32)).astype(BF16)
        lg_ref[:, r] = (lax.dot_general(wr_hi, h_hi, NT_DIMS, preferred_element_type=F32)
                        + lax.dot_general(wr_hi, h_lo, NT_DIMS, preferred_element_type=F32)
                        + lax.dot_general(wr_lo, h_hi, NT_DIMS, preferred_element_type=F32))


def _outproj_call(yr, yn, x, mod3, norm_w, w1, w2, wr_t, tm=512):
    b, s, d = x.shape
    dh = yr.shape[-1]
    ne = wr_t.shape[0]
    nt = s // tm
    full = lambda a: pl.BlockSpec(a.shape, lambda bi, i: (0,) * a.ndim)
    return pl.pallas_call(
        _outproj_kernel,
        grid=(b, nt),
        in_specs=[pl.BlockSpec((1, tm, dh), lambda bi, i: (bi, i, 0)),
                  pl.BlockSpec((1, tm, dh), lambda bi, i: (bi, i, 0)),
                  pl.BlockSpec((1, tm, d), lambda bi, i: (bi, i, 0)),
                  pl.BlockSpec((1, 6, d), lambda bi, i: (bi, 0, 0)),
                  full(norm_w), full(w1), full(w2), full(wr_t)],
        out_specs=[pl.BlockSpec((1, tm, d), lambda bi, i: (bi, i, 0)),
                   pl.BlockSpec((1, tm, d // 2 // LANES, LANES), lambda bi, i: (bi, i, 0, 0)),
                   pl.BlockSpec((1, tm, d // 2), lambda bi, i: (bi, i, 0)),
                   pl.BlockSpec((ne, tm), lambda bi, i: (0, bi * nt + i))],
        out_shape=[jax.ShapeDtypeStruct((b, s, d), F32),
                   jax.ShapeDtypeStruct((b, s, d // 2 // LANES, LANES), U32),
                   jax.ShapeDtypeStruct((b, s, d // 2), U32),
                   jax.ShapeDtypeStruct((ne, b * s), F32)],
        compiler_params=_cparams(("parallel", "parallel")),
        name="outproj",
    )(yr, yn, x, mod3, norm_w, w1, w2, wr_t)


def _first_argmax(x, idx, n):
    m = jnp.max(x, axis=0, keepdims=True)
    a = jnp.min(jnp.where(x == m, idx, n), axis=0, keepdims=True)
    return m, a


def _route_kernel(lg_ref, bias_ref, idx_ref, gw_ref, pos_ref, cnt_ref, *, tr):
    @pl.when(pl.program_id(0) == 0)
    def _():
        cnt_ref[...] = jnp.zeros(cnt_ref.shape, F32)

    gsz = N_EXPERTS // N_GROUPS
    scores = _sigmoid(lg_ref[...])
    biased = scores + bias_ref[:, 0:1]
    ig = lax.broadcasted_iota(I32, (gsz, tr), 0)
    grp_rows = []
    for g in range(N_GROUPS):
        blk = biased[g * gsz:(g + 1) * gsz]
        m1, a1 = _first_argmax(blk, ig, gsz)
        m2 = jnp.max(jnp.where(ig == a1, -jnp.inf, blk), axis=0, keepdims=True)
        grp_rows.append(m1 + m2)
    grp = jnp.concatenate(grp_rows, axis=0)
    ign = lax.broadcasted_iota(I32, (N_GROUPS, tr), 0)
    sel = jnp.zeros((N_GROUPS, tr), jnp.bool_)
    for _ in range(TOPK_GROUPS):
        _, a = _first_argmax(grp, ign, N_GROUPS)
        hit = ign == a
        sel = jnp.logical_or(sel, hit)
        grp = jnp.where(hit, -jnp.inf, grp)
    masked = jnp.concatenate(
        [jnp.where(sel[g:g + 1], biased[g * gsz:(g + 1) * gsz], -jnp.inf) for g in range(N_GROUPS)], axis=0)
    ie = lax.broadcasted_iota(I32, (N_EXPERTS, tr), 0)
    picks, pick_scores = [], []
    onehot = jnp.zeros((N_EXPERTS, tr), F32)
    for _ in range(TOP_K):
        _, a = _first_argmax(masked, ie, N_EXPERTS)
        hit = ie == a
        picks.append(a)
        pick_scores.append(jnp.sum(jnp.where(hit, scores, 0.0), axis=0, keepdims=True))
        onehot = onehot + hit.astype(F32)
        masked = jnp.where(hit, -jnp.inf, masked)
    total = pick_scores[0]
    for sc in pick_scores[1:]:
        total = total + sc
    t0 = lax.broadcasted_iota(I32, (tr, tr), 0)
    t1 = lax.broadcasted_iota(I32, (tr, tr), 1)
    before = (t0 < t1).astype(BF16)
    rank = jnp.dot(onehot.astype(BF16), before, preferred_element_type=F32) + cnt_ref[:, 0:1]
    zero_i = jnp.zeros((1, tr), I32)
    zero_f = jnp.zeros((1, tr), F32)
    for k in range(8):
        if k < TOP_K:
            idx_ref[k:k + 1, :] = picks[k]
            gw_ref[k:k + 1, :] = pick_scores[k] / total * ROUTED_SCALE
            pos = jnp.sum(jnp.where(ie == picks[k], rank, 0.0), axis=0, keepdims=True)
            pos_ref[k:k + 1, :] = pos.astype(I32)
        else:
            idx_ref[k:k + 1, :] = zero_i
            gw_ref[k:k + 1, :] = zero_f
            pos_ref[k:k + 1, :] = zero_i
    cnt_ref[...] = cnt_ref[...] + jnp.sum(onehot, axis=1, keepdims=True)


def _route_call(lg_t, bias2, tr=512):
    ne, t = lg_t.shape
    tr = min(tr, t)
    tok = pl.BlockSpec((8, tr), lambda i: (0, i))
    return pl.pallas_call(
        functools.partial(_route_kernel, tr=tr),
        grid=(t // tr,),
        in_specs=[pl.BlockSpec((ne, tr), lambda i: (0, i)),
                  pl.BlockSpec(bias2.shape, lambda i: (0, 0))],
        out_specs=[tok, tok, tok, pl.BlockSpec((ne, LANES), lambda i: (0, 0))],
        out_shape=[jax.ShapeDtypeStruct((8, t), I32), jax.ShapeDtypeStruct((8, t), F32),
                   jax.ShapeDtypeStruct((8, t), I32), jax.ShapeDtypeStruct((ne, LANES), F32)],
        compiler_params=_cparams(("arbitrary",)),
        name="route",
    )(lg_t, bias2)


FILL_CHUNKS = (1, 8, 64)
ZERO_ROWS = FILL_CHUNKS[-1]


def _dispatch_kernel(slot_ref, fill_lo_ref, fill_hi_ref, h_ref, xs_ref, zrow, sem, *, td):
    def row_copy(t, slot):
        return pltpu.make_async_copy(h_ref.at[t], xs_ref.at[slot], sem)

    def start(t, carry):
        for k in range(TOP_K):
            row_copy(t, slot_ref[t * SLOT_STRIDE + k]).start(priority=k % 2)
        return carry

    lax.fori_loop(0, td, start, 0)
    for k in range(TOP_K):
        pltpu.make_async_copy(h_ref, xs_ref.at[pl.ds(0, td)], sem).wait()

    @pl.when(pl.program_id(0) == pl.num_programs(0) - 1)
    def _():
        zrow[...] = jnp.zeros(zrow.shape, zrow.dtype)

        def zero_copy(row, n):
            if n > 1:
                row = pl.multiple_of(row, FILL_CHUNKS[1])
            return pltpu.make_async_copy(zrow.at[pl.ds(0, n)], xs_ref.at[pl.ds(row, n)], sem)

        def for_each_chunk(r, fn):
            lo, hi = fill_lo_ref[r], fill_hi_ref[r]
            for ci, n in enumerate(FILL_CHUNKS):
                if ci + 1 < len(FILL_CHUNKS):
                    count = jnp.minimum(((-lo) % FILL_CHUNKS[ci + 1]) // n, (hi - lo) // n)
                else:
                    count = (hi - lo) // n
                fn(lo, n, count)
                lo = lo + count * n

        def start_range(r, carry):
            def go(lo, n, count):
                lax.fori_loop(0, count, lambda i, c: (zero_copy(lo + i * n, n).start(), c)[1], 0)
            for_each_chunk(r, go)
            return carry

        def wait_range(r, carry):
            def go(lo, n, count):
                lax.fori_loop(0, count, lambda i, c: (zero_copy(0, n).wait(), c)[1], 0)
            for_each_chunk(r, go)
            return carry

        n_ranges = fill_lo_ref.shape[0]
        lax.fori_loop(0, n_ranges, start_range, 0)
        lax.fori_loop(0, n_ranges, wait_range, 0)


def _dispatch_call(slots_flat, fill_lo, fill_hi, h2, n_rows, td=256):
    t, nj, _ = h2.shape
    smem = pl.BlockSpec(memory_space=pltpu.SMEM)
    return pl.pallas_call(
        functools.partial(_dispatch_kernel, td=td),
        grid=(t // td,),
        in_specs=[pl.BlockSpec((td * SLOT_STRIDE,), lambda i: (i,), memory_space=pltpu.SMEM),
                  smem, smem,
                  pl.BlockSpec((td, nj, LANES), lambda i: (i, 0, 0))],
        out_specs=pl.BlockSpec(memory_space=pl.ANY),
        out_shape=jax.ShapeDtypeStruct((n_rows, nj, LANES), h2.dtype),
        scratch_shapes=[pltpu.VMEM((ZERO_ROWS, nj, LANES), h2.dtype), pltpu.SemaphoreType.DMA(())],
        compiler_params=_cparams(("arbitrary",)),
        name="dispatch",
    )(slots_flat, fill_lo, fill_hi, h2)


def _experts_kernel(be_ref, first_ref, nxt_ref, par_ref, rows_ref, nv_ref, xs_hbm, wg_hbm, wu_hbm, wd_hbm, ys_ref,
                    wg_buf, wu_buf, wd_buf, wg_sc, wu_sc, wd_sc, xpack, sem, isem):
    nj = xs_hbm.shape[1]
    b = pl.program_id(0)
    m = xpack.shape[1]
    half = wg_sc.shape[0] // 2
    oslot = lax.rem(b, 2)

    def in_copies(slot, row0):
        return [pltpu.make_async_copy(xs_hbm.at[pl.ds(row0, m), j],
                                      xpack.at[slot, :, pl.ds(j * LANES, LANES)], isem.at[slot]) for j in range(nj)]

    @pl.when(b == 0)
    def _():
        for c in in_copies(0, 0):
            c.start()

    @pl.when(b + 1 < nv_ref[0])
    def _():
        for c in in_copies(1 - oslot, pl.multiple_of((b + 1) * m, m)):
            c.start()

    def weight_copies(e, slot):
        return [pltpu.make_async_copy(src.at[e], dst.at[slot], sem.at[slot, i])
                for i, (src, dst) in enumerate(((wg_hbm, wg_buf), (wu_hbm, wu_buf), (wd_hbm, wd_buf)))]

    def compute(nrows):
        x_lo, x_hi = _unpack_halves(xpack[oslot, 0:nrows])
        x_lo, x_hi = x_lo.astype(BF16), x_hi.astype(BF16)
        g = (jnp.dot(x_lo, wg_sc[0:half], preferred_element_type=F32)
             + jnp.dot(x_hi, wg_sc[half:], preferred_element_type=F32))
        u = (jnp.dot(x_lo, wu_sc[0:half], preferred_element_type=F32)
             + jnp.dot(x_hi, wu_sc[half:], preferred_element_type=F32))
        a = (_silu(g) * u).astype(BF16)
        ys_ref[0:nrows] = _pack_halves(jnp.dot(a, wd_sc[...], preferred_element_type=F32))

    @pl.when(b < nv_ref[0])
    def _():
        for c in in_copies(oslot, 0):
            c.wait()

        @pl.when(b == 0)
        def _():
            for c in weight_copies(be_ref[0], par_ref[0]):
                c.start()

        @pl.when(first_ref[b] == 1)
        def _():
            slot = par_ref[b]
            for c in weight_copies(be_ref[b], slot):
                c.wait()

            @pl.when(nxt_ref[b] >= 0)
            def _():
                for c in weight_copies(nxt_ref[b], 1 - slot):
                    c.start()

            wg_sc[...] = wg_buf[slot].astype(BF16)
            wu_sc[...] = wu_buf[slot].astype(BF16)
            wd_sc[...] = wd_buf[slot].astype(BF16)

        @pl.when(rows_ref[b] > m // 2)
        def _():
            compute(m)

        @pl.when(rows_ref[b] <= m // 2)
        def _():
            compute(m // 2)
            ys_ref[m // 2:] = jnp.zeros((m - m // 2, ys_ref.shape[1]), ys_ref.dtype)

    @pl.when(b >= nv_ref[0])
    def _():
        ys_ref[...] = jnp.zeros(ys_ref.shape, ys_ref.dtype)


def _experts_call(block_e, first, nxt, par, rows_valid, n_valid, xs, wg, wu, wd):
    total, nj, _ = xs.shape
    d = wg.shape[1]
    m = DISPATCH_BLOCK
    ff = wg.shape[-1]

    hbm = pl.BlockSpec(memory_space=pl.ANY)
    grid_spec = pltpu.PrefetchScalarGridSpec(
        num_scalar_prefetch=6,
        grid=(total // m,),
        in_specs=[hbm, hbm, hbm, hbm],
        out_specs=pl.BlockSpec((m, nj * LANES), lambda b, *s: (b, 0)),
        scratch_shapes=[pltpu.VMEM((2, d, ff), F32), pltpu.VMEM((2, d, ff), F32), pltpu.VMEM((2, ff, d), F32),
                        pltpu.VMEM((d, ff), BF16), pltpu.VMEM((d, ff), BF16), pltpu.VMEM((ff, d), BF16),
                        pltpu.VMEM((2, m, nj * LANES), U32),
                        pltpu.SemaphoreType.DMA((2, 3)), pltpu.SemaphoreType.DMA((2,))],
    )
    return pl.pallas_call(
        _experts_kernel,
        grid_spec=grid_spec,
        out_shape=jax.ShapeDtypeStruct((total, nj * LANES), U32),
        compiler_params=_cparams(("arbitrary",)),
        name="experts",
    )(block_e, first, nxt, par, rows_valid, n_valid, xs, wg, wu, wd)


def _combine_kernel(slot_ref, x1_ref, h2_ref, gw_ref, mod_ref, wg_ref, wu_ref, wd_ref, ys_ref, o_ref,
                    buf, sem, *, tc):
    def row_copy(t, k, slot):
        return pltpu.make_async_copy(ys_ref.at[pl.ds(slot, 1)], buf.at[k, pl.ds(t, 1)], sem)

    def start(t, carry):
        for k in range(TOP_K):
            row_copy(t, k, slot_ref[t * SLOT_STRIDE + k]).start(priority=k % 2)
        return carry

    lax.fori_loop(0, tc, start, 0)
    half = wg_ref.shape[0] // 2
    h_lo, h_hi = _unpack_halves(h2_ref[0])
    h_lo, h_hi = h_lo.astype(BF16), h_hi.astype(BF16)
    g = (jnp.dot(h_lo, wg_ref[0:half], preferred_element_type=F32)
         + jnp.dot(h_hi, wg_ref[half:], preferred_element_type=F32))
    u = (jnp.dot(h_lo, wu_ref[0:half], preferred_element_type=F32)
         + jnp.dot(h_hi, wu_ref[half:], preferred_element_type=F32))
    acc = jnp.dot((_silu(g) * u).astype(BF16), wd_ref[...], preferred_element_type=F32)
    for k in range(TOP_K):
        pltpu.make_async_copy(ys_ref.at[pl.ds(0, tc)], buf.at[k], sem).wait()
    gw = gw_ref[...]
    acc_lo, acc_hi = acc[:, :half], acc[:, half:]
    for k in range(TOP_K):
        y_lo, y_hi = _unpack_halves(buf[k])
        acc_lo = acc_lo + gw[:, k:k + 1] * y_lo
        acc_hi = acc_hi + gw[:, k:k + 1] * y_hi
    gate = mod_ref[0, 5:6, :]
    o_ref[0, :, 0:half] = x1_ref[0, :, 0:half] + gate[:, :half] * acc_lo
    o_ref[0, :, half:] = x1_ref[0, :, half:] + gate[:, half:] * acc_hi


def _combine_call(slots_flat, x1, h2, gw_tok, mod3, wg, wu, wd, ys, tc=256):
    b, s, d = x1.shape
    nt = s // tc
    full = lambda a: pl.BlockSpec(a.shape, lambda bi, i: (0,) * a.ndim)
    return pl.pallas_call(
        functools.partial(_combine_kernel, tc=tc),
        grid=(b, nt),
        in_specs=[pl.BlockSpec((tc * SLOT_STRIDE,), lambda bi, i: (bi * nt + i,), memory_space=pltpu.SMEM),
                  pl.BlockSpec((1, tc, d), lambda bi, i: (bi, i, 0)),
                  pl.BlockSpec((1, tc, d // 2), lambda bi, i: (bi, i, 0)),
                  pl.BlockSpec((tc, 8), lambda bi, i: (bi * nt + i, 0)),
                  pl.BlockSpec((1, 6, d), lambda bi, i: (bi, 0, 0)),
                  full(wg), full(wu), full(wd),
                  pl.BlockSpec(memory_space=pl.ANY)],
        out_specs=pl.BlockSpec((1, tc, d), lambda bi, i: (bi, i, 0)),
        out_shape=jax.ShapeDtypeStruct((b, s, d), F32),
        scratch_shapes=[pltpu.VMEM((TOP_K, tc, d // 2), U32), pltpu.SemaphoreType.DMA(())],
        compiler_params=_cparams(("arbitrary", "arbitrary")),
        name="combine",
    )(slots_flat, x1, h2, gw_tok, mod3, wg, wu, wd, ys)


def _pad_rows(w, rows, at=0):
    out = jnp.zeros((rows, w.shape[1]), w.dtype)
    return out.at[at:at + w.shape[0]].set(w)


def _mixing_stage(p, rwkv_shift, w0_f, w_up_f, w0_b, w_up_b, a0_f, a_up_f, a0_b, a_up_b, g_up,
                  k_k, k_a, r_k, ln_x_w, ln_x_b, q_norm_w, k_norm_w, rel_bias):
    d_rwkv = w0_f.shape[0]
    nhp = d_rwkv // LANES
    mbd = _head_sum_matrix()
    n_shift = rwkv_shift.shape[1]
    sw = jnp.pad(rwkv_shift, ((0, 0), (0, N_SHIFT_BLOCKS * LANES - n_shift))).reshape(3, N_SHIFT_BLOCKS, LANES)
    wupf = _pad_rows(w_up_f, LANES, 0).astype(BF16)
    wupb = _pad_rows(w_up_b, LANES, DECAY_LORA).astype(BF16)
    aupf = _pad_rows(a_up_f, LANES, 0).astype(BF16)
    aupb = _pad_rows(a_up_b, LANES, AAA_LORA).astype(BF16)
    gup = _pad_rows(g_up, 2 * LANES, 0).astype(BF16)
    vecs = jnp.stack([w0_f, w0_b, a0_f, a0_b, k_k, k_a, r_k.reshape(-1)]).reshape(7, nhp, LANES)
    rb, lw = _prep_call(p, sw, wupf, wupb, aupf, aupb, gup, vecs, mbd, nhp)
    y_f, y_b = _wkv_call(rb, lw, nhp)
    y_rwkv = _post_call(y_f, y_b, rb, ln_x_w.reshape(nhp, LANES), ln_x_b.reshape(nhp, LANES), mbd, nhp)
    bias = _na_bias_table(rel_bias)
    qw = jnp.tile(q_norm_w, HEADS_PER_BLOCK).reshape(1, LANES)
    kw = jnp.tile(k_norm_w, HEADS_PER_BLOCK).reshape(1, LANES)
    y_na = _na_call(p, bias, qw, kw, mbd, N_SHIFT_BLOCKS, nhp)
    return y_rwkv, y_na


def _moe_stage(x1, h2_tiles, h2_rows, lg_t, mod3, router_bias, w_gate_e, w_up_e, w_down_e, w_gate_s, w_up_s,
               w_down_s):
    b, s, d = x1.shape
    t = b * s
    m = DISPATCH_BLOCK
    bias2 = jnp.broadcast_to(router_bias.astype(F32)[:, None], (N_EXPERTS, LANES))
    idx_t, gw_t, pos_t, cnt = _route_call(lg_t, bias2)
    counts = cnt[:, 0].astype(I32)
    padded = (counts + m - 1) // m * m
    pad_end = jnp.cumsum(padded)
    pad_start = pad_end - padded
    n_blocks = -(-(t * TOP_K + N_EXPERTS * m) // m)
    n_valid = (pad_end[-1] // m).astype(I32).reshape(1)
    block_row0 = jnp.arange(n_blocks, dtype=I32) * m
    block_e = jnp.minimum(jnp.sum((pad_end[None, :] <= block_row0[:, None]).astype(I32), axis=1),
                          N_EXPERTS - 1).astype(I32)
    assert m % ZERO_ROWS == 0
    expert_ids = jnp.arange(N_EXPERTS, dtype=I32)
    start_of = jnp.sum(jnp.where(idx_t[:, :, None] == expert_ids, pad_start, 0), axis=-1)
    slots = (start_of + pos_t).astype(I32)
    slots_flat = slots.T.reshape(-1)
    gw_tok = gw_t.T
    fill_lo = jnp.concatenate([pad_start + counts, pad_end[-1:]]).astype(I32)
    fill_hi = jnp.concatenate([pad_end, jnp.full((1,), n_blocks * m, I32)]).astype(I32)
    xs = _dispatch_call(slots_flat, fill_lo, fill_hi, h2_tiles.reshape(t, *h2_tiles.shape[2:]), n_blocks * m)
    row_end = (pad_start + counts).astype(I32)
    nonempty = counts > 0
    first = jnp.logical_and(block_row0 == pad_start[block_e], block_row0 < pad_end[-1]).astype(I32)
    ordinal = jnp.cumsum(nonempty.astype(I32)) - 1
    par = (ordinal[block_e] & 1).astype(I32)
    cand = jnp.where(nonempty, expert_ids, N_EXPERTS)
    later = jnp.concatenate([lax.cummin(cand[::-1])[::-1][1:], jnp.full((1,), N_EXPERTS, I32)])
    nxt = jnp.where(later < N_EXPERTS, later, -1)[block_e].astype(I32)
    rows_valid = jnp.clip(row_end[block_e] - block_row0, 0, m).astype(I32)
    ys = _experts_call(block_e, first, nxt, par, rows_valid, n_valid, xs, w_gate_e, w_up_e, w_down_e)
    return _combine_call(slots_flat, x1, h2_rows, gw_tok, mod3, w_gate_s.astype(BF16), w_up_s.astype(BF16),
                         w_down_s.astype(BF16), ys)


def kernel(x, c, w_ada, b_ada, norm1_w, w_in, rwkv_shift, w0_f, w_up_f, w0_b, w_up_b, a0_f, a_up_f, a0_b,
           a_up_b, g_up, k_k, k_a, r_k, ln_x_w, ln_x_b, q_norm_w, k_norm_w, rel_bias, w_out, norm2_w,
           w_router, router_bias, w_gate_e, w_up_e, w_down_e, w_gate_s, w_up_s, w_down_s):
    bn, sn, d = x.shape
    depth = w_ada.shape[0]
    for l in range(depth):
        c_pad = jnp.pad(c, ((0, 8 - bn % 8 if bn % 8 else 0), (0, 0)))
        mod = _mod_call(c_pad, w_ada[l], b_ada[l].reshape(1, -1))[:bn]
        mod3 = mod.reshape(bn, 6, d)
        rwkv_in = rwkv_shift.shape[-1]
        pad_cols = N_SHIFT_BLOCKS * LANES - rwkv_in
        w_in_r = jnp.concatenate([w_in[l][:, :rwkv_in], jnp.zeros((d, pad_cols), w_in.dtype),
                                  w_in[l][:, rwkv_in:]], axis=1).astype(BF16)
        p = _proj_call(x, mod3, norm1_w[l].reshape(1, d), w_in_r)
        y_rwkv, y_na = _mixing_stage(p, rwkv_shift[l], w0_f[l], w_up_f[l], w0_b[l], w_up_b[l], a0_f[l],
                                     a_up_f[l], a0_b[l], a_up_b[l], g_up[l], k_k[l], k_a[l], r_k[l],
                                     ln_x_w[l], ln_x_b[l], q_norm_w[l], k_norm_w[l], rel_bias[l])
        d_rwkv = y_rwkv.shape[-1]
        w_o = w_out[l].astype(BF16)
        x1, h2_tiles, h2_rows, lg_t = _outproj_call(y_rwkv, y_na, x, mod3, norm2_w[l].reshape(1, d), w_o[:d_rwkv],
                                                    w_o[d_rwkv:], w_router[l].T)
        x = _moe_stage(x1, h2_tiles, h2_rows, lg_t, mod3, router_bias[l], w_gate_e[l], w_up_e[l], w_down_e[l],
                       w_gate_s[l], w_up_s[l], w_down_s[l])
    return x
```

```python
import functools

import jax
import jax.numpy as jnp
import numpy as np
from jax import lax
from jax.experimental import pallas as pl
from jax.experimental.pallas import tpu as pltpu

F32 = jnp.float32
BF16 = jnp.bfloat16
I32 = jnp.int32
HI = lax.Precision.HIGHEST

LANES = 128
HEAD_DIM = 64
HEADS_PER_BLOCK = LANES // HEAD_DIM
GRID_W = 64
WIN_R = 8
WIN_C = 16
DECAY_LORA = 64
AAA_LORA = 64
GATE_LORA = 160
N_EXPERTS = 64
TOP_K = 6
SLOT_STRIDE = 8
N_GROUPS = 8
TOPK_GROUPS = 4
ROUTED_SCALE = 2.5
DISPATCH_BLOCK = 512
NORM_EPS = 1e-6
GN_EPS = 64e-5
L2_EPS = 1e-12
NEG_BIG = -1e30
WKV_CHUNK = 64
VMEM_LIMIT = 56 * 1024 * 1024

NT_DIMS = (((1,), (1,)), ((), ()))
TN_DIMS = (((0,), (0,)), ((), ()))


def _cparams(sem):
    return pltpu.CompilerParams(dimension_semantics=sem, vmem_limit_bytes=VMEM_LIMIT)


def _sigmoid(x):
    return 1.0 / (1.0 + jnp.exp(-x))


def _silu(x):
    return x * _sigmoid(x)


U32 = jnp.uint32
HI_HALF_MASK = 0xFFFF0000


def _pack_halves(x):
    n = x.shape[-1] // 2
    bits = lax.bitcast_convert_type(x.astype(BF16).astype(F32), U32)
    return (bits[:, :n] >> 16) | bits[:, n:]


def _unpack_halves(w):
    lo = lax.bitcast_convert_type(w << 16, F32)
    hi = lax.bitcast_convert_type(w & U32(HI_HALF_MASK), F32)
    return lo, hi


def _head_sum_matrix():
    a = np.arange(LANES) // HEAD_DIM
    return jnp.asarray((a[:, None] == a[None, :]).astype(np.float32))


def _head_sums_split(x, mbd):
    m16 = mbd.astype(BF16)
    hi = x.astype(BF16)
    lo = (x - hi.astype(F32)).astype(BF16)
    return jnp.dot(hi, m16, preferred_element_type=F32) + jnp.dot(lo, m16, preferred_element_type=F32)


def _mod_kernel(c_ref, w_ref, b_ref, o_ref):
    c = c_ref[...]
    o_ref[...] = jnp.dot(_silu(c), w_ref[...], preferred_element_type=F32, precision=HI) + b_ref[...]


def _mod_call(c_pad, w_ada, b_ada):
    rows, d = c_pad.shape
    n = w_ada.shape[1]
    tn = 1024
    return pl.pallas_call(
        _mod_kernel,
        grid=(n // tn,),
        in_specs=[pl.BlockSpec((rows, d), lambda j: (0, 0)),
                  pl.BlockSpec((d, tn), lambda j: (0, j)),
                  pl.BlockSpec((1, tn), lambda j: (0, j))],
        out_specs=pl.BlockSpec((rows, tn), lambda j: (0, j)),
        out_shape=jax.ShapeDtypeStruct((rows, n), F32),
        compiler_params=_cparams(("parallel",)),
        name="mod",
    )(c_pad, w_ada, b_ada)


def _proj_kernel(x_ref, mod_ref, nw_ref, w_ref, o_ref, h_sc, *, nq):
    @pl.when(pl.program_id(2) == 0)
    def _():
        x = x_ref[0]
        ms = jnp.mean(x * x, axis=-1, keepdims=True)
        y = x * lax.rsqrt(ms + NORM_EPS) * nw_ref[...]
        h_sc[...] = (y * (1.0 + mod_ref[0, 1:2, :]) + mod_ref[0, 0:1, :]).astype(BF16)

    acc = jnp.dot(h_sc[...], w_ref[...], preferred_element_type=F32)
    for q in range(nq):
        o_ref[0, q] = acc[:, q * LANES:(q + 1) * LANES].astype(o_ref.dtype)


def _proj_call(x, mod3, norm_w, w_in_r, tm=1024, tn=1664):
    b, s, d = x.shape
    n = w_in_r.shape[1]
    nq = tn // LANES
    return pl.pallas_call(
        functools.partial(_proj_kernel, nq=nq),
        grid=(b, s // tm, n // tn),
        in_specs=[pl.BlockSpec((1, tm, d), lambda bi, i, j: (bi, i, 0)),
                  pl.BlockSpec((1, 6, d), lambda bi, i, j: (bi, 0, 0)),
                  pl.BlockSpec((1, d), lambda bi, i, j: (0, 0)),
                  pl.BlockSpec((d, tn), lambda bi, i, j: (0, j))],
        out_specs=pl.BlockSpec((1, nq, tm, LANES), lambda bi, i, j: (bi, j, i, 0)),
        out_shape=jax.ShapeDtypeStruct((b, n // LANES, s, LANES), BF16),
        scratch_shapes=[pltpu.VMEM((tm, d), BF16)],
        compiler_params=_cparams(("parallel", "parallel", "arbitrary")),
        name="proj",
    )(x, mod3, norm_w, w_in_r)


RB_R, RB_V, RB_KK, RB_G, RB_KF, RB_KB, RB_AF, RB_AB, RB_BONUS = range(9)
VEC_W0F, VEC_W0B, VEC_A0F, VEC_A0B, VEC_KK, VEC_KA, VEC_RK = range(7)
N_RKV_BLOCKS = 24
N_SHIFT_BLOCKS = 28


def _softplus(u):
    return jnp.maximum(u, 0.0) + jnp.log(1.0 + jnp.exp(-jnp.abs(u)))


def _prep_kernel(p_ref, pp_ref, pn_ref, sw_ref, wupf_ref, wupb_ref, aupf_ref, aupb_ref, gup_ref,
                 vec_ref, mbd_ref, rb_ref, lw_ref, *, tb, n_t, nhp):
    i = pl.program_id(1)
    row = lax.broadcasted_iota(I32, (tb, LANES), 0)
    has_prev = i > 0
    has_next = i < n_t - 1
    halo = pp_ref.shape[2]

    def shifted(q):
        cur = p_ref[0, q].astype(F32)
        prev_row = jnp.where(has_prev, pp_ref[0, q, halo - 1:halo, :].astype(F32), 0.0)
        next_row = jnp.where(has_next, pn_ref[0, q, 0:1, :].astype(F32), 0.0)
        up = jnp.where(row == 0, prev_row, pltpu.roll(cur, 1, 0))
        dn = jnp.where(row == tb - 1, next_row, pltpu.roll(cur, tb - 1, 0))
        return sw_ref[0, q:q + 1, :] * up + sw_ref[1, q:q + 1, :] * cur + sw_ref[2, q:q + 1, :] * dn

    mbd = mbd_ref[...]
    t_wd = jnp.tanh(shifted(N_RKV_BLOCKS)).astype(BF16)
    z_ad = shifted(N_RKV_BLOCKS + 1).astype(BF16)
    s_gd = jnp.concatenate([_sigmoid(shifted(N_RKV_BLOCKS + 2)),
                            _sigmoid(shifted(N_RKV_BLOCKS + 3))], axis=1).astype(BF16)

    def vec(v, hp):
        return vec_ref[v, hp:hp + 1, :]

    for hp in range(nhp):
        sl = slice(hp * LANES, (hp + 1) * LANES)
        r = shifted(hp)
        k = shifted(nhp + hp)
        v = shifted(2 * nhp + hp)
        g = jnp.dot(s_gd, gup_ref[:, sl], preferred_element_type=F32)
        kk0 = k * vec(VEC_KK, hp)
        ss = _head_sums_split(kk0 * kk0, mbd)
        kk = kk0 * lax.rsqrt(jnp.maximum(ss, L2_EPS))
        kdirs = []
        for wup_ref, aup_ref, v_w0, v_a0, q_k, q_a, lw_slot in (
                (wupf_ref, aupf_ref, VEC_W0F, VEC_A0F, RB_KF, RB_AF, 0),
                (wupb_ref, aupb_ref, VEC_W0B, VEC_A0B, RB_KB, RB_AB, 1)):
            wl = vec(v_w0, hp) + jnp.dot(t_wd, wup_ref[:, sl], preferred_element_type=F32)
            w_log = -_softplus(-wl) - 0.5
            lw_ref[0, lw_slot * nhp + hp] = -jnp.exp(w_log)
            ag = _sigmoid(vec(v_a0, hp) + jnp.dot(z_ad, aup_ref[:, sl], preferred_element_type=F32))
            kd = k * (1.0 + (ag - 1.0) * vec(VEC_KA, hp))
            rb_ref[0, q_k * nhp + hp] = kd.astype(BF16)
            rb_ref[0, q_a * nhp + hp] = ag.astype(BF16)
            kdirs.append(kd)
        bonus = _head_sums_split(r * kdirs[0] * vec(VEC_RK, hp), mbd) * v
        rb_ref[0, RB_R * nhp + hp] = r.astype(BF16)
        rb_ref[0, RB_V * nhp + hp] = v.astype(BF16)
        rb_ref[0, RB_KK * nhp + hp] = kk.astype(BF16)
        rb_ref[0, RB_G * nhp + hp] = g.astype(BF16)
        rb_ref[0, RB_BONUS * nhp + hp] = bonus.astype(BF16)


def _prep_call(p, sw, wupf, wupb, aupf, aupb, gup, vecs, mbd, nhp, tb=256):
    b, _, s, _ = p.shape
    n_t = s // tb
    halo = 16
    hb = tb // halo
    full = lambda a: pl.BlockSpec(a.shape, lambda bi, i: (0,) * a.ndim)
    return pl.pallas_call(
        functools.partial(_prep_kernel, tb=tb, n_t=n_t, nhp=nhp),
        grid=(b, n_t),
        in_specs=[pl.BlockSpec((1, N_SHIFT_BLOCKS, tb, LANES), lambda bi, i: (bi, 0, i, 0)),
                  pl.BlockSpec((1, N_SHIFT_BLOCKS, halo, LANES),
                               lambda bi, i: (bi, 0, jnp.maximum(i * hb - 1, 0), 0)),
                  pl.BlockSpec((1, N_SHIFT_BLOCKS, halo, LANES),
                               lambda bi, i: (bi, 0, jnp.minimum((i + 1) * hb, s // halo - 1), 0)),
                  full(sw), full(wupf), full(wupb), full(aupf), full(aupb), full(gup), full(vecs), full(mbd)],
        out_specs=[pl.BlockSpec((1, 9 * nhp, tb, LANES), lambda bi, i: (bi, 0, i, 0)),
                   pl.BlockSpec((1, 2 * nhp, tb, LANES), lambda bi, i: (bi, 0, i, 0))],
        out_shape=[jax.ShapeDtypeStruct((b, 9 * nhp, s, LANES), BF16),
                   jax.ShapeDtypeStruct((b, 2 * nhp, s, LANES), F32)],
        compiler_params=_cparams(("parallel", "parallel")),
        name="prep",
    )(p, p, p, sw, wupf, wupb, aupf, aupb, gup, vecs, mbd)


def _wkv_kernel(r_ref, v_ref, kk_ref, kf_ref, af_ref, lf_ref, rr_ref, vr_ref, kkr_ref, kb_ref, ab_ref, lb_ref,
                yf_ref, yb_ref, s_sc, *, L, nhp):
    @pl.when(pl.program_id(1) == 0)
    def _():
        s_sc[...] = jnp.zeros(s_sc.shape, F32)

    n2 = HEADS_PER_BLOCK * L
    ri = lax.broadcasted_iota(I32, (n2, LANES), 0)
    ci = lax.broadcasted_iota(I32, (n2, LANES), 1)
    head_match = (ri // L) == (ci // HEAD_DIM)
    si = lax.broadcasted_iota(I32, (n2, n2), 0)
    sj = lax.broadcasted_iota(I32, (n2, n2), 1)
    st, ss = si % L, sj % L
    eye = (si == sj).astype(F32)
    strict = {False: st > ss, True: st < ss}
    incl = {False: st >= ss, True: st < ss}

    def stack(x):
        return jnp.where(head_match, jnp.concatenate([x] * HEADS_PER_BLOCK, axis=0), 0.0)

    chains = [(False, hp, r_ref, v_ref, kk_ref, kf_ref, af_ref, lf_ref) for hp in range(nhp)]
    chains += [(True, hp, rr_ref, vr_ref, kkr_ref, kb_ref, ab_ref, lb_ref) for hp in range(nhp)]
    n = len(chains)
    lws = [ch[7][0, ch[1]] for ch in chains]
    t_row = lax.broadcasted_iota(I32, (L, LANES), 0)

    def cumsum_time(x, rev):
        sh = 1
        while sh < L:
            if rev:
                x = x + jnp.where(t_row < L - sh, pltpu.roll(x, L - sh, 0), 0.0)
            else:
                x = x + jnp.where(t_row >= sh, pltpu.roll(x, sh, 0), 0.0)
            sh *= 2
        return x

    c_ins = [cumsum_time(lw, ch[0]) for ch, lw in zip(chains, lws)]
    lhs_l, rhs_l, vs_l, bk_l, pl_l = [], [], [], [], []
    for i, (rev, hp, rr, vr, kkr, kdr, agr, _) in enumerate(chains):
        r = rr[0, hp].astype(F32)
        kk = kkr[0, hp].astype(F32)
        kd = kdr[0, hp].astype(F32)
        ag = agr[0, hp].astype(F32)
        c_in = c_ins[i]
        e_ex = jnp.exp(c_in - lws[i])
        e_inv = jnp.exp(-c_in)
        if rev:
            e_r = e_ex
            p_last = jnp.exp(c_in[0:1])
        else:
            e_r = jnp.exp(c_in)
            p_last = jnp.exp(c_in[L - 1:L])
        b_s = stack(kk * ag * e_inv)
        k_s = stack(kd * e_inv)
        lhs_l.append(jnp.concatenate([stack(-kk * e_ex), stack(r * e_r)], axis=0).astype(BF16))
        rhs_l.append(jnp.concatenate([b_s, k_s], axis=0).astype(BF16))
        bk_l.append(jnp.concatenate([b_s * p_last, k_s * p_last], axis=0).astype(BF16))
        vs_l.append(stack(vr[0, hp].astype(F32)).astype(BF16))
        pl_l.append(p_last)
    g_l = [lax.dot_general(lhs_l[i], rhs_l[i], NT_DIMS, preferred_element_type=F32) for i in range(n)]
    s_old = [s_sc[i] for i in range(n)]
    ls_l = [lax.dot_general(lhs_l[i], s_old[i].astype(BF16), NT_DIMS, preferred_element_type=F32)
            for i in range(n)]
    a_ab = [jnp.where(strict[ch[0]], g[0:n2, 0:n2], 0.0) for ch, g in zip(chains, g_l)]
    a_ak = [jnp.where(strict[ch[0]], g[0:n2, n2:2 * n2], 0.0).astype(BF16) for ch, g in zip(chains, g_l)]
    a_r = [jnp.concatenate([jnp.where(incl[ch[0]], g[n2:2 * n2, 0:n2], 0.0),
                            jnp.where(incl[ch[0]], g[n2:2 * n2, n2:2 * n2], 0.0)], axis=1).astype(BF16)
           for ch, g in zip(chains, g_l)]
    x_l = [ls_l[i][0:n2] + jnp.dot(a_ak[i], vs_l[i], preferred_element_type=F32) for i in range(n)]
    rounds = L.bit_length() - 1
    pw = [jnp.dot(a.astype(BF16), a.astype(BF16), preferred_element_type=F32) for a in a_ab]
    t_inv = [eye + a for a in a_ab]
    for j in range(1, rounds):
        pb = [p.astype(BF16) for p in pw]
        if j + 1 < rounds:
            both = [jnp.dot(jnp.concatenate([pb[i], t_inv[i].astype(BF16)], axis=0), pb[i],
                            preferred_element_type=F32) for i in range(n)]
            pw = [m[0:n2] for m in both]
            t_inv = [t + m[n2:2 * n2] for t, m in zip(t_inv, both)]
        else:
            t_inv = [t + jnp.dot(t.astype(BF16), pb[i], preferred_element_type=F32)
                     for i, t in enumerate(t_inv)]
    u_l = [jnp.dot(t_inv[i].astype(BF16), x_l[i].astype(BF16), preferred_element_type=F32) for i in range(n)]
    uv_l = [jnp.concatenate([jnp.where(head_match, u_l[i], 0.0).astype(BF16), vs_l[i]], axis=0)
            for i in range(n)]
    y_l = [ls_l[i][n2:2 * n2] + jnp.dot(a_r[i], uv_l[i], preferred_element_type=F32) for i in range(n)]
    upd_l = [lax.dot_general(uv_l[i], bk_l[i], TN_DIMS, preferred_element_type=F32) for i in range(n)]
    for i, (rev, hp, *_) in enumerate(chains):
        y = y_l[i]
        out = y[0:L]
        for h in range(1, HEADS_PER_BLOCK):
            out = out + y[h * L:(h + 1) * L]
        (yb_ref if rev else yf_ref)[0, hp] = out
        s_sc[i] = s_old[i] * pl_l[i] + upd_l[i]


def _wkv_call(rb, lw, nhp, L=WKV_CHUNK):
    b, _, s, _ = rb.shape
    nc = s // L
    fwd = lambda q: pl.BlockSpec((1, nhp, L, LANES), lambda bi, c: (bi, q, c, 0))
    bwd = lambda q: pl.BlockSpec((1, nhp, L, LANES), lambda bi, c: (bi, q, nc - 1 - c, 0))
    y_shape = jax.ShapeDtypeStruct((b, nhp, s, LANES), F32)
    return pl.pallas_call(
        functools.partial(_wkv_kernel, L=L, nhp=nhp),
        grid=(b, nc),
        in_specs=[fwd(RB_R), fwd(RB_V), fwd(RB_KK), fwd(RB_KF), fwd(RB_AF), fwd(0),
                  bwd(RB_R), bwd(RB_V), bwd(RB_KK), bwd(RB_KB), bwd(RB_AB), bwd(1)],
        out_specs=[fwd(0), bwd(0)],
        out_shape=[y_shape, y_shape],
        scratch_shapes=[pltpu.VMEM((2 * nhp, LANES, LANES), F32)],
        compiler_params=_cparams(("parallel", "arbitrary")),
        name="wkv",
    )(rb, rb, rb, rb, rb, lw, rb, rb, rb, rb, rb, lw)


def _post_kernel(yf_ref, yb_ref, bonus_ref, g_ref, lnw_ref, lnb_ref, mbd_ref, o_ref, *, nhp):
    mbd = mbd_ref[...]
    inv_n = 1.0 / HEAD_DIM
    for hp in range(nhp):
        y = yf_ref[0, hp] + yb_ref[0, hp]
        mu = _head_sums_split(y, mbd) * inv_n
        d = y - mu
        var = _head_sums_split(d * d, mbd) * inv_n
        yn = d * lax.rsqrt(var + GN_EPS) * lnw_ref[hp:hp + 1, :] + lnb_ref[hp:hp + 1, :]
        out = (yn + bonus_ref[0, hp].astype(F32)) * g_ref[0, hp].astype(F32)
        o_ref[0, :, hp * LANES:(hp + 1) * LANES] = out.astype(o_ref.dtype)


def _post_call(yf, yb, rb, lnw, lnb, mbd, nhp, tb=256):
    b, _, s, _ = yf.shape
    yblk = pl.BlockSpec((1, nhp, tb, LANES), lambda bi, i: (bi, 0, i, 0))
    rblk = lambda q: pl.BlockSpec((1, nhp, tb, LANES), lambda bi, i: (bi, q, i, 0))
    full = lambda a: pl.BlockSpec(a.shape, lambda bi, i: (0,) * a.ndim)
    return pl.pallas_call(
        functools.partial(_post_kernel, nhp=nhp),
        grid=(b, s // tb),
        in_specs=[yblk, yblk, rblk(RB_BONUS), rblk(RB_G), full(lnw), full(lnb), full(mbd)],
        out_specs=pl.BlockSpec((1, tb, nhp * LANES), lambda bi, i: (bi, i, 0)),
        out_shape=jax.ShapeDtypeStruct((b, s, nhp * LANES), BF16),
        compiler_params=_cparams(("parallel", "parallel")),
        name="post",
    )(yf, yb, rb, rb, lnw, lnb, mbd)


def _na_kernel(q_ref, k_ref, v_ref, bias_ref, qw_ref, kw_ref, mbd_ref, o_ref, kn_sc, *, rq, rows):
    rbi = pl.program_id(2)
    mbd = mbd_ref[...]
    inv_n = 1.0 / HEAD_DIM
    span = WIN_R * GRID_W
    s_len = kn_sc.shape[0]

    @pl.when(rbi == 0)
    def _():
        def body(c, carry):
            off = pl.multiple_of(c * span, span)
            k = k_ref[0, 0, pl.ds(off, span), :].astype(F32)
            ms = _head_sums_split(k * k, mbd) * inv_n
            kn_sc[pl.ds(off, span), :] = (k * lax.rsqrt(ms + NORM_EPS) * kw_ref[...]).astype(BF16)
            return carry
        lax.fori_loop(0, s_len // span, body, 0)

    ri = lax.broadcasted_iota(I32, (HEADS_PER_BLOCK * GRID_W, LANES), 0)
    ci = lax.broadcasted_iota(I32, (HEADS_PER_BLOCK * GRID_W, LANES), 1)
    head_match = (ri // GRID_W) == (ci // HEAD_DIM)
    lane = lax.broadcasted_iota(I32, (GRID_W, LANES), 1)
    scale = HEAD_DIM ** -0.5
    q_all = q_ref[0, 0].astype(F32)
    ms = _head_sums_split(q_all * q_all, mbd) * inv_n
    qn_all = q_all * lax.rsqrt(ms + NORM_EPS) * qw_ref[...] * scale
    starts, offs = [], []
    for qi in range(rq):
        i = rbi * rq + qi
        rs = jnp.clip(i - WIN_R // 2, 0, rows - WIN_R)
        offs.append(rs - i + WIN_R - 1)
        starts.append(pl.multiple_of(rs * GRID_W, GRID_W))
    qs_l = [jnp.where(head_match, jnp.concatenate([qn_all[qi * GRID_W:(qi + 1) * GRID_W]] * HEADS_PER_BLOCK,
                                                  axis=0), 0.0).astype(BF16) for qi in range(rq)]
    def bias(off):
        return jnp.concatenate([bias_ref[0, off + 2 * i] for i in range(WIN_R // 2)], axis=1)

    s_l = [lax.dot_general(qs_l[qi], kn_sc[pl.ds(starts[qi], span), :], NT_DIMS, preferred_element_type=F32)
           + bias(offs[qi]) for qi in range(rq)]
    p_l, l_l = [], []
    for s in s_l:
        p = jnp.exp(s - jnp.max(s, axis=-1, keepdims=True))
        p_l.append(p.astype(BF16))
        l_l.append(jnp.sum(p, axis=-1, keepdims=True))
    o_l = [jnp.dot(p_l[qi], v_ref[0, 0, pl.ds(starts[qi], span), :], preferred_element_type=F32) / l_l[qi]
           for qi in range(rq)]
    for qi, o in enumerate(o_l):
        out = o[0:GRID_W]
        for h in range(1, HEADS_PER_BLOCK):
            out = jnp.where(lane // HEAD_DIM == h, o[h * GRID_W:(h + 1) * GRID_W], out)
        o_ref[0, qi * GRID_W:(qi + 1) * GRID_W, :] = out.astype(o_ref.dtype)


def _na_call(p, bias, qw, kw, mbd, q_blk0, nhp, rq=16):
    b, _, s, _ = p.shape
    rows = s // GRID_W
    rq = min(rq, rows)
    full = lambda a: pl.BlockSpec(a.shape, lambda bi, hp, r: (0,) * a.ndim)
    return pl.pallas_call(
        functools.partial(_na_kernel, rq=rq, rows=rows),
        grid=(b, nhp, rows // rq),
        in_specs=[pl.BlockSpec((1, 1, rq * GRID_W, LANES), lambda bi, hp, r: (bi, q_blk0 + hp, r, 0)),
                  pl.BlockSpec((1, 1, s, LANES), lambda bi, hp, r: (bi, q_blk0 + nhp + hp, 0, 0)),
                  pl.BlockSpec((1, 1, s, LANES), lambda bi, hp, r: (bi, q_blk0 + 2 * nhp + hp, 0, 0)),
                  pl.BlockSpec((1,) + bias.shape[1:], lambda bi, hp, r: (hp, 0, 0, 0)),
                  full(qw), full(kw), full(mbd)],
        out_specs=pl.BlockSpec((1, rq * GRID_W, LANES), lambda bi, hp, r: (bi, r, hp)),
        out_shape=jax.ShapeDtypeStruct((b, s, nhp * LANES), BF16),
        scratch_shapes=[pltpu.VMEM((s, LANES), BF16)],
        compiler_params=_cparams(("parallel", "parallel", "arbitrary")),
        name="na",
    )(p, p, p, bias, qw, kw, mbd)


def _na_bias_table(rel_bias):
    h = rel_bias.shape[0]
    qc = np.arange(GRID_W)
    kc = np.arange(GRID_W)
    win0 = np.clip(qc - WIN_C // 2, 0, GRID_W - WIN_C)
    valid = (kc[None, :] >= win0[:, None]) & (kc[None, :] < win0[:, None] + WIN_C)
    coff = np.clip(kc[None, :] - qc[:, None] + WIN_C - 1, 0, 2 * WIN_C - 2)
    pick = jnp.asarray((coff[None] == np.arange(2 * WIN_C - 1)[:, None, None]).astype(np.float32))
    t = jnp.einsum('hrc,cqk->hrqk', rel_bias.astype(F32), pick, precision=HI)
    t = jnp.where(valid[None, None], t, NEG_BIG)
    t = jnp.concatenate([t[:, :-1], t[:, 1:]], axis=-1)
    n_ro = t.shape[1]
    t = t.reshape(h // HEADS_PER_BLOCK, HEADS_PER_BLOCK, n_ro, GRID_W, 2 * GRID_W)
    t = jnp.transpose(t, (0, 2, 1, 3, 4))
    return t.reshape(h // HEADS_PER_BLOCK, n_ro, HEADS_PER_BLOCK * GRID_W, 2 * GRID_W)


def _outproj_kernel(yr_ref, yn_ref, x_ref, mod_ref, nw_ref, w1_ref, w2_ref, wr_ref, x1_ref, h2_ref, h2row_ref,
                    lg_ref):
    tm = x_ref.shape[1]
    sub = LANES
    tiles = [slice(i * sub, (i + 1) * sub) for i in range(tm // sub)]
    accs = [jnp.dot(yr_ref[0, r], w1_ref[...], preferred_element_type=F32)
            + jnp.dot(yn_ref[0, r], w2_ref[...], preferred_element_type=F32) for r in tiles]
    h2s = []
    for r, acc in zip(tiles, accs):
        x1 = x_ref[0, r] + mod_ref[0, 2:3, :] * acc
        x1_ref[0, r] = x1
        ms = jnp.mean(x1 * x1, axis=-1, keepdims=True)
        y = x1 * lax.rsqrt(ms + NORM_EPS) * nw_ref[...]
        h2 = y * (1.0 + mod_ref[0, 4:5, :]) + mod_ref[0, 3:4, :]
        packed = _pack_halves(h2)
        h2row_ref[0, r] = packed
        for j in range(h2_ref.shape[2]):
            h2_ref[0, r, j, :] = packed[:, j * LANES:(j + 1) * LANES]
        h2s.append(h2)
    wr = wr_ref[...]
    wr_hi = wr.astype(BF16)
    wr_lo = (wr - wr_hi.astype(F32)).astype(BF16)
    for r, h2 in zip(tiles, h2s):
        h_hi = h2.astype(BF16)
        h_lo = (h2 - h_hi.astype(F32)).astype(BF16)
        lg_ref[:, r] = (lax.dot_general(wr_hi, h_hi, NT_DIMS, preferred_element_type=F32)
                        + lax.dot_general(wr_hi, h_lo, NT_DIMS, preferred_element_type=F32)
                        + lax.dot_general(wr_lo, h_hi, NT_DIMS, preferred_element_type=F32))


def _outproj_call(yr, yn, x, mod3, norm_w, w_o, wr_t, tm=512):
    b, s, d = x.shape
    dh = yr.shape[-1]
    assert w_o.shape == (2 * dh, d)
    ne = wr_t.shape[0]
    nt = s // tm
    full = lambda a: pl.BlockSpec(a.shape, lambda bi, i: (0,) * a.ndim)
    return pl.pallas_call(
        _outproj_kernel,
        grid=(b, nt),
        in_specs=[pl.BlockSpec((1, tm, dh), lambda bi, i: (bi, i, 0)),
                  pl.BlockSpec((1, tm, dh), lambda bi, i: (bi, i, 0)),
                  pl.BlockSpec((1, tm, d), lambda bi, i: (bi, i, 0)),
                  pl.BlockSpec((1, 6, d), lambda bi, i: (bi, 0, 0)),
                  full(norm_w),
                  pl.BlockSpec((dh, d), lambda bi, i: (0, 0)),
                  pl.BlockSpec((dh, d), lambda bi, i: (1, 0)),
                  full(wr_t)],
        out_specs=[pl.BlockSpec((1, tm, d), lambda bi, i: (bi, i, 0)),
                   pl.BlockSpec((1, tm, d // 2 // LANES, LANES), lambda bi, i: (bi, i, 0, 0)),
                   pl.BlockSpec((1, tm, d // 2), lambda bi, i: (bi, i, 0)),
                   pl.BlockSpec((ne, tm), lambda bi, i: (0, bi * nt + i))],
        out_shape=[jax.ShapeDtypeStruct((b, s, d), F32),
                   jax.ShapeDtypeStruct((b, s, d // 2 // LANES, LANES), U32),
                   jax.ShapeDtypeStruct((b, s, d // 2), U32),
                   jax.ShapeDtypeStruct((ne, b * s), F32)],
        compiler_params=_cparams(("parallel", "parallel")),
        name="outproj",
    )(yr, yn, x, mod3, norm_w, w_o, w_o, wr_t)


def _first_argmax(x, idx, n):
    m = jnp.max(x, axis=0, keepdims=True)
    a = jnp.min(jnp.where(x == m, idx, n), axis=0, keepdims=True)
    return m, a


def _route_kernel(lg_ref, bias_ref, idx_ref, gw_ref, pos_ref, cnt_ref, *, tr):
    @pl.when(pl.program_id(0) == 0)
    def _():
        cnt_ref[...] = jnp.zeros(cnt_ref.shape, F32)

    gsz = N_EXPERTS // N_GROUPS
    scores = _sigmoid(lg_ref[...])
    biased = scores + bias_ref[:, 0:1]
    ig = lax.broadcasted_iota(I32, (gsz, tr), 0)
    grp_rows = []
    for g in range(N_GROUPS):
        blk = biased[g * gsz:(g + 1) * gsz]
        m1, a1 = _first_argmax(blk, ig, gsz)
        m2 = jnp.max(jnp.where(ig == a1, -jnp.inf, blk), axis=0, keepdims=True)
        grp_rows.append(m1 + m2)
    grp = jnp.concatenate(grp_rows, axis=0)
    ign = lax.broadcasted_iota(I32, (N_GROUPS, tr), 0)
    sel = jnp.zeros((N_GROUPS, tr), jnp.bool_)
    for _ in range(TOPK_GROUPS):
        _, a = _first_argmax(grp, ign, N_GROUPS)
        hit = ign == a
        sel = jnp.logical_or(sel, hit)
        grp = jnp.where(hit, -jnp.inf, grp)
    masked = jnp.concatenate(
        [jnp.where(sel[g:g + 1], biased[g * gsz:(g + 1) * gsz], -jnp.inf) for g in range(N_GROUPS)], axis=0)
    ie = lax.broadcasted_iota(I32, (N_EXPERTS, tr), 0)
    picks, pick_scores = [], []
    onehot = jnp.zeros((N_EXPERTS, tr), F32)
    for _ in range(TOP_K):
        _, a = _first_argmax(masked, ie, N_EXPERTS)
        hit = ie == a
        picks.append(a)
        pick_scores.append(jnp.sum(jnp.where(hit, scores, 0.0), axis=0, keepdims=True))
        onehot = onehot + hit.astype(F32)
        masked = jnp.where(hit, -jnp.inf, masked)
    total = pick_scores[0]
    for sc in pick_scores[1:]:
        total = total + sc
    t0 = lax.broadcasted_iota(I32, (tr, tr), 0)
    t1 = lax.broadcasted_iota(I32, (tr, tr), 1)
    before = (t0 < t1).astype(BF16)
    rank = jnp.dot(onehot.astype(BF16), before, preferred_element_type=F32) + cnt_ref[:, 0:1]
    zero_i = jnp.zeros((1, tr), I32)
    zero_f = jnp.zeros((1, tr), F32)
    for k in range(8):
        if k < TOP_K:
            idx_ref[k:k + 1, :] = picks[k]
            gw_ref[k:k + 1, :] = pick_scores[k] / total * ROUTED_SCALE
            pos = jnp.sum(jnp.where(ie == picks[k], rank, 0.0), axis=0, keepdims=True)
            pos_ref[k:k + 1, :] = pos.astype(I32)
        else:
            idx_ref[k:k + 1, :] = zero_i
            gw_ref[k:k + 1, :] = zero_f
            pos_ref[k:k + 1, :] = zero_i
    cnt_ref[...] = cnt_ref[...] + jnp.sum(onehot, axis=1, keepdims=True)


def _route_call(lg_t, bias2, tr=512):
    ne, t = lg_t.shape
    tr = min(tr, t)
    tok = pl.BlockSpec((8, tr), lambda i: (0, i))
    return pl.pallas_call(
        functools.partial(_route_kernel, tr=tr),
        grid=(t // tr,),
        in_specs=[pl.BlockSpec((ne, tr), lambda i: (0, i)),
                  pl.BlockSpec(bias2.shape, lambda i: (0, 0))],
        out_specs=[tok, tok, tok, pl.BlockSpec((ne, LANES), lambda i: (0, 0))],
        out_shape=[jax.ShapeDtypeStruct((8, t), I32), jax.ShapeDtypeStruct((8, t), F32),
                   jax.ShapeDtypeStruct((8, t), I32), jax.ShapeDtypeStruct((ne, LANES), F32)],
        compiler_params=_cparams(("arbitrary",)),
        name="route",
    )(lg_t, bias2)


FILL_CHUNKS = (1, 8, 64)
ZERO_ROWS = FILL_CHUNKS[-1]


def _dispatch_kernel(slot_ref, fill_lo_ref, fill_hi_ref, h_ref, xs_ref, zrow, sem, *, td):
    def row_copy(t, slot):
        return pltpu.make_async_copy(h_ref.at[t], xs_ref.at[slot], sem)

    def start(t, carry):
        for k in range(TOP_K):
            row_copy(t, slot_ref[t * SLOT_STRIDE + k]).start(priority=k % 2)
        return carry

    lax.fori_loop(0, td, start, 0)
    for k in range(TOP_K):
        pltpu.make_async_copy(h_ref, xs_ref.at[pl.ds(0, td)], sem).wait()

    @pl.when(pl.program_id(0) == pl.num_programs(0) - 1)
    def _():
        zrow[...] = jnp.zeros(zrow.shape, zrow.dtype)

        def zero_copy(row, n):
            if n > 1:
                row = pl.multiple_of(row, FILL_CHUNKS[1])
            return pltpu.make_async_copy(zrow.at[pl.ds(0, n)], xs_ref.at[pl.ds(row, n)], sem)

        def for_each_chunk(r, fn):
            lo, hi = fill_lo_ref[r], fill_hi_ref[r]
            for ci, n in enumerate(FILL_CHUNKS):
                if ci + 1 < len(FILL_CHUNKS):
                    count = jnp.minimum(((-lo) % FILL_CHUNKS[ci + 1]) // n, (hi - lo) // n)
                else:
                    count = (hi - lo) // n
                fn(lo, n, count)
                lo = lo + count * n

        def start_range(r, carry):
            def go(lo, n, count):
                lax.fori_loop(0, count, lambda i, c: (zero_copy(lo + i * n, n).start(), c)[1], 0)
            for_each_chunk(r, go)
            return carry

        def wait_range(r, carry):
            def go(lo, n, count):
                lax.fori_loop(0, count, lambda i, c: (zero_copy(0, n).wait(), c)[1], 0)
            for_each_chunk(r, go)
            return carry

        n_ranges = fill_lo_ref.shape[0]
        lax.fori_loop(0, n_ranges, start_range, 0)
        lax.fori_loop(0, n_ranges, wait_range, 0)


def _dispatch_call(slots_flat, fill_lo, fill_hi, h2, n_rows, td=256):
    t, nj, _ = h2.shape
    smem = pl.BlockSpec(memory_space=pltpu.SMEM)
    return pl.pallas_call(
        functools.partial(_dispatch_kernel, td=td),
        grid=(t // td,),
        in_specs=[pl.BlockSpec((td * SLOT_STRIDE,), lambda i: (i,), memory_space=pltpu.SMEM),
                  smem, smem,
                  pl.BlockSpec((td, nj, LANES), lambda i: (i, 0, 0))],
        out_specs=pl.BlockSpec(memory_space=pl.ANY),
        out_shape=jax.ShapeDtypeStruct((n_rows, nj, LANES), h2.dtype),
        scratch_shapes=[pltpu.VMEM((ZERO_ROWS, nj, LANES), h2.dtype), pltpu.SemaphoreType.DMA(())],
        compiler_params=_cparams(("arbitrary",)),
        name="dispatch",
    )(slots_flat, fill_lo, fill_hi, h2)


def _experts_kernel(be_ref, first_ref, nxt_ref, par_ref, rows_ref, nv_ref, xs_hbm, wg_hbm, wu_hbm, wd_hbm, ys_ref,
                    wg_buf, wu_buf, wd_buf, wg_sc, wu_sc, wd_sc, xpack, sem, isem):
    nj = xs_hbm.shape[1]
    b = pl.program_id(0)
    m = xpack.shape[1]
    half = wg_sc.shape[0] // 2
    oslot = lax.rem(b, 2)

    def in_copies(slot, row0):
        return [pltpu.make_async_copy(xs_hbm.at[pl.ds(row0, m), j],
                                      xpack.at[slot, :, pl.ds(j * LANES, LANES)], isem.at[slot]) for j in range(nj)]

    @pl.when(b == 0)
    def _():
        for c in in_copies(0, 0):
            c.start()

    @pl.when(b + 1 < nv_ref[0])
    def _():
        for c in in_copies(1 - oslot, pl.multiple_of((b + 1) * m, m)):
            c.start()

    def weight_copies(e, slot):
        return [pltpu.make_async_copy(src.at[e], dst.at[slot], sem.at[slot, i])
                for i, (src, dst) in enumerate(((wg_hbm, wg_buf), (wu_hbm, wu_buf), (wd_hbm, wd_buf)))]

    def compute(nrows):
        x_lo, x_hi = _unpack_halves(xpack[oslot, 0:nrows])
        x_lo, x_hi = x_lo.astype(BF16), x_hi.astype(BF16)
        g = (jnp.dot(x_lo, wg_sc[0:half], preferred_element_type=F32)
             + jnp.dot(x_hi, wg_sc[half:], preferred_element_type=F32))
        u = (jnp.dot(x_lo, wu_sc[0:half], preferred_element_type=F32)
             + jnp.dot(x_hi, wu_sc[half:], preferred_element_type=F32))
        a = (_silu(g) * u).astype(BF16)
        ys_ref[0:nrows] = _pack_halves(jnp.dot(a, wd_sc[...], preferred_element_type=F32))

    @pl.when(b < nv_ref[0])
    def _():
        for c in in_copies(oslot, 0):
            c.wait()

        @pl.when(b == 0)
        def _():
            for c in weight_copies(be_ref[0], par_ref[0]):
                c.start()

        @pl.when(first_ref[b] == 1)
        def _():
            slot = par_ref[b]
            for c in weight_copies(be_ref[b], slot):
                c.wait()

            @pl.when(nxt_ref[b] >= 0)
            def _():
                for c in weight_copies(nxt_ref[b], 1 - slot):
                    c.start()

            wg_sc[...] = wg_buf[slot].astype(BF16)
            wu_sc[...] = wu_buf[slot].astype(BF16)
            wd_sc[...] = wd_buf[slot].astype(BF16)

        @pl.when(rows_ref[b] > m // 2)
        def _():
            compute(m)

        @pl.when(rows_ref[b] <= m // 2)
        def _():
            compute(m // 2)
            ys_ref[m // 2:] = jnp.zeros((m - m // 2, ys_ref.shape[1]), ys_ref.dtype)

    @pl.when(b >= nv_ref[0])
    def _():
        ys_ref[...] = jnp.zeros(ys_ref.shape, ys_ref.dtype)


def _experts_call(block_e, first, nxt, par, rows_valid, n_valid, xs, wg, wu, wd):
    total, nj, _ = xs.shape
    d = wg.shape[1]
    m = DISPATCH_BLOCK
    ff = wg.shape[-1]

    hbm = pl.BlockSpec(memory_space=pl.ANY)
    grid_spec = pltpu.PrefetchScalarGridSpec(
        num_scalar_prefetch=6,
        grid=(total // m,),
        in_specs=[hbm, hbm, hbm, hbm],
        out_specs=pl.BlockSpec((m, nj * LANES), lambda b, *s: (b, 0)),
        scratch_shapes=[pltpu.VMEM((2, d, ff), F32), pltpu.VMEM((2, d, ff), F32), pltpu.VMEM((2, ff, d), F32),
                        pltpu.VMEM((d, ff), BF16), pltpu.VMEM((d, ff), BF16), pltpu.VMEM((ff, d), BF16),
                        pltpu.VMEM((2, m, nj * LANES), U32),
                        pltpu.SemaphoreType.DMA((2, 3)), pltpu.SemaphoreType.DMA((2,))],
    )
    return pl.pallas_call(
        _experts_kernel,
        grid_spec=grid_spec,
        out_shape=jax.ShapeDtypeStruct((total, nj * LANES), U32),
        compiler_params=_cparams(("arbitrary",)),
        name="experts",
    )(block_e, first, nxt, par, rows_valid, n_valid, xs, wg, wu, wd)


def _combine_kernel(slot_ref, x1_ref, h2_ref, gw_ref, mod_ref, wg_ref, wu_ref, wd_ref, ys_ref, o_ref,
                    buf, sem, *, tc):
    def row_copy(t, k, slot):
        return pltpu.make_async_copy(ys_ref.at[pl.ds(slot, 1)], buf.at[k, pl.ds(t, 1)], sem)

    def start(t, carry):
        for k in range(TOP_K):
            row_copy(t, k, slot_ref[t * SLOT_STRIDE + k]).start(priority=k % 2)
        return carry

    lax.fori_loop(0, tc, start, 0)
    half = wg_ref.shape[0] // 2
    h_lo, h_hi = _unpack_halves(h2_ref[0])
    h_lo, h_hi = h_lo.astype(BF16), h_hi.astype(BF16)
    g = (jnp.dot(h_lo, wg_ref[0:half], preferred_element_type=F32)
         + jnp.dot(h_hi, wg_ref[half:], preferred_element_type=F32))
    u = (jnp.dot(h_lo, wu_ref[0:half], preferred_element_type=F32)
         + jnp.dot(h_hi, wu_ref[half:], preferred_element_type=F32))
    acc = jnp.dot((_silu(g) * u).astype(BF16), wd_ref[...], preferred_element_type=F32)
    for k in range(TOP_K):
        pltpu.make_async_copy(ys_ref.at[pl.ds(0, tc)], buf.at[k], sem).wait()
    gw = gw_ref[...]
    acc_lo, acc_hi = acc[:, :half], acc[:, half:]
    for k in range(TOP_K):
        y_lo, y_hi = _unpack_halves(buf[k])
        acc_lo = acc_lo + gw[:, k:k + 1] * y_lo
        acc_hi = acc_hi + gw[:, k:k + 1] * y_hi
    gate = mod_ref[0, 5:6, :]
    o_ref[0, :, 0:half] = x1_ref[0, :, 0:half] + gate[:, :half] * acc_lo
    o_ref[0, :, half:] = x1_ref[0, :, half:] + gate[:, half:] * acc_hi


def _combine_call(slots_flat, x1, h2, gw_tok, mod3, wg, wu, wd, ys, tc=256):
    b, s, d = x1.shape
    nt = s // tc
    full = lambda a: pl.BlockSpec(a.shape, lambda bi, i: (0,) * a.ndim)
    return pl.pallas_call(
        functools.partial(_combine_kernel, tc=tc),
        grid=(b, nt),
        in_specs=[pl.BlockSpec((tc * SLOT_STRIDE,), lambda bi, i: (bi * nt + i,), memory_space=pltpu.SMEM),
                  pl.BlockSpec((1, tc, d), lambda bi, i: (bi, i, 0)),
                  pl.BlockSpec((1, tc, d // 2), lambda bi, i: (bi, i, 0)),
                  pl.BlockSpec((tc, 8), lambda bi, i: (bi * nt + i, 0)),
                  pl.BlockSpec((1, 6, d), lambda bi, i: (bi, 0, 0)),
                  full(wg), full(wu), full(wd),
                  pl.BlockSpec(memory_space=pl.ANY)],
        out_specs=pl.BlockSpec((1, tc, d), lambda bi, i: (bi, i, 0)),
        out_shape=jax.ShapeDtypeStruct((b, s, d), F32),
        scratch_shapes=[pltpu.VMEM((TOP_K, tc, d // 2), U32), pltpu.SemaphoreType.DMA(())],
        compiler_params=_cparams(("arbitrary", "arbitrary")),
        name="combine",
    )(slots_flat, x1, h2, gw_tok, mod3, wg, wu, wd, ys)


def _pad_rows(w, rows, at=0):
    out = jnp.zeros((rows, w.shape[1]), w.dtype)
    return out.at[at:at + w.shape[0]].set(w)


def _mixing_stage(p, rwkv_shift, w0_f, w_up_f, w0_b, w_up_b, a0_f, a_up_f, a0_b, a_up_b, g_up,
                  k_k, k_a, r_k, ln_x_w, ln_x_b, q_norm_w, k_norm_w, rel_bias):
    d_rwkv = w0_f.shape[0]
    nhp = d_rwkv // LANES
    mbd = _head_sum_matrix()
    n_shift = rwkv_shift.shape[1]
    sw = jnp.pad(rwkv_shift, ((0, 0), (0, N_SHIFT_BLOCKS * LANES - n_shift))).reshape(3, N_SHIFT_BLOCKS, LANES)
    wupf = _pad_rows(w_up_f, LANES, 0).astype(BF16)
    wupb = _pad_rows(w_up_b, LANES, DECAY_LORA).astype(BF16)
    aupf = _pad_rows(a_up_f, LANES, 0).astype(BF16)
    aupb = _pad_rows(a_up_b, LANES, AAA_LORA).astype(BF16)
    gup = _pad_rows(g_up, 2 * LANES, 0).astype(BF16)
    vecs = jnp.stack([w0_f, w0_b, a0_f, a0_b, k_k, k_a, r_k.reshape(-1)]).reshape(7, nhp, LANES)
    rb, lw = _prep_call(p, sw, wupf, wupb, aupf, aupb, gup, vecs, mbd, nhp)
    y_f, y_b = _wkv_call(rb, lw, nhp)
    y_rwkv = _post_call(y_f, y_b, rb, ln_x_w.reshape(nhp, LANES), ln_x_b.reshape(nhp, LANES), mbd, nhp)
    bias = _na_bias_table(rel_bias)
    qw = jnp.tile(q_norm_w, HEADS_PER_BLOCK).reshape(1, LANES)
    kw = jnp.tile(k_norm_w, HEADS_PER_BLOCK).reshape(1, LANES)
    y_na = _na_call(p, bias, qw, kw, mbd, N_SHIFT_BLOCKS, nhp)
    return y_rwkv, y_na


def _moe_stage(x1, h2_tiles, h2_rows, lg_t, mod3, router_bias, w_gate_e, w_up_e, w_down_e, w_gate_s, w_up_s,
               w_down_s):
    b, s, d = x1.shape
    t = b * s
    m = DISPATCH_BLOCK
    bias2 = jnp.broadcast_to(router_bias.astype(F32)[:, None], (N_EXPERTS, LANES))
    idx_t, gw_t, pos_t, cnt = _route_call(lg_t, bias2)
    counts = cnt[:, 0].astype(I32)
    padded = (counts + m - 1) // m * m
    pad_end = jnp.cumsum(padded)
    pad_start = pad_end - padded
    n_blocks = -(-(t * TOP_K + N_EXPERTS * m) // m)
    n_valid = (pad_end[-1] // m).astype(I32).reshape(1)
    block_row0 = jnp.arange(n_blocks, dtype=I32) * m
    block_e = jnp.minimum(jnp.sum((pad_end[None, :] <= block_row0[:, None]).astype(I32), axis=1),
                          N_EXPERTS - 1).astype(I32)
    assert m % ZERO_ROWS == 0
    expert_ids = jnp.arange(N_EXPERTS, dtype=I32)
    start_of = jnp.sum(jnp.where(idx_t[:, :, None] == expert_ids, pad_start, 0), axis=-1)
    slots = (start_of + pos_t).astype(I32)
    slots_flat = slots.T.reshape(-1)
    gw_tok = gw_t.T
    fill_lo = jnp.concatenate([pad_start + counts, pad_end[-1:]]).astype(I32)
    fill_hi = jnp.concatenate([pad_end, jnp.full((1,), n_blocks * m, I32)]).astype(I32)
    xs = _dispatch_call(slots_flat, fill_lo, fill_hi, h2_tiles.reshape(t, *h2_tiles.shape[2:]), n_blocks * m)
    row_end = (pad_start + counts).astype(I32)
    nonempty = counts > 0
    first = jnp.logical_and(block_row0 == pad_start[block_e], block_row0 < pad_end[-1]).astype(I32)
    ordinal = jnp.cumsum(nonempty.astype(I32)) - 1
    par = (ordinal[block_e] & 1).astype(I32)
    cand = jnp.where(nonempty, expert_ids, N_EXPERTS)
    later = jnp.concatenate([lax.cummin(cand[::-1])[::-1][1:], jnp.full((1,), N_EXPERTS, I32)])
    nxt = jnp.where(later < N_EXPERTS, later, -1)[block_e].astype(I32)
    rows_valid = jnp.clip(row_end[block_e] - block_row0, 0, m).astype(I32)
    ys = _experts_call(block_e, first, nxt, par, rows_valid, n_valid, xs, w_gate_e, w_up_e, w_down_e)
    return _combine_call(slots_flat, x1, h2_rows, gw_tok, mod3, w_gate_s.astype(BF16), w_up_s.astype(BF16),
                         w_down_s.astype(BF16), ys)


def kernel(x, c, w_ada, b_ada, norm1_w, w_in, rwkv_shift, w0_f, w_up_f, w0_b, w_up_b, a0_f, a_up_f, a0_b,
           a_up_b, g_up, k_k, k_a, r_k, ln_x_w, ln_x_b, q_norm_w, k_norm_w, rel_bias, w_out, norm2_w,
           w_router, router_bias, w_gate_e, w_up_e, w_down_e, w_gate_s, w_up_s, w_down_s):
    bn, sn, d = x.shape
    depth = w_ada.shape[0]
    for l in range(depth):
        c_pad = jnp.pad(c, ((0, 8 - bn % 8 if bn % 8 else 0), (0, 0)))
        mod = _mod_call(c_pad, w_ada[l], b_ada[l].reshape(1, -1))[:bn]
        mod3 = mod.reshape(bn, 6, d)
        rwkv_in = rwkv_shift.shape[-1]
        pad_cols = N_SHIFT_BLOCKS * LANES - rwkv_in
        w_in_r = jnp.concatenate([w_in[l][:, :rwkv_in], jnp.zeros((d, pad_cols), w_in.dtype),
                                  w_in[l][:, rwkv_in:]], axis=1).astype(BF16)
        p = _proj_call(x, mod3, norm1_w[l].reshape(1, d), w_in_r)
        y_rwkv, y_na = _mixing_stage(p, rwkv_shift[l], w0_f[l], w_up_f[l], w0_b[l], w_up_b[l], a0_f[l],
                                     a_up_f[l], a0_b[l], a_up_b[l], g_up[l], k_k[l], k_a[l], r_k[l],
                                     ln_x_w[l], ln_x_b[l], q_norm_w[l], k_norm_w[l], rel_bias[l])
        x1, h2_tiles, h2_rows, lg_t = _outproj_call(y_rwkv, y_na, x, mod3, norm2_w[l].reshape(1, d),
                                                    w_out[l].astype(BF16), w_router[l].T)
        x = _moe_stage(x1, h2_tiles, h2_rows, lg_t, mod3, router_bias[l], w_gate_e[l], w_up_e[l], w_down_e[l],
                       w_gate_s[l], w_up_s[l], w_down_s[l])
    return x
```

```python
import functools

import jax
import jax.numpy as jnp
import numpy as np
from jax import lax
from jax.experimental import pallas as pl
from jax.experimental.pallas import tpu as pltpu

F32 = jnp.float32
BF16 = jnp.bfloat16
I32 = jnp.int32
HI = lax.Precision.HIGHEST

LANES = 128
HEAD_DIM = 64
HEADS_PER_BLOCK = LANES // HEAD_DIM
GRID_W = 64
WIN_R = 8
WIN_C = 16
DECAY_LORA = 64
AAA_LORA = 64
GATE_LORA = 160
N_EXPERTS = 64
TOP_K = 6
SLOT_STRIDE = 8
N_GROUPS = 8
TOPK_GROUPS = 4
ROUTED_SCALE = 2.5
DISPATCH_BLOCK = 512
NORM_EPS = 1e-6
GN_EPS = 64e-5
L2_EPS = 1e-12
NEG_BIG = -1e30
WKV_CHUNK = 64
VMEM_LIMIT = 56 * 1024 * 1024

NT_DIMS = (((1,), (1,)), ((), ()))
TN_DIMS = (((0,), (0,)), ((), ()))


def _cparams(sem):
    return pltpu.CompilerParams(dimension_semantics=sem, vmem_limit_bytes=VMEM_LIMIT)


def _sigmoid(x):
    return 1.0 / (1.0 + jnp.exp(-x))


def _silu(x):
    return x * _sigmoid(x)


U32 = jnp.uint32
HI_HALF_MASK = 0xFFFF0000


def _pack_halves(x):
    n = x.shape[-1] // 2
    bits = lax.bitcast_convert_type(x.astype(BF16).astype(F32), U32)
    return (bits[:, :n] >> 16) | bits[:, n:]


def _unpack_halves(w):
    lo = lax.bitcast_convert_type(w << 16, F32)
    hi = lax.bitcast_convert_type(w & U32(HI_HALF_MASK), F32)
    return lo, hi


def _head_sum_matrix():
    a = np.arange(LANES) // HEAD_DIM
    return jnp.asarray((a[:, None] == a[None, :]).astype(np.float32))


def _head_sums_split(x, mbd):
    m16 = mbd.astype(BF16)
    hi = x.astype(BF16)
    lo = (x - hi.astype(F32)).astype(BF16)
    return jnp.dot(hi, m16, preferred_element_type=F32) + jnp.dot(lo, m16, preferred_element_type=F32)


def _mod_kernel(c_ref, w_ref, b_ref, o_ref):
    c = c_ref[...]
    o_ref[...] = jnp.dot(_silu(c), w_ref[...], preferred_element_type=F32, precision=HI) + b_ref[...]


def _mod_call(c_pad, w_ada, b_ada):
    rows, d = c_pad.shape
    n = w_ada.shape[1]
    tn = 1024
    return pl.pallas_call(
        _mod_kernel,
        grid=(n // tn,),
        in_specs=[pl.BlockSpec((rows, d), lambda j: (0, 0)),
                  pl.BlockSpec((d, tn), lambda j: (0, j)),
                  pl.BlockSpec((1, tn), lambda j: (0, j))],
        out_specs=pl.BlockSpec((rows, tn), lambda j: (0, j)),
        out_shape=jax.ShapeDtypeStruct((rows, n), F32),
        compiler_params=_cparams(("parallel",)),
        name="mod",
    )(c_pad, w_ada, b_ada)


def _proj_kernel(x_ref, mod_ref, nw_ref, w_ref, o_ref, h_sc, *, nq):
    @pl.when(pl.program_id(2) == 0)
    def _():
        x = x_ref[0]
        ms = jnp.mean(x * x, axis=-1, keepdims=True)
        y = x * lax.rsqrt(ms + NORM_EPS) * nw_ref[...]
        h_sc[...] = (y * (1.0 + mod_ref[0, 1:2, :]) + mod_ref[0, 0:1, :]).astype(BF16)

    acc = jnp.dot(h_sc[...], w_ref[...], preferred_element_type=F32)
    for q in range(nq):
        o_ref[0, q] = acc[:, q * LANES:(q + 1) * LANES].astype(o_ref.dtype)


def _proj_call(x, mod3, norm_w, w_in_r, tm=1024, tn=1664):
    b, s, d = x.shape
    n = w_in_r.shape[1]
    nq = tn // LANES
    return pl.pallas_call(
        functools.partial(_proj_kernel, nq=nq),
        grid=(b, s // tm, n // tn),
        in_specs=[pl.BlockSpec((1, tm, d), lambda bi, i, j: (bi, i, 0)),
                  pl.BlockSpec((1, 6, d), lambda bi, i, j: (bi, 0, 0)),
                  pl.BlockSpec((1, d), lambda bi, i, j: (0, 0)),
                  pl.BlockSpec((d, tn), lambda bi, i, j: (0, j))],
        out_specs=pl.BlockSpec((1, nq, tm, LANES), lambda bi, i, j: (bi, j, i, 0)),
        out_shape=jax.ShapeDtypeStruct((b, n // LANES, s, LANES), BF16),
        scratch_shapes=[pltpu.VMEM((tm, d), BF16)],
        compiler_params=_cparams(("parallel", "parallel", "arbitrary")),
        name="proj",
    )(x, mod3, norm_w, w_in_r)


RB_R, RB_V, RB_KK, RB_G, RB_KF, RB_KB, RB_AF, RB_AB, RB_BONUS = range(9)
VEC_W0F, VEC_W0B, VEC_A0F, VEC_A0B, VEC_KK, VEC_KA, VEC_RK = range(7)
N_RKV_BLOCKS = 24
N_SHIFT_BLOCKS = 28


def _softplus(u):
    return jnp.maximum(u, 0.0) + jnp.log(1.0 + jnp.exp(-jnp.abs(u)))


def _prep_kernel(p_ref, pp_ref, pn_ref, sw_ref, wupf_ref, wupb_ref, aupf_ref, aupb_ref, gup_ref,
                 vec_ref, mbd_ref, rb_ref, lw_ref, *, tb, n_t, nhp):
    i = pl.program_id(1)
    row = lax.broadcasted_iota(I32, (tb, LANES), 0)
    has_prev = i > 0
    has_next = i < n_t - 1
    halo = pp_ref.shape[2]

    def shifted(q):
        cur = p_ref[0, q].astype(F32)
        prev_row = jnp.where(has_prev, pp_ref[0, q, halo - 1:halo, :].astype(F32), 0.0)
        next_row = jnp.where(has_next, pn_ref[0, q, 0:1, :].astype(F32), 0.0)
        up = jnp.where(row == 0, prev_row, pltpu.roll(cur, 1, 0))
        dn = jnp.where(row == tb - 1, next_row, pltpu.roll(cur, tb - 1, 0))
        return sw_ref[0, q:q + 1, :] * up + sw_ref[1, q:q + 1, :] * cur + sw_ref[2, q:q + 1, :] * dn

    mbd = mbd_ref[...]
    t_wd = jnp.tanh(shifted(N_RKV_BLOCKS)).astype(BF16)
    z_ad = shifted(N_RKV_BLOCKS + 1).astype(BF16)
    s_gd = jnp.concatenate([_sigmoid(shifted(N_RKV_BLOCKS + 2)),
                            _sigmoid(shifted(N_RKV_BLOCKS + 3))], axis=1).astype(BF16)

    def vec(v, hp):
        return vec_ref[v, hp:hp + 1, :]

    for hp in range(nhp):
        sl = slice(hp * LANES, (hp + 1) * LANES)
        r = shifted(hp)
        k = shifted(nhp + hp)
        v = shifted(2 * nhp + hp)
        g = jnp.dot(s_gd, gup_ref[:, sl], preferred_element_type=F32)
        kk0 = k * vec(VEC_KK, hp)
        ss = _head_sums_split(kk0 * kk0, mbd)
        kk = kk0 * lax.rsqrt(jnp.maximum(ss, L2_EPS))
        kdirs = []
        for wup_ref, aup_ref, v_w0, v_a0, q_k, q_a, lw_slot in (
                (wupf_ref, aupf_ref, VEC_W0F, VEC_A0F, RB_KF, RB_AF, 0),
                (wupb_ref, aupb_ref, VEC_W0B, VEC_A0B, RB_KB, RB_AB, 1)):
            wl = vec(v_w0, hp) + jnp.dot(t_wd, wup_ref[:, sl], preferred_element_type=F32)
            w_log = -_softplus(-wl) - 0.5
            lw_ref[0, lw_slot * nhp + hp] = -jnp.exp(w_log)
            ag = _sigmoid(vec(v_a0, hp) + jnp.dot(z_ad, aup_ref[:, sl], preferred_element_type=F32))
            kd = k * (1.0 + (ag - 1.0) * vec(VEC_KA, hp))
            rb_ref[0, q_k * nhp + hp] = kd.astype(BF16)
            rb_ref[0, q_a * nhp + hp] = ag.astype(BF16)
            kdirs.append(kd)
        bonus = _head_sums_split(r * kdirs[0] * vec(VEC_RK, hp), mbd) * v
        rb_ref[0, RB_R * nhp + hp] = r.astype(BF16)
        rb_ref[0, RB_V * nhp + hp] = v.astype(BF16)
        rb_ref[0, RB_KK * nhp + hp] = kk.astype(BF16)
        rb_ref[0, RB_G * nhp + hp] = g.astype(BF16)
        rb_ref[0, RB_BONUS * nhp + hp] = bonus.astype(BF16)


def _prep_call(p, sw, wupf, wupb, aupf, aupb, gup, vecs, mbd, nhp, tb=256):
    b, _, s, _ = p.shape
    n_t = s // tb
    halo = 16
    hb = tb // halo
    full = lambda a: pl.BlockSpec(a.shape, lambda bi, i: (0,) * a.ndim)
    return pl.pallas_call(
        functools.partial(_prep_kernel, tb=tb, n_t=n_t, nhp=nhp),
        grid=(b, n_t),
        in_specs=[pl.BlockSpec((1, N_SHIFT_BLOCKS, tb, LANES), lambda bi, i: (bi, 0, i, 0)),
                  pl.BlockSpec((1, N_SHIFT_BLOCKS, halo, LANES),
                               lambda bi, i: (bi, 0, jnp.maximum(i * hb - 1, 0), 0)),
                  pl.BlockSpec((1, N_SHIFT_BLOCKS, halo, LANES),
                               lambda bi, i: (bi, 0, jnp.minimum((i + 1) * hb, s // halo - 1), 0)),
                  full(sw), full(wupf), full(wupb), full(aupf), full(aupb), full(gup), full(vecs), full(mbd)],
        out_specs=[pl.BlockSpec((1, 9 * nhp, tb, LANES), lambda bi, i: (bi, 0, i, 0)),
                   pl.BlockSpec((1, 2 * nhp, tb, LANES), lambda bi, i: (bi, 0, i, 0))],
        out_shape=[jax.ShapeDtypeStruct((b, 9 * nhp, s, LANES), BF16),
                   jax.ShapeDtypeStruct((b, 2 * nhp, s, LANES), F32)],
        compiler_params=_cparams(("parallel", "parallel")),
        name="prep",
    )(p, p, p, sw, wupf, wupb, aupf, aupb, gup, vecs, mbd)


def _wkv_kernel(r_ref, v_ref, kk_ref, kf_ref, af_ref, lf_ref, rr_ref, vr_ref, kkr_ref, kb_ref, ab_ref, lb_ref,
                yf_ref, yb_ref, s_sc, *, L, nhp):
    @pl.when(pl.program_id(1) == 0)
    def _():
        s_sc[...] = jnp.zeros(s_sc.shape, F32)

    n2 = HEADS_PER_BLOCK * L
    ri = lax.broadcasted_iota(I32, (n2, LANES), 0)
    ci = lax.broadcasted_iota(I32, (n2, LANES), 1)
    head_match = (ri // L) == (ci // HEAD_DIM)
    si = lax.broadcasted_iota(I32, (n2, n2), 0)
    sj = lax.broadcasted_iota(I32, (n2, n2), 1)
    st, ss = si % L, sj % L
    eye = (si == sj).astype(F32)
    strict = {False: st > ss, True: st < ss}
    incl = {False: st >= ss, True: st < ss}

    def stack(x):
        return jnp.where(head_match, jnp.concatenate([x] * HEADS_PER_BLOCK, axis=0), 0.0)

    chains = [(False, hp, r_ref, v_ref, kk_ref, kf_ref, af_ref, lf_ref) for hp in range(nhp)]
    chains += [(True, hp, rr_ref, vr_ref, kkr_ref, kb_ref, ab_ref, lb_ref) for hp in range(nhp)]
    n = len(chains)
    lws = [ch[7][0, ch[1]] for ch in chains]
    t_row = lax.broadcasted_iota(I32, (L, LANES), 0)

    def cumsum_time(x, rev):
        sh = 1
        while sh < L:
            if rev:
                x = x + jnp.where(t_row < L - sh, pltpu.roll(x, L - sh, 0), 0.0)
            else:
                x = x + jnp.where(t_row >= sh, pltpu.roll(x, sh, 0), 0.0)
            sh *= 2
        return x

    c_ins = [cumsum_time(lw, ch[0]) for ch, lw in zip(chains, lws)]
    lhs_l, rhs_l, vs_l, bk_l, pl_l = [], [], [], [], []
    for i, (rev, hp, rr, vr, kkr, kdr, agr, _) in enumerate(chains):
        r = rr[0, hp].astype(F32)
        kk = kkr[0, hp].astype(F32)
        kd = kdr[0, hp].astype(F32)
        ag = agr[0, hp].astype(F32)
        c_in = c_ins[i]
        e_ex = jnp.exp(c_in - lws[i])
        e_inv = jnp.exp(-c_in)
        if rev:
            e_r = e_ex
            p_last = jnp.exp(c_in[0:1])
        else:
            e_r = jnp.exp(c_in)
            p_last = jnp.exp(c_in[L - 1:L])
        b_s = stack(kk * ag * e_inv)
        k_s = stack(kd * e_inv)
        lhs_l.append(jnp.concatenate([stack(-kk * e_ex), stack(r * e_r)], axis=0).astype(BF16))
        rhs_l.append(jnp.concatenate([b_s, k_s], axis=0).astype(BF16))
        bk_l.append(jnp.concatenate([b_s * p_last, k_s * p_last], axis=0).astype(BF16))
        vs_l.append(stack(vr[0, hp].astype(F32)).astype(BF16))
        pl_l.append(p_last)
    g_l = [lax.dot_general(lhs_l[i], rhs_l[i], NT_DIMS, preferred_element_type=F32) for i in range(n)]
    s_old = [s_sc[i] for i in range(n)]
    ls_l = [lax.dot_general(lhs_l[i], s_old[i].astype(BF16), NT_DIMS, preferred_element_type=F32)
            for i in range(n)]
    a_ab = [jnp.where(strict[ch[0]], g[0:n2, 0:n2], 0.0) for ch, g in zip(chains, g_l)]
    a_ak = [jnp.where(strict[ch[0]], g[0:n2, n2:2 * n2], 0.0).astype(BF16) for ch, g in zip(chains, g_l)]
    a_r = [jnp.concatenate([jnp.where(incl[ch[0]], g[n2:2 * n2, 0:n2], 0.0),
                            jnp.where(incl[ch[0]], g[n2:2 * n2, n2:2 * n2], 0.0)], axis=1).astype(BF16)
           for ch, g in zip(chains, g_l)]
    x_l = [ls_l[i][0:n2] + jnp.dot(a_ak[i], vs_l[i], preferred_element_type=F32) for i in range(n)]
    rounds = L.bit_length() - 1
    pw = [jnp.dot(a.astype(BF16), a.astype(BF16), preferred_element_type=F32) for a in a_ab]
    t_inv = [eye + a for a in a_ab]
    for j in range(1, rounds):
        pb = [p.astype(BF16) for p in pw]
        if j + 1 < rounds:
            both = [jnp.dot(jnp.concatenate([pb[i], t_inv[i].astype(BF16)], axis=0), pb[i],
                            preferred_element_type=F32) for i in range(n)]
            pw = [m[0:n2] for m in both]
            t_inv = [t + m[n2:2 * n2] for t, m in zip(t_inv, both)]
        else:
            t_inv = [t + jnp.dot(t.astype(BF16), pb[i], preferred_element_type=F32)
                     for i, t in enumerate(t_inv)]
    u_l = [jnp.dot(t_inv[i].astype(BF16), x_l[i].astype(BF16), preferred_element_type=F32) for i in range(n)]
    uv_l = [jnp.concatenate([jnp.where(head_match, u_l[i], 0.0).astype(BF16), vs_l[i]], axis=0)
            for i in range(n)]
    y_l = [ls_l[i][n2:2 * n2] + jnp.dot(a_r[i], uv_l[i], preferred_element_type=F32) for i in range(n)]
    upd_l = [lax.dot_general(uv_l[i], bk_l[i], TN_DIMS, preferred_element_type=F32) for i in range(n)]
    for i, (rev, hp, *_) in enumerate(chains):
        y = y_l[i]
        out = y[0:L]
        for h in range(1, HEADS_PER_BLOCK):
            out = out + y[h * L:(h + 1) * L]
        (yb_ref if rev else yf_ref)[0, hp] = out
        s_sc[i] = s_old[i] * pl_l[i] + upd_l[i]


def _wkv_call(rb, lw, nhp, L=WKV_CHUNK):
    b, _, s, _ = rb.shape
    nc = s // L
    fwd = lambda q: pl.BlockSpec((1, nhp, L, LANES), lambda bi, c: (bi, q, c, 0))
    bwd = lambda q: pl.BlockSpec((1, nhp, L, LANES), lambda bi, c: (bi, q, nc - 1 - c, 0))
    y_shape = jax.ShapeDtypeStruct((b, nhp, s, LANES), F32)
    return pl.pallas_call(
        functools.partial(_wkv_kernel, L=L, nhp=nhp),
        grid=(b, nc),
        in_specs=[fwd(RB_R), fwd(RB_V), fwd(RB_KK), fwd(RB_KF), fwd(RB_AF), fwd(0),
                  bwd(RB_R), bwd(RB_V), bwd(RB_KK), bwd(RB_KB), bwd(RB_AB), bwd(1)],
        out_specs=[fwd(0), bwd(0)],
        out_shape=[y_shape, y_shape],
        scratch_shapes=[pltpu.VMEM((2 * nhp, LANES, LANES), F32)],
        compiler_params=_cparams(("parallel", "arbitrary")),
        name="wkv",
    )(rb, rb, rb, rb, rb, lw, rb, rb, rb, rb, rb, lw)


def _post_kernel(yf_ref, yb_ref, bonus_ref, g_ref, lnw_ref, lnb_ref, mbd_ref, o_ref, *, nhp):
    mbd = mbd_ref[...]
    inv_n = 1.0 / HEAD_DIM
    for hp in range(nhp):
        y = yf_ref[0, hp] + yb_ref[0, hp]
        mu = _head_sums_split(y, mbd) * inv_n
        d = y - mu
        var = _head_sums_split(d * d, mbd) * inv_n
        yn = d * lax.rsqrt(var + GN_EPS) * lnw_ref[hp:hp + 1, :] + lnb_ref[hp:hp + 1, :]
        out = (yn + bonus_ref[0, hp].astype(F32)) * g_ref[0, hp].astype(F32)
        o_ref[0, :, hp * LANES:(hp + 1) * LANES] = out.astype(o_ref.dtype)


def _post_call(yf, yb, rb, lnw, lnb, mbd, nhp, tb=256):
    b, _, s, _ = yf.shape
    yblk = pl.BlockSpec((1, nhp, tb, LANES), lambda bi, i: (bi, 0, i, 0))
    rblk = lambda q: pl.BlockSpec((1, nhp, tb, LANES), lambda bi, i: (bi, q, i, 0))
    full = lambda a: pl.BlockSpec(a.shape, lambda bi, i: (0,) * a.ndim)
    return pl.pallas_call(
        functools.partial(_post_kernel, nhp=nhp),
        grid=(b, s // tb),
        in_specs=[yblk, yblk, rblk(RB_BONUS), rblk(RB_G), full(lnw), full(lnb), full(mbd)],
        out_specs=pl.BlockSpec((1, tb, nhp * LANES), lambda bi, i: (bi, i, 0)),
        out_shape=jax.ShapeDtypeStruct((b, s, nhp * LANES), BF16),
        compiler_params=_cparams(("parallel", "parallel")),
        name="post",
    )(yf, yb, rb, rb, lnw, lnb, mbd)


def _na_kernel(q_ref, k_ref, v_ref, bias_ref, qw_ref, kw_ref, mbd_ref, o_ref, kn_sc, *, rq, rows):
    rbi = pl.program_id(2)
    mbd = mbd_ref[...]
    inv_n = 1.0 / HEAD_DIM
    span = WIN_R * GRID_W
    s_len = kn_sc.shape[0]

    @pl.when(rbi == 0)
    def _():
        def body(c, carry):
            off = pl.multiple_of(c * span, span)
            k = k_ref[0, 0, pl.ds(off, span), :].astype(F32)
            ms = _head_sums_split(k * k, mbd) * inv_n
            kn_sc[pl.ds(off, span), :] = (k * lax.rsqrt(ms + NORM_EPS) * kw_ref[...]).astype(BF16)
            return carry
        lax.fori_loop(0, s_len // span, body, 0)

    ri = lax.broadcasted_iota(I32, (HEADS_PER_BLOCK * GRID_W, LANES), 0)
    ci = lax.broadcasted_iota(I32, (HEADS_PER_BLOCK * GRID_W, LANES), 1)
    head_match = (ri // GRID_W) == (ci // HEAD_DIM)
    lane = lax.broadcasted_iota(I32, (GRID_W, LANES), 1)
    scale = HEAD_DIM ** -0.5
    q_all = q_ref[0, 0].astype(F32)
    ms = _head_sums_split(q_all * q_all, mbd) * inv_n
    qn_all = q_all * lax.rsqrt(ms + NORM_EPS) * qw_ref[...] * scale
    starts, offs = [], []
    for qi in range(rq):
        i = rbi * rq + qi
        rs = jnp.clip(i - WIN_R // 2, 0, rows - WIN_R)
        offs.append(rs - i + WIN_R - 1)
        starts.append(pl.multiple_of(rs * GRID_W, GRID_W))
    qs_l = [jnp.where(head_match, jnp.concatenate([qn_all[qi * GRID_W:(qi + 1) * GRID_W]] * HEADS_PER_BLOCK,
                                                  axis=0), 0.0).astype(BF16) for qi in range(rq)]
    def bias(off):
        return jnp.concatenate([bias_ref[0, off + 2 * i] for i in range(WIN_R // 2)], axis=1)

    s_l = [lax.dot_general(qs_l[qi], kn_sc[pl.ds(starts[qi], span), :], NT_DIMS, preferred_element_type=F32)
           + bias(offs[qi]) for qi in range(rq)]
    p_l, l_l = [], []
    for s in s_l:
        p = jnp.exp(s - jnp.max(s, axis=-1, keepdims=True))
        p_l.append(p.astype(BF16))
        l_l.append(jnp.sum(p, axis=-1, keepdims=True))
    o_l = [jnp.dot(p_l[qi], v_ref[0, 0, pl.ds(starts[qi], span), :], preferred_element_type=F32) / l_l[qi]
           for qi in range(rq)]
    for qi, o in enumerate(o_l):
        out = o[0:GRID_W]
        for h in range(1, HEADS_PER_BLOCK):
            out = jnp.where(lane // HEAD_DIM == h, o[h * GRID_W:(h + 1) * GRID_W], out)
        o_ref[0, qi * GRID_W:(qi + 1) * GRID_W, :] = out.astype(o_ref.dtype)


def _na_call(p, bias, qw, kw, mbd, q_blk0, nhp, rq=16):
    b, _, s, _ = p.shape
    rows = s // GRID_W
    rq = min(rq, rows)
    full = lambda a: pl.BlockSpec(a.shape, lambda bi, hp, r: (0,) * a.ndim)
    return pl.pallas_call(
        functools.partial(_na_kernel, rq=rq, rows=rows),
        grid=(b, nhp, rows // rq),
        in_specs=[pl.BlockSpec((1, 1, rq * GRID_W, LANES), lambda bi, hp, r: (bi, q_blk0 + hp, r, 0)),
                  pl.BlockSpec((1, 1, s, LANES), lambda bi, hp, r: (bi, q_blk0 + nhp + hp, 0, 0)),
                  pl.BlockSpec((1, 1, s, LANES), lambda bi, hp, r: (bi, q_blk0 + 2 * nhp + hp, 0, 0)),
                  pl.BlockSpec((1,) + bias.shape[1:], lambda bi, hp, r: (hp, 0, 0, 0)),
                  full(qw), full(kw), full(mbd)],
        out_specs=pl.BlockSpec((1, rq * GRID_W, LANES), lambda bi, hp, r: (bi, r, hp)),
        out_shape=jax.ShapeDtypeStruct((b, s, nhp * LANES), BF16),
        scratch_shapes=[pltpu.VMEM((s, LANES), BF16)],
        compiler_params=_cparams(("parallel", "parallel", "arbitrary")),
        name="na",
    )(p, p, p, bias, qw, kw, mbd)


def _na_bias_table(rel_bias):
    h = rel_bias.shape[0]
    qc = np.arange(GRID_W)
    kc = np.arange(GRID_W)
    win0 = np.clip(qc - WIN_C // 2, 0, GRID_W - WIN_C)
    valid = (kc[None, :] >= win0[:, None]) & (kc[None, :] < win0[:, None] + WIN_C)
    coff = np.clip(kc[None, :] - qc[:, None] + WIN_C - 1, 0, 2 * WIN_C - 2)
    pick = jnp.asarray((coff[None] == np.arange(2 * WIN_C - 1)[:, None, None]).astype(np.float32))
    t = jnp.einsum('hrc,cqk->hrqk', rel_bias.astype(F32), pick, precision=HI)
    t = jnp.where(valid[None, None], t, NEG_BIG)
    t = jnp.concatenate([t[:, :-1], t[:, 1:]], axis=-1)
    n_ro = t.shape[1]
    t = t.reshape(h // HEADS_PER_BLOCK, HEADS_PER_BLOCK, n_ro, GRID_W, 2 * GRID_W)
    t = jnp.transpose(t, (0, 2, 1, 3, 4))
    return t.reshape(h // HEADS_PER_BLOCK, n_ro, HEADS_PER_BLOCK * GRID_W, 2 * GRID_W)


def _outproj_kernel(yr_ref, yn_ref, x_ref, mod_ref, nw_ref, w1_ref, w2_ref, wr_ref, x1_ref, h2_ref, h2row_ref,
                    lg_ref):
    tm = x_ref.shape[1]
    sub = LANES
    tiles = [slice(i * sub, (i + 1) * sub) for i in range(tm // sub)]
    accs = [jnp.dot(yr_ref[0, r], w1_ref[...], preferred_element_type=F32)
            + jnp.dot(yn_ref[0, r], w2_ref[...], preferred_element_type=F32) for r in tiles]
    h2s = []
    for r, acc in zip(tiles, accs):
        x1 = x_ref[0, r] + mod_ref[0, 2:3, :] * acc
        x1_ref[0, r] = x1
        ms = jnp.mean(x1 * x1, axis=-1, keepdims=True)
        y = x1 * lax.rsqrt(ms + NORM_EPS) * nw_ref[...]
        h2 = y * (1.0 + mod_ref[0, 4:5, :]) + mod_ref[0, 3:4, :]
        packed = _pack_halves(h2)
        h2row_ref[0, r] = packed
        for j in range(h2_ref.shape[2]):
            h2_ref[0, r, j, :] = packed[:, j * LANES:(j + 1) * LANES]
        h2s.append(h2)
    wr = wr_ref[...]
    wr_hi = wr.astype(BF16)
    wr_lo = (wr - wr_hi.astype(F32)).astype(BF16)
    for r, h2 in zip(tiles, h2s):
        h_hi = h2.astype(BF16)
        h_lo = (h2 - h_hi.astype(F32)).astype(BF16)
        lg_ref[:, r] = (lax.dot_general(wr_hi, h_hi, NT_DIMS, preferred_element_type=F32)
                        + lax.dot_general(wr_hi, h_lo, NT_DIMS, preferred_element_type=F32)
                        + lax.dot_general(wr_lo, h_hi, NT_DIMS, preferred_element_type=F32))


def _outproj_call(yr, yn, x, mod3, norm_w, w_o, wr_t, tm=512):
    b, s, d = x.shape
    dh = yr.shape[-1]
    assert w_o.shape == (2 * dh, d)
    ne = wr_t.shape[0]
    nt = s // tm
    full = lambda a: pl.BlockSpec(a.shape, lambda bi, i: (0,) * a.ndim)
    return pl.pallas_call(
        _outproj_kernel,
        grid=(b, nt),
        in_specs=[pl.BlockSpec((1, tm, dh), lambda bi, i: (bi, i, 0)),
                  pl.BlockSpec((1, tm, dh), lambda bi, i: (bi, i, 0)),
                  pl.BlockSpec((1, tm, d), lambda bi, i: (bi, i, 0)),
                  pl.BlockSpec((1, 6, d), lambda bi, i: (bi, 0, 0)),
                  full(norm_w),
                  pl.BlockSpec((dh, d), lambda bi, i: (0, 0)),
                  pl.BlockSpec((dh, d), lambda bi, i: (1, 0)),
                  full(wr_t)],
        out_specs=[pl.BlockSpec((1, tm, d), lambda bi, i: (bi, i, 0)),
                   pl.BlockSpec((1, tm, d // 2 // LANES, LANES), lambda bi, i: (bi, i, 0, 0)),
                   pl.BlockSpec((1, tm, d // 2), lambda bi, i: (bi, i, 0)),
                   pl.BlockSpec((ne, tm), lambda bi, i: (0, bi * nt + i))],
        out_shape=[jax.ShapeDtypeStruct((b, s, d), F32),
                   jax.ShapeDtypeStruct((b, s, d // 2 // LANES, LANES), U32),
                   jax.ShapeDtypeStruct((b, s, d // 2), U32),
                   jax.ShapeDtypeStruct((ne, b * s), F32)],
        compiler_params=_cparams(("parallel", "parallel")),
        name="outproj",
    )(yr, yn, x, mod3, norm_w, w_o, w_o, wr_t)


def _first_argmax(x, idx, n):
    m = jnp.max(x, axis=0, keepdims=True)
    a = jnp.min(jnp.where(x == m, idx, n), axis=0, keepdims=True)
    return m, a


def _route_kernel(lg_ref, bias_ref, idx_ref, gw_ref, pos_ref, cnt_ref, *, tr):
    @pl.when(pl.program_id(0) == 0)
    def _():
        cnt_ref[...] = jnp.zeros(cnt_ref.shape, F32)

    gsz = N_EXPERTS // N_GROUPS
    scores = _sigmoid(lg_ref[...])
    biased = scores + bias_ref[:, 0:1]
    ig = lax.broadcasted_iota(I32, (gsz, tr), 0)
    grp_rows = []
    for g in range(N_GROUPS):
        blk = biased[g * gsz:(g + 1) * gsz]
        m1, a1 = _first_argmax(blk, ig, gsz)
        m2 = jnp.max(jnp.where(ig == a1, -jnp.inf, blk), axis=0, keepdims=True)
        grp_rows.append(m1 + m2)
    grp = jnp.concatenate(grp_rows, axis=0)
    ign = lax.broadcasted_iota(I32, (N_GROUPS, tr), 0)
    sel = jnp.zeros((N_GROUPS, tr), jnp.bool_)
    for _ in range(TOPK_GROUPS):
        _, a = _first_argmax(grp, ign, N_GROUPS)
        hit = ign == a
        sel = jnp.logical_or(sel, hit)
        grp = jnp.where(hit, -jnp.inf, grp)
    masked = jnp.concatenate(
        [jnp.where(sel[g:g + 1], biased[g * gsz:(g + 1) * gsz], -jnp.inf) for g in range(N_GROUPS)], axis=0)
    ie = lax.broadcasted_iota(I32, (N_EXPERTS, tr), 0)
    picks, pick_scores = [], []
    onehot = jnp.zeros((N_EXPERTS, tr), F32)
    for _ in range(TOP_K):
        _, a = _first_argmax(masked, ie, N_EXPERTS)
        hit = ie == a
        picks.append(a)
        pick_scores.append(jnp.sum(jnp.where(hit, scores, 0.0), axis=0, keepdims=True))
        onehot = onehot + hit.astype(F32)
        masked = jnp.where(hit, -jnp.inf, masked)
    total = pick_scores[0]
    for sc in pick_scores[1:]:
        total = total + sc
    t0 = lax.broadcasted_iota(I32, (tr, tr), 0)
    t1 = lax.broadcasted_iota(I32, (tr, tr), 1)
    before = (t0 < t1).astype(BF16)
    rank = jnp.dot(onehot.astype(BF16), before, preferred_element_type=F32) + cnt_ref[:, 0:1]
    zero_i = jnp.zeros((1, tr), I32)
    zero_f = jnp.zeros((1, tr), F32)
    for k in range(8):
        if k < TOP_K:
            idx_ref[k:k + 1, :] = picks[k]
            gw_ref[k:k + 1, :] = pick_scores[k] / total * ROUTED_SCALE
            pos = jnp.sum(jnp.where(ie == picks[k], rank, 0.0), axis=0, keepdims=True)
            pos_ref[k:k + 1, :] = pos.astype(I32)
        else:
            idx_ref[k:k + 1, :] = zero_i
            gw_ref[k:k + 1, :] = zero_f
            pos_ref[k:k + 1, :] = zero_i
    cnt_ref[...] = cnt_ref[...] + jnp.sum(onehot, axis=1, keepdims=True)


def _route_call(lg_t, bias2, tr=512):
    ne, t = lg_t.shape
    tr = min(tr, t)
    tok = pl.BlockSpec((8, tr), lambda i: (0, i))
    return pl.pallas_call(
        functools.partial(_route_kernel, tr=tr),
        grid=(t // tr,),
        in_specs=[pl.BlockSpec((ne, tr), lambda i: (0, i)),
                  pl.BlockSpec(bias2.shape, lambda i: (0, 0))],
        out_specs=[tok, tok, tok, pl.BlockSpec((ne, LANES), lambda i: (0, 0))],
        out_shape=[jax.ShapeDtypeStruct((8, t), I32), jax.ShapeDtypeStruct((8, t), F32),
                   jax.ShapeDtypeStruct((8, t), I32), jax.ShapeDtypeStruct((ne, LANES), F32)],
        compiler_params=_cparams(("arbitrary",)),
        name="route",
    )(lg_t, bias2)


FILL_CHUNKS = (1, 8, 64)
ZERO_ROWS = FILL_CHUNKS[-1]


def _dispatch_kernel(slot_ref, fill_lo_ref, fill_hi_ref, h_ref, xs_ref, zrow, sem, *, td):
    def row_copy(t, slot):
        return pltpu.make_async_copy(h_ref.at[t], xs_ref.at[slot], sem)

    def start(t, carry):
        for k in range(TOP_K):
            row_copy(t, slot_ref[t * SLOT_STRIDE + k]).start(priority=k % 2)
        return carry

    lax.fori_loop(0, td, start, 0)
    for k in range(TOP_K):
        pltpu.make_async_copy(h_ref, xs_ref.at[pl.ds(0, td)], sem).wait()

    @pl.when(pl.program_id(0) == pl.num_programs(0) - 1)
    def _():
        zrow[...] = jnp.zeros(zrow.shape, zrow.dtype)

        def zero_copy(row, n):
            if n > 1:
                row = pl.multiple_of(row, FILL_CHUNKS[1])
            return pltpu.make_async_copy(zrow.at[pl.ds(0, n)], xs_ref.at[pl.ds(row, n)], sem)

        def for_each_chunk(r, fn):
            lo, hi = fill_lo_ref[r], fill_hi_ref[r]
            for ci, n in enumerate(FILL_CHUNKS):
                if ci + 1 < len(FILL_CHUNKS):
                    count = jnp.minimum(((-lo) % FILL_CHUNKS[ci + 1]) // n, (hi - lo) // n)
                else:
                    count = (hi - lo) // n
                fn(lo, n, count)
                lo = lo + count * n

        def start_range(r, carry):
            def go(lo, n, count):
                lax.fori_loop(0, count, lambda i, c: (zero_copy(lo + i * n, n).start(), c)[1], 0)
            for_each_chunk(r, go)
            return carry

        def wait_range(r, carry):
            def go(lo, n, count):
                lax.fori_loop(0, count, lambda i, c: (zero_copy(0, n).wait(), c)[1], 0)
            for_each_chunk(r, go)
            return carry

        n_ranges = fill_lo_ref.shape[0]
        lax.fori_loop(0, n_ranges, start_range, 0)
        lax.fori_loop(0, n_ranges, wait_range, 0)


def _dispatch_call(slots_flat, fill_lo, fill_hi, h2, n_rows, td=256):
    t, nj, _ = h2.shape
    smem = pl.BlockSpec(memory_space=pltpu.SMEM)
    return pl.pallas_call(
        functools.partial(_dispatch_kernel, td=td),
        grid=(t // td,),
        in_specs=[pl.BlockSpec((td * SLOT_STRIDE,), lambda i: (i,), memory_space=pltpu.SMEM),
                  smem, smem,
                  pl.BlockSpec((td, nj, LANES), lambda i: (i, 0, 0))],
        out_specs=pl.BlockSpec(memory_space=pl.ANY),
        out_shape=jax.ShapeDtypeStruct((n_rows, nj, LANES), h2.dtype),
        scratch_shapes=[pltpu.VMEM((ZERO_ROWS, nj, LANES), h2.dtype), pltpu.SemaphoreType.DMA(())],
        compiler_params=_cparams(("arbitrary",)),
        name="dispatch",
    )(slots_flat, fill_lo, fill_hi, h2)


def _experts_kernel(be_ref, first_ref, nxt_ref, par_ref, rows_ref, nv_ref, xs_hbm, wg_hbm, wu_hbm, wd_hbm, ys_ref,
                    wg_buf, wu_buf, wd_buf, wg_sc, wu_sc, wd_sc, xpack, sem, isem):
    nj = xs_hbm.shape[1]
    b = pl.program_id(0)
    m = xpack.shape[1]
    half = wg_sc.shape[0] // 2
    oslot = lax.rem(b, 2)

    def in_copies(slot, row0):
        return [pltpu.make_async_copy(xs_hbm.at[pl.ds(row0, m), j],
                                      xpack.at[slot, :, pl.ds(j * LANES, LANES)], isem.at[slot]) for j in range(nj)]

    @pl.when(b == 0)
    def _():
        for c in in_copies(0, 0):
            c.start()

    @pl.when(b + 1 < nv_ref[0])
    def _():
        for c in in_copies(1 - oslot, pl.multiple_of((b + 1) * m, m)):
            c.start()

    def weight_copies(e, slot):
        return [pltpu.make_async_copy(src.at[e], dst.at[slot], sem.at[slot, i])
                for i, (src, dst) in enumerate(((wg_hbm, wg_buf), (wu_hbm, wu_buf), (wd_hbm, wd_buf)))]

    def compute(nrows):
        x_lo, x_hi = _unpack_halves(xpack[oslot, 0:nrows])
        x_lo, x_hi = x_lo.astype(BF16), x_hi.astype(BF16)
        g = (jnp.dot(x_lo, wg_sc[0:half], preferred_element_type=F32)
             + jnp.dot(x_hi, wg_sc[half:], preferred_element_type=F32))
        u = (jnp.dot(x_lo, wu_sc[0:half], preferred_element_type=F32)
             + jnp.dot(x_hi, wu_sc[half:], preferred_element_type=F32))
        a = (_silu(g) * u).astype(BF16)
        ys_ref[0:nrows] = _pack_halves(jnp.dot(a, wd_sc[...], preferred_element_type=F32))

    @pl.when(b < nv_ref[0])
    def _():
        for c in in_copies(oslot, 0):
            c.wait()

        @pl.when(b == 0)
        def _():
            for c in weight_copies(be_ref[0], par_ref[0]):
                c.start()

        @pl.when(first_ref[b] == 1)
        def _():
            slot = par_ref[b]
            for c in weight_copies(be_ref[b], slot):
                c.wait()

            @pl.when(nxt_ref[b] >= 0)
            def _():
                for c in weight_copies(nxt_ref[b], 1 - slot):
                    c.start()

            wg_sc[...] = wg_buf[slot].astype(BF16)
            wu_sc[...] = wu_buf[slot].astype(BF16)
            wd_sc[...] = wd_buf[slot].astype(BF16)

        @pl.when(rows_ref[b] > m // 2)
        def _():
            compute(m)

        @pl.when(rows_ref[b] <= m // 2)
        def _():
            compute(m // 2)
            ys_ref[m // 2:] = jnp.zeros((m - m // 2, ys_ref.shape[1]), ys_ref.dtype)

    @pl.when(b >= nv_ref[0])
    def _():
        ys_ref[...] = jnp.zeros(ys_ref.shape, ys_ref.dtype)


def _experts_call(block_e, first, nxt, par, rows_valid, n_valid, xs, wg, wu, wd):
    total, nj, _ = xs.shape
    d = wg.shape[1]
    m = DISPATCH_BLOCK
    ff = wg.shape[-1]

    hbm = pl.BlockSpec(memory_space=pl.ANY)
    grid_spec = pltpu.PrefetchScalarGridSpec(
        num_scalar_prefetch=6,
        grid=(total // m,),
        in_specs=[hbm, hbm, hbm, hbm],
        out_specs=pl.BlockSpec((m, nj * LANES), lambda b, *s: (b, 0)),
        scratch_shapes=[pltpu.VMEM((2, d, ff), F32), pltpu.VMEM((2, d, ff), F32), pltpu.VMEM((2, ff, d), F32),
                        pltpu.VMEM((d, ff), BF16), pltpu.VMEM((d, ff), BF16), pltpu.VMEM((ff, d), BF16),
                        pltpu.VMEM((2, m, nj * LANES), U32),
                        pltpu.SemaphoreType.DMA((2, 3)), pltpu.SemaphoreType.DMA((2,))],
    )
    return pl.pallas_call(
        _experts_kernel,
        grid_spec=grid_spec,
        out_shape=jax.ShapeDtypeStruct((total, nj * LANES), U32),
        compiler_params=_cparams(("arbitrary",)),
        name="experts",
    )(block_e, first, nxt, par, rows_valid, n_valid, xs, wg, wu, wd)


COMBINE_INLINE_ROWS = 64


def _combine_kernel(slot_ref, x1_ref, h2_ref, gw_ref, mod_ref, wg_ref, wu_ref, wd_ref, ys_ref, o_ref,
                    buf, sem, *, tc):
    def row_copy(t, k, slot):
        return pltpu.make_async_copy(ys_ref.at[pl.ds(slot, 1)], buf.at[k, pl.ds(t, 1)], sem)

    def start(t, carry):
        for k in range(TOP_K):
            row_copy(t, k, slot_ref[t * SLOT_STRIDE + k]).start(priority=k % 2)
        return carry

    inline = min(COMBINE_INLINE_ROWS, tc)
    lax.fori_loop(0, tc - inline, start, 0)
    for t in range(tc - inline, tc):
        start(t, 0)
    half = wg_ref.shape[0] // 2
    h_lo, h_hi = _unpack_halves(h2_ref[0])
    h_lo, h_hi = h_lo.astype(BF16), h_hi.astype(BF16)
    g = (jnp.dot(h_lo, wg_ref[0:half], preferred_element_type=F32)
         + jnp.dot(h_hi, wg_ref[half:], preferred_element_type=F32))
    u = (jnp.dot(h_lo, wu_ref[0:half], preferred_element_type=F32)
         + jnp.dot(h_hi, wu_ref[half:], preferred_element_type=F32))
    acc = jnp.dot((_silu(g) * u).astype(BF16), wd_ref[...], preferred_element_type=F32)
    for k in range(TOP_K):
        pltpu.make_async_copy(ys_ref.at[pl.ds(0, tc)], buf.at[k], sem).wait()
    gw = gw_ref[...]
    acc_lo, acc_hi = acc[:, :half], acc[:, half:]
    for k in range(TOP_K):
        y_lo, y_hi = _unpack_halves(buf[k])
        acc_lo = acc_lo + gw[:, k:k + 1] * y_lo
        acc_hi = acc_hi + gw[:, k:k + 1] * y_hi
    gate = mod_ref[0, 5:6, :]
    o_ref[0, :, 0:half] = x1_ref[0, :, 0:half] + gate[:, :half] * acc_lo
    o_ref[0, :, half:] = x1_ref[0, :, half:] + gate[:, half:] * acc_hi


def _combine_call(slots_flat, x1, h2, gw_tok, mod3, wg, wu, wd, ys, tc=256):
    b, s, d = x1.shape
    nt = s // tc
    full = lambda a: pl.BlockSpec(a.shape, lambda bi, i: (0,) * a.ndim)
    return pl.pallas_call(
        functools.partial(_combine_kernel, tc=tc),
        grid=(b, nt),
        in_specs=[pl.BlockSpec((tc * SLOT_STRIDE,), lambda bi, i: (bi * nt + i,), memory_space=pltpu.SMEM),
                  pl.BlockSpec((1, tc, d), lambda bi, i: (bi, i, 0)),
                  pl.BlockSpec((1, tc, d // 2), lambda bi, i: (bi, i, 0)),
                  pl.BlockSpec((tc, 8), lambda bi, i: (bi * nt + i, 0)),
                  pl.BlockSpec((1, 6, d), lambda bi, i: (bi, 0, 0)),
                  full(wg), full(wu), full(wd),
                  pl.BlockSpec(memory_space=pl.ANY)],
        out_specs=pl.BlockSpec((1, tc, d), lambda bi, i: (bi, i, 0)),
        out_shape=jax.ShapeDtypeStruct((b, s, d), F32),
        scratch_shapes=[pltpu.VMEM((TOP_K, tc, d // 2), U32), pltpu.SemaphoreType.DMA(())],
        compiler_params=_cparams(("arbitrary", "arbitrary")),
        name="combine",
    )(slots_flat, x1, h2, gw_tok, mod3, wg, wu, wd, ys)


def _pad_rows(w, rows, at=0):
    out = jnp.zeros((rows, w.shape[1]), w.dtype)
    return out.at[at:at + w.shape[0]].set(w)


def _mixing_stage(p, rwkv_shift, w0_f, w_up_f, w0_b, w_up_b, a0_f, a_up_f, a0_b, a_up_b, g_up,
                  k_k, k_a, r_k, ln_x_w, ln_x_b, q_norm_w, k_norm_w, rel_bias):
    d_rwkv = w0_f.shape[0]
    nhp = d_rwkv // LANES
    mbd = _head_sum_matrix()
    n_shift = rwkv_shift.shape[1]
    sw = jnp.pad(rwkv_shift, ((0, 0), (0, N_SHIFT_BLOCKS * LANES - n_shift))).reshape(3, N_SHIFT_BLOCKS, LANES)
    wupf = _pad_rows(w_up_f, LANES, 0).astype(BF16)
    wupb = _pad_rows(w_up_b, LANES, DECAY_LORA).astype(BF16)
    aupf = _pad_rows(a_up_f, LANES, 0).astype(BF16)
    aupb = _pad_rows(a_up_b, LANES, AAA_LORA).astype(BF16)
    gup = _pad_rows(g_up, 2 * LANES, 0).astype(BF16)
    vecs = jnp.stack([w0_f, w0_b, a0_f, a0_b, k_k, k_a, r_k.reshape(-1)]).reshape(7, nhp, LANES)
    rb, lw = _prep_call(p, sw, wupf, wupb, aupf, aupb, gup, vecs, mbd, nhp)
    y_f, y_b = _wkv_call(rb, lw, nhp)
    y_rwkv = _post_call(y_f, y_b, rb, ln_x_w.reshape(nhp, LANES), ln_x_b.reshape(nhp, LANES), mbd, nhp)
    bias = _na_bias_table(rel_bias)
    qw = jnp.tile(q_norm_w, HEADS_PER_BLOCK).reshape(1, LANES)
    kw = jnp.tile(k_norm_w, HEADS_PER_BLOCK).reshape(1, LANES)
    y_na = _na_call(p, bias, qw, kw, mbd, N_SHIFT_BLOCKS, nhp)
    return y_rwkv, y_na


def _moe_stage(x1, h2_tiles, h2_rows, lg_t, mod3, router_bias, w_gate_e, w_up_e, w_down_e, w_gate_s, w_up_s,
               w_down_s):
    b, s, d = x1.shape
    t = b * s
    m = DISPATCH_BLOCK
    bias2 = jnp.broadcast_to(router_bias.astype(F32)[:, None], (N_EXPERTS, LANES))
    idx_t, gw_t, pos_t, cnt = _route_call(lg_t, bias2)
    counts = cnt[:, 0].astype(I32)
    padded = (counts + m - 1) // m * m
    pad_end = jnp.cumsum(padded)
    pad_start = pad_end - padded
    n_blocks = -(-(t * TOP_K + N_EXPERTS * m) // m)
    n_valid = (pad_end[-1] // m).astype(I32).reshape(1)
    block_row0 = jnp.arange(n_blocks, dtype=I32) * m
    block_e = jnp.minimum(jnp.sum((pad_end[None, :] <= block_row0[:, None]).astype(I32), axis=1),
                          N_EXPERTS - 1).astype(I32)
    assert m % ZERO_ROWS == 0
    expert_ids = jnp.arange(N_EXPERTS, dtype=I32)
    start_of = jnp.sum(jnp.where(idx_t[:, :, None] == expert_ids, pad_start, 0), axis=-1)
    slots = (start_of + pos_t).astype(I32)
    slots_flat = slots.T.reshape(-1)
    gw_tok = gw_t.T
    fill_lo = jnp.concatenate([pad_start + counts, pad_end[-1:]]).astype(I32)
    fill_hi = jnp.concatenate([pad_end, jnp.full((1,), n_blocks * m, I32)]).astype(I32)
    xs = _dispatch_call(slots_flat, fill_lo, fill_hi, h2_tiles.reshape(t, *h2_tiles.shape[2:]), n_blocks * m)
    row_end = (pad_start + counts).astype(I32)
    nonempty = counts > 0
    first = jnp.logical_and(block_row0 == pad_start[block_e], block_row0 < pad_end[-1]).astype(I32)
    ordinal = jnp.cumsum(nonempty.astype(I32)) - 1
    par = (ordinal[block_e] & 1).astype(I32)
    cand = jnp.where(nonempty, expert_ids, N_EXPERTS)
    later = jnp.concatenate([lax.cummin(cand[::-1])[::-1][1:], jnp.full((1,), N_EXPERTS, I32)])
    nxt = jnp.where(later < N_EXPERTS, later, -1)[block_e].astype(I32)
    rows_valid = jnp.clip(row_end[block_e] - block_row0, 0, m).astype(I32)
    ys = _experts_call(block_e, first, nxt, par, rows_valid, n_valid, xs, w_gate_e, w_up_e, w_down_e)
    return _combine_call(slots_flat, x1, h2_rows, gw_tok, mod3, w_gate_s.astype(BF16), w_up_s.astype(BF16),
                         w_down_s.astype(BF16), ys)


def kernel(x, c, w_ada, b_ada, norm1_w, w_in, rwkv_shift, w0_f, w_up_f, w0_b, w_up_b, a0_f, a_up_f, a0_b,
           a_up_b, g_up, k_k, k_a, r_k, ln_x_w, ln_x_b, q_norm_w, k_norm_w, rel_bias, w_out, norm2_w,
           w_router, router_bias, w_gate_e, w_up_e, w_down_e, w_gate_s, w_up_s, w_down_s):
    bn, sn, d = x.shape
    depth = w_ada.shape[0]
    for l in range(depth):
        c_pad = jnp.pad(c, ((0, 8 - bn % 8 if bn % 8 else 0), (0, 0)))
        mod = _mod_call(c_pad, w_ada[l], b_ada[l].reshape(1, -1))[:bn]
        mod3 = mod.reshape(bn, 6, d)
        rwkv_in = rwkv_shift.shape[-1]
        pad_cols = N_SHIFT_BLOCKS * LANES - rwkv_in
        w_in16 = w_in[l].astype(BF16)
        w_in_r = jnp.concatenate([w_in16[:, :rwkv_in], jnp.zeros((d, pad_cols), BF16), w_in16[:, rwkv_in:]], axis=1)
        p = _proj_call(x, mod3, norm1_w[l].reshape(1, d), w_in_r)
        y_rwkv, y_na = _mixing_stage(p, rwkv_shift[l], w0_f[l], w_up_f[l], w0_b[l], w_up_b[l], a0_f[l],
                                     a_up_f[l], a0_b[l], a_up_b[l], g_up[l], k_k[l], k_a[l], r_k[l],
                                     ln_x_w[l], ln_x_b[l], q_norm_w[l], k_norm_w[l], rel_bias[l])
        x1, h2_tiles, h2_rows, lg_t = _outproj_call(y_rwkv, y_na, x, mod3, norm2_w[l].reshape(1, d),
                                                    w_out[l].astype(BF16), w_router[l].T)
        x = _moe_stage(x1, h2_tiles, h2_rows, lg_t, mod3, router_bias[l], w_gate_e[l], w_up_e[l], w_down_e[l],
                       w_gate_s[l], w_up_s[l], w_down_s[l])
    return x
```

```python
import functools

import jax
import jax.numpy as jnp
import numpy as np
from jax import lax
from jax.experimental import pallas as pl
from jax.experimental.pallas import tpu as pltpu

F32 = jnp.float32
BF16 = jnp.bfloat16
I32 = jnp.int32
HI = lax.Precision.HIGHEST

LANES = 128
HEAD_DIM = 64
HEADS_PER_BLOCK = LANES // HEAD_DIM
GRID_W = 64
WIN_R = 8
WIN_C = 16
DECAY_LORA = 64
AAA_LORA = 64
GATE_LORA = 160
N_EXPERTS = 64
TOP_K = 6
SLOT_STRIDE = 8
N_GROUPS = 8
TOPK_GROUPS = 4
ROUTED_SCALE = 2.5
DISPATCH_BLOCK = 512
NORM_EPS = 1e-6
GN_EPS = 64e-5
L2_EPS = 1e-12
NEG_BIG = -1e30
WKV_CHUNK = 64
VMEM_LIMIT = 56 * 1024 * 1024

NT_DIMS = (((1,), (1,)), ((), ()))
TN_DIMS = (((0,), (0,)), ((), ()))


def _cparams(sem):
    return pltpu.CompilerParams(dimension_semantics=sem, vmem_limit_bytes=VMEM_LIMIT)


def _sigmoid(x):
    return 1.0 / (1.0 + jnp.exp(-x))


def _silu(x):
    return x * _sigmoid(x)


U32 = jnp.uint32
HI_HALF_MASK = 0xFFFF0000


def _pack_halves(x):
    n = x.shape[-1] // 2
    bits = lax.bitcast_convert_type(x.astype(BF16).astype(F32), U32)
    return (bits[:, :n] >> 16) | bits[:, n:]


def _unpack_halves(w):
    lo = lax.bitcast_convert_type(w << 16, F32)
    hi = lax.bitcast_convert_type(w & U32(HI_HALF_MASK), F32)
    return lo, hi


def _head_sum_matrix():
    a = np.arange(LANES) // HEAD_DIM
    return jnp.asarray((a[:, None] == a[None, :]).astype(np.float32))


def _head_sums_split(x, mbd):
    m16 = mbd.astype(BF16)
    hi = x.astype(BF16)
    lo = (x - hi.astype(F32)).astype(BF16)
    return jnp.dot(hi, m16, preferred_element_type=F32) + jnp.dot(lo, m16, preferred_element_type=F32)


def _mod_kernel(c_ref, w_ref, b_ref, o_ref):
    c = c_ref[...]
    o_ref[...] = jnp.dot(_silu(c), w_ref[...], preferred_element_type=F32, precision=HI) + b_ref[...]


def _mod_call(c_pad, w_ada, b_ada):
    rows, d = c_pad.shape
    n = w_ada.shape[1]
    tn = 1024
    return pl.pallas_call(
        _mod_kernel,
        grid=(n // tn,),
        in_specs=[pl.BlockSpec((rows, d), lambda j: (0, 0)),
                  pl.BlockSpec((d, tn), lambda j: (0, j)),
                  pl.BlockSpec((1, tn), lambda j: (0, j))],
        out_specs=pl.BlockSpec((rows, tn), lambda j: (0, j)),
        out_shape=jax.ShapeDtypeStruct((rows, n), F32),
        compiler_params=_cparams(("parallel",)),
        name="mod",
    )(c_pad, w_ada, b_ada)


def _proj_kernel(x_ref, mod_ref, nw_ref, w_ref, o_ref, h_sc, *, nq):
    @pl.when(pl.program_id(2) == 0)
    def _():
        x = x_ref[0]
        ms = jnp.mean(x * x, axis=-1, keepdims=True)
        y = x * lax.rsqrt(ms + NORM_EPS) * nw_ref[...]
        h_sc[...] = (y * (1.0 + mod_ref[0, 1:2, :]) + mod_ref[0, 0:1, :]).astype(BF16)

    acc = jnp.dot(h_sc[...], w_ref[...], preferred_element_type=F32)
    for q in range(nq):
        o_ref[0, q] = acc[:, q * LANES:(q + 1) * LANES].astype(o_ref.dtype)


def _proj_call(x, mod3, norm_w, w_in_r, tm=1024, tn=1664):
    b, s, d = x.shape
    n = w_in_r.shape[1]
    nq = tn // LANES
    return pl.pallas_call(
        functools.partial(_proj_kernel, nq=nq),
        grid=(b, s // tm, n // tn),
        in_specs=[pl.BlockSpec((1, tm, d), lambda bi, i, j: (bi, i, 0)),
                  pl.BlockSpec((1, 6, d), lambda bi, i, j: (bi, 0, 0)),
                  pl.BlockSpec((1, d), lambda bi, i, j: (0, 0)),
                  pl.BlockSpec((d, tn), lambda bi, i, j: (0, j))],
        out_specs=pl.BlockSpec((1, nq, tm, LANES), lambda bi, i, j: (bi, j, i, 0)),
        out_shape=jax.ShapeDtypeStruct((b, n // LANES, s, LANES), BF16),
        scratch_shapes=[pltpu.VMEM((tm, d), BF16)],
        compiler_params=_cparams(("parallel", "parallel", "arbitrary")),
        name="proj",
    )(x, mod3, norm_w, w_in_r)


RB_R, RB_V, RB_KK, RB_G, RB_KF, RB_KB, RB_AF, RB_AB, RB_BONUS = range(9)
VEC_W0F, VEC_W0B, VEC_A0F, VEC_A0B, VEC_KK, VEC_KA, VEC_RK = range(7)
N_RKV_BLOCKS = 24
N_SHIFT_BLOCKS = 28


def _softplus(u):
    return jnp.maximum(u, 0.0) + jnp.log(1.0 + jnp.exp(-jnp.abs(u)))


def _prep_kernel(p_ref, pp_ref, pn_ref, sw_ref, wupf_ref, wupb_ref, aupf_ref, aupb_ref, gup_ref,
                 vec_ref, mbd_ref, rb_ref, lw_ref, *, tb, n_t, nhp):
    i = pl.program_id(1)
    row = lax.broadcasted_iota(I32, (tb, LANES), 0)
    has_prev = i > 0
    has_next = i < n_t - 1
    halo = pp_ref.shape[2]

    def shifted(q):
        cur = p_ref[0, q].astype(F32)
        prev_row = jnp.where(has_prev, pp_ref[0, q, halo - 1:halo, :].astype(F32), 0.0)
        next_row = jnp.where(has_next, pn_ref[0, q, 0:1, :].astype(F32), 0.0)
        up = jnp.where(row == 0, prev_row, pltpu.roll(cur, 1, 0))
        dn = jnp.where(row == tb - 1, next_row, pltpu.roll(cur, tb - 1, 0))
        return sw_ref[0, q:q + 1, :] * up + sw_ref[1, q:q + 1, :] * cur + sw_ref[2, q:q + 1, :] * dn

    mbd = mbd_ref[...]
    t_wd = jnp.tanh(shifted(N_RKV_BLOCKS)).astype(BF16)
    z_ad = shifted(N_RKV_BLOCKS + 1).astype(BF16)
    s_gd = jnp.concatenate([_sigmoid(shifted(N_RKV_BLOCKS + 2)),
                            _sigmoid(shifted(N_RKV_BLOCKS + 3))], axis=1).astype(BF16)

    def vec(v, hp):
        return vec_ref[v, hp:hp + 1, :]

    for hp in range(nhp):
        sl = slice(hp * LANES, (hp + 1) * LANES)
        r = shifted(hp)
        k = shifted(nhp + hp)
        v = shifted(2 * nhp + hp)
        g = jnp.dot(s_gd, gup_ref[:, sl], preferred_element_type=F32)
        kk0 = k * vec(VEC_KK, hp)
        ss = _head_sums_split(kk0 * kk0, mbd)
        kk = kk0 * lax.rsqrt(jnp.maximum(ss, L2_EPS))
        kdirs = []
        for wup_ref, aup_ref, v_w0, v_a0, q_k, q_a, lw_slot in (
                (wupf_ref, aupf_ref, VEC_W0F, VEC_A0F, RB_KF, RB_AF, 0),
                (wupb_ref, aupb_ref, VEC_W0B, VEC_A0B, RB_KB, RB_AB, 1)):
            wl = vec(v_w0, hp) + jnp.dot(t_wd, wup_ref[:, sl], preferred_element_type=F32)
            w_log = -_softplus(-wl) - 0.5
            lw_ref[0, lw_slot * nhp + hp] = -jnp.exp(w_log)
            ag = _sigmoid(vec(v_a0, hp) + jnp.dot(z_ad, aup_ref[:, sl], preferred_element_type=F32))
            kd = k * (1.0 + (ag - 1.0) * vec(VEC_KA, hp))
            rb_ref[0, q_k * nhp + hp] = kd.astype(BF16)
            rb_ref[0, q_a * nhp + hp] = ag.astype(BF16)
            kdirs.append(kd)
        bonus = _head_sums_split(r * kdirs[0] * vec(VEC_RK, hp), mbd) * v
        rb_ref[0, RB_R * nhp + hp] = r.astype(BF16)
        rb_ref[0, RB_V * nhp + hp] = v.astype(BF16)
        rb_ref[0, RB_KK * nhp + hp] = kk.astype(BF16)
        rb_ref[0, RB_G * nhp + hp] = g.astype(BF16)
        rb_ref[0, RB_BONUS * nhp + hp] = bonus.astype(BF16)


def _prep_call(p, sw, wupf, wupb, aupf, aupb, gup, vecs, mbd, nhp, tb=256):
    b, _, s, _ = p.shape
    n_t = s // tb
    halo = 16
    hb = tb // halo
    full = lambda a: pl.BlockSpec(a.shape, lambda bi, i: (0,) * a.ndim)
    return pl.pallas_call(
        functools.partial(_prep_kernel, tb=tb, n_t=n_t, nhp=nhp),
        grid=(b, n_t),
        in_specs=[pl.BlockSpec((1, N_SHIFT_BLOCKS, tb, LANES), lambda bi, i: (bi, 0, i, 0)),
                  pl.BlockSpec((1, N_SHIFT_BLOCKS, halo, LANES),
                               lambda bi, i: (bi, 0, jnp.maximum(i * hb - 1, 0), 0)),
                  pl.BlockSpec((1, N_SHIFT_BLOCKS, halo, LANES),
                               lambda bi, i: (bi, 0, jnp.minimum((i + 1) * hb, s // halo - 1), 0)),
                  full(sw), full(wupf), full(wupb), full(aupf), full(aupb), full(gup), full(vecs), full(mbd)],
        out_specs=[pl.BlockSpec((1, 9 * nhp, tb, LANES), lambda bi, i: (bi, 0, i, 0)),
                   pl.BlockSpec((1, 2 * nhp, tb, LANES), lambda bi, i: (bi, 0, i, 0))],
        out_shape=[jax.ShapeDtypeStruct((b, 9 * nhp, s, LANES), BF16),
                   jax.ShapeDtypeStruct((b, 2 * nhp, s, LANES), F32)],
        compiler_params=_cparams(("parallel", "parallel")),
        name="prep",
    )(p, p, p, sw, wupf, wupb, aupf, aupb, gup, vecs, mbd)


def _wkv_kernel(r_ref, v_ref, kk_ref, kf_ref, af_ref, lf_ref, rr_ref, vr_ref, kkr_ref, kb_ref, ab_ref, lb_ref,
                yf_ref, yb_ref, s_sc, *, L, nhp):
    @pl.when(pl.program_id(1) == 0)
    def _():
        s_sc[...] = jnp.zeros(s_sc.shape, F32)

    assert HEADS_PER_BLOCK == 2 and L == HEAD_DIM
    n2 = HEADS_PER_BLOCK * L
    ri = lax.broadcasted_iota(I32, (n2, LANES), 0)
    ci = lax.broadcasted_iota(I32, (n2, LANES), 1)
    head_match = (ri // L) == (ci // HEAD_DIM)
    st = lax.broadcasted_iota(I32, (L, LANES), 0)
    ss = lax.broadcasted_iota(I32, (L, LANES), 1) % L
    first_head = lax.broadcasted_iota(I32, (L, LANES), 1) < HEAD_DIM
    eye = (st == ss).astype(F32)
    strict = {False: st > ss, True: st < ss}
    incl = {False: st >= ss, True: st < ss}

    def stack(x):
        return jnp.where(head_match, jnp.concatenate([x] * HEADS_PER_BLOCK, axis=0), 0.0)

    def bdiag(x):
        return jnp.concatenate([jnp.where(first_head, x, 0.0), jnp.where(first_head, 0.0, x)], axis=0)

    chains = [(False, hp, r_ref, v_ref, kk_ref, kf_ref, af_ref, lf_ref) for hp in range(nhp)]
    chains += [(True, hp, rr_ref, vr_ref, kkr_ref, kb_ref, ab_ref, lb_ref) for hp in range(nhp)]
    n = len(chains)
    lws = [ch[7][0, ch[1]] for ch in chains]
    t_row = lax.broadcasted_iota(I32, (L, LANES), 0)

    def cumsum_time(x, rev):
        sh = 1
        while sh < L:
            if rev:
                x = x + jnp.where(t_row < L - sh, pltpu.roll(x, L - sh, 0), 0.0)
            else:
                x = x + jnp.where(t_row >= sh, pltpu.roll(x, sh, 0), 0.0)
            sh *= 2
        return x

    c_ins = [cumsum_time(lw, ch[0]) for ch, lw in zip(chains, lws)]
    lhs_l, rhs_l, v_l, vbd_l, bk_l, pl_l = [], [], [], [], [], []
    for i, (rev, hp, rr, vr, kkr, kdr, agr, _) in enumerate(chains):
        r = rr[0, hp].astype(F32)
        kk = kkr[0, hp].astype(F32)
        kd = kdr[0, hp].astype(F32)
        ag = agr[0, hp].astype(F32)
        v = vr[0, hp].astype(F32)
        c_in = c_ins[i]
        e_ex = jnp.exp(c_in - lws[i])
        e_inv = jnp.exp(-c_in)
        if rev:
            e_r = e_ex
            p_last = jnp.exp(c_in[0:1])
        else:
            e_r = jnp.exp(c_in)
            p_last = jnp.exp(c_in[L - 1:L])
        b_t = kk * ag * e_inv
        k_t = kd * e_inv
        lhs_l.append(jnp.concatenate([-kk * e_ex, r * e_r], axis=0).astype(BF16))
        rhs_l.append(jnp.concatenate([stack(b_t), stack(k_t)], axis=0).astype(BF16))
        bk_l.append(jnp.concatenate([b_t * p_last, k_t * p_last], axis=0).astype(BF16))
        v_l.append(v.astype(BF16))
        vbd_l.append(bdiag(v).astype(BF16))
        pl_l.append(p_last)
    g_l = [lax.dot_general(lhs_l[i], rhs_l[i], NT_DIMS, preferred_element_type=F32) for i in range(n)]
    s_old = [s_sc[i] for i in range(n)]
    ls_l = [lax.dot_general(lhs_l[i], s_old[i].astype(BF16), NT_DIMS, preferred_element_type=F32)
            for i in range(n)]
    a_ab = [jnp.where(strict[ch[0]], g[0:L, 0:n2], 0.0) for ch, g in zip(chains, g_l)]
    a_ak = [jnp.where(strict[ch[0]], g[0:L, n2:2 * n2], 0.0).astype(BF16) for ch, g in zip(chains, g_l)]
    a_r = [jnp.concatenate([jnp.where(incl[ch[0]], g[L:2 * L, 0:n2], 0.0),
                            jnp.where(incl[ch[0]], g[L:2 * L, n2:2 * n2], 0.0)], axis=1).astype(BF16)
           for ch, g in zip(chains, g_l)]
    x_l = [ls_l[i][0:L] + jnp.dot(a_ak[i], vbd_l[i], preferred_element_type=F32) for i in range(n)]
    rounds = L.bit_length() - 1
    pw = [jnp.dot(a.astype(BF16), bdiag(a).astype(BF16), preferred_element_type=F32) for a in a_ab]
    t_inv = [eye + a for a in a_ab]
    for j in range(1, rounds):
        pbd = [bdiag(p).astype(BF16) for p in pw]
        if j + 1 < rounds:
            both = [jnp.dot(jnp.concatenate([pw[i], t_inv[i]], axis=0).astype(BF16), pbd[i],
                            preferred_element_type=F32) for i in range(n)]
            pw = [m[0:L] for m in both]
            t_inv = [t + m[L:2 * L] for t, m in zip(t_inv, both)]
        else:
            t_inv = [t + jnp.dot(t.astype(BF16), pbd[i], preferred_element_type=F32)
                     for i, t in enumerate(t_inv)]
    u_l = [jnp.dot(t_inv[i].astype(BF16), bdiag(x_l[i]).astype(BF16), preferred_element_type=F32)
           for i in range(n)]
    y_l = [ls_l[i][L:2 * L] + jnp.dot(a_r[i], jnp.concatenate([bdiag(u_l[i]).astype(BF16), vbd_l[i]], axis=0),
                                      preferred_element_type=F32) for i in range(n)]
    upd_l = [lax.dot_general(jnp.concatenate([u_l[i].astype(BF16), v_l[i]], axis=0), bk_l[i], TN_DIMS,
                             preferred_element_type=F32) for i in range(n)]
    for i, (rev, hp, *_) in enumerate(chains):
        (yb_ref if rev else yf_ref)[0, hp] = y_l[i]
        s_sc[i] = s_old[i] * pl_l[i] + jnp.where(head_match, upd_l[i], 0.0)


def _wkv_call(rb, lw, nhp, L=WKV_CHUNK):
    b, _, s, _ = rb.shape
    nc = s // L
    fwd = lambda q: pl.BlockSpec((1, nhp, L, LANES), lambda bi, c: (bi, q, c, 0))
    bwd = lambda q: pl.BlockSpec((1, nhp, L, LANES), lambda bi, c: (bi, q, nc - 1 - c, 0))
    y_shape = jax.ShapeDtypeStruct((b, nhp, s, LANES), F32)
    return pl.pallas_call(
        functools.partial(_wkv_kernel, L=L, nhp=nhp),
        grid=(b, nc),
        in_specs=[fwd(RB_R), fwd(RB_V), fwd(RB_KK), fwd(RB_KF), fwd(RB_AF), fwd(0),
                  bwd(RB_R), bwd(RB_V), bwd(RB_KK), bwd(RB_KB), bwd(RB_AB), bwd(1)],
        out_specs=[fwd(0), bwd(0)],
        out_shape=[y_shape, y_shape],
        scratch_shapes=[pltpu.VMEM((2 * nhp, LANES, LANES), F32)],
        compiler_params=_cparams(("parallel", "arbitrary")),
        name="wkv",
    )(rb, rb, rb, rb, rb, lw, rb, rb, rb, rb, rb, lw)


def _post_kernel(yf_ref, yb_ref, bonus_ref, g_ref, lnw_ref, lnb_ref, mbd_ref, o_ref, *, nhp):
    mbd = mbd_ref[...]
    inv_n = 1.0 / HEAD_DIM
    for hp in range(nhp):
        y = yf_ref[0, hp] + yb_ref[0, hp]
        mu = _head_sums_split(y, mbd) * inv_n
        d = y - mu
        var = _head_sums_split(d * d, mbd) * inv_n
        yn = d * lax.rsqrt(var + GN_EPS) * lnw_ref[hp:hp + 1, :] + lnb_ref[hp:hp + 1, :]
        out = (yn + bonus_ref[0, hp].astype(F32)) * g_ref[0, hp].astype(F32)
        o_ref[0, :, hp * LANES:(hp + 1) * LANES] = out.astype(o_ref.dtype)


def _post_call(yf, yb, rb, lnw, lnb, mbd, nhp, tb=256):
    b, _, s, _ = yf.shape
    yblk = pl.BlockSpec((1, nhp, tb, LANES), lambda bi, i: (bi, 0, i, 0))
    rblk = lambda q: pl.BlockSpec((1, nhp, tb, LANES), lambda bi, i: (bi, q, i, 0))
    full = lambda a: pl.BlockSpec(a.shape, lambda bi, i: (0,) * a.ndim)
    return pl.pallas_call(
        functools.partial(_post_kernel, nhp=nhp),
        grid=(b, s // tb),
        in_specs=[yblk, yblk, rblk(RB_BONUS), rblk(RB_G), full(lnw), full(lnb), full(mbd)],
        out_specs=pl.BlockSpec((1, tb, nhp * LANES), lambda bi, i: (bi, i, 0)),
        out_shape=jax.ShapeDtypeStruct((b, s, nhp * LANES), BF16),
        compiler_params=_cparams(("parallel", "parallel")),
        name="post",
    )(yf, yb, rb, rb, lnw, lnb, mbd)


def _na_kernel(q_ref, k_ref, v_ref, bias_ref, qw_ref, kw_ref, mbd_ref, o_ref, kn_sc, *, rq, rows):
    rbi = pl.program_id(2)
    mbd = mbd_ref[...]
    inv_n = 1.0 / HEAD_DIM
    span = WIN_R * GRID_W
    s_len = kn_sc.shape[0]

    @pl.when(rbi == 0)
    def _():
        def body(c, carry):
            off = pl.multiple_of(c * span, span)
            k = k_ref[0, 0, pl.ds(off, span), :].astype(F32)
            ms = _head_sums_split(k * k, mbd) * inv_n
            kn_sc[pl.ds(off, span), :] = (k * lax.rsqrt(ms + NORM_EPS) * kw_ref[...]).astype(BF16)
            return carry
        lax.fori_loop(0, s_len // span, body, 0)

    ri = lax.broadcasted_iota(I32, (HEADS_PER_BLOCK * GRID_W, LANES), 0)
    ci = lax.broadcasted_iota(I32, (HEADS_PER_BLOCK * GRID_W, LANES), 1)
    head_match = (ri // GRID_W) == (ci // HEAD_DIM)
    lane = lax.broadcasted_iota(I32, (GRID_W, LANES), 1)
    scale = HEAD_DIM ** -0.5
    q_all = q_ref[0, 0].astype(F32)
    ms = _head_sums_split(q_all * q_all, mbd) * inv_n
    qn_all = q_all * lax.rsqrt(ms + NORM_EPS) * qw_ref[...] * scale
    starts, offs = [], []
    for qi in range(rq):
        i = rbi * rq + qi
        rs = jnp.clip(i - WIN_R // 2, 0, rows - WIN_R)
        offs.append(rs - i + WIN_R - 1)
        starts.append(pl.multiple_of(rs * GRID_W, GRID_W))
    qs_l = [jnp.where(head_match, jnp.concatenate([qn_all[qi * GRID_W:(qi + 1) * GRID_W]] * HEADS_PER_BLOCK,
                                                  axis=0), 0.0).astype(BF16) for qi in range(rq)]
    def bias(off):
        return jnp.concatenate([bias_ref[0, off + 2 * i] for i in range(WIN_R // 2)], axis=1)

    s_l = [lax.dot_general(qs_l[qi], kn_sc[pl.ds(starts[qi], span), :], NT_DIMS, preferred_element_type=F32)
           + bias(offs[qi]) for qi in range(rq)]
    p_l, l_l = [], []
    for s in s_l:
        p = jnp.exp(s - jnp.max(s, axis=-1, keepdims=True))
        p_l.append(p.astype(BF16))
        l_l.append(jnp.sum(p, axis=-1, keepdims=True))
    o_l = [jnp.dot(p_l[qi], v_ref[0, 0, pl.ds(starts[qi], span), :], preferred_element_type=F32) / l_l[qi]
           for qi in range(rq)]
    for qi, o in enumerate(o_l):
        out = o[0:GRID_W]
        for h in range(1, HEADS_PER_BLOCK):
            out = jnp.where(lane // HEAD_DIM == h, o[h * GRID_W:(h + 1) * GRID_W], out)
        o_ref[0, qi * GRID_W:(qi + 1) * GRID_W, :] = out.astype(o_ref.dtype)


def _na_call(p, bias, qw, kw, mbd, q_blk0, nhp, rq=16):
    b, _, s, _ = p.shape
    rows = s // GRID_W
    rq = min(rq, rows)
    full = lambda a: pl.BlockSpec(a.shape, lambda bi, hp, r: (0,) * a.ndim)
    return pl.pallas_call(
        functools.partial(_na_kernel, rq=rq, rows=rows),
        grid=(b, nhp, rows // rq),
        in_specs=[pl.BlockSpec((1, 1, rq * GRID_W, LANES), lambda bi, hp, r: (bi, q_blk0 + hp, r, 0)),
                  pl.BlockSpec((1, 1, s, LANES), lambda bi, hp, r: (bi, q_blk0 + nhp + hp, 0, 0)),
                  pl.BlockSpec((1, 1, s, LANES), lambda bi, hp, r: (bi, q_blk0 + 2 * nhp + hp, 0, 0)),
                  pl.BlockSpec((1,) + bias.shape[1:], lambda bi, hp, r: (hp, 0, 0, 0)),
                  full(qw), full(kw), full(mbd)],
        out_specs=pl.BlockSpec((1, rq * GRID_W, LANES), lambda bi, hp, r: (bi, r, hp)),
        out_shape=jax.ShapeDtypeStruct((b, s, nhp * LANES), BF16),
        scratch_shapes=[pltpu.VMEM((s, LANES), BF16)],
        compiler_params=_cparams(("parallel", "parallel", "arbitrary")),
        name="na",
    )(p, p, p, bias, qw, kw, mbd)


def _na_bias_table(rel_bias):
    h = rel_bias.shape[0]
    qc = np.arange(GRID_W)
    kc = np.arange(GRID_W)
    win0 = np.clip(qc - WIN_C // 2, 0, GRID_W - WIN_C)
    valid = (kc[None, :] >= win0[:, None]) & (kc[None, :] < win0[:, None] + WIN_C)
    coff = np.clip(kc[None, :] - qc[:, None] + WIN_C - 1, 0, 2 * WIN_C - 2)
    n_row, n_col = 2 * WIN_R - 1, 2 * WIN_C - 1
    n_ro = n_row - 1
    sel = np.zeros((n_ro, 2, n_row), np.float32)
    for ro in range(n_ro):
        for part in range(2):
            sel[ro, part, ro + part] = 1.0
    pick = np.zeros((2, n_col, GRID_W, 2 * GRID_W), np.float32)
    for part in range(2):
        pick[part, coff, qc[:, None], part * GRID_W + kc[None, :]] = 1.0
    a = jnp.einsum('hrc,apr->hapc', rel_bias.astype(F32), jnp.asarray(sel), precision=HI)
    a = a.reshape(h // HEADS_PER_BLOCK, HEADS_PER_BLOCK, n_ro, 2 * n_col).transpose(0, 2, 1, 3)
    t = jnp.einsum('ahbx,xql->ahbql', a, jnp.asarray(pick.reshape(2 * n_col, GRID_W, 2 * GRID_W)),
                   precision=HI)
    t = jnp.where(np.tile(valid, (1, 2)), t, NEG_BIG)
    return t.reshape(h // HEADS_PER_BLOCK, n_ro, HEADS_PER_BLOCK * GRID_W, 2 * GRID_W)


def _outproj_kernel(yr_ref, yn_ref, x_ref, mod_ref, nw_ref, w1_ref, w2_ref, wr_ref, x1_ref, h2_ref, h2row_ref,
                    lg_ref):
    tm = x_ref.shape[1]
    sub = LANES
    tiles = [slice(i * sub, (i + 1) * sub) for i in range(tm // sub)]
    accs = [jnp.dot(yr_ref[0, r], w1_ref[...], preferred_element_type=F32)
            + jnp.dot(yn_ref[0, r], w2_ref[...], preferred_element_type=F32) for r in tiles]
    h2s = []
    for r, acc in zip(tiles, accs):
        x1 = x_ref[0, r] + mod_ref[0, 2:3, :] * acc
        x1_ref[0, r] = x1
        ms = jnp.mean(x1 * x1, axis=-1, keepdims=True)
        y = x1 * lax.rsqrt(ms + NORM_EPS) * nw_ref[...]
        h2 = y * (1.0 + mod_ref[0, 4:5, :]) + mod_ref[0, 3:4, :]
        packed = _pack_halves(h2)
        h2row_ref[0, r] = packed
        for j in range(h2_ref.shape[2]):
            h2_ref[0, r, j, :] = packed[:, j * LANES:(j + 1) * LANES]
        h2s.append(h2)
    wr = wr_ref[...]
    wr_hi = wr.astype(BF16)
    wr_lo = (wr - wr_hi.astype(F32)).astype(BF16)
    for r, h2 in zip(tiles, h2s):
        h_hi = h2.astype(BF16)
        h_lo = (h2 - h_hi.astype(F32)).astype(BF16)
        lg_ref[:, r] = (lax.dot_general(wr_hi, h_hi, NT_DIMS, preferred_element_type=F32)
                        + lax.dot_general(wr_hi, h_lo, NT_DIMS, preferred_element_type=F32)
                        + lax.dot_general(wr_lo, h_hi, NT_DIMS, preferred_element_type=F32))


def _outproj_call(yr, yn, x, mod3, norm_w, w_o, wr_t, tm=512):
    b, s, d = x.shape
    dh = yr.shape[-1]
    assert w_o.shape == (2 * dh, d)
    ne = wr_t.shape[0]
    nt = s // tm
    full = lambda a: pl.BlockSpec(a.shape, lambda bi, i: (0,) * a.ndim)
    return pl.pallas_call(
        _outproj_kernel,
        grid=(b, nt),
        in_specs=[pl.BlockSpec((1, tm, dh), lambda bi, i: (bi, i, 0)),
                  pl.BlockSpec((1, tm, dh), lambda bi, i: (bi, i, 0)),
                  pl.BlockSpec((1, tm, d), lambda bi, i: (bi, i, 0)),
                  pl.BlockSpec((1, 6, d), lambda bi, i: (bi, 0, 0)),
                  full(norm_w),
                  pl.BlockSpec((dh, d), lambda bi, i: (0, 0)),
                  pl.BlockSpec((dh, d), lambda bi, i: (1, 0)),
                  full(wr_t)],
        out_specs=[pl.BlockSpec((1, tm, d), lambda bi, i: (bi, i, 0)),
                   pl.BlockSpec((1, tm, d // 2 // LANES, LANES), lambda bi, i: (bi, i, 0, 0)),
                   pl.BlockSpec((1, tm, d // 2), lambda bi, i: (bi, i, 0)),
                   pl.BlockSpec((ne, tm), lambda bi, i: (0, bi * nt + i))],
        out_shape=[jax.ShapeDtypeStruct((b, s, d), F32),
                   jax.ShapeDtypeStruct((b, s, d // 2 // LANES, LANES), U32),
                   jax.ShapeDtypeStruct((b, s, d // 2), U32),
                   jax.ShapeDtypeStruct((ne, b * s), F32)],
        compiler_params=_cparams(("parallel", "parallel")),
        name="outproj",
    )(yr, yn, x, mod3, norm_w, w_o, w_o, wr_t)


def _first_argmax(x, idx, n):
    m = jnp.max(x, axis=0, keepdims=True)
    a = jnp.min(jnp.where(x == m, idx, n), axis=0, keepdims=True)
    return m, a


def _route_kernel(lg_ref, bias_ref, idx_ref, gw_ref, pos_ref, cnt_ref, *, tr):
    @pl.when(pl.program_id(0) == 0)
    def _():
        cnt_ref[...] = jnp.zeros(cnt_ref.shape, F32)

    gsz = N_EXPERTS // N_GROUPS
    scores = _sigmoid(lg_ref[...])
    biased = scores + bias_ref[:, 0:1]
    ig = lax.broadcasted_iota(I32, (gsz, tr), 0)
    grp_rows = []
    for g in range(N_GROUPS):
        blk = biased[g * gsz:(g + 1) * gsz]
        m1, a1 = _first_argmax(blk, ig, gsz)
        m2 = jnp.max(jnp.where(ig == a1, -jnp.inf, blk), axis=0, keepdims=True)
        grp_rows.append(m1 + m2)
    grp = jnp.concatenate(grp_rows, axis=0)
    ign = lax.broadcasted_iota(I32, (N_GROUPS, tr), 0)
    sel = jnp.zeros((N_GROUPS, tr), jnp.bool_)
    for _ in range(TOPK_GROUPS):
        _, a = _first_argmax(grp, ign, N_GROUPS)
        hit = ign == a
        sel = jnp.logical_or(sel, hit)
        grp = jnp.where(hit, -jnp.inf, grp)
    masked = jnp.concatenate(
        [jnp.where(sel[g:g + 1], biased[g * gsz:(g + 1) * gsz], -jnp.inf) for g in range(N_GROUPS)], axis=0)
    ie = lax.broadcasted_iota(I32, (N_EXPERTS, tr), 0)
    picks, pick_scores = [], []
    onehot = jnp.zeros((N_EXPERTS, tr), F32)
    for _ in range(TOP_K):
        _, a = _first_argmax(masked, ie, N_EXPERTS)
        hit = ie == a
        picks.append(a)
        pick_scores.append(jnp.sum(jnp.where(hit, scores, 0.0), axis=0, keepdims=True))
        onehot = onehot + hit.astype(F32)
        masked = jnp.where(hit, -jnp.inf, masked)
    total = pick_scores[0]
    for sc in pick_scores[1:]:
        total = total + sc
    t0 = lax.broadcasted_iota(I32, (tr, tr), 0)
    t1 = lax.broadcasted_iota(I32, (tr, tr), 1)
    before = (t0 < t1).astype(BF16)
    rank = jnp.dot(onehot.astype(BF16), before, preferred_element_type=F32) + cnt_ref[:, 0:1]
    zero_i = jnp.zeros((1, tr), I32)
    zero_f = jnp.zeros((1, tr), F32)
    for k in range(8):
        if k < TOP_K:
            idx_ref[k:k + 1, :] = picks[k]
            gw_ref[k:k + 1, :] = pick_scores[k] / total * ROUTED_SCALE
            pos = jnp.sum(jnp.where(ie == picks[k], rank, 0.0), axis=0, keepdims=True)
            pos_ref[k:k + 1, :] = pos.astype(I32)
        else:
            idx_ref[k:k + 1, :] = zero_i
            gw_ref[k:k + 1, :] = zero_f
            pos_ref[k:k + 1, :] = zero_i
    cnt_ref[...] = cnt_ref[...] + jnp.sum(onehot, axis=1, keepdims=True)


def _route_call(lg_t, bias2, tr=512):
    ne, t = lg_t.shape
    tr = min(tr, t)
    tok = pl.BlockSpec((8, tr), lambda i: (0, i))
    return pl.pallas_call(
        functools.partial(_route_kernel, tr=tr),
        grid=(t // tr,),
        in_specs=[pl.BlockSpec((ne, tr), lambda i: (0, i)),
                  pl.BlockSpec(bias2.shape, lambda i: (0, 0))],
        out_specs=[tok, tok, tok, pl.BlockSpec((ne, LANES), lambda i: (0, 0))],
        out_shape=[jax.ShapeDtypeStruct((8, t), I32), jax.ShapeDtypeStruct((8, t), F32),
                   jax.ShapeDtypeStruct((8, t), I32), jax.ShapeDtypeStruct((ne, LANES), F32)],
        compiler_params=_cparams(("arbitrary",)),
        name="route",
    )(lg_t, bias2)


FILL_CHUNKS = (1, 8, 64)
ZERO_ROWS = FILL_CHUNKS[-1]


def _dispatch_kernel(slot_ref, fill_lo_ref, fill_hi_ref, h_ref, xs_ref, zrow, sem, *, td):
    def row_copy(t, slot):
        return pltpu.make_async_copy(h_ref.at[t], xs_ref.at[slot], sem)

    def start(t, carry):
        for k in range(TOP_K):
            row_copy(t, slot_ref[t * SLOT_STRIDE + k]).start(priority=k % 2)
        return carry

    lax.fori_loop(0, td, start, 0)
    for k in range(TOP_K):
        pltpu.make_async_copy(h_ref, xs_ref.at[pl.ds(0, td)], sem).wait()

    @pl.when(pl.program_id(0) == pl.num_programs(0) - 1)
    def _():
        zrow[...] = jnp.zeros(zrow.shape, zrow.dtype)

        def zero_copy(row, n):
            if n > 1:
                row = pl.multiple_of(row, FILL_CHUNKS[1])
            return pltpu.make_async_copy(zrow.at[pl.ds(0, n)], xs_ref.at[pl.ds(row, n)], sem)

        def for_each_chunk(r, fn):
            lo, hi = fill_lo_ref[r], fill_hi_ref[r]
            for ci, n in enumerate(FILL_CHUNKS):
                if ci + 1 < len(FILL_CHUNKS):
                    count = jnp.minimum(((-lo) % FILL_CHUNKS[ci + 1]) // n, (hi - lo) // n)
                else:
                    count = (hi - lo) // n
                fn(lo, n, count)
                lo = lo + count * n

        def start_range(r, carry):
            def go(lo, n, count):
                lax.fori_loop(0, count, lambda i, c: (zero_copy(lo + i * n, n).start(), c)[1], 0)
            for_each_chunk(r, go)
            return carry

        def wait_range(r, carry):
            def go(lo, n, count):
                lax.fori_loop(0, count, lambda i, c: (zero_copy(0, n).wait(), c)[1], 0)
            for_each_chunk(r, go)
            return carry

        n_ranges = fill_lo_ref.shape[0]
        lax.fori_loop(0, n_ranges, start_range, 0)
        lax.fori_loop(0, n_ranges, wait_range, 0)


def _dispatch_call(slots_flat, fill_lo, fill_hi, h2, n_rows, td=256):
    t, nj, _ = h2.shape
    smem = pl.BlockSpec(memory_space=pltpu.SMEM)
    return pl.pallas_call(
        functools.partial(_dispatch_kernel, td=td),
        grid=(t // td,),
        in_specs=[pl.BlockSpec((td * SLOT_STRIDE,), lambda i: (i,), memory_space=pltpu.SMEM),
                  smem, smem,
                  pl.BlockSpec((td, nj, LANES), lambda i: (i, 0, 0))],
        out_specs=pl.BlockSpec(memory_space=pl.ANY),
        out_shape=jax.ShapeDtypeStruct((n_rows, nj, LANES), h2.dtype),
        scratch_shapes=[pltpu.VMEM((ZERO_ROWS, nj, LANES), h2.dtype), pltpu.SemaphoreType.DMA(())],
        compiler_params=_cparams(("arbitrary",)),
        name="dispatch",
    )(slots_flat, fill_lo, fill_hi, h2)


def _experts_kernel(be_ref, first_ref, nxt_ref, par_ref, rows_ref, nv_ref, xs_hbm, wg_hbm, wu_hbm, wd_hbm, ys_ref,
                    wg_buf, wu_buf, wd_buf, wg_sc, wu_sc, wd_sc, xpack, sem, isem):
    nj = xs_hbm.shape[1]
    b = pl.program_id(0)
    m = xpack.shape[1]
    half = wg_sc.shape[0] // 2
    oslot = lax.rem(b, 2)

    def in_copies(slot, row0):
        return [pltpu.make_async_copy(xs_hbm.at[pl.ds(row0, m), j],
                                      xpack.at[slot, :, pl.ds(j * LANES, LANES)], isem.at[slot]) for j in range(nj)]

    @pl.when(b == 0)
    def _():
        for c in in_copies(0, 0):
            c.start()

    @pl.when(b + 1 < nv_ref[0])
    def _():
        for c in in_copies(1 - oslot, pl.multiple_of((b + 1) * m, m)):
            c.start()

    def weight_copies(e, slot):
        return [pltpu.make_async_copy(src.at[e], dst.at[slot], sem.at[slot, i])
                for i, (src, dst) in enumerate(((wg_hbm, wg_buf), (wu_hbm, wu_buf), (wd_hbm, wd_buf)))]

    def compute(nrows):
        x_lo, x_hi = _unpack_halves(xpack[oslot, 0:nrows])
        x_lo, x_hi = x_lo.astype(BF16), x_hi.astype(BF16)
        g = (jnp.dot(x_lo, wg_sc[0:half], preferred_element_type=F32)
             + jnp.dot(x_hi, wg_sc[half:], preferred_element_type=F32))
        u = (jnp.dot(x_lo, wu_sc[0:half], preferred_element_type=F32)
             + jnp.dot(x_hi, wu_sc[half:], preferred_element_type=F32))
        a = (_silu(g) * u).astype(BF16)
        ys_ref[0:nrows] = _pack_halves(jnp.dot(a, wd_sc[...], preferred_element_type=F32))

    @pl.when(b < nv_ref[0])
    def _():
        for c in in_copies(oslot, 0):
            c.wait()

        @pl.when(b == 0)
        def _():
            for c in weight_copies(be_ref[0], par_ref[0]):
                c.start()

        @pl.when(first_ref[b] == 1)
        def _():
            slot = par_ref[b]
            for c in weight_copies(be_ref[b], slot):
                c.wait()

            @pl.when(nxt_ref[b] >= 0)
            def _():
                for c in weight_copies(nxt_ref[b], 1 - slot):
                    c.start()

            wg_sc[...] = wg_buf[slot].astype(BF16)
            wu_sc[...] = wu_buf[slot].astype(BF16)
            wd_sc[...] = wd_buf[slot].astype(BF16)

        @pl.when(rows_ref[b] > m // 2)
        def _():
            compute(m)

        @pl.when(rows_ref[b] <= m // 2)
        def _():
            compute(m // 2)
            ys_ref[m // 2:] = jnp.zeros((m - m // 2, ys_ref.shape[1]), ys_ref.dtype)

    @pl.when(b >= nv_ref[0])
    def _():
        ys_ref[...] = jnp.zeros(ys_ref.shape, ys_ref.dtype)


def _experts_call(block_e, first, nxt, par, rows_valid, n_valid, xs, wg, wu, wd):
    total, nj, _ = xs.shape
    d = wg.shape[1]
    m = DISPATCH_BLOCK
    ff = wg.shape[-1]

    hbm = pl.BlockSpec(memory_space=pl.ANY)
    grid_spec = pltpu.PrefetchScalarGridSpec(
        num_scalar_prefetch=6,
        grid=(total // m,),
        in_specs=[hbm, hbm, hbm, hbm],
        out_specs=pl.BlockSpec((m, nj * LANES), lambda b, *s: (b, 0)),
        scratch_shapes=[pltpu.VMEM((2, d, ff), F32), pltpu.VMEM((2, d, ff), F32), pltpu.VMEM((2, ff, d), F32),
                        pltpu.VMEM((d, ff), BF16), pltpu.VMEM((d, ff), BF16), pltpu.VMEM((ff, d), BF16),
                        pltpu.VMEM((2, m, nj * LANES), U32),
                        pltpu.SemaphoreType.DMA((2, 3)), pltpu.SemaphoreType.DMA((2,))],
    )
    return pl.pallas_call(
        _experts_kernel,
        grid_spec=grid_spec,
        out_shape=jax.ShapeDtypeStruct((total, nj * LANES), U32),
        compiler_params=_cparams(("arbitrary",)),
        name="experts",
    )(block_e, first, nxt, par, rows_valid, n_valid, xs, wg, wu, wd)


COMBINE_INLINE_ROWS = 64


def _combine_kernel(slot_ref, x1_ref, h2_ref, gw_ref, mod_ref, wg_ref, wu_ref, wd_ref, ys_ref, o_ref,
                    buf, sem, *, tc):
    def row_copy(t, k, slot):
        return pltpu.make_async_copy(ys_ref.at[pl.ds(slot, 1)], buf.at[k, pl.ds(t, 1)], sem)

    def start(t, carry):
        for k in range(TOP_K):
            row_copy(t, k, slot_ref[t * SLOT_STRIDE + k]).start(priority=k % 2)
        return carry

    inline = min(COMBINE_INLINE_ROWS, tc)
    lax.fori_loop(0, tc - inline, start, 0)
    for t in range(tc - inline, tc):
        start(t, 0)
    half = wg_ref.shape[0] // 2
    h_lo, h_hi = _unpack_halves(h2_ref[0])
    h_lo, h_hi = h_lo.astype(BF16), h_hi.astype(BF16)
    g = (jnp.dot(h_lo, wg_ref[0:half], preferred_element_type=F32)
         + jnp.dot(h_hi, wg_ref[half:], preferred_element_type=F32))
    u = (jnp.dot(h_lo, wu_ref[0:half], preferred_element_type=F32)
         + jnp.dot(h_hi, wu_ref[half:], preferred_element_type=F32))
    acc = jnp.dot((_silu(g) * u).astype(BF16), wd_ref[...], preferred_element_type=F32)
    for k in range(TOP_K):
        pltpu.make_async_copy(ys_ref.at[pl.ds(0, tc)], buf.at[k], sem).wait()
    gw = gw_ref[...]
    acc_lo, acc_hi = acc[:, :half], acc[:, half:]
    for k in range(TOP_K):
        y_lo, y_hi = _unpack_halves(buf[k])
        acc_lo = acc_lo + gw[:, k:k + 1] * y_lo
        acc_hi = acc_hi + gw[:, k:k + 1] * y_hi
    gate = mod_ref[0, 5:6, :]
    o_ref[0, :, 0:half] = x1_ref[0, :, 0:half] + gate[:, :half] * acc_lo
    o_ref[0, :, half:] = x1_ref[0, :, half:] + gate[:, half:] * acc_hi


def _combine_call(slots_flat, x1, h2, gw_tok, mod3, wg, wu, wd, ys, tc=256):
    b, s, d = x1.shape
    nt = s // tc
    full = lambda a: pl.BlockSpec(a.shape, lambda bi, i: (0,) * a.ndim)
    return pl.pallas_call(
        functools.partial(_combine_kernel, tc=tc),
        grid=(b, nt),
        in_specs=[pl.BlockSpec((tc * SLOT_STRIDE,), lambda bi, i: (bi * nt + i,), memory_space=pltpu.SMEM),
                  pl.BlockSpec((1, tc, d), lambda bi, i: (bi, i, 0)),
                  pl.BlockSpec((1, tc, d // 2), lambda bi, i: (bi, i, 0)),
                  pl.BlockSpec((tc, 8), lambda bi, i: (bi * nt + i, 0)),
                  pl.BlockSpec((1, 6, d), lambda bi, i: (bi, 0, 0)),
                  full(wg), full(wu), full(wd),
                  pl.BlockSpec(memory_space=pl.ANY)],
        out_specs=pl.BlockSpec((1, tc, d), lambda bi, i: (bi, i, 0)),
        out_shape=jax.ShapeDtypeStruct((b, s, d), F32),
        scratch_shapes=[pltpu.VMEM((TOP_K, tc, d // 2), U32), pltpu.SemaphoreType.DMA(())],
        compiler_params=_cparams(("arbitrary", "arbitrary")),
        name="combine",
    )(slots_flat, x1, h2, gw_tok, mod3, wg, wu, wd, ys)


def _pad_rows(w, rows, at=0):
    out = jnp.zeros((rows, w.shape[1]), w.dtype)
    return out.at[at:at + w.shape[0]].set(w)


def _mixing_stage(p, rwkv_shift, w0_f, w_up_f, w0_b, w_up_b, a0_f, a_up_f, a0_b, a_up_b, g_up,
                  k_k, k_a, r_k, ln_x_w, ln_x_b, q_norm_w, k_norm_w, rel_bias):
    d_rwkv = w0_f.shape[0]
    nhp = d_rwkv // LANES
    mbd = _head_sum_matrix()
    n_shift = rwkv_shift.shape[1]
    sw = jnp.pad(rwkv_shift, ((0, 0), (0, N_SHIFT_BLOCKS * LANES - n_shift))).reshape(3, N_SHIFT_BLOCKS, LANES)
    wupf = _pad_rows(w_up_f, LANES, 0).astype(BF16)
    wupb = _pad_rows(w_up_b, LANES, DECAY_LORA).astype(BF16)
    aupf = _pad_rows(a_up_f, LANES, 0).astype(BF16)
    aupb = _pad_rows(a_up_b, LANES, AAA_LORA).astype(BF16)
    gup = _pad_rows(g_up, 2 * LANES, 0).astype(BF16)
    vecs = jnp.stack([w0_f, w0_b, a0_f, a0_b, k_k, k_a, r_k.reshape(-1)]).reshape(7, nhp, LANES)
    rb, lw = _prep_call(p, sw, wupf, wupb, aupf, aupb, gup, vecs, mbd, nhp)
    y_f, y_b = _wkv_call(rb, lw, nhp)
    y_rwkv = _post_call(y_f, y_b, rb, ln_x_w.reshape(nhp, LANES), ln_x_b.reshape(nhp, LANES), mbd, nhp)
    bias = _na_bias_table(rel_bias)
    qw = jnp.tile(q_norm_w, HEADS_PER_BLOCK).reshape(1, LANES)
    kw = jnp.tile(k_norm_w, HEADS_PER_BLOCK).reshape(1, LANES)
    y_na = _na_call(p, bias, qw, kw, mbd, N_SHIFT_BLOCKS, nhp)
    return y_rwkv, y_na


def _moe_stage(x1, h2_tiles, h2_rows, lg_t, mod3, router_bias, w_gate_e, w_up_e, w_down_e, w_gate_s, w_up_s,
               w_down_s):
    b, s, d = x1.shape
    t = b * s
    m = DISPATCH_BLOCK
    bias2 = jnp.broadcast_to(router_bias.astype(F32)[:, None], (N_EXPERTS, LANES))
    idx_t, gw_t, pos_t, cnt = _route_call(lg_t, bias2)
    counts = cnt[:, 0].astype(I32)
    padded = (counts + m - 1) // m * m
    pad_end = jnp.cumsum(padded)
    pad_start = pad_end - padded
    n_blocks = -(-(t * TOP_K + N_EXPERTS * m) // m)
    n_valid = (pad_end[-1] // m).astype(I32).reshape(1)
    block_row0 = jnp.arange(n_blocks, dtype=I32) * m
    block_e = jnp.minimum(jnp.sum((pad_end[None, :] <= block_row0[:, None]).astype(I32), axis=1),
                          N_EXPERTS - 1).astype(I32)
    assert m % ZERO_ROWS == 0
    expert_ids = jnp.arange(N_EXPERTS, dtype=I32)
    start_of = jnp.sum(jnp.where(idx_t[:, :, None] == expert_ids, pad_start, 0), axis=-1)
    slots = (start_of + pos_t).astype(I32)
    slots_flat = slots.T.reshape(-1)
    gw_tok = gw_t.T
    fill_lo = jnp.concatenate([pad_start + counts, pad_end[-1:]]).astype(I32)
    fill_hi = jnp.concatenate([pad_end, jnp.full((1,), n_blocks * m, I32)]).astype(I32)
    xs = _dispatch_call(slots_flat, fill_lo, fill_hi, h2_tiles.reshape(t, *h2_tiles.shape[2:]), n_blocks * m)
    row_end = (pad_start + counts).astype(I32)
    nonempty = counts > 0
    first = jnp.logical_and(block_row0 == pad_start[block_e], block_row0 < pad_end[-1]).astype(I32)
    ordinal = jnp.cumsum(nonempty.astype(I32)) - 1
    par = (ordinal[block_e] & 1).astype(I32)
    cand = jnp.where(nonempty, expert_ids, N_EXPERTS)
    later = jnp.concatenate([lax.cummin(cand[::-1])[::-1][1:], jnp.full((1,), N_EXPERTS, I32)])
    nxt = jnp.where(later < N_EXPERTS, later, -1)[block_e].astype(I32)
    rows_valid = jnp.clip(row_end[block_e] - block_row0, 0, m).astype(I32)
    ys = _experts_call(block_e, first, nxt, par, rows_valid, n_valid, xs, w_gate_e, w_up_e, w_down_e)
    return _combine_call(slots_flat, x1, h2_rows, gw_tok, mod3, w_gate_s.astype(BF16), w_up_s.astype(BF16),
                         w_down_s.astype(BF16), ys)


def kernel(x, c, w_ada, b_ada, norm1_w, w_in, rwkv_shift, w0_f, w_up_f, w0_b, w_up_b, a0_f, a_up_f, a0_b,
           a_up_b, g_up, k_k, k_a, r_k, ln_x_w, ln_x_b, q_norm_w, k_norm_w, rel_bias, w_out, norm2_w,
           w_router, router_bias, w_gate_e, w_up_e, w_down_e, w_gate_s, w_up_s, w_down_s):
    bn, sn, d = x.shape
    depth = w_ada.shape[0]
    for l in range(depth):
        c_pad = jnp.pad(c, ((0, 8 - bn % 8 if bn % 8 else 0), (0, 0)))
        mod = _mod_call(c_pad, w_ada[l], b_ada[l].reshape(1, -1))[:bn]
        mod3 = mod.reshape(bn, 6, d)
        rwkv_in = rwkv_shift.shape[-1]
        pad_cols = N_SHIFT_BLOCKS * LANES - rwkv_in
        w_in16 = w_in[l].astype(BF16)
        w_in_r = jnp.concatenate([w_in16[:, :rwkv_in], jnp.zeros((d, pad_cols), BF16), w_in16[:, rwkv_in:]], axis=1)
        p = _proj_call(x, mod3, norm1_w[l].reshape(1, d), w_in_r)
        y_rwkv, y_na = _mixing_stage(p, rwkv_shift[l], w0_f[l], w_up_f[l], w0_b[l], w_up_b[l], a0_f[l],
                                     a_up_f[l], a0_b[l], a_up_b[l], g_up[l], k_k[l], k_a[l], r_k[l],
                                     ln_x_w[l], ln_x_b[l], q_norm_w[l], k_norm_w[l], rel_bias[l])
        x1, h2_tiles, h2_rows, lg_t = _outproj_call(y_rwkv, y_na, x, mod3, norm2_w[l].reshape(1, d),
                                                    w_out[l].astype(BF16), w_router[l].T)
        x = _moe_stage(x1, h2_tiles, h2_rows, lg_t, mod3, router_bias[l], w_gate_e[l], w_up_e[l], w_down_e[l],
                       w_gate_s[l], w_up_s[l], w_down_s[l])
    return x
```

```python
import functools

import jax
import jax.numpy as jnp
import numpy as np
from jax import lax
from jax.experimental import pallas as pl
from jax.experimental.pallas import tpu as pltpu

F32 = jnp.float32
BF16 = jnp.bfloat16
I32 = jnp.int32
HI = lax.Precision.HIGHEST

LANES = 128
HEAD_DIM = 64
HEADS_PER_BLOCK = LANES // HEAD_DIM
GRID_W = 64
WIN_R = 8
WIN_C = 16
DECAY_LORA = 64
AAA_LORA = 64
GATE_LORA = 160
N_EXPERTS = 64
TOP_K = 6
SLOT_STRIDE = 8
N_GROUPS = 8
TOPK_GROUPS = 4
ROUTED_SCALE = 2.5
DISPATCH_BLOCK = 512
NORM_EPS = 1e-6
GN_EPS = 64e-5
L2_EPS = 1e-12
NEG_BIG = -1e30
WKV_CHUNK = 64
VMEM_LIMIT = 56 * 1024 * 1024

NT_DIMS = (((1,), (1,)), ((), ()))
TN_DIMS = (((0,), (0,)), ((), ()))


def _cparams(sem):
    return pltpu.CompilerParams(dimension_semantics=sem, vmem_limit_bytes=VMEM_LIMIT)


def _sigmoid(x):
    return 1.0 / (1.0 + jnp.exp(-x))


def _silu(x):
    return x * _sigmoid(x)


U32 = jnp.uint32
HI_HALF_MASK = 0xFFFF0000


def _pack_halves(x):
    n = x.shape[-1] // 2
    bits = lax.bitcast_convert_type(x.astype(BF16).astype(F32), U32)
    return (bits[:, :n] >> 16) | bits[:, n:]


def _unpack_halves(w):
    lo = lax.bitcast_convert_type(w << 16, F32)
    hi = lax.bitcast_convert_type(w & U32(HI_HALF_MASK), F32)
    return lo, hi


def _head_sum_matrix():
    a = np.arange(LANES) // HEAD_DIM
    return jnp.asarray((a[:, None] == a[None, :]).astype(np.float32))


def _head_sums_split(x, mbd):
    m16 = mbd.astype(BF16)
    hi = x.astype(BF16)
    lo = (x - hi.astype(F32)).astype(BF16)
    return jnp.dot(hi, m16, preferred_element_type=F32) + jnp.dot(lo, m16, preferred_element_type=F32)


def _mod_kernel(c_ref, w_ref, b_ref, o_ref):
    c = c_ref[...]
    o_ref[...] = jnp.dot(_silu(c), w_ref[...], preferred_element_type=F32, precision=HI) + b_ref[...]


def _mod_call(c_pad, w_ada, b_ada):
    rows, d = c_pad.shape
    n = w_ada.shape[1]
    tn = 2048
    return pl.pallas_call(
        _mod_kernel,
        grid=(n // tn,),
        in_specs=[pl.BlockSpec((rows, d), lambda j: (0, 0)),
                  pl.BlockSpec((d, tn), lambda j: (0, j)),
                  pl.BlockSpec((1, tn), lambda j: (0, j))],
        out_specs=pl.BlockSpec((rows, tn), lambda j: (0, j)),
        out_shape=jax.ShapeDtypeStruct((rows, n), F32),
        compiler_params=_cparams(("parallel",)),
        name="mod",
    )(c_pad, w_ada, b_ada)


def _widen_kernel(w_ref, o_ref, *, split, pad):
    w = w_ref[...]
    o_ref[...] = jnp.concatenate([w[:, :split], jnp.zeros((w.shape[0], pad), w.dtype), w[:, split:]],
                                 axis=1).astype(o_ref.dtype)


def _widen_call(w, split, pad, tr=256):
    rows, cols = w.shape
    return pl.pallas_call(
        functools.partial(_widen_kernel, split=split, pad=pad),
        grid=(rows // tr,),
        in_specs=[pl.BlockSpec((tr, cols), lambda i: (i, 0))],
        out_specs=pl.BlockSpec((tr, cols + pad), lambda i: (i, 0)),
        out_shape=jax.ShapeDtypeStruct((rows, cols + pad), BF16),
        compiler_params=_cparams(("parallel",)),
        name="widen",
    )(w)


def _proj_kernel(x_ref, mod_ref, nw_ref, w_ref, o_ref, h_sc, *, nq):
    @pl.when(pl.program_id(2) == 0)
    def _():
        x = x_ref[0]
        ms = jnp.mean(x * x, axis=-1, keepdims=True)
        y = x * lax.rsqrt(ms + NORM_EPS) * nw_ref[...]
        h_sc[...] = (y * (1.0 + mod_ref[0, 1:2, :]) + mod_ref[0, 0:1, :]).astype(BF16)

    acc = jnp.dot(h_sc[...], w_ref[...], preferred_element_type=F32)
    for q in range(nq):
        o_ref[0, q] = acc[:, q * LANES:(q + 1) * LANES].astype(o_ref.dtype)


def _proj_call(x, mod3, norm_w, w_in_r, tm=1024, tn=1664):
    b, s, d = x.shape
    n = w_in_r.shape[1]
    nq = tn // LANES
    return pl.pallas_call(
        functools.partial(_proj_kernel, nq=nq),
        grid=(b, s // tm, n // tn),
        in_specs=[pl.BlockSpec((1, tm, d), lambda bi, i, j: (bi, i, 0)),
                  pl.BlockSpec((1, 6, d), lambda bi, i, j: (bi, 0, 0)),
                  pl.BlockSpec((1, d), lambda bi, i, j: (0, 0)),
                  pl.BlockSpec((d, tn), lambda bi, i, j: (0, j))],
        out_specs=pl.BlockSpec((1, nq, tm, LANES), lambda bi, i, j: (bi, j, i, 0)),
        out_shape=jax.ShapeDtypeStruct((b, n // LANES, s, LANES), BF16),
        scratch_shapes=[pltpu.VMEM((tm, d), BF16)],
        compiler_params=_cparams(("parallel", "parallel", "arbitrary")),
        name="proj",
    )(x, mod3, norm_w, w_in_r)


RB_R, RB_V, RB_KK, RB_G, RB_KF, RB_KB, RB_AF, RB_AB, RB_BONUS = range(9)
VEC_W0F, VEC_W0B, VEC_A0F, VEC_A0B, VEC_KK, VEC_KA, VEC_RK = range(7)
N_RKV_BLOCKS = 24
N_SHIFT_BLOCKS = 28


def _softplus(u):
    return jnp.maximum(u, 0.0) + jnp.log(1.0 + jnp.exp(-jnp.abs(u)))


def _prep_kernel(p_ref, pp_ref, pn_ref, sw_ref, wupf_ref, wupb_ref, aupf_ref, aupb_ref, gup_ref,
                 vec_ref, mbd_ref, rb_ref, lw_ref, *, tb, n_t, nhp):
    i = pl.program_id(1)
    row = lax.broadcasted_iota(I32, (tb, LANES), 0)
    has_prev = i > 0
    has_next = i < n_t - 1
    halo = pp_ref.shape[2]

    def shifted(q):
        cur = p_ref[0, q].astype(F32)
        prev_row = jnp.where(has_prev, pp_ref[0, q, halo - 1:halo, :].astype(F32), 0.0)
        next_row = jnp.where(has_next, pn_ref[0, q, 0:1, :].astype(F32), 0.0)
        up = jnp.where(row == 0, prev_row, pltpu.roll(cur, 1, 0))
        dn = jnp.where(row == tb - 1, next_row, pltpu.roll(cur, tb - 1, 0))
        return sw_ref[0, q:q + 1, :] * up + sw_ref[1, q:q + 1, :] * cur + sw_ref[2, q:q + 1, :] * dn

    mbd = mbd_ref[...]
    t_wd = jnp.tanh(shifted(N_RKV_BLOCKS)).astype(BF16)
    z_ad = shifted(N_RKV_BLOCKS + 1).astype(BF16)
    s_gd = jnp.concatenate([_sigmoid(shifted(N_RKV_BLOCKS + 2)),
                            _sigmoid(shifted(N_RKV_BLOCKS + 3))], axis=1).astype(BF16)

    def vec(v, hp):
        return vec_ref[v, hp:hp + 1, :]

    for hp in range(nhp):
        sl = slice(hp * LANES, (hp + 1) * LANES)
        r = shifted(hp)
        k = shifted(nhp + hp)
        v = shifted(2 * nhp + hp)
        g = jnp.dot(s_gd, gup_ref[:, sl], preferred_element_type=F32)
        kk0 = k * vec(VEC_KK, hp)
        ss = _head_sums_split(kk0 * kk0, mbd)
        kk = kk0 * lax.rsqrt(jnp.maximum(ss, L2_EPS))
        kdirs = []
        for wup_ref, aup_ref, v_w0, v_a0, q_k, q_a, lw_slot in (
                (wupf_ref, aupf_ref, VEC_W0F, VEC_A0F, RB_KF, RB_AF, 0),
                (wupb_ref, aupb_ref, VEC_W0B, VEC_A0B, RB_KB, RB_AB, 1)):
            wl = vec(v_w0, hp) + jnp.dot(t_wd, wup_ref[:, sl], preferred_element_type=F32)
            w_log = -_softplus(-wl) - 0.5
            lw_ref[0, lw_slot * nhp + hp] = -jnp.exp(w_log)
            ag = _sigmoid(vec(v_a0, hp) + jnp.dot(z_ad, aup_ref[:, sl], preferred_element_type=F32))
            kd = k * (1.0 + (ag - 1.0) * vec(VEC_KA, hp))
            rb_ref[0, q_k * nhp + hp] = kd.astype(BF16)
            rb_ref[0, q_a * nhp + hp] = ag.astype(BF16)
            kdirs.append(kd)
        bonus = _head_sums_split(r * kdirs[0] * vec(VEC_RK, hp), mbd) * v
        rb_ref[0, RB_R * nhp + hp] = r.astype(BF16)
        rb_ref[0, RB_V * nhp + hp] = v.astype(BF16)
        rb_ref[0, RB_KK * nhp + hp] = kk.astype(BF16)
        rb_ref[0, RB_G * nhp + hp] = g.astype(BF16)
        rb_ref[0, RB_BONUS * nhp + hp] = bonus.astype(BF16)


def _prep_call(p, sw, wupf, wupb, aupf, aupb, gup, vecs, mbd, nhp, tb=256):
    b, _, s, _ = p.shape
    n_t = s // tb
    halo = 16
    hb = tb // halo
    full = lambda a: pl.BlockSpec(a.shape, lambda bi, i: (0,) * a.ndim)
    return pl.pallas_call(
        functools.partial(_prep_kernel, tb=tb, n_t=n_t, nhp=nhp),
        grid=(b, n_t),
        in_specs=[pl.BlockSpec((1, N_SHIFT_BLOCKS, tb, LANES), lambda bi, i: (bi, 0, i, 0)),
                  pl.BlockSpec((1, N_SHIFT_BLOCKS, halo, LANES),
                               lambda bi, i: (bi, 0, jnp.maximum(i * hb - 1, 0), 0)),
                  pl.BlockSpec((1, N_SHIFT_BLOCKS, halo, LANES),
                               lambda bi, i: (bi, 0, jnp.minimum((i + 1) * hb, s // halo - 1), 0)),
                  full(sw), full(wupf), full(wupb), full(aupf), full(aupb), full(gup), full(vecs), full(mbd)],
        out_specs=[pl.BlockSpec((1, 9 * nhp, tb, LANES), lambda bi, i: (bi, 0, i, 0)),
                   pl.BlockSpec((1, 2 * nhp, tb, LANES), lambda bi, i: (bi, 0, i, 0))],
        out_shape=[jax.ShapeDtypeStruct((b, 9 * nhp, s, LANES), BF16),
                   jax.ShapeDtypeStruct((b, 2 * nhp, s, LANES), F32)],
        compiler_params=_cparams(("parallel", "parallel")),
        name="prep",
    )(p, p, p, sw, wupf, wupb, aupf, aupb, gup, vecs, mbd)


def _wkv_kernel(r_ref, v_ref, kk_ref, kf_ref, af_ref, lf_ref, rr_ref, vr_ref, kkr_ref, kb_ref, ab_ref, lb_ref,
                yf_ref, yb_ref, s_sc, *, L, nhp):
    @pl.when(pl.program_id(1) == 0)
    def _():
        s_sc[...] = jnp.zeros(s_sc.shape, F32)

    assert HEADS_PER_BLOCK == 2 and L == HEAD_DIM
    n2 = HEADS_PER_BLOCK * L
    ri = lax.broadcasted_iota(I32, (n2, LANES), 0)
    ci = lax.broadcasted_iota(I32, (n2, LANES), 1)
    head_match = (ri // L) == (ci // HEAD_DIM)
    st = lax.broadcasted_iota(I32, (L, LANES), 0)
    ss = lax.broadcasted_iota(I32, (L, LANES), 1) % L
    first_head = lax.broadcasted_iota(I32, (L, LANES), 1) < HEAD_DIM
    eye = (st == ss).astype(F32)
    strict = {False: st > ss, True: st < ss}
    incl = {False: st >= ss, True: st < ss}

    def stack(x):
        return jnp.where(head_match, jnp.concatenate([x] * HEADS_PER_BLOCK, axis=0), 0.0)

    def bdiag(x):
        return jnp.concatenate([jnp.where(first_head, x, 0.0), jnp.where(first_head, 0.0, x)], axis=0)

    chains = [(False, hp, r_ref, v_ref, kk_ref, kf_ref, af_ref, lf_ref) for hp in range(nhp)]
    chains += [(True, hp, rr_ref, vr_ref, kkr_ref, kb_ref, ab_ref, lb_ref) for hp in range(nhp)]
    n = len(chains)
    lws = [ch[7][0, ch[1]] for ch in chains]
    t_row = lax.broadcasted_iota(I32, (L, LANES), 0)

    def cumsum_time(x, rev):
        sh = 1
        while sh < L:
            if rev:
                x = x + jnp.where(t_row < L - sh, pltpu.roll(x, L - sh, 0), 0.0)
            else:
                x = x + jnp.where(t_row >= sh, pltpu.roll(x, sh, 0), 0.0)
            sh *= 2
        return x

    c_ins = [cumsum_time(lw, ch[0]) for ch, lw in zip(chains, lws)]
    lhs_l, rhs_l, v_l, vbd_l, bk_l, pl_l = [], [], [], [], [], []
    for i, (rev, hp, rr, vr, kkr, kdr, agr, _) in enumerate(chains):
        r = rr[0, hp].astype(F32)
        kk = kkr[0, hp].astype(F32)
        kd = kdr[0, hp].astype(F32)
        ag = agr[0, hp].astype(F32)
        v = vr[0, hp].astype(F32)
        c_in = c_ins[i]
        e_ex = jnp.exp(c_in - lws[i])
        e_inv = jnp.exp(-c_in)
        if rev:
            e_r = e_ex
            p_last = jnp.exp(c_in[0:1])
        else:
            e_r = jnp.exp(c_in)
            p_last = jnp.exp(c_in[L - 1:L])
        b_t = kk * ag * e_inv
        k_t = kd * e_inv
        lhs_l.append(jnp.concatenate([-kk * e_ex, r * e_r], axis=0).astype(BF16))
        rhs_l.append(jnp.concatenate([stack(b_t), stack(k_t)], axis=0).astype(BF16))
        bk_l.append(jnp.concatenate([b_t * p_last, k_t * p_last], axis=0).astype(BF16))
        v_l.append(v.astype(BF16))
        vbd_l.append(bdiag(v).astype(BF16))
        pl_l.append(p_last)
    g_l = [lax.dot_general(lhs_l[i], rhs_l[i], NT_DIMS, preferred_element_type=F32) for i in range(n)]
    s_old = [s_sc[i] for i in range(n)]
    ls_l = [lax.dot_general(lhs_l[i], s_old[i].astype(BF16), NT_DIMS, preferred_element_type=F32)
            for i in range(n)]
    a_ab = [jnp.where(strict[ch[0]], g[0:L, 0:n2], 0.0) for ch, g in zip(chains, g_l)]
    a_ak = [jnp.where(strict[ch[0]], g[0:L, n2:2 * n2], 0.0).astype(BF16) for ch, g in zip(chains, g_l)]
    a_r = [jnp.concatenate([jnp.where(incl[ch[0]], g[L:2 * L, 0:n2], 0.0),
                            jnp.where(incl[ch[0]], g[L:2 * L, n2:2 * n2], 0.0)], axis=1).astype(BF16)
           for ch, g in zip(chains, g_l)]
    x_l = [ls_l[i][0:L] + jnp.dot(a_ak[i], vbd_l[i], preferred_element_type=F32) for i in range(n)]
    rounds = L.bit_length() - 1
    pw = [jnp.dot(a.astype(BF16), bdiag(a).astype(BF16), preferred_element_type=F32) for a in a_ab]
    t_inv = [eye + a for a in a_ab]
    for j in range(1, rounds):
        pbd = [bdiag(p).astype(BF16) for p in pw]
        if j + 1 < rounds:
            both = [jnp.dot(jnp.concatenate([pw[i], t_inv[i]], axis=0).astype(BF16), pbd[i],
                            preferred_element_type=F32) for i in range(n)]
            pw = [m[0:L] for m in both]
            t_inv = [t + m[L:2 * L] for t, m in zip(t_inv, both)]
        else:
            t_inv = [t + jnp.dot(t.astype(BF16), pbd[i], preferred_element_type=F32)
                     for i, t in enumerate(t_inv)]
    u_l = [jnp.dot(t_inv[i].astype(BF16), bdiag(x_l[i]).astype(BF16), preferred_element_type=F32)
           for i in range(n)]
    y_l = [ls_l[i][L:2 * L] + jnp.dot(a_r[i], jnp.concatenate([bdiag(u_l[i]).astype(BF16), vbd_l[i]], axis=0),
                                      preferred_element_type=F32) for i in range(n)]
    upd_l = [lax.dot_general(jnp.concatenate([u_l[i].astype(BF16), v_l[i]], axis=0), bk_l[i], TN_DIMS,
                             preferred_element_type=F32) for i in range(n)]
    for i, (rev, hp, *_) in enumerate(chains):
        (yb_ref if rev else yf_ref)[0, hp] = y_l[i]
        s_sc[i] = s_old[i] * pl_l[i] + jnp.where(head_match, upd_l[i], 0.0)


def _wkv_call(rb, lw, nhp, L=WKV_CHUNK):
    b, _, s, _ = rb.shape
    nc = s // L
    fwd = lambda q: pl.BlockSpec((1, nhp, L, LANES), lambda bi, c: (bi, q, c, 0))
    bwd = lambda q: pl.BlockSpec((1, nhp, L, LANES), lambda bi, c: (bi, q, nc - 1 - c, 0))
    y_shape = jax.ShapeDtypeStruct((b, nhp, s, LANES), F32)
    return pl.pallas_call(
        functools.partial(_wkv_kernel, L=L, nhp=nhp),
        grid=(b, nc),
        in_specs=[fwd(RB_R), fwd(RB_V), fwd(RB_KK), fwd(RB_KF), fwd(RB_AF), fwd(0),
                  bwd(RB_R), bwd(RB_V), bwd(RB_KK), bwd(RB_KB), bwd(RB_AB), bwd(1)],
        out_specs=[fwd(0), bwd(0)],
        out_shape=[y_shape, y_shape],
        scratch_shapes=[pltpu.VMEM((2 * nhp, LANES, LANES), F32)],
        compiler_params=_cparams(("parallel", "arbitrary")),
        name="wkv",
    )(rb, rb, rb, rb, rb, lw, rb, rb, rb, rb, rb, lw)


def _post_kernel(yf_ref, yb_ref, bonus_ref, g_ref, lnw_ref, lnb_ref, mbd_ref, o_ref, *, nhp):
    mbd = mbd_ref[...]
    inv_n = 1.0 / HEAD_DIM
    for hp in range(nhp):
        y = yf_ref[0, hp] + yb_ref[0, hp]
        mu = _head_sums_split(y, mbd) * inv_n
        d = y - mu
        var = _head_sums_split(d * d, mbd) * inv_n
        yn = d * lax.rsqrt(var + GN_EPS) * lnw_ref[hp:hp + 1, :] + lnb_ref[hp:hp + 1, :]
        out = (yn + bonus_ref[0, hp].astype(F32)) * g_ref[0, hp].astype(F32)
        o_ref[0, :, hp * LANES:(hp + 1) * LANES] = out.astype(o_ref.dtype)


def _post_call(yf, yb, rb, lnw, lnb, mbd, nhp, tb=256):
    b, _, s, _ = yf.shape
    yblk = pl.BlockSpec((1, nhp, tb, LANES), lambda bi, i: (bi, 0, i, 0))
    rblk = lambda q: pl.BlockSpec((1, nhp, tb, LANES), lambda bi, i: (bi, q, i, 0))
    full = lambda a: pl.BlockSpec(a.shape, lambda bi, i: (0,) * a.ndim)
    return pl.pallas_call(
        functools.partial(_post_kernel, nhp=nhp),
        grid=(b, s // tb),
        in_specs=[yblk, yblk, rblk(RB_BONUS), rblk(RB_G), full(lnw), full(lnb), full(mbd)],
        out_specs=pl.BlockSpec((1, tb, nhp * LANES), lambda bi, i: (bi, i, 0)),
        out_shape=jax.ShapeDtypeStruct((b, s, nhp * LANES), BF16),
        compiler_params=_cparams(("parallel", "parallel")),
        name="post",
    )(yf, yb, rb, rb, lnw, lnb, mbd)


def _na_kernel(q_ref, k_ref, v_ref, bias_ref, qw_ref, kw_ref, mbd_ref, o_ref, kn_sc, *, rq, rows):
    rbi = pl.program_id(2)
    mbd = mbd_ref[...]
    inv_n = 1.0 / HEAD_DIM
    span = WIN_R * GRID_W
    s_len = kn_sc.shape[0]

    @pl.when(rbi == 0)
    def _():
        def body(c, carry):
            off = pl.multiple_of(c * span, span)
            k = k_ref[0, 0, pl.ds(off, span), :].astype(F32)
            ms = _head_sums_split(k * k, mbd) * inv_n
            kn_sc[pl.ds(off, span), :] = (k * lax.rsqrt(ms + NORM_EPS) * kw_ref[...]).astype(BF16)
            return carry
        lax.fori_loop(0, s_len // span, body, 0)

    ri = lax.broadcasted_iota(I32, (HEADS_PER_BLOCK * GRID_W, LANES), 0)
    ci = lax.broadcasted_iota(I32, (HEADS_PER_BLOCK * GRID_W, LANES), 1)
    head_match = (ri // GRID_W) == (ci // HEAD_DIM)
    lane = lax.broadcasted_iota(I32, (GRID_W, LANES), 1)
    scale = HEAD_DIM ** -0.5
    q_all = q_ref[0, 0].astype(F32)
    ms = _head_sums_split(q_all * q_all, mbd) * inv_n
    qn_all = q_all * lax.rsqrt(ms + NORM_EPS) * qw_ref[...] * scale
    starts, offs = [], []
    for qi in range(rq):
        i = rbi * rq + qi
        rs = jnp.clip(i - WIN_R // 2, 0, rows - WIN_R)
        offs.append(rs - i + WIN_R - 1)
        starts.append(pl.multiple_of(rs * GRID_W, GRID_W))
    qs_l = [jnp.where(head_match, jnp.concatenate([qn_all[qi * GRID_W:(qi + 1) * GRID_W]] * HEADS_PER_BLOCK,
                                                  axis=0), 0.0).astype(BF16) for qi in range(rq)]
    def bias(off):
        return jnp.concatenate([bias_ref[0, off + 2 * i] for i in range(WIN_R // 2)], axis=1)

    s_l = [lax.dot_general(qs_l[qi], kn_sc[pl.ds(starts[qi], span), :], NT_DIMS, preferred_element_type=F32)
           + bias(offs[qi]) for qi in range(rq)]
    p_l, l_l = [], []
    for s in s_l:
        p = jnp.exp(s - jnp.max(s, axis=-1, keepdims=True))
        p_l.append(p.astype(BF16))
        l_l.append(jnp.sum(p, axis=-1, keepdims=True))
    o_l = [jnp.dot(p_l[qi], v_ref[0, 0, pl.ds(starts[qi], span), :], preferred_element_type=F32) / l_l[qi]
           for qi in range(rq)]
    for qi, o in enumerate(o_l):
        out = o[0:GRID_W]
        for h in range(1, HEADS_PER_BLOCK):
            out = jnp.where(lane // HEAD_DIM == h, o[h * GRID_W:(h + 1) * GRID_W], out)
        o_ref[0, qi * GRID_W:(qi + 1) * GRID_W, :] = out.astype(o_ref.dtype)


def _na_call(p, bias, qw, kw, mbd, q_blk0, nhp, rq=16):
    b, _, s, _ = p.shape
    rows = s // GRID_W
    rq = min(rq, rows)
    full = lambda a: pl.BlockSpec(a.shape, lambda bi, hp, r: (0,) * a.ndim)
    return pl.pallas_call(
        functools.partial(_na_kernel, rq=rq, rows=rows),
        grid=(b, nhp, rows // rq),
        in_specs=[pl.BlockSpec((1, 1, rq * GRID_W, LANES), lambda bi, hp, r: (bi, q_blk0 + hp, r, 0)),
                  pl.BlockSpec((1, 1, s, LANES), lambda bi, hp, r: (bi, q_blk0 + nhp + hp, 0, 0)),
                  pl.BlockSpec((1, 1, s, LANES), lambda bi, hp, r: (bi, q_blk0 + 2 * nhp + hp, 0, 0)),
                  pl.BlockSpec((1,) + bias.shape[1:], lambda bi, hp, r: (hp, 0, 0, 0)),
                  full(qw), full(kw), full(mbd)],
        out_specs=pl.BlockSpec((1, rq * GRID_W, LANES), lambda bi, hp, r: (bi, r, hp)),
        out_shape=jax.ShapeDtypeStruct((b, s, nhp * LANES), BF16),
        scratch_shapes=[pltpu.VMEM((s, LANES), BF16)],
        compiler_params=_cparams(("parallel", "parallel", "arbitrary")),
        name="na",
    )(p, p, p, bias, qw, kw, mbd)


def _na_bias_table(rel_bias):
    h = rel_bias.shape[0]
    qc = np.arange(GRID_W)
    kc = np.arange(GRID_W)
    win0 = np.clip(qc - WIN_C // 2, 0, GRID_W - WIN_C)
    valid = (kc[None, :] >= win0[:, None]) & (kc[None, :] < win0[:, None] + WIN_C)
    coff = np.clip(kc[None, :] - qc[:, None] + WIN_C - 1, 0, 2 * WIN_C - 2)
    n_row, n_col = 2 * WIN_R - 1, 2 * WIN_C - 1
    n_ro = n_row - 1
    sel = np.zeros((n_ro, 2, n_row), np.float32)
    for ro in range(n_ro):
        for part in range(2):
            sel[ro, part, ro + part] = 1.0
    pick = np.zeros((2, n_col, GRID_W, 2 * GRID_W), np.float32)
    for part in range(2):
        pick[part, coff, qc[:, None], part * GRID_W + kc[None, :]] = 1.0
    a = jnp.einsum('hrc,apr->hapc', rel_bias.astype(F32), jnp.asarray(sel), precision=HI)
    a = a.reshape(h // HEADS_PER_BLOCK, HEADS_PER_BLOCK, n_ro, 2 * n_col).transpose(0, 2, 1, 3)
    t = jnp.einsum('ahbx,xql->ahbql', a, jnp.asarray(pick.reshape(2 * n_col, GRID_W, 2 * GRID_W)),
                   precision=HI)
    t = jnp.where(np.tile(valid, (1, 2)), t, NEG_BIG)
    return t.reshape(h // HEADS_PER_BLOCK, n_ro, HEADS_PER_BLOCK * GRID_W, 2 * GRID_W)


def _outproj_kernel(yr_ref, yn_ref, x_ref, mod_ref, nw_ref, w1_ref, w2_ref, wr_ref, x1_ref, h2_ref, h2row_ref,
                    lg_ref):
    tm = x_ref.shape[1]
    sub = LANES
    tiles = [slice(i * sub, (i + 1) * sub) for i in range(tm // sub)]
    accs = [jnp.dot(yr_ref[0, r], w1_ref[...], preferred_element_type=F32)
            + jnp.dot(yn_ref[0, r], w2_ref[...], preferred_element_type=F32) for r in tiles]
    h2s = []
    for r, acc in zip(tiles, accs):
        x1 = x_ref[0, r] + mod_ref[0, 2:3, :] * acc
        x1_ref[0, r] = x1
        ms = jnp.mean(x1 * x1, axis=-1, keepdims=True)
        y = x1 * lax.rsqrt(ms + NORM_EPS) * nw_ref[...]
        h2 = y * (1.0 + mod_ref[0, 4:5, :]) + mod_ref[0, 3:4, :]
        packed = _pack_halves(h2)
        h2row_ref[0, r] = packed
        for j in range(h2_ref.shape[2]):
            h2_ref[0, r, j, :] = packed[:, j * LANES:(j + 1) * LANES]
        h2s.append(h2)
    wr = wr_ref[...]
    wr_hi = wr.astype(BF16)
    wr_lo = (wr - wr_hi.astype(F32)).astype(BF16)
    for r, h2 in zip(tiles, h2s):
        h_hi = h2.astype(BF16)
        h_lo = (h2 - h_hi.astype(F32)).astype(BF16)
        lg_ref[:, r] = (lax.dot_general(wr_hi, h_hi, NT_DIMS, preferred_element_type=F32)
                        + lax.dot_general(wr_hi, h_lo, NT_DIMS, preferred_element_type=F32)
                        + lax.dot_general(wr_lo, h_hi, NT_DIMS, preferred_element_type=F32))


def _outproj_call(yr, yn, x, mod3, norm_w, w_o, wr_t, tm=512):
    b, s, d = x.shape
    dh = yr.shape[-1]
    assert w_o.shape == (2 * dh, d)
    ne = wr_t.shape[0]
    nt = s // tm
    full = lambda a: pl.BlockSpec(a.shape, lambda bi, i: (0,) * a.ndim)
    return pl.pallas_call(
        _outproj_kernel,
        grid=(b, nt),
        in_specs=[pl.BlockSpec((1, tm, dh), lambda bi, i: (bi, i, 0)),
                  pl.BlockSpec((1, tm, dh), lambda bi, i: (bi, i, 0)),
                  pl.BlockSpec((1, tm, d), lambda bi, i: (bi, i, 0)),
                  pl.BlockSpec((1, 6, d), lambda bi, i: (bi, 0, 0)),
                  full(norm_w),
                  pl.BlockSpec((dh, d), lambda bi, i: (0, 0)),
                  pl.BlockSpec((dh, d), lambda bi, i: (1, 0)),
                  full(wr_t)],
        out_specs=[pl.BlockSpec((1, tm, d), lambda bi, i: (bi, i, 0)),
                   pl.BlockSpec((1, tm, d // 2 // LANES, LANES), lambda bi, i: (bi, i, 0, 0)),
                   pl.BlockSpec((1, tm, d // 2), lambda bi, i: (bi, i, 0)),
                   pl.BlockSpec((ne, tm), lambda bi, i: (0, bi * nt + i))],
        out_shape=[jax.ShapeDtypeStruct((b, s, d), F32),
                   jax.ShapeDtypeStruct((b, s, d // 2 // LANES, LANES), U32),
                   jax.ShapeDtypeStruct((b, s, d // 2), U32),
                   jax.ShapeDtypeStruct((ne, b * s), F32)],
        compiler_params=_cparams(("parallel", "parallel")),
        name="outproj",
    )(yr, yn, x, mod3, norm_w, w_o, w_o, wr_t)


def _first_argmax(x, idx, n):
    m = jnp.max(x, axis=0, keepdims=True)
    a = jnp.min(jnp.where(x == m, idx, n), axis=0, keepdims=True)
    return m, a


def _route_kernel(lg_ref, bias_ref, idx_ref, gw_ref, pos_ref, cnt_ref, *, tr):
    @pl.when(pl.program_id(0) == 0)
    def _():
        cnt_ref[...] = jnp.zeros(cnt_ref.shape, F32)

    gsz = N_EXPERTS // N_GROUPS
    scores = _sigmoid(lg_ref[...])
    biased = scores + bias_ref[:, 0:1]
    ig = lax.broadcasted_iota(I32, (gsz, tr), 0)
    grp_rows = []
    for g in range(N_GROUPS):
        blk = biased[g * gsz:(g + 1) * gsz]
        m1, a1 = _first_argmax(blk, ig, gsz)
        m2 = jnp.max(jnp.where(ig == a1, -jnp.inf, blk), axis=0, keepdims=True)
        grp_rows.append(m1 + m2)
    grp = jnp.concatenate(grp_rows, axis=0)
    ign = lax.broadcasted_iota(I32, (N_GROUPS, tr), 0)
    sel = jnp.zeros((N_GROUPS, tr), jnp.bool_)
    for _ in range(TOPK_GROUPS):
        _, a = _first_argmax(grp, ign, N_GROUPS)
        hit = ign == a
        sel = jnp.logical_or(sel, hit)
        grp = jnp.where(hit, -jnp.inf, grp)
    masked = jnp.concatenate(
        [jnp.where(sel[g:g + 1], biased[g * gsz:(g + 1) * gsz], -jnp.inf) for g in range(N_GROUPS)], axis=0)
    ie = lax.broadcasted_iota(I32, (N_EXPERTS, tr), 0)
    picks, pick_scores = [], []
    onehot = jnp.zeros((N_EXPERTS, tr), F32)
    for _ in range(TOP_K):
        _, a = _first_argmax(masked, ie, N_EXPERTS)
        hit = ie == a
        picks.append(a)
        pick_scores.append(jnp.sum(jnp.where(hit, scores, 0.0), axis=0, keepdims=True))
        onehot = onehot + hit.astype(F32)
        masked = jnp.where(hit, -jnp.inf, masked)
    total = pick_scores[0]
    for sc in pick_scores[1:]:
        total = total + sc
    t0 = lax.broadcasted_iota(I32, (tr, tr), 0)
    t1 = lax.broadcasted_iota(I32, (tr, tr), 1)
    before = (t0 < t1).astype(BF16)
    rank = jnp.dot(onehot.astype(BF16), before, preferred_element_type=F32) + cnt_ref[:, 0:1]
    zero_i = jnp.zeros((1, tr), I32)
    zero_f = jnp.zeros((1, tr), F32)
    for k in range(8):
        if k < TOP_K:
            idx_ref[k:k + 1, :] = picks[k]
            gw_ref[k:k + 1, :] = pick_scores[k] / total * ROUTED_SCALE
            pos = jnp.sum(jnp.where(ie == picks[k], rank, 0.0), axis=0, keepdims=True)
            pos_ref[k:k + 1, :] = pos.astype(I32)
        else:
            idx_ref[k:k + 1, :] = zero_i
            gw_ref[k:k + 1, :] = zero_f
            pos_ref[k:k + 1, :] = zero_i
    cnt_ref[...] = cnt_ref[...] + jnp.sum(onehot, axis=1, keepdims=True)


def _route_call(lg_t, bias2, tr=512):
    ne, t = lg_t.shape
    tr = min(tr, t)
    tok = pl.BlockSpec((8, tr), lambda i: (0, i))
    return pl.pallas_call(
        functools.partial(_route_kernel, tr=tr),
        grid=(t // tr,),
        in_specs=[pl.BlockSpec((ne, tr), lambda i: (0, i)),
                  pl.BlockSpec(bias2.shape, lambda i: (0, 0))],
        out_specs=[tok, tok, tok, pl.BlockSpec((ne, LANES), lambda i: (0, 0))],
        out_shape=[jax.ShapeDtypeStruct((8, t), I32), jax.ShapeDtypeStruct((8, t), F32),
                   jax.ShapeDtypeStruct((8, t), I32), jax.ShapeDtypeStruct((ne, LANES), F32)],
        compiler_params=_cparams(("arbitrary",)),
        name="route",
    )(lg_t, bias2)


FILL_CHUNKS = (1, 8, 64)
ZERO_ROWS = FILL_CHUNKS[-1]


def _dispatch_kernel(slot_ref, fill_lo_ref, fill_hi_ref, h_ref, xs_ref, zrow, sem, *, td):
    def row_copy(t, slot):
        return pltpu.make_async_copy(h_ref.at[t], xs_ref.at[slot], sem)

    def start(t, carry):
        for k in range(TOP_K):
            row_copy(t, slot_ref[t * SLOT_STRIDE + k]).start(priority=k % 2)
        return carry

    lax.fori_loop(0, td, start, 0)
    for k in range(TOP_K):
        pltpu.make_async_copy(h_ref, xs_ref.at[pl.ds(0, td)], sem).wait()

    @pl.when(pl.program_id(0) == pl.num_programs(0) - 1)
    def _():
        zrow[...] = jnp.zeros(zrow.shape, zrow.dtype)

        def zero_copy(row, n):
            if n > 1:
                row = pl.multiple_of(row, FILL_CHUNKS[1])
            return pltpu.make_async_copy(zrow.at[pl.ds(0, n)], xs_ref.at[pl.ds(row, n)], sem)

        def for_each_chunk(r, fn):
            lo, hi = fill_lo_ref[r], fill_hi_ref[r]
            for ci, n in enumerate(FILL_CHUNKS):
                if ci + 1 < len(FILL_CHUNKS):
                    count = jnp.minimum(((-lo) % FILL_CHUNKS[ci + 1]) // n, (hi - lo) // n)
                else:
                    count = (hi - lo) // n
                fn(lo, n, count)
                lo = lo + count * n

        def start_range(r, carry):
            def go(lo, n, count):
                lax.fori_loop(0, count, lambda i, c: (zero_copy(lo + i * n, n).start(), c)[1], 0)
            for_each_chunk(r, go)
            return carry

        def wait_range(r, carry):
            def go(lo, n, count):
                lax.fori_loop(0, count, lambda i, c: (zero_copy(0, n).wait(), c)[1], 0)
            for_each_chunk(r, go)
            return carry

        n_ranges = fill_lo_ref.shape[0]
        lax.fori_loop(0, n_ranges, start_range, 0)
        lax.fori_loop(0, n_ranges, wait_range, 0)


def _dispatch_call(slots_flat, fill_lo, fill_hi, h2, n_rows, td=256):
    t, nj, _ = h2.shape
    smem = pl.BlockSpec(memory_space=pltpu.SMEM)
    return pl.pallas_call(
        functools.partial(_dispatch_kernel, td=td),
        grid=(t // td,),
        in_specs=[pl.BlockSpec((td * SLOT_STRIDE,), lambda i: (i,), memory_space=pltpu.SMEM),
                  smem, smem,
                  pl.BlockSpec((td, nj, LANES), lambda i: (i, 0, 0))],
        out_specs=pl.BlockSpec(memory_space=pl.ANY),
        out_shape=jax.ShapeDtypeStruct((n_rows, nj, LANES), h2.dtype),
        scratch_shapes=[pltpu.VMEM((ZERO_ROWS, nj, LANES), h2.dtype), pltpu.SemaphoreType.DMA(())],
        compiler_params=_cparams(("arbitrary",)),
        name="dispatch",
    )(slots_flat, fill_lo, fill_hi, h2)


def _experts_kernel(be_ref, first_ref, nxt_ref, par_ref, rows_ref, nv_ref, xs_hbm, wg_hbm, wu_hbm, wd_hbm, ys_ref,
                    wg_buf, wu_buf, wd_buf, wg_sc, wu_sc, wd_sc, xpack, sem, isem):
    nj = xs_hbm.shape[1]
    b = pl.program_id(0)
    m = xpack.shape[1]
    half = wg_sc.shape[0] // 2
    oslot = lax.rem(b, 2)

    def in_copies(slot, row0):
        return [pltpu.make_async_copy(xs_hbm.at[pl.ds(row0, m), j],
                                      xpack.at[slot, :, pl.ds(j * LANES, LANES)], isem.at[slot]) for j in range(nj)]

    @pl.when(b == 0)
    def _():
        for c in in_copies(0, 0):
            c.start()

    @pl.when(b + 1 < nv_ref[0])
    def _():
        for c in in_copies(1 - oslot, pl.multiple_of((b + 1) * m, m)):
            c.start()

    def weight_copies(e, slot):
        return [pltpu.make_async_copy(src.at[e], dst.at[slot], sem.at[slot, i])
                for i, (src, dst) in enumerate(((wg_hbm, wg_buf), (wu_hbm, wu_buf), (wd_hbm, wd_buf)))]

    def compute(nrows):
        x_lo, x_hi = _unpack_halves(xpack[oslot, 0:nrows])
        x_lo, x_hi = x_lo.astype(BF16), x_hi.astype(BF16)
        g = (jnp.dot(x_lo, wg_sc[0:half], preferred_element_type=F32)
             + jnp.dot(x_hi, wg_sc[half:], preferred_element_type=F32))
        u = (jnp.dot(x_lo, wu_sc[0:half], preferred_element_type=F32)
             + jnp.dot(x_hi, wu_sc[half:], preferred_element_type=F32))
        a = (_silu(g) * u).astype(BF16)
        ys_ref[0:nrows] = _pack_halves(jnp.dot(a, wd_sc[...], preferred_element_type=F32))

    @pl.when(b < nv_ref[0])
    def _():
        for c in in_copies(oslot, 0):
            c.wait()

        @pl.when(b == 0)
        def _():
            for c in weight_copies(be_ref[0], par_ref[0]):
                c.start()

        @pl.when(first_ref[b] == 1)
        def _():
            slot = par_ref[b]
            for c in weight_copies(be_ref[b], slot):
                c.wait()

            @pl.when(nxt_ref[b] >= 0)
            def _():
                for c in weight_copies(nxt_ref[b], 1 - slot):
                    c.start()

            wg_sc[...] = wg_buf[slot].astype(BF16)
            wu_sc[...] = wu_buf[slot].astype(BF16)
            wd_sc[...] = wd_buf[slot].astype(BF16)

        @pl.when(rows_ref[b] > m // 2)
        def _():
            compute(m)

        @pl.when(rows_ref[b] <= m // 2)
        def _():
            compute(m // 2)
            ys_ref[m // 2:] = jnp.zeros((m - m // 2, ys_ref.shape[1]), ys_ref.dtype)

    @pl.when(b >= nv_ref[0])
    def _():
        ys_ref[...] = jnp.zeros(ys_ref.shape, ys_ref.dtype)


def _experts_call(block_e, first, nxt, par, rows_valid, n_valid, xs, wg, wu, wd):
    total, nj, _ = xs.shape
    d = wg.shape[1]
    m = DISPATCH_BLOCK
    ff = wg.shape[-1]

    hbm = pl.BlockSpec(memory_space=pl.ANY)
    grid_spec = pltpu.PrefetchScalarGridSpec(
        num_scalar_prefetch=6,
        grid=(total // m,),
        in_specs=[hbm, hbm, hbm, hbm],
        out_specs=pl.BlockSpec((m, nj * LANES), lambda b, *s: (b, 0)),
        scratch_shapes=[pltpu.VMEM((2, d, ff), F32), pltpu.VMEM((2, d, ff), F32), pltpu.VMEM((2, ff, d), F32),
                        pltpu.VMEM((d, ff), BF16), pltpu.VMEM((d, ff), BF16), pltpu.VMEM((ff, d), BF16),
                        pltpu.VMEM((2, m, nj * LANES), U32),
                        pltpu.SemaphoreType.DMA((2, 3)), pltpu.SemaphoreType.DMA((2,))],
    )
    return pl.pallas_call(
        _experts_kernel,
        grid_spec=grid_spec,
        out_shape=jax.ShapeDtypeStruct((total, nj * LANES), U32),
        compiler_params=_cparams(("arbitrary",)),
        name="experts",
    )(block_e, first, nxt, par, rows_valid, n_valid, xs, wg, wu, wd)


COMBINE_INLINE_ROWS = 64


def _combine_kernel(slot_ref, x1_ref, h2_ref, gw_ref, mod_ref, wg_ref, wu_ref, wd_ref, ys_ref, o_ref,
                    buf, sem, *, tc):
    def row_copy(t, k, slot):
        return pltpu.make_async_copy(ys_ref.at[pl.ds(slot, 1)], buf.at[k, pl.ds(t, 1)], sem)

    def start(t, carry):
        for k in range(TOP_K):
            row_copy(t, k, slot_ref[t * SLOT_STRIDE + k]).start(priority=k % 2)
        return carry

    inline = min(COMBINE_INLINE_ROWS, tc)
    lax.fori_loop(0, tc - inline, start, 0)
    for t in range(tc - inline, tc):
        start(t, 0)
    half = wg_ref.shape[0] // 2
    h_lo, h_hi = _unpack_halves(h2_ref[0])
    h_lo, h_hi = h_lo.astype(BF16), h_hi.astype(BF16)
    g = (jnp.dot(h_lo, wg_ref[0:half], preferred_element_type=F32)
         + jnp.dot(h_hi, wg_ref[half:], preferred_element_type=F32))
    u = (jnp.dot(h_lo, wu_ref[0:half], preferred_element_type=F32)
         + jnp.dot(h_hi, wu_ref[half:], preferred_element_type=F32))
    acc = jnp.dot((_silu(g) * u).astype(BF16), wd_ref[...], preferred_element_type=F32)
    for k in range(TOP_K):
        pltpu.make_async_copy(ys_ref.at[pl.ds(0, tc)], buf.at[k], sem).wait()
    gw = gw_ref[...]
    acc_lo, acc_hi = acc[:, :half], acc[:, half:]
    for k in range(TOP_K):
        y_lo, y_hi = _unpack_halves(buf[k])
        acc_lo = acc_lo + gw[:, k:k + 1] * y_lo
        acc_hi = acc_hi + gw[:, k:k + 1] * y_hi
    gate = mod_ref[0, 5:6, :]
    o_ref[0, :, 0:half] = x1_ref[0, :, 0:half] + gate[:, :half] * acc_lo
    o_ref[0, :, half:] = x1_ref[0, :, half:] + gate[:, half:] * acc_hi


def _combine_call(slots_flat, x1, h2, gw_tok, mod3, wg, wu, wd, ys, tc=256):
    b, s, d = x1.shape
    nt = s // tc
    full = lambda a: pl.BlockSpec(a.shape, lambda bi, i: (0,) * a.ndim)
    return pl.pallas_call(
        functools.partial(_combine_kernel, tc=tc),
        grid=(b, nt),
        in_specs=[pl.BlockSpec((tc * SLOT_STRIDE,), lambda bi, i: (bi * nt + i,), memory_space=pltpu.SMEM),
                  pl.BlockSpec((1, tc, d), lambda bi, i: (bi, i, 0)),
                  pl.BlockSpec((1, tc, d // 2), lambda bi, i: (bi, i, 0)),
                  pl.BlockSpec((tc, 8), lambda bi, i: (bi * nt + i, 0)),
                  pl.BlockSpec((1, 6, d), lambda bi, i: (bi, 0, 0)),
                  full(wg), full(wu), full(wd),
                  pl.BlockSpec(memory_space=pl.ANY)],
        out_specs=pl.BlockSpec((1, tc, d), lambda bi, i: (bi, i, 0)),
        out_shape=jax.ShapeDtypeStruct((b, s, d), F32),
        scratch_shapes=[pltpu.VMEM((TOP_K, tc, d // 2), U32), pltpu.SemaphoreType.DMA(())],
        compiler_params=_cparams(("arbitrary", "arbitrary")),
        name="combine",
    )(slots_flat, x1, h2, gw_tok, mod3, wg, wu, wd, ys)


def _pad_rows(w, rows, at=0):
    out = jnp.zeros((rows, w.shape[1]), w.dtype)
    return out.at[at:at + w.shape[0]].set(w)


def _mixing_stage(p, rwkv_shift, w0_f, w_up_f, w0_b, w_up_b, a0_f, a_up_f, a0_b, a_up_b, g_up,
                  k_k, k_a, r_k, ln_x_w, ln_x_b, q_norm_w, k_norm_w, rel_bias):
    d_rwkv = w0_f.shape[0]
    nhp = d_rwkv // LANES
    mbd = _head_sum_matrix()
    n_shift = rwkv_shift.shape[1]
    sw = jnp.pad(rwkv_shift, ((0, 0), (0, N_SHIFT_BLOCKS * LANES - n_shift))).reshape(3, N_SHIFT_BLOCKS, LANES)
    wupf = _pad_rows(w_up_f, LANES, 0).astype(BF16)
    wupb = _pad_rows(w_up_b, LANES, DECAY_LORA).astype(BF16)
    aupf = _pad_rows(a_up_f, LANES, 0).astype(BF16)
    aupb = _pad_rows(a_up_b, LANES, AAA_LORA).astype(BF16)
    gup = _pad_rows(g_up, 2 * LANES, 0).astype(BF16)
    vecs = jnp.stack([w0_f, w0_b, a0_f, a0_b, k_k, k_a, r_k.reshape(-1)]).reshape(7, nhp, LANES)
    rb, lw = _prep_call(p, sw, wupf, wupb, aupf, aupb, gup, vecs, mbd, nhp)
    y_f, y_b = _wkv_call(rb, lw, nhp)
    y_rwkv = _post_call(y_f, y_b, rb, ln_x_w.reshape(nhp, LANES), ln_x_b.reshape(nhp, LANES), mbd, nhp)
    bias = _na_bias_table(rel_bias)
    qw = jnp.tile(q_norm_w, HEADS_PER_BLOCK).reshape(1, LANES)
    kw = jnp.tile(k_norm_w, HEADS_PER_BLOCK).reshape(1, LANES)
    y_na = _na_call(p, bias, qw, kw, mbd, N_SHIFT_BLOCKS, nhp)
    return y_rwkv, y_na


def _moe_stage(x1, h2_tiles, h2_rows, lg_t, mod3, router_bias, w_gate_e, w_up_e, w_down_e, w_gate_s, w_up_s,
               w_down_s):
    b, s, d = x1.shape
    t = b * s
    m = DISPATCH_BLOCK
    bias2 = jnp.broadcast_to(router_bias.astype(F32)[:, None], (N_EXPERTS, LANES))
    idx_t, gw_t, pos_t, cnt = _route_call(lg_t, bias2)
    counts = cnt[:, 0].astype(I32)
    padded = (counts + m - 1) // m * m
    pad_end = jnp.cumsum(padded)
    pad_start = pad_end - padded
    n_blocks = -(-(t * TOP_K + N_EXPERTS * m) // m)
    n_valid = (pad_end[-1] // m).astype(I32).reshape(1)
    block_row0 = jnp.arange(n_blocks, dtype=I32) * m
    block_e = jnp.minimum(jnp.sum((pad_end[None, :] <= block_row0[:, None]).astype(I32), axis=1),
                          N_EXPERTS - 1).astype(I32)
    assert m % ZERO_ROWS == 0
    expert_ids = jnp.arange(N_EXPERTS, dtype=I32)
    start_of = jnp.sum(jnp.where(idx_t[:, :, None] == expert_ids, pad_start, 0), axis=-1)
    slots = (start_of + pos_t).astype(I32)
    slots_flat = slots.T.reshape(-1)
    gw_tok = gw_t.T
    fill_lo = jnp.concatenate([pad_start + counts, pad_end[-1:]]).astype(I32)
    fill_hi = jnp.concatenate([pad_end, jnp.full((1,), n_blocks * m, I32)]).astype(I32)
    xs = _dispatch_call(slots_flat, fill_lo, fill_hi, h2_tiles.reshape(t, *h2_tiles.shape[2:]), n_blocks * m)
    row_end = (pad_start + counts).astype(I32)
    nonempty = counts > 0
    first = jnp.logical_and(block_row0 == pad_start[block_e], block_row0 < pad_end[-1]).astype(I32)
    ordinal = jnp.cumsum(nonempty.astype(I32)) - 1
    par = (ordinal[block_e] & 1).astype(I32)
    cand = jnp.where(nonempty, expert_ids, N_EXPERTS)
    later = jnp.concatenate([lax.cummin(cand[::-1])[::-1][1:], jnp.full((1,), N_EXPERTS, I32)])
    nxt = jnp.where(later < N_EXPERTS, later, -1)[block_e].astype(I32)
    rows_valid = jnp.clip(row_end[block_e] - block_row0, 0, m).astype(I32)
    ys = _experts_call(block_e, first, nxt, par, rows_valid, n_valid, xs, w_gate_e, w_up_e, w_down_e)
    return _combine_call(slots_flat, x1, h2_rows, gw_tok, mod3, w_gate_s.astype(BF16), w_up_s.astype(BF16),
                         w_down_s.astype(BF16), ys)


def kernel(x, c, w_ada, b_ada, norm1_w, w_in, rwkv_shift, w0_f, w_up_f, w0_b, w_up_b, a0_f, a_up_f, a0_b,
           a_up_b, g_up, k_k, k_a, r_k, ln_x_w, ln_x_b, q_norm_w, k_norm_w, rel_bias, w_out, norm2_w,
           w_router, router_bias, w_gate_e, w_up_e, w_down_e, w_gate_s, w_up_s, w_down_s):
    bn, sn, d = x.shape
    depth = w_ada.shape[0]
    for l in range(depth):
        c_pad = jnp.pad(c, ((0, 8 - bn % 8 if bn % 8 else 0), (0, 0)))
        mod = _mod_call(c_pad, w_ada[l], b_ada[l].reshape(1, -1))[:bn]
        mod3 = mod.reshape(bn, 6, d)
        rwkv_in = rwkv_shift.shape[-1]
        pad_cols = N_SHIFT_BLOCKS * LANES - rwkv_in
        w_in_r = _widen_call(w_in[l], rwkv_in, pad_cols)
        p = _proj_call(x, mod3, norm1_w[l].reshape(1, d), w_in_r)
        y_rwkv, y_na = _mixing_stage(p, rwkv_shift[l], w0_f[l], w_up_f[l], w0_b[l], w_up_b[l], a0_f[l],
                                     a_up_f[l], a0_b[l], a_up_b[l], g_up[l], k_k[l], k_a[l], r_k[l],
                                     ln_x_w[l], ln_x_b[l], q_norm_w[l], k_norm_w[l], rel_bias[l])
        x1, h2_tiles, h2_rows, lg_t = _outproj_call(y_rwkv, y_na, x, mod3, norm2_w[l].reshape(1, d),
                                                    w_out[l].astype(BF16), w_router[l].T)
        x = _moe_stage(x1, h2_tiles, h2_rows, lg_t, mod3, router_bias[l], w_gate_e[l], w_up_e[l], w_down_e[l],
                       w_gate_s[l], w_up_s[l], w_down_s[l])
    return x
```

```python
import functools

import jax
import jax.numpy as jnp
import numpy as np
from jax import lax
from jax.experimental import pallas as pl
from jax.experimental.pallas import tpu as pltpu

F32 = jnp.float32
BF16 = jnp.bfloat16
I32 = jnp.int32
HI = lax.Precision.HIGHEST

LANES = 128
HEAD_DIM = 64
HEADS_PER_BLOCK = LANES // HEAD_DIM
GRID_W = 64
WIN_R = 8
WIN_C = 16
DECAY_LORA = 64
AAA_LORA = 64
GATE_LORA = 160
N_EXPERTS = 64
TOP_K = 6
SLOT_STRIDE = 8
N_GROUPS = 8
TOPK_GROUPS = 4
ROUTED_SCALE = 2.5
DISPATCH_BLOCK = 512
NORM_EPS = 1e-6
GN_EPS = 64e-5
L2_EPS = 1e-12
NEG_BIG = -1e30
WKV_CHUNK = 64
VMEM_LIMIT = 56 * 1024 * 1024

NT_DIMS = (((1,), (1,)), ((), ()))
TN_DIMS = (((0,), (0,)), ((), ()))


def _cparams(sem):
    return pltpu.CompilerParams(dimension_semantics=sem, vmem_limit_bytes=VMEM_LIMIT)


def _sigmoid(x):
    return 1.0 / (1.0 + jnp.exp(-x))


def _silu(x):
    return x * _sigmoid(x)


U32 = jnp.uint32
HI_HALF_MASK = 0xFFFF0000


def _pack_halves(x):
    n = x.shape[-1] // 2
    bits = lax.bitcast_convert_type(x.astype(BF16).astype(F32), U32)
    return (bits[:, :n] >> 16) | bits[:, n:]


def _unpack_halves(w):
    lo = lax.bitcast_convert_type(w << 16, F32)
    hi = lax.bitcast_convert_type(w & U32(HI_HALF_MASK), F32)
    return lo, hi


def _head_sum_matrix():
    a = np.arange(LANES) // HEAD_DIM
    return jnp.asarray((a[:, None] == a[None, :]).astype(np.float32))


def _head_sums_split(x, mbd):
    m16 = mbd.astype(BF16)
    hi = x.astype(BF16)
    lo = (x - hi.astype(F32)).astype(BF16)
    return jnp.dot(hi, m16, preferred_element_type=F32) + jnp.dot(lo, m16, preferred_element_type=F32)


def _mod_kernel(c_ref, w_ref, b_ref, o_ref):
    c = c_ref[...]
    o_ref[...] = jnp.dot(_silu(c), w_ref[...], preferred_element_type=F32, precision=HI) + b_ref[...]


def _mod_call(c_pad, w_ada, b_ada):
    rows, d = c_pad.shape
    n = w_ada.shape[1]
    tn = 1024
    return pl.pallas_call(
        _mod_kernel,
        grid=(n // tn,),
        in_specs=[pl.BlockSpec((rows, d), lambda j: (0, 0)),
                  pl.BlockSpec((d, tn), lambda j: (0, j)),
                  pl.BlockSpec((1, tn), lambda j: (0, j))],
        out_specs=pl.BlockSpec((rows, tn), lambda j: (0, j)),
        out_shape=jax.ShapeDtypeStruct((rows, n), F32),
        compiler_params=_cparams(("parallel",)),
        name="mod",
    )(c_pad, w_ada, b_ada)


def _widen_kernel(w_ref, o_ref, *, split, pad):
    w = w_ref[...]
    o_ref[...] = jnp.concatenate([w[:, :split], jnp.zeros((w.shape[0], pad), w.dtype), w[:, split:]],
                                 axis=1).astype(o_ref.dtype)


def _widen_call(w_stack, layer, split, pad, tr=256):
    _, rows, cols = w_stack.shape
    return pl.pallas_call(
        functools.partial(_widen_kernel, split=split, pad=pad),
        grid=(rows // tr,),
        in_specs=[pl.BlockSpec((None, tr, cols), lambda i: (layer, i, 0))],
        out_specs=pl.BlockSpec((tr, cols + pad), lambda i: (i, 0)),
        out_shape=jax.ShapeDtypeStruct((rows, cols + pad), BF16),
        compiler_params=_cparams(("parallel",)),
        name="widen",
    )(w_stack)


def _proj_kernel(x_ref, mod_ref, nw_ref, w_ref, o_ref, h_sc, *, nq):
    @pl.when(pl.program_id(2) == 0)
    def _():
        x = x_ref[0]
        ms = jnp.mean(x * x, axis=-1, keepdims=True)
        y = x * lax.rsqrt(ms + NORM_EPS) * nw_ref[...]
        h_sc[...] = (y * (1.0 + mod_ref[0, 1:2, :]) + mod_ref[0, 0:1, :]).astype(BF16)

    acc = jnp.dot(h_sc[...], w_ref[...], preferred_element_type=F32)
    for q in range(nq):
        o_ref[0, q] = acc[:, q * LANES:(q + 1) * LANES].astype(o_ref.dtype)


def _proj_call(x, mod3, norm_w, w_in_r, tm=1024, tn=1664):
    b, s, d = x.shape
    n = w_in_r.shape[1]
    nq = tn // LANES
    return pl.pallas_call(
        functools.partial(_proj_kernel, nq=nq),
        grid=(b, s // tm, n // tn),
        in_specs=[pl.BlockSpec((1, tm, d), lambda bi, i, j: (bi, i, 0)),
                  pl.BlockSpec((1, 6, d), lambda bi, i, j: (bi, 0, 0)),
                  pl.BlockSpec((1, d), lambda bi, i, j: (0, 0)),
                  pl.BlockSpec((d, tn), lambda bi, i, j: (0, j))],
        out_specs=pl.BlockSpec((1, nq, tm, LANES), lambda bi, i, j: (bi, j, i, 0)),
        out_shape=jax.ShapeDtypeStruct((b, n // LANES, s, LANES), BF16),
        scratch_shapes=[pltpu.VMEM((tm, d), BF16)],
        compiler_params=_cparams(("parallel", "parallel", "arbitrary")),
        name="proj",
    )(x, mod3, norm_w, w_in_r)


RB_R, RB_V, RB_KK, RB_G, RB_KF, RB_KB, RB_AF, RB_AB, RB_BONUS = range(9)
VEC_W0F, VEC_W0B, VEC_A0F, VEC_A0B, VEC_KK, VEC_KA, VEC_RK = range(7)
N_RKV_BLOCKS = 24
N_SHIFT_BLOCKS = 28


def _softplus(u):
    return jnp.maximum(u, 0.0) + jnp.log(1.0 + jnp.exp(-jnp.abs(u)))


def _prep_kernel(p_ref, pp_ref, pn_ref, sw_ref, wupf_ref, wupb_ref, aupf_ref, aupb_ref, gup_ref,
                 vec_ref, mbd_ref, rb_ref, lw_ref, *, tb, n_t, nhp):
    i = pl.program_id(1)
    row = lax.broadcasted_iota(I32, (tb, LANES), 0)
    has_prev = i > 0
    has_next = i < n_t - 1
    halo = pp_ref.shape[2]

    def shifted(q):
        cur = p_ref[0, q].astype(F32)
        prev_row = jnp.where(has_prev, pp_ref[0, q, halo - 1:halo, :].astype(F32), 0.0)
        next_row = jnp.where(has_next, pn_ref[0, q, 0:1, :].astype(F32), 0.0)
        up = jnp.where(row == 0, prev_row, pltpu.roll(cur, 1, 0))
        dn = jnp.where(row == tb - 1, next_row, pltpu.roll(cur, tb - 1, 0))
        return sw_ref[0, q:q + 1, :] * up + sw_ref[1, q:q + 1, :] * cur + sw_ref[2, q:q + 1, :] * dn

    mbd = mbd_ref[...]
    t_wd = jnp.tanh(shifted(N_RKV_BLOCKS)).astype(BF16)
    z_ad = shifted(N_RKV_BLOCKS + 1).astype(BF16)
    s_gd = jnp.concatenate([_sigmoid(shifted(N_RKV_BLOCKS + 2)),
                            _sigmoid(shifted(N_RKV_BLOCKS + 3))], axis=1).astype(BF16)

    def vec(v, hp):
        return vec_ref[v, hp:hp + 1, :]

    for hp in range(nhp):
        sl = slice(hp * LANES, (hp + 1) * LANES)
        r = shifted(hp)
        k = shifted(nhp + hp)
        v = shifted(2 * nhp + hp)
        g = jnp.dot(s_gd, gup_ref[:, sl], preferred_element_type=F32)
        kk0 = k * vec(VEC_KK, hp)
        ss = _head_sums_split(kk0 * kk0, mbd)
        kk = kk0 * lax.rsqrt(jnp.maximum(ss, L2_EPS))
        kdirs = []
        for wup_ref, aup_ref, v_w0, v_a0, q_k, q_a, lw_slot in (
                (wupf_ref, aupf_ref, VEC_W0F, VEC_A0F, RB_KF, RB_AF, 0),
                (wupb_ref, aupb_ref, VEC_W0B, VEC_A0B, RB_KB, RB_AB, 1)):
            wl = vec(v_w0, hp) + jnp.dot(t_wd, wup_ref[:, sl], preferred_element_type=F32)
            w_log = -_softplus(-wl) - 0.5
            lw_ref[0, lw_slot * nhp + hp] = -jnp.exp(w_log)
            ag = _sigmoid(vec(v_a0, hp) + jnp.dot(z_ad, aup_ref[:, sl], preferred_element_type=F32))
            kd = k * (1.0 + (ag - 1.0) * vec(VEC_KA, hp))
            rb_ref[0, q_k * nhp + hp] = kd.astype(BF16)
            rb_ref[0, q_a * nhp + hp] = ag.astype(BF16)
            kdirs.append(kd)
        bonus = _head_sums_split(r * kdirs[0] * vec(VEC_RK, hp), mbd) * v
        rb_ref[0, RB_R * nhp + hp] = r.astype(BF16)
        rb_ref[0, RB_V * nhp + hp] = v.astype(BF16)
        rb_ref[0, RB_KK * nhp + hp] = kk.astype(BF16)
        rb_ref[0, RB_G * nhp + hp] = g.astype(BF16)
        rb_ref[0, RB_BONUS * nhp + hp] = bonus.astype(BF16)


def _prep_call(p, sw, wupf, wupb, aupf, aupb, gup, vecs, mbd, nhp, tb=256):
    b, _, s, _ = p.shape
    n_t = s // tb
    halo = 16
    hb = tb // halo
    full = lambda a: pl.BlockSpec(a.shape, lambda bi, i: (0,) * a.ndim)
    return pl.pallas_call(
        functools.partial(_prep_kernel, tb=tb, n_t=n_t, nhp=nhp),
        grid=(b, n_t),
        in_specs=[pl.BlockSpec((1, N_SHIFT_BLOCKS, tb, LANES), lambda bi, i: (bi, 0, i, 0)),
                  pl.BlockSpec((1, N_SHIFT_BLOCKS, halo, LANES),
                               lambda bi, i: (bi, 0, jnp.maximum(i * hb - 1, 0), 0)),
                  pl.BlockSpec((1, N_SHIFT_BLOCKS, halo, LANES),
                               lambda bi, i: (bi, 0, jnp.minimum((i + 1) * hb, s // halo - 1), 0)),
                  full(sw), full(wupf), full(wupb), full(aupf), full(aupb), full(gup), full(vecs), full(mbd)],
        out_specs=[pl.BlockSpec((1, 9 * nhp, tb, LANES), lambda bi, i: (bi, 0, i, 0)),
                   pl.BlockSpec((1, 2 * nhp, tb, LANES), lambda bi, i: (bi, 0, i, 0))],
        out_shape=[jax.ShapeDtypeStruct((b, 9 * nhp, s, LANES), BF16),
                   jax.ShapeDtypeStruct((b, 2 * nhp, s, LANES), F32)],
        compiler_params=_cparams(("parallel", "parallel")),
        name="prep",
    )(p, p, p, sw, wupf, wupb, aupf, aupb, gup, vecs, mbd)


def _wkv_kernel(r_ref, v_ref, kk_ref, kf_ref, af_ref, lf_ref, rr_ref, vr_ref, kkr_ref, kb_ref, ab_ref, lb_ref,
                yf_ref, yb_ref, s_sc, *, L, nhp):
    @pl.when(pl.program_id(1) == 0)
    def _():
        s_sc[...] = jnp.zeros(s_sc.shape, F32)

    assert HEADS_PER_BLOCK == 2 and L == HEAD_DIM
    n2 = HEADS_PER_BLOCK * L
    ri = lax.broadcasted_iota(I32, (n2, LANES), 0)
    ci = lax.broadcasted_iota(I32, (n2, LANES), 1)
    head_match = (ri // L) == (ci // HEAD_DIM)
    st = lax.broadcasted_iota(I32, (L, LANES), 0)
    ss = lax.broadcasted_iota(I32, (L, LANES), 1) % L
    first_head = lax.broadcasted_iota(I32, (L, LANES), 1) < HEAD_DIM
    eye = (st == ss).astype(F32)
    strict = {False: st > ss, True: st < ss}
    incl = {False: st >= ss, True: st < ss}

    def stack(x):
        return jnp.where(head_match, jnp.concatenate([x] * HEADS_PER_BLOCK, axis=0), 0.0)

    def bdiag(x):
        return jnp.concatenate([jnp.where(first_head, x, 0.0), jnp.where(first_head, 0.0, x)], axis=0)

    chains = [(False, hp, r_ref, v_ref, kk_ref, kf_ref, af_ref, lf_ref) for hp in range(nhp)]
    chains += [(True, hp, rr_ref, vr_ref, kkr_ref, kb_ref, ab_ref, lb_ref) for hp in range(nhp)]
    n = len(chains)
    lws = [ch[7][0, ch[1]] for ch in chains]
    t_row = lax.broadcasted_iota(I32, (L, LANES), 0)

    def cumsum_time(x, rev):
        sh = 1
        while sh < L:
            if rev:
                x = x + jnp.where(t_row < L - sh, pltpu.roll(x, L - sh, 0), 0.0)
            else:
                x = x + jnp.where(t_row >= sh, pltpu.roll(x, sh, 0), 0.0)
            sh *= 2
        return x

    st = [dict() for _ in range(n)]

    def prologue(i):
        rev, hp, rr, vr, kkr, kdr, agr, _ = chains[i]
        r = rr[0, hp].astype(F32)
        kk = kkr[0, hp].astype(F32)
        kd = kdr[0, hp].astype(F32)
        ag = agr[0, hp].astype(F32)
        v = vr[0, hp].astype(F32)
        c_in = cumsum_time(lws[i], rev)
        e_ex = jnp.exp(c_in - lws[i])
        e_inv = jnp.exp(-c_in)
        if rev:
            e_r = e_ex
            p_last = jnp.exp(c_in[0:1])
        else:
            e_r = jnp.exp(c_in)
            p_last = jnp.exp(c_in[L - 1:L])
        b_t = kk * ag * e_inv
        k_t = kd * e_inv
        st[i].update(
            lhs=jnp.concatenate([-kk * e_ex, r * e_r], axis=0).astype(BF16),
            rhs=jnp.concatenate([stack(b_t), stack(k_t)], axis=0).astype(BF16),
            bk=jnp.concatenate([b_t * p_last, k_t * p_last], axis=0).astype(BF16),
            v=v.astype(BF16), vbd=bdiag(v).astype(BF16), p_last=p_last)

    def stage_scores(i):
        c = st[i]
        c['g'] = lax.dot_general(c['lhs'], c['rhs'], NT_DIMS, preferred_element_type=F32)
        c['s_old'] = s_sc[i]
        c['ls'] = lax.dot_general(c['lhs'], c['s_old'].astype(BF16), NT_DIMS, preferred_element_type=F32)

    def stage_solve_setup(i):
        c = st[i]
        rev, g = chains[i][0], c.pop('g')
        a_ab = jnp.where(strict[rev], g[0:L, 0:n2], 0.0)
        a_ak = jnp.where(strict[rev], g[0:L, n2:2 * n2], 0.0).astype(BF16)
        c['a_r'] = jnp.concatenate([jnp.where(incl[rev], g[L:2 * L, 0:n2], 0.0),
                                    jnp.where(incl[rev], g[L:2 * L, n2:2 * n2], 0.0)], axis=1).astype(BF16)
        c['x'] = c['ls'][0:L] + jnp.dot(a_ak, c['vbd'], preferred_element_type=F32)
        c['pw'] = jnp.dot(a_ab.astype(BF16), bdiag(a_ab).astype(BF16), preferred_element_type=F32)
        c['t'] = eye + a_ab

    def stage_round(i):
        c = st[i]
        both = jnp.dot(jnp.concatenate([c['pw'], c['t']], axis=0).astype(BF16), bdiag(c['pw']).astype(BF16),
                       preferred_element_type=F32)
        c['pw'] = both[0:L]
        c['t'] = c['t'] + both[L:2 * L]

    def stage_last_round(i):
        c = st[i]
        c['t'] = c['t'] + jnp.dot(c['t'].astype(BF16), bdiag(c.pop('pw')).astype(BF16),
                                  preferred_element_type=F32)

    def stage_solve(i):
        c = st[i]
        c['u'] = jnp.dot(c.pop('t').astype(BF16), bdiag(c.pop('x')).astype(BF16), preferred_element_type=F32)

    def stage_out(i):
        c = st[i]
        rev, hp = chains[i][0], chains[i][1]
        u = c['u']
        y = c['ls'][L:2 * L] + jnp.dot(c['a_r'], jnp.concatenate([bdiag(u).astype(BF16), c['vbd']], axis=0),
                                       preferred_element_type=F32)
        upd = lax.dot_general(jnp.concatenate([u.astype(BF16), c['v']], axis=0), c['bk'], TN_DIMS,
                              preferred_element_type=F32)
        (yb_ref if rev else yf_ref)[0, hp] = y
        s_sc[i] = c['s_old'] * c['p_last'] + jnp.where(head_match, upd, 0.0)

    rounds = L.bit_length() - 1
    stages = ([stage_scores, stage_solve_setup] + [stage_round] * (rounds - 2)
              + [stage_last_round, stage_solve, stage_out])
    for i in range(n):
        prologue(i)
    for stage in stages:
        for i in range(n):
            stage(i)


def _wkv_call(rb, lw, nhp, L=WKV_CHUNK):
    b, _, s, _ = rb.shape
    nc = s // L
    fwd = lambda q: pl.BlockSpec((1, nhp, L, LANES), lambda bi, c: (bi, q, c, 0))
    bwd = lambda q: pl.BlockSpec((1, nhp, L, LANES), lambda bi, c: (bi, q, nc - 1 - c, 0))
    y_shape = jax.ShapeDtypeStruct((b, nhp, s, LANES), F32)
    return pl.pallas_call(
        functools.partial(_wkv_kernel, L=L, nhp=nhp),
        grid=(b, nc),
        in_specs=[fwd(RB_R), fwd(RB_V), fwd(RB_KK), fwd(RB_KF), fwd(RB_AF), fwd(0),
                  bwd(RB_R), bwd(RB_V), bwd(RB_KK), bwd(RB_KB), bwd(RB_AB), bwd(1)],
        out_specs=[fwd(0), bwd(0)],
        out_shape=[y_shape, y_shape],
        scratch_shapes=[pltpu.VMEM((2 * nhp, LANES, LANES), F32)],
        compiler_params=_cparams(("parallel", "arbitrary")),
        name="wkv",
    )(rb, rb, rb, rb, rb, lw, rb, rb, rb, rb, rb, lw)


def _post_kernel(yf_ref, yb_ref, bonus_ref, g_ref, lnw_ref, lnb_ref, mbd_ref, o_ref, *, nhp):
    mbd = mbd_ref[...]
    inv_n = 1.0 / HEAD_DIM
    for hp in range(nhp):
        y = yf_ref[0, hp] + yb_ref[0, hp]
        mu = _head_sums_split(y, mbd) * inv_n
        d = y - mu
        var = _head_sums_split(d * d, mbd) * inv_n
        yn = d * lax.rsqrt(var + GN_EPS) * lnw_ref[hp:hp + 1, :] + lnb_ref[hp:hp + 1, :]
        out = (yn + bonus_ref[0, hp].astype(F32)) * g_ref[0, hp].astype(F32)
        o_ref[0, :, hp * LANES:(hp + 1) * LANES] = out.astype(o_ref.dtype)


def _post_call(yf, yb, rb, lnw, lnb, mbd, nhp, tb=256):
    b, _, s, _ = yf.shape
    yblk = pl.BlockSpec((1, nhp, tb, LANES), lambda bi, i: (bi, 0, i, 0))
    rblk = lambda q: pl.BlockSpec((1, nhp, tb, LANES), lambda bi, i: (bi, q, i, 0))
    full = lambda a: pl.BlockSpec(a.shape, lambda bi, i: (0,) * a.ndim)
    return pl.pallas_call(
        functools.partial(_post_kernel, nhp=nhp),
        grid=(b, s // tb),
        in_specs=[yblk, yblk, rblk(RB_BONUS), rblk(RB_G), full(lnw), full(lnb), full(mbd)],
        out_specs=pl.BlockSpec((1, tb, nhp * LANES), lambda bi, i: (bi, i, 0)),
        out_shape=jax.ShapeDtypeStruct((b, s, nhp * LANES), BF16),
        compiler_params=_cparams(("parallel", "parallel")),
        name="post",
    )(yf, yb, rb, rb, lnw, lnb, mbd)


def _na_kernel(q_ref, k_ref, v_ref, bias_ref, qw_ref, kw_ref, mbd_ref, o_ref, kn_sc, *, rq, rows):
    rbi = pl.program_id(2)
    mbd = mbd_ref[...]
    inv_n = 1.0 / HEAD_DIM
    span = WIN_R * GRID_W
    s_len = kn_sc.shape[0]

    @pl.when(rbi == 0)
    def _():
        def body(c, carry):
            off = pl.multiple_of(c * span, span)
            k = k_ref[0, 0, pl.ds(off, span), :].astype(F32)
            ms = _head_sums_split(k * k, mbd) * inv_n
            kn_sc[pl.ds(off, span), :] = (k * lax.rsqrt(ms + NORM_EPS) * kw_ref[...]).astype(BF16)
            return carry
        lax.fori_loop(0, s_len // span, body, 0)

    ri = lax.broadcasted_iota(I32, (HEADS_PER_BLOCK * GRID_W, LANES), 0)
    ci = lax.broadcasted_iota(I32, (HEADS_PER_BLOCK * GRID_W, LANES), 1)
    head_match = (ri // GRID_W) == (ci // HEAD_DIM)
    lane = lax.broadcasted_iota(I32, (GRID_W, LANES), 1)
    scale = HEAD_DIM ** -0.5
    q_all = q_ref[0, 0].astype(F32)
    ms = _head_sums_split(q_all * q_all, mbd) * inv_n
    qn_all = q_all * lax.rsqrt(ms + NORM_EPS) * qw_ref[...] * scale
    starts, offs = [], []
    for qi in range(rq):
        i = rbi * rq + qi
        rs = jnp.clip(i - WIN_R // 2, 0, rows - WIN_R)
        offs.append(rs - i + WIN_R - 1)
        starts.append(pl.multiple_of(rs * GRID_W, GRID_W))
    qs_l = [jnp.where(head_match, jnp.concatenate([qn_all[qi * GRID_W:(qi + 1) * GRID_W]] * HEADS_PER_BLOCK,
                                                  axis=0), 0.0).astype(BF16) for qi in range(rq)]
    def bias(off):
        return jnp.concatenate([bias_ref[0, off + 2 * i] for i in range(WIN_R // 2)], axis=1)

    s_l = [lax.dot_general(qs_l[qi], kn_sc[pl.ds(starts[qi], span), :], NT_DIMS, preferred_element_type=F32)
           + bias(offs[qi]) for qi in range(rq)]
    p_l, l_l = [], []
    for s in s_l:
        p = jnp.exp(s - jnp.max(s, axis=-1, keepdims=True))
        p_l.append(p.astype(BF16))
        l_l.append(jnp.sum(p, axis=-1, keepdims=True))
    o_l = [jnp.dot(p_l[qi], v_ref[0, 0, pl.ds(starts[qi], span), :], preferred_element_type=F32) / l_l[qi]
           for qi in range(rq)]
    for qi, o in enumerate(o_l):
        out = o[0:GRID_W]
        for h in range(1, HEADS_PER_BLOCK):
            out = jnp.where(lane // HEAD_DIM == h, o[h * GRID_W:(h + 1) * GRID_W], out)
        o_ref[0, qi * GRID_W:(qi + 1) * GRID_W, :] = out.astype(o_ref.dtype)


def _na_call(p, bias, qw, kw, mbd, q_blk0, nhp, rq=16):
    b, _, s, _ = p.shape
    rows = s // GRID_W
    rq = min(rq, rows)
    full = lambda a: pl.BlockSpec(a.shape, lambda bi, hp, r: (0,) * a.ndim)
    return pl.pallas_call(
        functools.partial(_na_kernel, rq=rq, rows=rows),
        grid=(b, nhp, rows // rq),
        in_specs=[pl.BlockSpec((1, 1, rq * GRID_W, LANES), lambda bi, hp, r: (bi, q_blk0 + hp, r, 0)),
                  pl.BlockSpec((1, 1, s, LANES), lambda bi, hp, r: (bi, q_blk0 + nhp + hp, 0, 0)),
                  pl.BlockSpec((1, 1, s, LANES), lambda bi, hp, r: (bi, q_blk0 + 2 * nhp + hp, 0, 0)),
                  pl.BlockSpec((1,) + bias.shape[1:], lambda bi, hp, r: (hp, 0, 0, 0)),
                  full(qw), full(kw), full(mbd)],
        out_specs=pl.BlockSpec((1, rq * GRID_W, LANES), lambda bi, hp, r: (bi, r, hp)),
        out_shape=jax.ShapeDtypeStruct((b, s, nhp * LANES), BF16),
        scratch_shapes=[pltpu.VMEM((s, LANES), BF16)],
        compiler_params=_cparams(("parallel", "parallel", "arbitrary")),
        name="na",
    )(p, p, p, bias, qw, kw, mbd)


def _na_bias_table(rel_bias):
    h = rel_bias.shape[0]
    qc = np.arange(GRID_W)
    kc = np.arange(GRID_W)
    win0 = np.clip(qc - WIN_C // 2, 0, GRID_W - WIN_C)
    valid = (kc[None, :] >= win0[:, None]) & (kc[None, :] < win0[:, None] + WIN_C)
    coff = np.clip(kc[None, :] - qc[:, None] + WIN_C - 1, 0, 2 * WIN_C - 2)
    n_row, n_col = 2 * WIN_R - 1, 2 * WIN_C - 1
    n_ro = n_row - 1
    sel = np.zeros((n_ro, 2, n_row), np.float32)
    for ro in range(n_ro):
        for part in range(2):
            sel[ro, part, ro + part] = 1.0
    pick = np.zeros((2, n_col, GRID_W, 2 * GRID_W), np.float32)
    for part in range(2):
        pick[part, coff, qc[:, None], part * GRID_W + kc[None, :]] = 1.0
    a = jnp.einsum('hrc,apr->hapc', rel_bias.astype(F32), jnp.asarray(sel), precision=HI)
    a = a.reshape(h // HEADS_PER_BLOCK, HEADS_PER_BLOCK, n_ro, 2 * n_col).transpose(0, 2, 1, 3)
    t = jnp.einsum('ahbx,xql->ahbql', a, jnp.asarray(pick.reshape(2 * n_col, GRID_W, 2 * GRID_W)),
                   precision=HI)
    t = jnp.where(np.tile(valid, (1, 2)), t, NEG_BIG)
    return t.reshape(h // HEADS_PER_BLOCK, n_ro, HEADS_PER_BLOCK * GRID_W, 2 * GRID_W)


def _outproj_kernel(yr_ref, yn_ref, x_ref, mod_ref, nw_ref, w1_ref, w2_ref, wr_ref, x1_ref, h2_ref, h2row_ref,
                    lg_ref):
    tm = x_ref.shape[1]
    sub = LANES
    tiles = [slice(i * sub, (i + 1) * sub) for i in range(tm // sub)]
    accs = [jnp.dot(yr_ref[0, r], w1_ref[...], preferred_element_type=F32)
            + jnp.dot(yn_ref[0, r], w2_ref[...], preferred_element_type=F32) for r in tiles]
    h2s = []
    for r, acc in zip(tiles, accs):
        x1 = x_ref[0, r] + mod_ref[0, 2:3, :] * acc
        x1_ref[0, r] = x1
        ms = jnp.mean(x1 * x1, axis=-1, keepdims=True)
        y = x1 * lax.rsqrt(ms + NORM_EPS) * nw_ref[...]
        h2 = y * (1.0 + mod_ref[0, 4:5, :]) + mod_ref[0, 3:4, :]
        packed = _pack_halves(h2)
        h2row_ref[0, r] = packed
        for j in range(h2_ref.shape[2]):
            h2_ref[0, r, j, :] = packed[:, j * LANES:(j + 1) * LANES]
        h2s.append(h2)
    wr = wr_ref[...]
    wr_hi = wr.astype(BF16)
    wr_lo = (wr - wr_hi.astype(F32)).astype(BF16)
    for r, h2 in zip(tiles, h2s):
        h_hi = h2.astype(BF16)
        h_lo = (h2 - h_hi.astype(F32)).astype(BF16)
        lg_ref[:, r] = (lax.dot_general(wr_hi, h_hi, NT_DIMS, preferred_element_type=F32)
                        + lax.dot_general(wr_hi, h_lo, NT_DIMS, preferred_element_type=F32)
                        + lax.dot_general(wr_lo, h_hi, NT_DIMS, preferred_element_type=F32))


def _outproj_call(yr, yn, x, mod3, norm_w, w_o, wr_t, tm=512):
    b, s, d = x.shape
    dh = yr.shape[-1]
    assert w_o.shape == (2 * dh, d)
    ne = wr_t.shape[0]
    nt = s // tm
    full = lambda a: pl.BlockSpec(a.shape, lambda bi, i: (0,) * a.ndim)
    return pl.pallas_call(
        _outproj_kernel,
        grid=(b, nt),
        in_specs=[pl.BlockSpec((1, tm, dh), lambda bi, i: (bi, i, 0)),
                  pl.BlockSpec((1, tm, dh), lambda bi, i: (bi, i, 0)),
                  pl.BlockSpec((1, tm, d), lambda bi, i: (bi, i, 0)),
                  pl.BlockSpec((1, 6, d), lambda bi, i: (bi, 0, 0)),
                  full(norm_w),
                  pl.BlockSpec((dh, d), lambda bi, i: (0, 0)),
                  pl.BlockSpec((dh, d), lambda bi, i: (1, 0)),
                  full(wr_t)],
        out_specs=[pl.BlockSpec((1, tm, d), lambda bi, i: (bi, i, 0)),
                   pl.BlockSpec((1, tm, d // 2 // LANES, LANES), lambda bi, i: (bi, i, 0, 0)),
                   pl.BlockSpec((1, tm, d // 2), lambda bi, i: (bi, i, 0)),
                   pl.BlockSpec((ne, tm), lambda bi, i: (0, bi * nt + i))],
        out_shape=[jax.ShapeDtypeStruct((b, s, d), F32),
                   jax.ShapeDtypeStruct((b, s, d // 2 // LANES, LANES), U32),
                   jax.ShapeDtypeStruct((b, s, d // 2), U32),
                   jax.ShapeDtypeStruct((ne, b * s), F32)],
        compiler_params=_cparams(("parallel", "parallel")),
        name="outproj",
    )(yr, yn, x, mod3, norm_w, w_o, w_o, wr_t)


def _first_argmax(x, idx, n):
    m = jnp.max(x, axis=0, keepdims=True)
    a = jnp.min(jnp.where(x == m, idx, n), axis=0, keepdims=True)
    return m, a


def _route_kernel(lg_ref, bias_ref, idx_ref, gw_ref, pos_ref, cnt_ref, *, tr):
    @pl.when(pl.program_id(0) == 0)
    def _():
        cnt_ref[...] = jnp.zeros(cnt_ref.shape, F32)

    gsz = N_EXPERTS // N_GROUPS
    scores = _sigmoid(lg_ref[...])
    biased = scores + bias_ref[:, 0:1]
    ig = lax.broadcasted_iota(I32, (gsz, tr), 0)
    grp_rows = []
    for g in range(N_GROUPS):
        blk = biased[g * gsz:(g + 1) * gsz]
        m1, a1 = _first_argmax(blk, ig, gsz)
        m2 = jnp.max(jnp.where(ig == a1, -jnp.inf, blk), axis=0, keepdims=True)
        grp_rows.append(m1 + m2)
    grp = jnp.concatenate(grp_rows, axis=0)
    ign = lax.broadcasted_iota(I32, (N_GROUPS, tr), 0)
    sel = jnp.zeros((N_GROUPS, tr), jnp.bool_)
    for _ in range(TOPK_GROUPS):
        _, a = _first_argmax(grp, ign, N_GROUPS)
        hit = ign == a
        sel = jnp.logical_or(sel, hit)
        grp = jnp.where(hit, -jnp.inf, grp)
    masked = jnp.concatenate(
        [jnp.where(sel[g:g + 1], biased[g * gsz:(g + 1) * gsz], -jnp.inf) for g in range(N_GROUPS)], axis=0)
    ie = lax.broadcasted_iota(I32, (N_EXPERTS, tr), 0)
    picks, pick_scores = [], []
    onehot = jnp.zeros((N_EXPERTS, tr), F32)
    for _ in range(TOP_K):
        _, a = _first_argmax(masked, ie, N_EXPERTS)
        hit = ie == a
        picks.append(a)
        pick_scores.append(jnp.sum(jnp.where(hit, scores, 0.0), axis=0, keepdims=True))
        onehot = onehot + hit.astype(F32)
        masked = jnp.where(hit, -jnp.inf, masked)
    total = pick_scores[0]
    for sc in pick_scores[1:]:
        total = total + sc
    t0 = lax.broadcasted_iota(I32, (tr, tr), 0)
    t1 = lax.broadcasted_iota(I32, (tr, tr), 1)
    before = (t0 < t1).astype(BF16)
    rank = jnp.dot(onehot.astype(BF16), before, preferred_element_type=F32) + cnt_ref[:, 0:1]
    zero_i = jnp.zeros((1, tr), I32)
    zero_f = jnp.zeros((1, tr), F32)
    for k in range(8):
        if k < TOP_K:
            idx_ref[k:k + 1, :] = picks[k]
            gw_ref[k:k + 1, :] = pick_scores[k] / total * ROUTED_SCALE
            pos = jnp.sum(jnp.where(ie == picks[k], rank, 0.0), axis=0, keepdims=True)
            pos_ref[k:k + 1, :] = pos.astype(I32)
        else:
            idx_ref[k:k + 1, :] = zero_i
            gw_ref[k:k + 1, :] = zero_f
            pos_ref[k:k + 1, :] = zero_i
    cnt_ref[...] = cnt_ref[...] + jnp.sum(onehot, axis=1, keepdims=True)


def _route_call(lg_t, bias2, tr=512):
    ne, t = lg_t.shape
    tr = min(tr, t)
    tok = pl.BlockSpec((8, tr), lambda i: (0, i))
    return pl.pallas_call(
        functools.partial(_route_kernel, tr=tr),
        grid=(t // tr,),
        in_specs=[pl.BlockSpec((ne, tr), lambda i: (0, i)),
                  pl.BlockSpec(bias2.shape, lambda i: (0, 0))],
        out_specs=[tok, tok, tok, pl.BlockSpec((ne, LANES), lambda i: (0, 0))],
        out_shape=[jax.ShapeDtypeStruct((8, t), I32), jax.ShapeDtypeStruct((8, t), F32),
                   jax.ShapeDtypeStruct((8, t), I32), jax.ShapeDtypeStruct((ne, LANES), F32)],
        compiler_params=_cparams(("arbitrary",)),
        name="route",
    )(lg_t, bias2)


FILL_CHUNKS = (1, 8, 64)
ZERO_ROWS = FILL_CHUNKS[-1]


def _dispatch_kernel(slot_ref, fill_lo_ref, fill_hi_ref, h_ref, xs_ref, zrow, sem, *, td):
    def row_copy(t, slot):
        return pltpu.make_async_copy(h_ref.at[t], xs_ref.at[slot], sem)

    def start(t, carry):
        for k in range(TOP_K):
            row_copy(t, slot_ref[t * SLOT_STRIDE + k]).start(priority=k % 2)
        return carry

    lax.fori_loop(0, td, start, 0)
    for k in range(TOP_K):
        pltpu.make_async_copy(h_ref, xs_ref.at[pl.ds(0, td)], sem).wait()

    @pl.when(pl.program_id(0) == pl.num_programs(0) - 1)
    def _():
        zrow[...] = jnp.zeros(zrow.shape, zrow.dtype)

        def zero_copy(row, n):
            if n > 1:
                row = pl.multiple_of(row, FILL_CHUNKS[1])
            return pltpu.make_async_copy(zrow.at[pl.ds(0, n)], xs_ref.at[pl.ds(row, n)], sem)

        def for_each_chunk(r, fn):
            lo, hi = fill_lo_ref[r], fill_hi_ref[r]
            for ci, n in enumerate(FILL_CHUNKS):
                if ci + 1 < len(FILL_CHUNKS):
                    count = jnp.minimum(((-lo) % FILL_CHUNKS[ci + 1]) // n, (hi - lo) // n)
                else:
                    count = (hi - lo) // n
                fn(lo, n, count)
                lo = lo + count * n

        def start_range(r, carry):
            def go(lo, n, count):
                lax.fori_loop(0, count, lambda i, c: (zero_copy(lo + i * n, n).start(), c)[1], 0)
            for_each_chunk(r, go)
            return carry

        def wait_range(r, carry):
            def go(lo, n, count):
                lax.fori_loop(0, count, lambda i, c: (zero_copy(0, n).wait(), c)[1], 0)
            for_each_chunk(r, go)
            return carry

        n_ranges = fill_lo_ref.shape[0]
        lax.fori_loop(0, n_ranges, start_range, 0)
        lax.fori_loop(0, n_ranges, wait_range, 0)


def _dispatch_call(slots_flat, fill_lo, fill_hi, h2, n_rows, td=256):
    t, nj, _ = h2.shape
    smem = pl.BlockSpec(memory_space=pltpu.SMEM)
    return pl.pallas_call(
        functools.partial(_dispatch_kernel, td=td),
        grid=(t // td,),
        in_specs=[pl.BlockSpec((td * SLOT_STRIDE,), lambda i: (i,), memory_space=pltpu.SMEM),
                  smem, smem,
                  pl.BlockSpec((td, nj, LANES), lambda i: (i, 0, 0))],
        out_specs=pl.BlockSpec(memory_space=pl.ANY),
        out_shape=jax.ShapeDtypeStruct((n_rows, nj, LANES), h2.dtype),
        scratch_shapes=[pltpu.VMEM((ZERO_ROWS, nj, LANES), h2.dtype), pltpu.SemaphoreType.DMA(())],
        compiler_params=_cparams(("arbitrary",)),
        name="dispatch",
    )(slots_flat, fill_lo, fill_hi, h2)


def _experts_kernel(be_ref, first_ref, nxt_ref, par_ref, rows_ref, nv_ref, xs_hbm, wg_hbm, wu_hbm, wd_hbm, ys_ref,
                    wg_buf, wu_buf, wd_buf, wg_sc, wu_sc, wd_sc, xpack, sem, isem):
    nj = xs_hbm.shape[1]
    b = pl.program_id(0)
    m = xpack.shape[1]
    half = wg_sc.shape[0] // 2
    oslot = lax.rem(b, 2)

    def in_copies(slot, row0):
        return [pltpu.make_async_copy(xs_hbm.at[pl.ds(row0, m), j],
                                      xpack.at[slot, :, pl.ds(j * LANES, LANES)], isem.at[slot]) for j in range(nj)]

    @pl.when(b == 0)
    def _():
        for c in in_copies(0, 0):
            c.start()

    @pl.when(b + 1 < nv_ref[0])
    def _():
        for c in in_copies(1 - oslot, pl.multiple_of((b + 1) * m, m)):
            c.start()

    def weight_copies(e, slot):
        return [pltpu.make_async_copy(src.at[e], dst.at[slot], sem.at[slot, i])
                for i, (src, dst) in enumerate(((wg_hbm, wg_buf), (wu_hbm, wu_buf), (wd_hbm, wd_buf)))]

    def compute(nrows):
        x_lo, x_hi = _unpack_halves(xpack[oslot, 0:nrows])
        x_lo, x_hi = x_lo.astype(BF16), x_hi.astype(BF16)
        g = (jnp.dot(x_lo, wg_sc[0:half], preferred_element_type=F32)
             + jnp.dot(x_hi, wg_sc[half:], preferred_element_type=F32))
        u = (jnp.dot(x_lo, wu_sc[0:half], preferred_element_type=F32)
             + jnp.dot(x_hi, wu_sc[half:], preferred_element_type=F32))
        a = (_silu(g) * u).astype(BF16)
        ys_ref[0:nrows] = _pack_halves(jnp.dot(a, wd_sc[...], preferred_element_type=F32))

    @pl.when(b < nv_ref[0])
    def _():
        for c in in_copies(oslot, 0):
            c.wait()

        @pl.when(b == 0)
        def _():
            for c in weight_copies(be_ref[0], par_ref[0]):
                c.start()

        @pl.when(first_ref[b] == 1)
        def _():
            slot = par_ref[b]
            for c in weight_copies(be_ref[b], slot):
                c.wait()

            @pl.when(nxt_ref[b] >= 0)
            def _():
                for c in weight_copies(nxt_ref[b], 1 - slot):
                    c.start()

            wg_sc[...] = wg_buf[slot].astype(BF16)
            wu_sc[...] = wu_buf[slot].astype(BF16)
            wd_sc[...] = wd_buf[slot].astype(BF16)

        @pl.when(rows_ref[b] > m // 2)
        def _():
            compute(m)

        @pl.when(rows_ref[b] <= m // 2)
        def _():
            compute(m // 2)
            ys_ref[m // 2:] = jnp.zeros((m - m // 2, ys_ref.shape[1]), ys_ref.dtype)

    @pl.when(b >= nv_ref[0])
    def _():
        ys_ref[...] = jnp.zeros(ys_ref.shape, ys_ref.dtype)


def _experts_call(block_e, first, nxt, par, rows_valid, n_valid, xs, wg, wu, wd):
    total, nj, _ = xs.shape
    d = wg.shape[1]
    m = DISPATCH_BLOCK
    ff = wg.shape[-1]

    hbm = pl.BlockSpec(memory_space=pl.ANY)
    grid_spec = pltpu.PrefetchScalarGridSpec(
        num_scalar_prefetch=6,
        grid=(total // m,),
        in_specs=[hbm, hbm, hbm, hbm],
        out_specs=pl.BlockSpec((m, nj * LANES), lambda b, *s: (b, 0)),
        scratch_shapes=[pltpu.VMEM((2, d, ff), F32), pltpu.VMEM((2, d, ff), F32), pltpu.VMEM((2, ff, d), F32),
                        pltpu.VMEM((d, ff), BF16), pltpu.VMEM((d, ff), BF16), pltpu.VMEM((ff, d), BF16),
                        pltpu.VMEM((2, m, nj * LANES), U32),
                        pltpu.SemaphoreType.DMA((2, 3)), pltpu.SemaphoreType.DMA((2,))],
    )
    return pl.pallas_call(
        _experts_kernel,
        grid_spec=grid_spec,
        out_shape=jax.ShapeDtypeStruct((total, nj * LANES), U32),
        compiler_params=_cparams(("arbitrary",)),
        name="experts",
    )(block_e, first, nxt, par, rows_valid, n_valid, xs, wg, wu, wd)


COMBINE_INLINE_ROWS = 64


def _combine_kernel(slot_ref, x1_ref, h2_ref, gw_ref, mod_ref, wg_ref, wu_ref, wd_ref, ys_ref, o_ref,
                    buf, sem, *, tc):
    def row_copy(t, k, slot):
        return pltpu.make_async_copy(ys_ref.at[pl.ds(slot, 1)], buf.at[k, pl.ds(t, 1)], sem)

    def start(t, carry):
        for k in range(TOP_K):
            row_copy(t, k, slot_ref[t * SLOT_STRIDE + k]).start(priority=k % 2)
        return carry

    inline = min(COMBINE_INLINE_ROWS, tc)
    lax.fori_loop(0, tc - inline, start, 0)
    for t in range(tc - inline, tc):
        start(t, 0)
    half = wg_ref.shape[0] // 2
    h_lo, h_hi = _unpack_halves(h2_ref[0])
    h_lo, h_hi = h_lo.astype(BF16), h_hi.astype(BF16)
    g = (jnp.dot(h_lo, wg_ref[0:half], preferred_element_type=F32)
         + jnp.dot(h_hi, wg_ref[half:], preferred_element_type=F32))
    u = (jnp.dot(h_lo, wu_ref[0:half], preferred_element_type=F32)
         + jnp.dot(h_hi, wu_ref[half:], preferred_element_type=F32))
    acc = jnp.dot((_silu(g) * u).astype(BF16), wd_ref[...], preferred_element_type=F32)
    for k in range(TOP_K):
        pltpu.make_async_copy(ys_ref.at[pl.ds(0, tc)], buf.at[k], sem).wait()
    gw = gw_ref[...]
    acc_lo, acc_hi = acc[:, :half], acc[:, half:]
    for k in range(TOP_K):
        y_lo, y_hi = _unpack_halves(buf[k])
        acc_lo = acc_lo + gw[:, k:k + 1] * y_lo
        acc_hi = acc_hi + gw[:, k:k + 1] * y_hi
    gate = mod_ref[0, 5:6, :]
    o_ref[0, :, 0:half] = x1_ref[0, :, 0:half] + gate[:, :half] * acc_lo
    o_ref[0, :, half:] = x1_ref[0, :, half:] + gate[:, half:] * acc_hi


def _combine_call(slots_flat, x1, h2, gw_tok, mod3, wg, wu, wd, ys, tc=256):
    b, s, d = x1.shape
    nt = s // tc
    full = lambda a: pl.BlockSpec(a.shape, lambda bi, i: (0,) * a.ndim)
    return pl.pallas_call(
        functools.partial(_combine_kernel, tc=tc),
        grid=(b, nt),
        in_specs=[pl.BlockSpec((tc * SLOT_STRIDE,), lambda bi, i: (bi * nt + i,), memory_space=pltpu.SMEM),
                  pl.BlockSpec((1, tc, d), lambda bi, i: (bi, i, 0)),
                  pl.BlockSpec((1, tc, d // 2), lambda bi, i: (bi, i, 0)),
                  pl.BlockSpec((tc, 8), lambda bi, i: (bi * nt + i, 0)),
                  pl.BlockSpec((1, 6, d), lambda bi, i: (bi, 0, 0)),
                  full(wg), full(wu), full(wd),
                  pl.BlockSpec(memory_space=pl.ANY)],
        out_specs=pl.BlockSpec((1, tc, d), lambda bi, i: (bi, i, 0)),
        out_shape=jax.ShapeDtypeStruct((b, s, d), F32),
        scratch_shapes=[pltpu.VMEM((TOP_K, tc, d // 2), U32), pltpu.SemaphoreType.DMA(())],
        compiler_params=_cparams(("arbitrary", "arbitrary")),
        name="combine",
    )(slots_flat, x1, h2, gw_tok, mod3, wg, wu, wd, ys)


def _pad_rows(w, rows, at=0):
    out = jnp.zeros((rows, w.shape[1]), w.dtype)
    return out.at[at:at + w.shape[0]].set(w)


def _mixing_stage(p, rwkv_shift, w0_f, w_up_f, w0_b, w_up_b, a0_f, a_up_f, a0_b, a_up_b, g_up,
                  k_k, k_a, r_k, ln_x_w, ln_x_b, q_norm_w, k_norm_w, rel_bias):
    d_rwkv = w0_f.shape[0]
    nhp = d_rwkv // LANES
    mbd = _head_sum_matrix()
    n_shift = rwkv_shift.shape[1]
    sw = jnp.pad(rwkv_shift, ((0, 0), (0, N_SHIFT_BLOCKS * LANES - n_shift))).reshape(3, N_SHIFT_BLOCKS, LANES)
    wupf = _pad_rows(w_up_f, LANES, 0).astype(BF16)
    wupb = _pad_rows(w_up_b, LANES, DECAY_LORA).astype(BF16)
    aupf = _pad_rows(a_up_f, LANES, 0).astype(BF16)
    aupb = _pad_rows(a_up_b, LANES, AAA_LORA).astype(BF16)
    gup = _pad_rows(g_up, 2 * LANES, 0).astype(BF16)
    vecs = jnp.stack([w0_f, w0_b, a0_f, a0_b, k_k, k_a, r_k.reshape(-1)]).reshape(7, nhp, LANES)
    rb, lw = _prep_call(p, sw, wupf, wupb, aupf, aupb, gup, vecs, mbd, nhp)
    y_f, y_b = _wkv_call(rb, lw, nhp)
    y_rwkv = _post_call(y_f, y_b, rb, ln_x_w.reshape(nhp, LANES), ln_x_b.reshape(nhp, LANES), mbd, nhp)
    bias = _na_bias_table(rel_bias)
    qw = jnp.tile(q_norm_w, HEADS_PER_BLOCK).reshape(1, LANES)
    kw = jnp.tile(k_norm_w, HEADS_PER_BLOCK).reshape(1, LANES)
    y_na = _na_call(p, bias, qw, kw, mbd, N_SHIFT_BLOCKS, nhp)
    return y_rwkv, y_na


def _moe_stage(x1, h2_tiles, h2_rows, lg_t, mod3, router_bias, w_gate_e, w_up_e, w_down_e, w_gate_s, w_up_s,
               w_down_s):
    b, s, d = x1.shape
    t = b * s
    m = DISPATCH_BLOCK
    bias2 = jnp.broadcast_to(router_bias.astype(F32)[:, None], (N_EXPERTS, LANES))
    idx_t, gw_t, pos_t, cnt = _route_call(lg_t, bias2)
    counts = cnt[:, 0].astype(I32)
    padded = (counts + m - 1) // m * m
    pad_end = jnp.cumsum(padded)
    pad_start = pad_end - padded
    n_blocks = -(-(t * TOP_K + N_EXPERTS * m) // m)
    n_valid = (pad_end[-1] // m).astype(I32).reshape(1)
    block_row0 = jnp.arange(n_blocks, dtype=I32) * m
    block_e = jnp.minimum(jnp.sum((pad_end[None, :] <= block_row0[:, None]).astype(I32), axis=1),
                          N_EXPERTS - 1).astype(I32)
    assert m % ZERO_ROWS == 0
    expert_ids = jnp.arange(N_EXPERTS, dtype=I32)
    start_of = jnp.sum(jnp.where(idx_t[:, :, None] == expert_ids, pad_start, 0), axis=-1)
    slots = (start_of + pos_t).astype(I32)
    slots_flat = slots.T.reshape(-1)
    gw_tok = gw_t.T
    fill_lo = jnp.concatenate([pad_start + counts, pad_end[-1:]]).astype(I32)
    fill_hi = jnp.concatenate([pad_end, jnp.full((1,), n_blocks * m, I32)]).astype(I32)
    xs = _dispatch_call(slots_flat, fill_lo, fill_hi, h2_tiles.reshape(t, *h2_tiles.shape[2:]), n_blocks * m)
    row_end = (pad_start + counts).astype(I32)
    nonempty = counts > 0
    first = jnp.logical_and(block_row0 == pad_start[block_e], block_row0 < pad_end[-1]).astype(I32)
    ordinal = jnp.cumsum(nonempty.astype(I32)) - 1
    par = (ordinal[block_e] & 1).astype(I32)
    cand = jnp.where(nonempty, expert_ids, N_EXPERTS)
    later = jnp.concatenate([lax.cummin(cand[::-1])[::-1][1:], jnp.full((1,), N_EXPERTS, I32)])
    nxt = jnp.where(later < N_EXPERTS, later, -1)[block_e].astype(I32)
    rows_valid = jnp.clip(row_end[block_e] - block_row0, 0, m).astype(I32)
    ys = _experts_call(block_e, first, nxt, par, rows_valid, n_valid, xs, w_gate_e, w_up_e, w_down_e)
    return _combine_call(slots_flat, x1, h2_rows, gw_tok, mod3, w_gate_s.astype(BF16), w_up_s.astype(BF16),
                         w_down_s.astype(BF16), ys)


def kernel(x, c, w_ada, b_ada, norm1_w, w_in, rwkv_shift, w0_f, w_up_f, w0_b, w_up_b, a0_f, a_up_f, a0_b,
           a_up_b, g_up, k_k, k_a, r_k, ln_x_w, ln_x_b, q_norm_w, k_norm_w, rel_bias, w_out, norm2_w,
           w_router, router_bias, w_gate_e, w_up_e, w_down_e, w_gate_s, w_up_s, w_down_s):
    bn, sn, d = x.shape
    depth = w_ada.shape[0]
    for l in range(depth):
        c_pad = jnp.pad(c, ((0, 8 - bn % 8 if bn % 8 else 0), (0, 0)))
        mod = _mod_call(c_pad, w_ada[l], b_ada[l].reshape(1, -1))[:bn]
        mod3 = mod.reshape(bn, 6, d)
        rwkv_in = rwkv_shift.shape[-1]
        pad_cols = N_SHIFT_BLOCKS * LANES - rwkv_in
        w_in_r = _widen_call(w_in, l, rwkv_in, pad_cols)
        p = _proj_call(x, mod3, norm1_w[l].reshape(1, d), w_in_r)
        y_rwkv, y_na = _mixing_stage(p, rwkv_shift[l], w0_f[l], w_up_f[l], w0_b[l], w_up_b[l], a0_f[l],
                                     a_up_f[l], a0_b[l], a_up_b[l], g_up[l], k_k[l], k_a[l], r_k[l],
                                     ln_x_w[l], ln_x_b[l], q_norm_w[l], k_norm_w[l], rel_bias[l])
        x1, h2_tiles, h2_rows, lg_t = _outproj_call(y_rwkv, y_na, x, mod3, norm2_w[l].reshape(1, d),
                                                    w_out[l].astype(BF16), w_router[l].T)
        x = _moe_stage(x1, h2_tiles, h2_rows, lg_t, mod3, router_bias[l], w_gate_e[l], w_up_e[l], w_down_e[l],
                       w_gate_s[l], w_up_s[l], w_down_s[l])
    return x
```

```python
import functools

import jax
import jax.numpy as jnp
import numpy as np
from jax import lax
from jax.experimental import pallas as pl
from jax.experimental.pallas import tpu as pltpu

F32 = jnp.float32
BF16 = jnp.bfloat16
I32 = jnp.int32
HI = lax.Precision.HIGHEST

LANES = 128
HEAD_DIM = 64
HEADS_PER_BLOCK = LANES // HEAD_DIM
GRID_W = 64
WIN_R = 8
WIN_C = 16
DECAY_LORA = 64
AAA_LORA = 64
GATE_LORA = 160
N_EXPERTS = 64
TOP_K = 6
SLOT_STRIDE = 8
N_GROUPS = 8
TOPK_GROUPS = 4
ROUTED_SCALE = 2.5
DISPATCH_BLOCK = 512
EXPERT_BLOCK_PARTS = 4
NORM_EPS = 1e-6
GN_EPS = 64e-5
L2_EPS = 1e-12
NEG_BIG = -1e30
WKV_CHUNK = 64
VMEM_LIMIT = 56 * 1024 * 1024

NT_DIMS = (((1,), (1,)), ((), ()))
TN_DIMS = (((0,), (0,)), ((), ()))


def _cparams(sem):
    return pltpu.CompilerParams(dimension_semantics=sem, vmem_limit_bytes=VMEM_LIMIT)


def _sigmoid(x):
    return 1.0 / (1.0 + jnp.exp(-x))


def _silu(x):
    return x * _sigmoid(x)


U32 = jnp.uint32
HI_HALF_MASK = 0xFFFF0000


def _pack_halves(x):
    n = x.shape[-1] // 2
    bits = lax.bitcast_convert_type(x.astype(BF16).astype(F32), U32)
    return (bits[:, :n] >> 16) | bits[:, n:]


def _unpack_halves(w):
    lo = lax.bitcast_convert_type(w << 16, F32)
    hi = lax.bitcast_convert_type(w & U32(HI_HALF_MASK), F32)
    return lo, hi


def _head_sum_matrix():
    a = np.arange(LANES) // HEAD_DIM
    return jnp.asarray((a[:, None] == a[None, :]).astype(np.float32))


def _head_sums_split(x, mbd):
    m16 = mbd.astype(BF16)
    hi = x.astype(BF16)
    lo = (x - hi.astype(F32)).astype(BF16)
    return jnp.dot(hi, m16, preferred_element_type=F32) + jnp.dot(lo, m16, preferred_element_type=F32)


def _mod_kernel(c_ref, w_ref, b_ref, o_ref):
    c = c_ref[...]
    o_ref[...] = jnp.dot(_silu(c), w_ref[...], preferred_element_type=F32, precision=HI) + b_ref[...]


def _mod_call(c_pad, w_ada, b_ada):
    rows, d = c_pad.shape
    n = w_ada.shape[1]
    tn = 1024
    return pl.pallas_call(
        _mod_kernel,
        grid=(n // tn,),
        in_specs=[pl.BlockSpec((rows, d), lambda j: (0, 0)),
                  pl.BlockSpec((d, tn), lambda j: (0, j)),
                  pl.BlockSpec((1, tn), lambda j: (0, j))],
        out_specs=pl.BlockSpec((rows, tn), lambda j: (0, j)),
        out_shape=jax.ShapeDtypeStruct((rows, n), F32),
        compiler_params=_cparams(("parallel",)),
        name="mod",
    )(c_pad, w_ada, b_ada)


def _widen_kernel(w_ref, o_ref, *, split, pad):
    w = w_ref[...]
    o_ref[...] = jnp.concatenate([w[:, :split], jnp.zeros((w.shape[0], pad), w.dtype), w[:, split:]],
                                 axis=1).astype(o_ref.dtype)


def _widen_call(w_stack, layer, split, pad, tr=256):
    _, rows, cols = w_stack.shape
    return pl.pallas_call(
        functools.partial(_widen_kernel, split=split, pad=pad),
        grid=(rows // tr,),
        in_specs=[pl.BlockSpec((None, tr, cols), lambda i: (layer, i, 0))],
        out_specs=pl.BlockSpec((tr, cols + pad), lambda i: (i, 0)),
        out_shape=jax.ShapeDtypeStruct((rows, cols + pad), BF16),
        compiler_params=_cparams(("parallel",)),
        name="widen",
    )(w_stack)


def _proj_kernel(x_ref, mod_ref, nw_ref, w_ref, o_ref, h_sc, *, nq):
    @pl.when(pl.program_id(2) == 0)
    def _():
        x = x_ref[0]
        ms = jnp.mean(x * x, axis=-1, keepdims=True)
        y = x * lax.rsqrt(ms + NORM_EPS) * nw_ref[...]
        h_sc[...] = (y * (1.0 + mod_ref[0, 1:2, :]) + mod_ref[0, 0:1, :]).astype(BF16)

    acc = jnp.dot(h_sc[...], w_ref[...], preferred_element_type=F32)
    for q in range(nq):
        o_ref[0, q] = acc[:, q * LANES:(q + 1) * LANES].astype(o_ref.dtype)


def _proj_call(x, mod3, norm_w, w_in_r, tm=1024, tn=1664):
    b, s, d = x.shape
    n = w_in_r.shape[1]
    nq = tn // LANES
    return pl.pallas_call(
        functools.partial(_proj_kernel, nq=nq),
        grid=(b, s // tm, n // tn),
        in_specs=[pl.BlockSpec((1, tm, d), lambda bi, i, j: (bi, i, 0)),
                  pl.BlockSpec((1, 6, d), lambda bi, i, j: (bi, 0, 0)),
                  pl.BlockSpec((1, d), lambda bi, i, j: (0, 0)),
                  pl.BlockSpec((d, tn), lambda bi, i, j: (0, j))],
        out_specs=pl.BlockSpec((1, nq, tm, LANES), lambda bi, i, j: (bi, j, i, 0)),
        out_shape=jax.ShapeDtypeStruct((b, n // LANES, s, LANES), BF16),
        scratch_shapes=[pltpu.VMEM((tm, d), BF16)],
        compiler_params=_cparams(("parallel", "parallel", "arbitrary")),
        name="proj",
    )(x, mod3, norm_w, w_in_r)


RB_R, RB_V, RB_KK, RB_G, RB_KF, RB_KB, RB_AF, RB_AB, RB_BONUS = range(9)
VEC_W0F, VEC_W0B, VEC_A0F, VEC_A0B, VEC_KK, VEC_KA, VEC_RK = range(7)
N_RKV_BLOCKS = 24
N_SHIFT_BLOCKS = 28


def _softplus(u):
    return jnp.maximum(u, 0.0) + jnp.log(1.0 + jnp.exp(-jnp.abs(u)))


def _prep_kernel(p_ref, pp_ref, pn_ref, sw_ref, wupf_ref, wupb_ref, aupf_ref, aupb_ref, gup_ref,
                 vec_ref, mbd_ref, rb_ref, lw_ref, *, tb, n_t, nhp):
    i = pl.program_id(1)
    row = lax.broadcasted_iota(I32, (tb, LANES), 0)
    has_prev = i > 0
    has_next = i < n_t - 1
    halo = pp_ref.shape[2]

    def shifted(q):
        cur = p_ref[0, q].astype(F32)
        prev_row = jnp.where(has_prev, pp_ref[0, q, halo - 1:halo, :].astype(F32), 0.0)
        next_row = jnp.where(has_next, pn_ref[0, q, 0:1, :].astype(F32), 0.0)
        up = jnp.where(row == 0, prev_row, pltpu.roll(cur, 1, 0))
        dn = jnp.where(row == tb - 1, next_row, pltpu.roll(cur, tb - 1, 0))
        return sw_ref[0, q:q + 1, :] * up + sw_ref[1, q:q + 1, :] * cur + sw_ref[2, q:q + 1, :] * dn

    mbd = mbd_ref[...]
    t_wd = jnp.tanh(shifted(N_RKV_BLOCKS)).astype(BF16)
    z_ad = shifted(N_RKV_BLOCKS + 1).astype(BF16)
    s_gd = jnp.concatenate([_sigmoid(shifted(N_RKV_BLOCKS + 2)),
                            _sigmoid(shifted(N_RKV_BLOCKS + 3))], axis=1).astype(BF16)

    def vec(v, hp):
        return vec_ref[v, hp:hp + 1, :]

    for hp in range(nhp):
        sl = slice(hp * LANES, (hp + 1) * LANES)
        r = shifted(hp)
        k = shifted(nhp + hp)
        v = shifted(2 * nhp + hp)
        g = jnp.dot(s_gd, gup_ref[:, sl], preferred_element_type=F32)
        kk0 = k * vec(VEC_KK, hp)
        ss = _head_sums_split(kk0 * kk0, mbd)
        kk = kk0 * lax.rsqrt(jnp.maximum(ss, L2_EPS))
        kdirs = []
        for wup_ref, aup_ref, v_w0, v_a0, q_k, q_a, lw_slot in (
                (wupf_ref, aupf_ref, VEC_W0F, VEC_A0F, RB_KF, RB_AF, 0),
                (wupb_ref, aupb_ref, VEC_W0B, VEC_A0B, RB_KB, RB_AB, 1)):
            wl = vec(v_w0, hp) + jnp.dot(t_wd, wup_ref[:, sl], preferred_element_type=F32)
            w_log = -_softplus(-wl) - 0.5
            lw_ref[0, lw_slot * nhp + hp] = -jnp.exp(w_log)
            ag = _sigmoid(vec(v_a0, hp) + jnp.dot(z_ad, aup_ref[:, sl], preferred_element_type=F32))
            kd = k * (1.0 + (ag - 1.0) * vec(VEC_KA, hp))
            rb_ref[0, q_k * nhp + hp] = kd.astype(BF16)
            rb_ref[0, q_a * nhp + hp] = ag.astype(BF16)
            kdirs.append(kd)
        bonus = _head_sums_split(r * kdirs[0] * vec(VEC_RK, hp), mbd) * v
        rb_ref[0, RB_R * nhp + hp] = r.astype(BF16)
        rb_ref[0, RB_V * nhp + hp] = v.astype(BF16)
        rb_ref[0, RB_KK * nhp + hp] = kk.astype(BF16)
        rb_ref[0, RB_G * nhp + hp] = g.astype(BF16)
        rb_ref[0, RB_BONUS * nhp + hp] = bonus.astype(BF16)


def _prep_call(p, sw, wupf, wupb, aupf, aupb, gup, vecs, mbd, nhp, tb=256):
    b, _, s, _ = p.shape
    n_t = s // tb
    halo = 16
    hb = tb // halo
    full = lambda a: pl.BlockSpec(a.shape, lambda bi, i: (0,) * a.ndim)
    return pl.pallas_call(
        functools.partial(_prep_kernel, tb=tb, n_t=n_t, nhp=nhp),
        grid=(b, n_t),
        in_specs=[pl.BlockSpec((1, N_SHIFT_BLOCKS, tb, LANES), lambda bi, i: (bi, 0, i, 0)),
                  pl.BlockSpec((1, N_SHIFT_BLOCKS, halo, LANES),
                               lambda bi, i: (bi, 0, jnp.maximum(i * hb - 1, 0), 0)),
                  pl.BlockSpec((1, N_SHIFT_BLOCKS, halo, LANES),
                               lambda bi, i: (bi, 0, jnp.minimum((i + 1) * hb, s // halo - 1), 0)),
                  full(sw), full(wupf), full(wupb), full(aupf), full(aupb), full(gup), full(vecs), full(mbd)],
        out_specs=[pl.BlockSpec((1, 9 * nhp, tb, LANES), lambda bi, i: (bi, 0, i, 0)),
                   pl.BlockSpec((1, 2 * nhp, tb, LANES), lambda bi, i: (bi, 0, i, 0))],
        out_shape=[jax.ShapeDtypeStruct((b, 9 * nhp, s, LANES), BF16),
                   jax.ShapeDtypeStruct((b, 2 * nhp, s, LANES), F32)],
        compiler_params=_cparams(("parallel", "parallel")),
        name="prep",
    )(p, p, p, sw, wupf, wupb, aupf, aupb, gup, vecs, mbd)


def _wkv_kernel(r_ref, v_ref, kk_ref, kf_ref, af_ref, lf_ref, rr_ref, vr_ref, kkr_ref, kb_ref, ab_ref, lb_ref,
                yf_ref, yb_ref, s_sc, *, L, nhp):
    @pl.when(pl.program_id(1) == 0)
    def _():
        s_sc[...] = jnp.zeros(s_sc.shape, F32)

    assert HEADS_PER_BLOCK == 2 and L == HEAD_DIM
    n2 = HEADS_PER_BLOCK * L
    ri = lax.broadcasted_iota(I32, (n2, LANES), 0)
    ci = lax.broadcasted_iota(I32, (n2, LANES), 1)
    head_match = (ri // L) == (ci // HEAD_DIM)
    st = lax.broadcasted_iota(I32, (L, LANES), 0)
    ss = lax.broadcasted_iota(I32, (L, LANES), 1) % L
    first_head = lax.broadcasted_iota(I32, (L, LANES), 1) < HEAD_DIM
    eye = (st == ss).astype(F32)
    strict = {False: st > ss, True: st < ss}
    incl = {False: st >= ss, True: st < ss}

    def stack(x):
        return jnp.where(head_match, jnp.concatenate([x] * HEADS_PER_BLOCK, axis=0), 0.0)

    def bdiag(x):
        return jnp.concatenate([jnp.where(first_head, x, 0.0), jnp.where(first_head, 0.0, x)], axis=0)

    chains = [(False, hp, r_ref, v_ref, kk_ref, kf_ref, af_ref, lf_ref) for hp in range(nhp)]
    chains += [(True, hp, rr_ref, vr_ref, kkr_ref, kb_ref, ab_ref, lb_ref) for hp in range(nhp)]
    n = len(chains)
    lws = [ch[7][0, ch[1]] for ch in chains]
    t_row = lax.broadcasted_iota(I32, (L, LANES), 0)

    def cumsum_time(x, rev):
        sh = 1
        while sh < L:
            if rev:
                x = x + jnp.where(t_row < L - sh, pltpu.roll(x, L - sh, 0), 0.0)
            else:
                x = x + jnp.where(t_row >= sh, pltpu.roll(x, sh, 0), 0.0)
            sh *= 2
        return x

    st = [dict() for _ in range(n)]

    def prologue(i):
        rev, hp, rr, vr, kkr, kdr, agr, _ = chains[i]
        r = rr[0, hp].astype(F32)
        kk = kkr[0, hp].astype(F32)
        kd = kdr[0, hp].astype(F32)
        ag = agr[0, hp].astype(F32)
        v = vr[0, hp].astype(F32)
        c_in = cumsum_time(lws[i], rev)
        e_ex = jnp.exp(c_in - lws[i])
        e_inv = jnp.exp(-c_in)
        if rev:
            e_r = e_ex
            p_last = jnp.exp(c_in[0:1])
        else:
            e_r = jnp.exp(c_in)
            p_last = jnp.exp(c_in[L - 1:L])
        b_t = kk * ag * e_inv
        k_t = kd * e_inv
        st[i].update(
            lhs=jnp.concatenate([-kk * e_ex, r * e_r], axis=0).astype(BF16),
            rhs=jnp.concatenate([stack(b_t), stack(k_t)], axis=0).astype(BF16),
            bk=jnp.concatenate([b_t * p_last, k_t * p_last], axis=0).astype(BF16),
            v=v.astype(BF16), vbd=bdiag(v).astype(BF16), p_last=p_last)

    def stage_scores(i):
        c = st[i]
        c['g'] = lax.dot_general(c['lhs'], c['rhs'], NT_DIMS, preferred_element_type=F32)
        c['s_old'] = s_sc[i]
        c['ls'] = lax.dot_general(c['lhs'], c['s_old'].astype(BF16), NT_DIMS, preferred_element_type=F32)

    def stage_solve_setup(i):
        c = st[i]
        rev, g = chains[i][0], c.pop('g')
        a_ab = jnp.where(strict[rev], g[0:L, 0:n2], 0.0)
        a_ak = jnp.where(strict[rev], g[0:L, n2:2 * n2], 0.0).astype(BF16)
        c['a_r'] = jnp.concatenate([jnp.where(incl[rev], g[L:2 * L, 0:n2], 0.0),
                                    jnp.where(incl[rev], g[L:2 * L, n2:2 * n2], 0.0)], axis=1).astype(BF16)
        c['x'] = c['ls'][0:L] + jnp.dot(a_ak, c['vbd'], preferred_element_type=F32)
        c['pw'] = jnp.dot(a_ab.astype(BF16), bdiag(a_ab).astype(BF16), preferred_element_type=F32)
        c['t'] = eye + a_ab

    def stage_round(i):
        c = st[i]
        both = jnp.dot(jnp.concatenate([c['pw'], c['t']], axis=0).astype(BF16), bdiag(c['pw']).astype(BF16),
                       preferred_element_type=F32)
        c['pw'] = both[0:L]
        c['t'] = c['t'] + both[L:2 * L]

    def stage_last_round(i):
        c = st[i]
        c['t'] = c['t'] + jnp.dot(c['t'].astype(BF16), bdiag(c.pop('pw')).astype(BF16),
                                  preferred_element_type=F32)

    def stage_solve(i):
        c = st[i]
        c['u'] = jnp.dot(c.pop('t').astype(BF16), bdiag(c.pop('x')).astype(BF16), preferred_element_type=F32)

    def stage_out(i):
        c = st[i]
        rev, hp = chains[i][0], chains[i][1]
        u = c['u']
        y = c['ls'][L:2 * L] + jnp.dot(c['a_r'], jnp.concatenate([bdiag(u).astype(BF16), c['vbd']], axis=0),
                                       preferred_element_type=F32)
        upd = lax.dot_general(jnp.concatenate([u.astype(BF16), c['v']], axis=0), c['bk'], TN_DIMS,
                              preferred_element_type=F32)
        (yb_ref if rev else yf_ref)[0, hp] = y
        s_sc[i] = c['s_old'] * c['p_last'] + jnp.where(head_match, upd, 0.0)

    rounds = L.bit_length() - 1
    stages = ([stage_scores, stage_solve_setup] + [stage_round] * (rounds - 2)
              + [stage_last_round, stage_solve, stage_out])
    for i in range(n):
        prologue(i)
    for stage in stages:
        for i in range(n):
            stage(i)


def _wkv_call(rb, lw, nhp, L=WKV_CHUNK):
    b, _, s, _ = rb.shape
    nc = s // L
    fwd = lambda q: pl.BlockSpec((1, nhp, L, LANES), lambda bi, c: (bi, q, c, 0))
    bwd = lambda q: pl.BlockSpec((1, nhp, L, LANES), lambda bi, c: (bi, q, nc - 1 - c, 0))
    y_shape = jax.ShapeDtypeStruct((b, nhp, s, LANES), F32)
    return pl.pallas_call(
        functools.partial(_wkv_kernel, L=L, nhp=nhp),
        grid=(b, nc),
        in_specs=[fwd(RB_R), fwd(RB_V), fwd(RB_KK), fwd(RB_KF), fwd(RB_AF), fwd(0),
                  bwd(RB_R), bwd(RB_V), bwd(RB_KK), bwd(RB_KB), bwd(RB_AB), bwd(1)],
        out_specs=[fwd(0), bwd(0)],
        out_shape=[y_shape, y_shape],
        scratch_shapes=[pltpu.VMEM((2 * nhp, LANES, LANES), F32)],
        compiler_params=_cparams(("parallel", "arbitrary")),
        name="wkv",
    )(rb, rb, rb, rb, rb, lw, rb, rb, rb, rb, rb, lw)


def _post_kernel(yf_ref, yb_ref, bonus_ref, g_ref, lnw_ref, lnb_ref, mbd_ref, o_ref, *, nhp):
    mbd = mbd_ref[...]
    inv_n = 1.0 / HEAD_DIM
    for hp in range(nhp):
        y = yf_ref[0, hp] + yb_ref[0, hp]
        mu = _head_sums_split(y, mbd) * inv_n
        d = y - mu
        var = _head_sums_split(d * d, mbd) * inv_n
        yn = d * lax.rsqrt(var + GN_EPS) * lnw_ref[hp:hp + 1, :] + lnb_ref[hp:hp + 1, :]
        out = (yn + bonus_ref[0, hp].astype(F32)) * g_ref[0, hp].astype(F32)
        o_ref[0, :, hp * LANES:(hp + 1) * LANES] = out.astype(o_ref.dtype)


def _post_call(yf, yb, rb, lnw, lnb, mbd, nhp, tb=256):
    b, _, s, _ = yf.shape
    yblk = pl.BlockSpec((1, nhp, tb, LANES), lambda bi, i: (bi, 0, i, 0))
    rblk = lambda q: pl.BlockSpec((1, nhp, tb, LANES), lambda bi, i: (bi, q, i, 0))
    full = lambda a: pl.BlockSpec(a.shape, lambda bi, i: (0,) * a.ndim)
    return pl.pallas_call(
        functools.partial(_post_kernel, nhp=nhp),
        grid=(b, s // tb),
        in_specs=[yblk, yblk, rblk(RB_BONUS), rblk(RB_G), full(lnw), full(lnb), full(mbd)],
        out_specs=pl.BlockSpec((1, tb, nhp * LANES), lambda bi, i: (bi, i, 0)),
        out_shape=jax.ShapeDtypeStruct((b, s, nhp * LANES), BF16),
        compiler_params=_cparams(("parallel", "parallel")),
        name="post",
    )(yf, yb, rb, rb, lnw, lnb, mbd)


def _na_kernel(q_ref, k_ref, v_ref, bias_ref, qw_ref, kw_ref, mbd_ref, o_ref, kn_sc, *, rq, rows):
    rbi = pl.program_id(2)
    mbd = mbd_ref[...]
    inv_n = 1.0 / HEAD_DIM
    span = WIN_R * GRID_W
    s_len = kn_sc.shape[0]

    @pl.when(rbi == 0)
    def _():
        def body(c, carry):
            off = pl.multiple_of(c * span, span)
            k = k_ref[0, 0, pl.ds(off, span), :].astype(F32)
            ms = _head_sums_split(k * k, mbd) * inv_n
            kn_sc[pl.ds(off, span), :] = (k * lax.rsqrt(ms + NORM_EPS) * kw_ref[...]).astype(BF16)
            return carry
        lax.fori_loop(0, s_len // span, body, 0)

    ri = lax.broadcasted_iota(I32, (HEADS_PER_BLOCK * GRID_W, LANES), 0)
    ci = lax.broadcasted_iota(I32, (HEADS_PER_BLOCK * GRID_W, LANES), 1)
    head_match = (ri // GRID_W) == (ci // HEAD_DIM)
    lane = lax.broadcasted_iota(I32, (GRID_W, LANES), 1)
    scale = HEAD_DIM ** -0.5
    q_all = q_ref[0, 0].astype(F32)
    ms = _head_sums_split(q_all * q_all, mbd) * inv_n
    qn_all = q_all * lax.rsqrt(ms + NORM_EPS) * qw_ref[...] * scale
    starts, offs = [], []
    for qi in range(rq):
        i = rbi * rq + qi
        rs = jnp.clip(i - WIN_R // 2, 0, rows - WIN_R)
        offs.append(rs - i + WIN_R - 1)
        starts.append(pl.multiple_of(rs * GRID_W, GRID_W))
    qs_l = [jnp.where(head_match, jnp.concatenate([qn_all[qi * GRID_W:(qi + 1) * GRID_W]] * HEADS_PER_BLOCK,
                                                  axis=0), 0.0).astype(BF16) for qi in range(rq)]
    def bias(off):
        return jnp.concatenate([bias_ref[0, off + 2 * i] for i in range(WIN_R // 2)], axis=1)

    s_l = [lax.dot_general(qs_l[qi], kn_sc[pl.ds(starts[qi], span), :], NT_DIMS, preferred_element_type=F32)
           + bias(offs[qi]) for qi in range(rq)]
    p_l, l_l = [], []
    for s in s_l:
        p = jnp.exp(s - jnp.max(s, axis=-1, keepdims=True))
        p_l.append(p.astype(BF16))
        l_l.append(jnp.sum(p, axis=-1, keepdims=True))
    o_l = [jnp.dot(p_l[qi], v_ref[0, 0, pl.ds(starts[qi], span), :], preferred_element_type=F32) / l_l[qi]
           for qi in range(rq)]
    for qi, o in enumerate(o_l):
        out = o[0:GRID_W]
        for h in range(1, HEADS_PER_BLOCK):
            out = jnp.where(lane // HEAD_DIM == h, o[h * GRID_W:(h + 1) * GRID_W], out)
        o_ref[0, qi * GRID_W:(qi + 1) * GRID_W, :] = out.astype(o_ref.dtype)


def _na_call(p, bias, qw, kw, mbd, q_blk0, nhp, rq=16):
    b, _, s, _ = p.shape
    rows = s // GRID_W
    rq = min(rq, rows)
    full = lambda a: pl.BlockSpec(a.shape, lambda bi, hp, r: (0,) * a.ndim)
    return pl.pallas_call(
        functools.partial(_na_kernel, rq=rq, rows=rows),
        grid=(b, nhp, rows // rq),
        in_specs=[pl.BlockSpec((1, 1, rq * GRID_W, LANES), lambda bi, hp, r: (bi, q_blk0 + hp, r, 0)),
                  pl.BlockSpec((1, 1, s, LANES), lambda bi, hp, r: (bi, q_blk0 + nhp + hp, 0, 0)),
                  pl.BlockSpec((1, 1, s, LANES), lambda bi, hp, r: (bi, q_blk0 + 2 * nhp + hp, 0, 0)),
                  pl.BlockSpec((1,) + bias.shape[1:], lambda bi, hp, r: (hp, 0, 0, 0)),
                  full(qw), full(kw), full(mbd)],
        out_specs=pl.BlockSpec((1, rq * GRID_W, LANES), lambda bi, hp, r: (bi, r, hp)),
        out_shape=jax.ShapeDtypeStruct((b, s, nhp * LANES), BF16),
        scratch_shapes=[pltpu.VMEM((s, LANES), BF16)],
        compiler_params=_cparams(("parallel", "parallel", "arbitrary")),
        name="na",
    )(p, p, p, bias, qw, kw, mbd)


def _na_bias_table(rel_bias):
    h = rel_bias.shape[0]
    qc = np.arange(GRID_W)
    kc = np.arange(GRID_W)
    win0 = np.clip(qc - WIN_C // 2, 0, GRID_W - WIN_C)
    valid = (kc[None, :] >= win0[:, None]) & (kc[None, :] < win0[:, None] + WIN_C)
    coff = np.clip(kc[None, :] - qc[:, None] + WIN_C - 1, 0, 2 * WIN_C - 2)
    n_row, n_col = 2 * WIN_R - 1, 2 * WIN_C - 1
    n_ro = n_row - 1
    sel = np.zeros((n_ro, 2, n_row), np.float32)
    for ro in range(n_ro):
        for part in range(2):
            sel[ro, part, ro + part] = 1.0
    pick = np.zeros((2, n_col, GRID_W, 2 * GRID_W), np.float32)
    for part in range(2):
        pick[part, coff, qc[:, None], part * GRID_W + kc[None, :]] = 1.0
    a = jnp.einsum('hrc,apr->hapc', rel_bias.astype(F32), jnp.asarray(sel), precision=HI)
    a = a.reshape(h // HEADS_PER_BLOCK, HEADS_PER_BLOCK, n_ro, 2 * n_col).transpose(0, 2, 1, 3)
    t = jnp.einsum('ahbx,xql->ahbql', a, jnp.asarray(pick.reshape(2 * n_col, GRID_W, 2 * GRID_W)),
                   precision=HI)
    t = jnp.where(np.tile(valid, (1, 2)), t, NEG_BIG)
    return t.reshape(h // HEADS_PER_BLOCK, n_ro, HEADS_PER_BLOCK * GRID_W, 2 * GRID_W)


def _outproj_kernel(yr_ref, yn_ref, x_ref, mod_ref, nw_ref, w1_ref, w2_ref, wr_ref, x1_ref, h2_ref, h2row_ref,
                    lg_ref):
    tm = x_ref.shape[1]
    sub = LANES
    tiles = [slice(i * sub, (i + 1) * sub) for i in range(tm // sub)]
    accs = [jnp.dot(yr_ref[0, r], w1_ref[...], preferred_element_type=F32)
            + jnp.dot(yn_ref[0, r], w2_ref[...], preferred_element_type=F32) for r in tiles]
    h2s = []
    for r, acc in zip(tiles, accs):
        x1 = x_ref[0, r] + mod_ref[0, 2:3, :] * acc
        x1_ref[0, r] = x1
        ms = jnp.mean(x1 * x1, axis=-1, keepdims=True)
        y = x1 * lax.rsqrt(ms + NORM_EPS) * nw_ref[...]
        h2 = y * (1.0 + mod_ref[0, 4:5, :]) + mod_ref[0, 3:4, :]
        packed = _pack_halves(h2)
        h2row_ref[0, r] = packed
        for j in range(h2_ref.shape[2]):
            h2_ref[0, r, j, :] = packed[:, j * LANES:(j + 1) * LANES]
        h2s.append(h2)
    wr = wr_ref[...]
    wr_hi = wr.astype(BF16)
    wr_lo = (wr - wr_hi.astype(F32)).astype(BF16)
    for r, h2 in zip(tiles, h2s):
        h_hi = h2.astype(BF16)
        h_lo = (h2 - h_hi.astype(F32)).astype(BF16)
        lg_ref[:, r] = (lax.dot_general(wr_hi, h_hi, NT_DIMS, preferred_element_type=F32)
                        + lax.dot_general(wr_hi, h_lo, NT_DIMS, preferred_element_type=F32)
                        + lax.dot_general(wr_lo, h_hi, NT_DIMS, preferred_element_type=F32))


def _outproj_call(yr, yn, x, mod3, norm_w, w_o, wr_t, tm=512):
    b, s, d = x.shape
    dh = yr.shape[-1]
    assert w_o.shape == (2 * dh, d)
    ne = wr_t.shape[0]
    nt = s // tm
    full = lambda a: pl.BlockSpec(a.shape, lambda bi, i: (0,) * a.ndim)
    return pl.pallas_call(
        _outproj_kernel,
        grid=(b, nt),
        in_specs=[pl.BlockSpec((1, tm, dh), lambda bi, i: (bi, i, 0)),
                  pl.BlockSpec((1, tm, dh), lambda bi, i: (bi, i, 0)),
                  pl.BlockSpec((1, tm, d), lambda bi, i: (bi, i, 0)),
                  pl.BlockSpec((1, 6, d), lambda bi, i: (bi, 0, 0)),
                  full(norm_w),
                  pl.BlockSpec((dh, d), lambda bi, i: (0, 0)),
                  pl.BlockSpec((dh, d), lambda bi, i: (1, 0)),
                  full(wr_t)],
        out_specs=[pl.BlockSpec((1, tm, d), lambda bi, i: (bi, i, 0)),
                   pl.BlockSpec((1, tm, d // 2 // LANES, LANES), lambda bi, i: (bi, i, 0, 0)),
                   pl.BlockSpec((1, tm, d // 2), lambda bi, i: (bi, i, 0)),
                   pl.BlockSpec((ne, tm), lambda bi, i: (0, bi * nt + i))],
        out_shape=[jax.ShapeDtypeStruct((b, s, d), F32),
                   jax.ShapeDtypeStruct((b, s, d // 2 // LANES, LANES), U32),
                   jax.ShapeDtypeStruct((b, s, d // 2), U32),
                   jax.ShapeDtypeStruct((ne, b * s), F32)],
        compiler_params=_cparams(("parallel", "parallel")),
        name="outproj",
    )(yr, yn, x, mod3, norm_w, w_o, w_o, wr_t)


def _first_argmax(x, idx, n):
    m = jnp.max(x, axis=0, keepdims=True)
    a = jnp.min(jnp.where(x == m, idx, n), axis=0, keepdims=True)
    return m, a


def _route_kernel(lg_ref, bias_ref, idx_ref, gw_ref, pos_ref, cnt_ref, *, tr):
    @pl.when(pl.program_id(0) == 0)
    def _():
        cnt_ref[...] = jnp.zeros(cnt_ref.shape, F32)

    gsz = N_EXPERTS // N_GROUPS
    scores = _sigmoid(lg_ref[...])
    biased = scores + bias_ref[:, 0:1]
    ig = lax.broadcasted_iota(I32, (gsz, tr), 0)
    grp_rows = []
    for g in range(N_GROUPS):
        blk = biased[g * gsz:(g + 1) * gsz]
        m1, a1 = _first_argmax(blk, ig, gsz)
        m2 = jnp.max(jnp.where(ig == a1, -jnp.inf, blk), axis=0, keepdims=True)
        grp_rows.append(m1 + m2)
    grp = jnp.concatenate(grp_rows, axis=0)
    ign = lax.broadcasted_iota(I32, (N_GROUPS, tr), 0)
    sel = jnp.zeros((N_GROUPS, tr), jnp.bool_)
    for _ in range(TOPK_GROUPS):
        _, a = _first_argmax(grp, ign, N_GROUPS)
        hit = ign == a
        sel = jnp.logical_or(sel, hit)
        grp = jnp.where(hit, -jnp.inf, grp)
    masked = jnp.concatenate(
        [jnp.where(sel[g:g + 1], biased[g * gsz:(g + 1) * gsz], -jnp.inf) for g in range(N_GROUPS)], axis=0)
    ie = lax.broadcasted_iota(I32, (N_EXPERTS, tr), 0)
    picks, pick_scores = [], []
    onehot = jnp.zeros((N_EXPERTS, tr), F32)
    for _ in range(TOP_K):
        _, a = _first_argmax(masked, ie, N_EXPERTS)
        hit = ie == a
        picks.append(a)
        pick_scores.append(jnp.sum(jnp.where(hit, scores, 0.0), axis=0, keepdims=True))
        onehot = onehot + hit.astype(F32)
        masked = jnp.where(hit, -jnp.inf, masked)
    total = pick_scores[0]
    for sc in pick_scores[1:]:
        total = total + sc
    t0 = lax.broadcasted_iota(I32, (tr, tr), 0)
    t1 = lax.broadcasted_iota(I32, (tr, tr), 1)
    before = (t0 < t1).astype(BF16)
    rank = jnp.dot(onehot.astype(BF16), before, preferred_element_type=F32) + cnt_ref[:, 0:1]
    zero_i = jnp.zeros((1, tr), I32)
    zero_f = jnp.zeros((1, tr), F32)
    for k in range(8):
        if k < TOP_K:
            idx_ref[k:k + 1, :] = picks[k]
            gw_ref[k:k + 1, :] = pick_scores[k] / total * ROUTED_SCALE
            pos = jnp.sum(jnp.where(ie == picks[k], rank, 0.0), axis=0, keepdims=True)
            pos_ref[k:k + 1, :] = pos.astype(I32)
        else:
            idx_ref[k:k + 1, :] = zero_i
            gw_ref[k:k + 1, :] = zero_f
            pos_ref[k:k + 1, :] = zero_i
    cnt_ref[...] = cnt_ref[...] + jnp.sum(onehot, axis=1, keepdims=True)


def _route_call(lg_t, bias2, tr=512):
    ne, t = lg_t.shape
    tr = min(tr, t)
    tok = pl.BlockSpec((8, tr), lambda i: (0, i))
    return pl.pallas_call(
        functools.partial(_route_kernel, tr=tr),
        grid=(t // tr,),
        in_specs=[pl.BlockSpec((ne, tr), lambda i: (0, i)),
                  pl.BlockSpec(bias2.shape, lambda i: (0, 0))],
        out_specs=[tok, tok, tok, pl.BlockSpec((ne, LANES), lambda i: (0, 0))],
        out_shape=[jax.ShapeDtypeStruct((8, t), I32), jax.ShapeDtypeStruct((8, t), F32),
                   jax.ShapeDtypeStruct((8, t), I32), jax.ShapeDtypeStruct((ne, LANES), F32)],
        compiler_params=_cparams(("arbitrary",)),
        name="route",
    )(lg_t, bias2)


FILL_CHUNKS = (1, 8, 64)
ZERO_ROWS = FILL_CHUNKS[-1]


def _dispatch_kernel(slot_ref, fill_lo_ref, fill_hi_ref, h_ref, xs_ref, zrow, sem, *, td):
    def row_copy(t, slot):
        return pltpu.make_async_copy(h_ref.at[t], xs_ref.at[slot], sem)

    def start(t, carry):
        for k in range(TOP_K):
            row_copy(t, slot_ref[t * SLOT_STRIDE + k]).start(priority=k % 2)
        return carry

    lax.fori_loop(0, td, start, 0)
    for k in range(TOP_K):
        pltpu.make_async_copy(h_ref, xs_ref.at[pl.ds(0, td)], sem).wait()

    @pl.when(pl.program_id(0) == pl.num_programs(0) - 1)
    def _():
        zrow[...] = jnp.zeros(zrow.shape, zrow.dtype)

        def zero_copy(row, n):
            if n > 1:
                row = pl.multiple_of(row, FILL_CHUNKS[1])
            return pltpu.make_async_copy(zrow.at[pl.ds(0, n)], xs_ref.at[pl.ds(row, n)], sem)

        def for_each_chunk(r, fn):
            lo, hi = fill_lo_ref[r], fill_hi_ref[r]
            for ci, n in enumerate(FILL_CHUNKS):
                if ci + 1 < len(FILL_CHUNKS):
                    count = jnp.minimum(((-lo) % FILL_CHUNKS[ci + 1]) // n, (hi - lo) // n)
                else:
                    count = (hi - lo) // n
                fn(lo, n, count)
                lo = lo + count * n

        def start_range(r, carry):
            def go(lo, n, count):
                lax.fori_loop(0, count, lambda i, c: (zero_copy(lo + i * n, n).start(), c)[1], 0)
            for_each_chunk(r, go)
            return carry

        def wait_range(r, carry):
            def go(lo, n, count):
                lax.fori_loop(0, count, lambda i, c: (zero_copy(0, n).wait(), c)[1], 0)
            for_each_chunk(r, go)
            return carry

        n_ranges = fill_lo_ref.shape[0]
        lax.fori_loop(0, n_ranges, start_range, 0)
        lax.fori_loop(0, n_ranges, wait_range, 0)


def _dispatch_call(slots_flat, fill_lo, fill_hi, h2, n_rows, td=512):
    t, nj, _ = h2.shape
    smem = pl.BlockSpec(memory_space=pltpu.SMEM)
    return pl.pallas_call(
        functools.partial(_dispatch_kernel, td=td),
        grid=(t // td,),
        in_specs=[pl.BlockSpec((td * SLOT_STRIDE,), lambda i: (i,), memory_space=pltpu.SMEM),
                  smem, smem,
                  pl.BlockSpec((td, nj, LANES), lambda i: (i, 0, 0))],
        out_specs=pl.BlockSpec(memory_space=pl.ANY),
        out_shape=jax.ShapeDtypeStruct((n_rows, nj, LANES), h2.dtype),
        scratch_shapes=[pltpu.VMEM((ZERO_ROWS, nj, LANES), h2.dtype), pltpu.SemaphoreType.DMA(())],
        compiler_params=_cparams(("arbitrary",)),
        name="dispatch",
    )(slots_flat, fill_lo, fill_hi, h2)


def _experts_kernel(be_ref, first_ref, nxt_ref, par_ref, rows_ref, nv_ref, xs_hbm, wg_hbm, wu_hbm, wd_hbm, ys_ref,
                    wg_buf, wu_buf, wd_buf, wg_sc, wu_sc, wd_sc, xpack, sem, isem):
    nj = xs_hbm.shape[1]
    b = pl.program_id(0)
    m = xpack.shape[1]
    half = wg_sc.shape[0] // 2
    oslot = lax.rem(b, 2)

    def in_copies(slot, row0):
        return [pltpu.make_async_copy(xs_hbm.at[pl.ds(row0, m), j],
                                      xpack.at[slot, :, pl.ds(j * LANES, LANES)], isem.at[slot]) for j in range(nj)]

    @pl.when(b == 0)
    def _():
        for c in in_copies(0, 0):
            c.start()

    @pl.when(b + 1 < nv_ref[0])
    def _():
        for c in in_copies(1 - oslot, pl.multiple_of((b + 1) * m, m)):
            c.start()

    def weight_copies(e, slot):
        return [pltpu.make_async_copy(src.at[e], dst.at[slot], sem.at[slot, i])
                for i, (src, dst) in enumerate(((wg_hbm, wg_buf), (wu_hbm, wu_buf), (wd_hbm, wd_buf)))]

    def compute(nrows):
        x_lo, x_hi = _unpack_halves(xpack[oslot, 0:nrows])
        x_lo, x_hi = x_lo.astype(BF16), x_hi.astype(BF16)
        g = (jnp.dot(x_lo, wg_sc[0:half], preferred_element_type=F32)
             + jnp.dot(x_hi, wg_sc[half:], preferred_element_type=F32))
        u = (jnp.dot(x_lo, wu_sc[0:half], preferred_element_type=F32)
             + jnp.dot(x_hi, wu_sc[half:], preferred_element_type=F32))
        a = (_silu(g) * u).astype(BF16)
        ys_ref[0:nrows] = _pack_halves(jnp.dot(a, wd_sc[...], preferred_element_type=F32))

    @pl.when(b < nv_ref[0])
    def _():
        for c in in_copies(oslot, 0):
            c.wait()

        @pl.when(b == 0)
        def _():
            for c in weight_copies(be_ref[0], par_ref[0]):
                c.start()

        @pl.when(first_ref[b] == 1)
        def _():
            slot = par_ref[b]
            for c in weight_copies(be_ref[b], slot):
                c.wait()

            @pl.when(nxt_ref[b] >= 0)
            def _():
                for c in weight_copies(nxt_ref[b], 1 - slot):
                    c.start()

            wg_sc[...] = wg_buf[slot].astype(BF16)
            wu_sc[...] = wu_buf[slot].astype(BF16)
            wd_sc[...] = wd_buf[slot].astype(BF16)

        quarter = m // EXPERT_BLOCK_PARTS
        for part in range(1, EXPERT_BLOCK_PARTS + 1):
            nrows = part * quarter
            if part == 1:
                cond = rows_ref[b] <= nrows
            elif part == EXPERT_BLOCK_PARTS:
                cond = rows_ref[b] > nrows - quarter
            else:
                cond = jnp.logical_and(rows_ref[b] > nrows - quarter, rows_ref[b] <= nrows)

            @pl.when(cond)
            def _(nrows=nrows):
                compute(nrows)
                if nrows < m:
                    ys_ref[nrows:] = jnp.zeros((m - nrows, ys_ref.shape[1]), ys_ref.dtype)

    @pl.when(b >= nv_ref[0])
    def _():
        ys_ref[...] = jnp.zeros(ys_ref.shape, ys_ref.dtype)


def _experts_call(block_e, first, nxt, par, rows_valid, n_valid, xs, wg, wu, wd):
    total, nj, _ = xs.shape
    d = wg.shape[1]
    m = DISPATCH_BLOCK
    ff = wg.shape[-1]

    hbm = pl.BlockSpec(memory_space=pl.ANY)
    grid_spec = pltpu.PrefetchScalarGridSpec(
        num_scalar_prefetch=6,
        grid=(total // m,),
        in_specs=[hbm, hbm, hbm, hbm],
        out_specs=pl.BlockSpec((m, nj * LANES), lambda b, *s: (b, 0)),
        scratch_shapes=[pltpu.VMEM((2, d, ff), F32), pltpu.VMEM((2, d, ff), F32), pltpu.VMEM((2, ff, d), F32),
                        pltpu.VMEM((d, ff), BF16), pltpu.VMEM((d, ff), BF16), pltpu.VMEM((ff, d), BF16),
                        pltpu.VMEM((2, m, nj * LANES), U32),
                        pltpu.SemaphoreType.DMA((2, 3)), pltpu.SemaphoreType.DMA((2,))],
    )
    return pl.pallas_call(
        _experts_kernel,
        grid_spec=grid_spec,
        out_shape=jax.ShapeDtypeStruct((total, nj * LANES), U32),
        compiler_params=_cparams(("arbitrary",)),
        name="experts",
    )(block_e, first, nxt, par, rows_valid, n_valid, xs, wg, wu, wd)


COMBINE_INLINE_ROWS = 128


def _combine_kernel(slot_ref, x1_ref, h2_ref, gw_ref, mod_ref, wg_ref, wu_ref, wd_ref, ys_ref, o_ref,
                    buf, sem, *, tc):
    def row_copy(t, k, slot):
        return pltpu.make_async_copy(ys_ref.at[pl.ds(slot, 1)], buf.at[k, pl.ds(t, 1)], sem)

    def start(t, carry):
        for k in range(TOP_K):
            row_copy(t, k, slot_ref[t * SLOT_STRIDE + k]).start(priority=k % 2)
        return carry

    inline = min(COMBINE_INLINE_ROWS, tc)
    lax.fori_loop(0, tc - inline, start, 0)
    for t in range(tc - inline, tc):
        start(t, 0)
    half = wg_ref.shape[0] // 2
    h_lo, h_hi = _unpack_halves(h2_ref[0])
    h_lo, h_hi = h_lo.astype(BF16), h_hi.astype(BF16)
    g = (jnp.dot(h_lo, wg_ref[0:half], preferred_element_type=F32)
         + jnp.dot(h_hi, wg_ref[half:], preferred_element_type=F32))
    u = (jnp.dot(h_lo, wu_ref[0:half], preferred_element_type=F32)
         + jnp.dot(h_hi, wu_ref[half:], preferred_element_type=F32))
    acc = jnp.dot((_silu(g) * u).astype(BF16), wd_ref[...], preferred_element_type=F32)
    for k in range(TOP_K):
        pltpu.make_async_copy(ys_ref.at[pl.ds(0, tc)], buf.at[k], sem).wait()
    gw = gw_ref[...]
    acc_lo, acc_hi = acc[:, :half], acc[:, half:]
    for k in range(TOP_K):
        y_lo, y_hi = _unpack_halves(buf[k])
        acc_lo = acc_lo + gw[:, k:k + 1] * y_lo
        acc_hi = acc_hi + gw[:, k:k + 1] * y_hi
    gate = mod_ref[0, 5:6, :]
    o_ref[0, :, 0:half] = x1_ref[0, :, 0:half] + gate[:, :half] * acc_lo
    o_ref[0, :, half:] = x1_ref[0, :, half:] + gate[:, half:] * acc_hi


def _combine_call(slots_flat, x1, h2, gw_tok, mod3, wg, wu, wd, ys, tc=512):
    b, s, d = x1.shape
    nt = s // tc
    full = lambda a: pl.BlockSpec(a.shape, lambda bi, i: (0,) * a.ndim)
    return pl.pallas_call(
        functools.partial(_combine_kernel, tc=tc),
        grid=(b, nt),
        in_specs=[pl.BlockSpec((tc * SLOT_STRIDE,), lambda bi, i: (bi * nt + i,), memory_space=pltpu.SMEM),
                  pl.BlockSpec((1, tc, d), lambda bi, i: (bi, i, 0)),
                  pl.BlockSpec((1, tc, d // 2), lambda bi, i: (bi, i, 0)),
                  pl.BlockSpec((tc, 8), lambda bi, i: (bi * nt + i, 0)),
                  pl.BlockSpec((1, 6, d), lambda bi, i: (bi, 0, 0)),
                  full(wg), full(wu), full(wd),
                  pl.BlockSpec(memory_space=pl.ANY)],
        out_specs=pl.BlockSpec((1, tc, d), lambda bi, i: (bi, i, 0)),
        out_shape=jax.ShapeDtypeStruct((b, s, d), F32),
        scratch_shapes=[pltpu.VMEM((TOP_K, tc, d // 2), U32), pltpu.SemaphoreType.DMA(())],
        compiler_params=_cparams(("arbitrary", "arbitrary")),
        name="combine",
    )(slots_flat, x1, h2, gw_tok, mod3, wg, wu, wd, ys)


def _pad_rows(w, rows, at=0):
    out = jnp.zeros((rows, w.shape[1]), w.dtype)
    return out.at[at:at + w.shape[0]].set(w)


def _mixing_stage(p, rwkv_shift, w0_f, w_up_f, w0_b, w_up_b, a0_f, a_up_f, a0_b, a_up_b, g_up,
                  k_k, k_a, r_k, ln_x_w, ln_x_b, q_norm_w, k_norm_w, rel_bias):
    d_rwkv = w0_f.shape[0]
    nhp = d_rwkv // LANES
    mbd = _head_sum_matrix()
    n_shift = rwkv_shift.shape[1]
    sw = jnp.pad(rwkv_shift, ((0, 0), (0, N_SHIFT_BLOCKS * LANES - n_shift))).reshape(3, N_SHIFT_BLOCKS, LANES)
    wupf = _pad_rows(w_up_f, LANES, 0).astype(BF16)
    wupb = _pad_rows(w_up_b, LANES, DECAY_LORA).astype(BF16)
    aupf = _pad_rows(a_up_f, LANES, 0).astype(BF16)
    aupb = _pad_rows(a_up_b, LANES, AAA_LORA).astype(BF16)
    gup = _pad_rows(g_up, 2 * LANES, 0).astype(BF16)
    vecs = jnp.stack([w0_f, w0_b, a0_f, a0_b, k_k, k_a, r_k.reshape(-1)]).reshape(7, nhp, LANES)
    rb, lw = _prep_call(p, sw, wupf, wupb, aupf, aupb, gup, vecs, mbd, nhp)
    y_f, y_b = _wkv_call(rb, lw, nhp)
    y_rwkv = _post_call(y_f, y_b, rb, ln_x_w.reshape(nhp, LANES), ln_x_b.reshape(nhp, LANES), mbd, nhp)
    bias = _na_bias_table(rel_bias)
    qw = jnp.tile(q_norm_w, HEADS_PER_BLOCK).reshape(1, LANES)
    kw = jnp.tile(k_norm_w, HEADS_PER_BLOCK).reshape(1, LANES)
    y_na = _na_call(p, bias, qw, kw, mbd, N_SHIFT_BLOCKS, nhp)
    return y_rwkv, y_na


def _moe_stage(x1, h2_tiles, h2_rows, lg_t, mod3, router_bias, w_gate_e, w_up_e, w_down_e, w_gate_s, w_up_s,
               w_down_s):
    b, s, d = x1.shape
    t = b * s
    m = DISPATCH_BLOCK
    bias2 = jnp.broadcast_to(router_bias.astype(F32)[:, None], (N_EXPERTS, LANES))
    idx_t, gw_t, pos_t, cnt = _route_call(lg_t, bias2)
    counts = cnt[:, 0].astype(I32)
    padded = (counts + m - 1) // m * m
    pad_end = jnp.cumsum(padded)
    pad_start = pad_end - padded
    n_blocks = -(-(t * TOP_K + N_EXPERTS * m) // m)
    n_valid = (pad_end[-1] // m).astype(I32).reshape(1)
    block_row0 = jnp.arange(n_blocks, dtype=I32) * m
    block_e = jnp.minimum(jnp.sum((pad_end[None, :] <= block_row0[:, None]).astype(I32), axis=1),
                          N_EXPERTS - 1).astype(I32)
    assert m % ZERO_ROWS == 0
    expert_ids = jnp.arange(N_EXPERTS, dtype=I32)
    start_of = jnp.sum(jnp.where(idx_t[:, :, None] == expert_ids, pad_start, 0), axis=-1)
    slots = (start_of + pos_t).astype(I32)
    slots_flat = slots.T.reshape(-1)
    gw_tok = gw_t.T
    fill_lo = jnp.concatenate([pad_start + counts, pad_end[-1:]]).astype(I32)
    fill_hi = jnp.concatenate([pad_end, jnp.full((1,), n_blocks * m, I32)]).astype(I32)
    xs = _dispatch_call(slots_flat, fill_lo, fill_hi, h2_tiles.reshape(t, *h2_tiles.shape[2:]), n_blocks * m)
    row_end = (pad_start + counts).astype(I32)
    nonempty = counts > 0
    first = jnp.logical_and(block_row0 == pad_start[block_e], block_row0 < pad_end[-1]).astype(I32)
    ordinal = jnp.cumsum(nonempty.astype(I32)) - 1
    par = (ordinal[block_e] & 1).astype(I32)
    cand = jnp.where(nonempty, expert_ids, N_EXPERTS)
    later = jnp.concatenate([lax.cummin(cand[::-1])[::-1][1:], jnp.full((1,), N_EXPERTS, I32)])
    nxt = jnp.where(later < N_EXPERTS, later, -1)[block_e].astype(I32)
    rows_valid = jnp.clip(row_end[block_e] - block_row0, 0, m).astype(I32)
    ys = _experts_call(block_e, first, nxt, par, rows_valid, n_valid, xs, w_gate_e, w_up_e, w_down_e)
    return _combine_call(slots_flat, x1, h2_rows, gw_tok, mod3, w_gate_s.astype(BF16), w_up_s.astype(BF16),
                         w_down_s.astype(BF16), ys)


def kernel(x, c, w_ada, b_ada, norm1_w, w_in, rwkv_shift, w0_f, w_up_f, w0_b, w_up_b, a0_f, a_up_f, a0_b,
           a_up_b, g_up, k_k, k_a, r_k, ln_x_w, ln_x_b, q_norm_w, k_norm_w, rel_bias, w_out, norm2_w,
           w_router, router_bias, w_gate_e, w_up_e, w_down_e, w_gate_s, w_up_s, w_down_s):
    bn, sn, d = x.shape
    depth = w_ada.shape[0]
    for l in range(depth):
        c_pad = jnp.pad(c, ((0, 8 - bn % 8 if bn % 8 else 0), (0, 0)))
        mod = _mod_call(c_pad, w_ada[l], b_ada[l].reshape(1, -1))[:bn]
        mod3 = mod.reshape(bn, 6, d)
        rwkv_in = rwkv_shift.shape[-1]
        pad_cols = N_SHIFT_BLOCKS * LANES - rwkv_in
        w_in_r = _widen_call(w_in, l, rwkv_in, pad_cols)
        p = _proj_call(x, mod3, norm1_w[l].reshape(1, d), w_in_r)
        y_rwkv, y_na = _mixing_stage(p, rwkv_shift[l], w0_f[l], w_up_f[l], w0_b[l], w_up_b[l], a0_f[l],
                                     a_up_f[l], a0_b[l], a_up_b[l], g_up[l], k_k[l], k_a[l], r_k[l],
                                     ln_x_w[l], ln_x_b[l], q_norm_w[l], k_norm_w[l], rel_bias[l])
        x1, h2_tiles, h2_rows, lg_t = _outproj_call(y_rwkv, y_na, x, mod3, norm2_w[l].reshape(1, d),
                                                    w_out[l].astype(BF16), w_router[l].T)
        x = _moe_stage(x1, h2_tiles, h2_rows, lg_t, mod3, router_bias[l], w_gate_e[l], w_up_e[l], w_down_e[l],
                       w_gate_s[l], w_up_s[l], w_down_s[l])
    return x
```

```python
import functools

import jax
import jax.numpy as jnp
import numpy as np
from jax import lax
from jax.experimental import pallas as pl
from jax.experimental.pallas import tpu as pltpu

F32 = jnp.float32
BF16 = jnp.bfloat16
I32 = jnp.int32
HI = lax.Precision.HIGHEST

LANES = 128
HEAD_DIM = 64
HEADS_PER_BLOCK = LANES // HEAD_DIM
GRID_W = 64
WIN_R = 8
WIN_C = 16
DECAY_LORA = 64
AAA_LORA = 64
GATE_LORA = 160
N_EXPERTS = 64
TOP_K = 6
SLOT_STRIDE = 8
N_GROUPS = 8
TOPK_GROUPS = 4
ROUTED_SCALE = 2.5
DISPATCH_BLOCK = 512
EXPERT_BLOCK_PARTS = 4
NORM_EPS = 1e-6
GN_EPS = 64e-5
L2_EPS = 1e-12
NEG_BIG = -1e30
WKV_CHUNK = 64
VMEM_LIMIT = 56 * 1024 * 1024

NT_DIMS = (((1,), (1,)), ((), ()))
TN_DIMS = (((0,), (0,)), ((), ()))


def _cparams(sem):
    return pltpu.CompilerParams(dimension_semantics=sem, vmem_limit_bytes=VMEM_LIMIT)


def _sigmoid(x):
    return 1.0 / (1.0 + jnp.exp(-x))


def _silu(x):
    return x * _sigmoid(x)


U32 = jnp.uint32
HI_HALF_MASK = 0xFFFF0000


def _pack_halves(x):
    n = x.shape[-1] // 2
    bits = lax.bitcast_convert_type(x.astype(BF16).astype(F32), U32)
    return (bits[:, :n] >> 16) | bits[:, n:]


def _unpack_halves(w):
    lo = lax.bitcast_convert_type(w << 16, F32)
    hi = lax.bitcast_convert_type(w & U32(HI_HALF_MASK), F32)
    return lo, hi


def _head_sum_matrix():
    a = np.arange(LANES) // HEAD_DIM
    return jnp.asarray((a[:, None] == a[None, :]).astype(np.float32))


def _head_sums_split(x, mbd):
    m16 = mbd.astype(BF16)
    hi = x.astype(BF16)
    lo = (x - hi.astype(F32)).astype(BF16)
    return jnp.dot(hi, m16, preferred_element_type=F32) + jnp.dot(lo, m16, preferred_element_type=F32)


def _mod_kernel(c_ref, w_ref, b_ref, o_ref):
    c = c_ref[...]
    o_ref[...] = jnp.dot(_silu(c), w_ref[...], preferred_element_type=F32, precision=HI) + b_ref[...]


def _mod_call(c_pad, w_ada, b_ada):
    rows, d = c_pad.shape
    n = w_ada.shape[1]
    tn = 1024
    return pl.pallas_call(
        _mod_kernel,
        grid=(n // tn,),
        in_specs=[pl.BlockSpec((rows, d), lambda j: (0, 0)),
                  pl.BlockSpec((d, tn), lambda j: (0, j)),
                  pl.BlockSpec((1, tn), lambda j: (0, j))],
        out_specs=pl.BlockSpec((rows, tn), lambda j: (0, j)),
        out_shape=jax.ShapeDtypeStruct((rows, n), F32),
        compiler_params=_cparams(("parallel",)),
        name="mod",
    )(c_pad, w_ada, b_ada)


def _widen_kernel(w_ref, o_ref, *, split, pad):
    w = w_ref[...]
    o_ref[...] = jnp.concatenate([w[:, :split], jnp.zeros((w.shape[0], pad), w.dtype), w[:, split:]],
                                 axis=1).astype(o_ref.dtype)


def _widen_call(w_stack, layer, split, pad, tr=256):
    _, rows, cols = w_stack.shape
    return pl.pallas_call(
        functools.partial(_widen_kernel, split=split, pad=pad),
        grid=(rows // tr,),
        in_specs=[pl.BlockSpec((None, tr, cols), lambda i: (layer, i, 0))],
        out_specs=pl.BlockSpec((tr, cols + pad), lambda i: (i, 0)),
        out_shape=jax.ShapeDtypeStruct((rows, cols + pad), BF16),
        compiler_params=_cparams(("parallel",)),
        name="widen",
    )(w_stack)


def _proj_kernel(x_ref, mod_ref, nw_ref, w_ref, o_ref, h_sc, *, nq):
    @pl.when(pl.program_id(2) == 0)
    def _():
        x = x_ref[0]
        ms = jnp.mean(x * x, axis=-1, keepdims=True)
        y = x * lax.rsqrt(ms + NORM_EPS) * nw_ref[...]
        h_sc[...] = (y * (1.0 + mod_ref[0, 1:2, :]) + mod_ref[0, 0:1, :]).astype(BF16)

    acc = jnp.dot(h_sc[...], w_ref[...], preferred_element_type=F32)
    for q in range(nq):
        o_ref[0, q] = acc[:, q * LANES:(q + 1) * LANES].astype(o_ref.dtype)


def _proj_call(x, mod3, norm_w, w_in_r, tm=1024, tn=1664):
    b, s, d = x.shape
    n = w_in_r.shape[1]
    nq = tn // LANES
    return pl.pallas_call(
        functools.partial(_proj_kernel, nq=nq),
        grid=(b, s // tm, n // tn),
        in_specs=[pl.BlockSpec((1, tm, d), lambda bi, i, j: (bi, i, 0)),
                  pl.BlockSpec((1, 6, d), lambda bi, i, j: (bi, 0, 0)),
                  pl.BlockSpec((1, d), lambda bi, i, j: (0, 0)),
                  pl.BlockSpec((d, tn), lambda bi, i, j: (0, j))],
        out_specs=pl.BlockSpec((1, nq, tm, LANES), lambda bi, i, j: (bi, j, i, 0)),
        out_shape=jax.ShapeDtypeStruct((b, n // LANES, s, LANES), BF16),
        scratch_shapes=[pltpu.VMEM((tm, d), BF16)],
        compiler_params=_cparams(("parallel", "parallel", "arbitrary")),
        name="proj",
    )(x, mod3, norm_w, w_in_r)


RB_R, RB_V, RB_KK, RB_G, RB_KF, RB_KB, RB_AF, RB_AB, RB_BONUS = range(9)
VEC_W0F, VEC_W0B, VEC_A0F, VEC_A0B, VEC_KK, VEC_KA, VEC_RK = range(7)
N_RKV_BLOCKS = 24
N_SHIFT_BLOCKS = 28


def _softplus(u):
    return jnp.maximum(u, 0.0) + jnp.log(1.0 + jnp.exp(-jnp.abs(u)))


def _prep_kernel(p_ref, pp_ref, pn_ref, sw_ref, wupf_ref, wupb_ref, aupf_ref, aupb_ref, gup_ref,
                 vec_ref, mbd_ref, rb_ref, lw_ref, *, tb, n_t, nhp):
    i = pl.program_id(1)
    row = lax.broadcasted_iota(I32, (tb, LANES), 0)
    has_prev = i > 0
    has_next = i < n_t - 1
    halo = pp_ref.shape[2]

    def shifted(q):
        cur = p_ref[0, q].astype(F32)
        prev_row = jnp.where(has_prev, pp_ref[0, q, halo - 1:halo, :].astype(F32), 0.0)
        next_row = jnp.where(has_next, pn_ref[0, q, 0:1, :].astype(F32), 0.0)
        up = jnp.where(row == 0, prev_row, pltpu.roll(cur, 1, 0))
        dn = jnp.where(row == tb - 1, next_row, pltpu.roll(cur, tb - 1, 0))
        return sw_ref[0, q:q + 1, :] * up + sw_ref[1, q:q + 1, :] * cur + sw_ref[2, q:q + 1, :] * dn

    mbd = mbd_ref[...]
    t_wd = jnp.tanh(shifted(N_RKV_BLOCKS)).astype(BF16)
    z_ad = shifted(N_RKV_BLOCKS + 1).astype(BF16)
    s_gd = jnp.concatenate([_sigmoid(shifted(N_RKV_BLOCKS + 2)),
                            _sigmoid(shifted(N_RKV_BLOCKS + 3))], axis=1).astype(BF16)

    def vec(v, hp):
        return vec_ref[v, hp:hp + 1, :]

    for hp in range(nhp):
        sl = slice(hp * LANES, (hp + 1) * LANES)
        r = shifted(hp)
        k = shifted(nhp + hp)
        v = shifted(2 * nhp + hp)
        g = jnp.dot(s_gd, gup_ref[:, sl], preferred_element_type=F32)
        kk0 = k * vec(VEC_KK, hp)
        ss = _head_sums_split(kk0 * kk0, mbd)
        kk = kk0 * lax.rsqrt(jnp.maximum(ss, L2_EPS))
        kdirs = []
        for wup_ref, aup_ref, v_w0, v_a0, q_k, q_a, lw_slot in (
                (wupf_ref, aupf_ref, VEC_W0F, VEC_A0F, RB_KF, RB_AF, 0),
                (wupb_ref, aupb_ref, VEC_W0B, VEC_A0B, RB_KB, RB_AB, 1)):
            wl = vec(v_w0, hp) + jnp.dot(t_wd, wup_ref[:, sl], preferred_element_type=F32)
            w_log = -_softplus(-wl) - 0.5
            lw_ref[0, lw_slot * nhp + hp] = -jnp.exp(w_log)
            ag = _sigmoid(vec(v_a0, hp) + jnp.dot(z_ad, aup_ref[:, sl], preferred_element_type=F32))
            kd = k * (1.0 + (ag - 1.0) * vec(VEC_KA, hp))
            rb_ref[0, q_k * nhp + hp] = kd.astype(BF16)
            rb_ref[0, q_a * nhp + hp] = ag.astype(BF16)
            kdirs.append(kd)
        bonus = _head_sums_split(r * kdirs[0] * vec(VEC_RK, hp), mbd) * v
        rb_ref[0, RB_R * nhp + hp] = r.astype(BF16)
        rb_ref[0, RB_V * nhp + hp] = v.astype(BF16)
        rb_ref[0, RB_KK * nhp + hp] = kk.astype(BF16)
        rb_ref[0, RB_G * nhp + hp] = g.astype(BF16)
        rb_ref[0, RB_BONUS * nhp + hp] = bonus.astype(BF16)


def _prep_call(p, sw, wupf, wupb, aupf, aupb, gup, vecs, mbd, nhp, tb=256):
    b, _, s, _ = p.shape
    n_t = s // tb
    halo = 16
    hb = tb // halo
    full = lambda a: pl.BlockSpec(a.shape, lambda bi, i: (0,) * a.ndim)
    return pl.pallas_call(
        functools.partial(_prep_kernel, tb=tb, n_t=n_t, nhp=nhp),
        grid=(b, n_t),
        in_specs=[pl.BlockSpec((1, N_SHIFT_BLOCKS, tb, LANES), lambda bi, i: (bi, 0, i, 0)),
                  pl.BlockSpec((1, N_SHIFT_BLOCKS, halo, LANES),
                               lambda bi, i: (bi, 0, jnp.maximum(i * hb - 1, 0), 0)),
                  pl.BlockSpec((1, N_SHIFT_BLOCKS, halo, LANES),
                               lambda bi, i: (bi, 0, jnp.minimum((i + 1) * hb, s // halo - 1), 0)),
                  full(sw), full(wupf), full(wupb), full(aupf), full(aupb), full(gup), full(vecs), full(mbd)],
        out_specs=[pl.BlockSpec((1, 9 * nhp, tb, LANES), lambda bi, i: (bi, 0, i, 0)),
                   pl.BlockSpec((1, 2 * nhp, tb, LANES), lambda bi, i: (bi, 0, i, 0))],
        out_shape=[jax.ShapeDtypeStruct((b, 9 * nhp, s, LANES), BF16),
                   jax.ShapeDtypeStruct((b, 2 * nhp, s, LANES), F32)],
        compiler_params=_cparams(("parallel", "parallel")),
        name="prep",
    )(p, p, p, sw, wupf, wupb, aupf, aupb, gup, vecs, mbd)


def _wkv_kernel(r_ref, v_ref, kk_ref, kf_ref, af_ref, lf_ref, rr_ref, vr_ref, kkr_ref, kb_ref, ab_ref, lb_ref,
                yf_ref, yb_ref, s_sc, *, L, nhp):
    @pl.when(pl.program_id(1) == 0)
    def _():
        s_sc[...] = jnp.zeros(s_sc.shape, F32)

    assert HEADS_PER_BLOCK == 2 and L == HEAD_DIM
    n2 = HEADS_PER_BLOCK * L
    ri = lax.broadcasted_iota(I32, (n2, LANES), 0)
    ci = lax.broadcasted_iota(I32, (n2, LANES), 1)
    head_match = (ri // L) == (ci // HEAD_DIM)
    st = lax.broadcasted_iota(I32, (L, LANES), 0)
    ss = lax.broadcasted_iota(I32, (L, LANES), 1) % L
    first_head = lax.broadcasted_iota(I32, (L, LANES), 1) < HEAD_DIM
    eye = (st == ss).astype(F32)
    strict = {False: st > ss, True: st < ss}
    incl = {False: st >= ss, True: st < ss}

    def stack(x):
        return jnp.where(head_match, jnp.concatenate([x] * HEADS_PER_BLOCK, axis=0), 0.0)

    def bdiag(x):
        return jnp.concatenate([jnp.where(first_head, x, 0.0), jnp.where(first_head, 0.0, x)], axis=0)

    chains = [(False, hp, r_ref, v_ref, kk_ref, kf_ref, af_ref, lf_ref) for hp in range(nhp)]
    chains += [(True, hp, rr_ref, vr_ref, kkr_ref, kb_ref, ab_ref, lb_ref) for hp in range(nhp)]
    n = len(chains)
    lws = [ch[7][0, ch[1]] for ch in chains]
    t_row = lax.broadcasted_iota(I32, (L, LANES), 0)

    def cumsum_time(x, rev):
        sh = 1
        while sh < L:
            if rev:
                x = x + jnp.where(t_row < L - sh, pltpu.roll(x, L - sh, 0), 0.0)
            else:
                x = x + jnp.where(t_row >= sh, pltpu.roll(x, sh, 0), 0.0)
            sh *= 2
        return x

    st = [dict() for _ in range(n)]

    def prologue(i):
        rev, hp, rr, vr, kkr, kdr, agr, _ = chains[i]
        r = rr[0, hp].astype(F32)
        kk = kkr[0, hp].astype(F32)
        kd = kdr[0, hp].astype(F32)
        ag = agr[0, hp].astype(F32)
        v = vr[0, hp].astype(F32)
        c_in = cumsum_time(lws[i], rev)
        e_ex = jnp.exp(c_in - lws[i])
        e_inv = jnp.exp(-c_in)
        if rev:
            e_r = e_ex
            p_last = jnp.exp(c_in[0:1])
        else:
            e_r = jnp.exp(c_in)
            p_last = jnp.exp(c_in[L - 1:L])
        b_t = kk * ag * e_inv
        k_t = kd * e_inv
        st[i].update(
            lhs=jnp.concatenate([-kk * e_ex, r * e_r], axis=0).astype(BF16),
            rhs=jnp.concatenate([stack(b_t), stack(k_t)], axis=0).astype(BF16),
            bk=jnp.concatenate([b_t * p_last, k_t * p_last], axis=0).astype(BF16),
            v=v.astype(BF16), vbd=bdiag(v).astype(BF16), p_last=p_last)

    def stage_scores(i):
        c = st[i]
        c['g'] = lax.dot_general(c['lhs'], c['rhs'], NT_DIMS, preferred_element_type=F32)
        c['s_old'] = s_sc[i]
        c['ls'] = lax.dot_general(c['lhs'], c['s_old'].astype(BF16), NT_DIMS, preferred_element_type=F32)

    def stage_solve_setup(i):
        c = st[i]
        rev, g = chains[i][0], c.pop('g')
        a_ab = jnp.where(strict[rev], g[0:L, 0:n2], 0.0)
        a_ak = jnp.where(strict[rev], g[0:L, n2:2 * n2], 0.0).astype(BF16)
        c['a_r'] = jnp.concatenate([jnp.where(incl[rev], g[L:2 * L, 0:n2], 0.0),
                                    jnp.where(incl[rev], g[L:2 * L, n2:2 * n2], 0.0)], axis=1).astype(BF16)
        c['x'] = c['ls'][0:L] + jnp.dot(a_ak, c['vbd'], preferred_element_type=F32)
        c['pw'] = jnp.dot(a_ab.astype(BF16), bdiag(a_ab).astype(BF16), preferred_element_type=F32)
        c['t'] = eye + a_ab

    def stage_round(i):
        c = st[i]
        both = jnp.dot(jnp.concatenate([c['pw'], c['t']], axis=0).astype(BF16), bdiag(c['pw']).astype(BF16),
                       preferred_element_type=F32)
        c['pw'] = both[0:L]
        c['t'] = c['t'] + both[L:2 * L]

    def stage_last_round(i):
        c = st[i]
        c['t'] = c['t'] + jnp.dot(c['t'].astype(BF16), bdiag(c.pop('pw')).astype(BF16),
                                  preferred_element_type=F32)

    def stage_solve(i):
        c = st[i]
        c['u'] = jnp.dot(c.pop('t').astype(BF16), bdiag(c.pop('x')).astype(BF16), preferred_element_type=F32)

    def stage_out(i):
        c = st[i]
        rev, hp = chains[i][0], chains[i][1]
        u = c['u']
        y = c['ls'][L:2 * L] + jnp.dot(c['a_r'], jnp.concatenate([bdiag(u).astype(BF16), c['vbd']], axis=0),
                                       preferred_element_type=F32)
        upd = lax.dot_general(jnp.concatenate([u.astype(BF16), c['v']], axis=0), c['bk'], TN_DIMS,
                              preferred_element_type=F32)
        (yb_ref if rev else yf_ref)[0, hp] = y
        s_sc[i] = c['s_old'] * c['p_last'] + jnp.where(head_match, upd, 0.0)

    rounds = L.bit_length() - 1
    stages = ([stage_scores, stage_solve_setup] + [stage_round] * (rounds - 2)
              + [stage_last_round, stage_solve, stage_out])
    for i in range(n):
        prologue(i)
    for stage in stages:
        for i in range(n):
            stage(i)


def _wkv_call(rb, lw, nhp, L=WKV_CHUNK):
    b, _, s, _ = rb.shape
    nc = s // L
    fwd = lambda q: pl.BlockSpec((1, nhp, L, LANES), lambda bi, c: (bi, q, c, 0))
    bwd = lambda q: pl.BlockSpec((1, nhp, L, LANES), lambda bi, c: (bi, q, nc - 1 - c, 0))
    y_shape = jax.ShapeDtypeStruct((b, nhp, s, LANES), F32)
    return pl.pallas_call(
        functools.partial(_wkv_kernel, L=L, nhp=nhp),
        grid=(b, nc),
        in_specs=[fwd(RB_R), fwd(RB_V), fwd(RB_KK), fwd(RB_KF), fwd(RB_AF), fwd(0),
                  bwd(RB_R), bwd(RB_V), bwd(RB_KK), bwd(RB_KB), bwd(RB_AB), bwd(1)],
        out_specs=[fwd(0), bwd(0)],
        out_shape=[y_shape, y_shape],
        scratch_shapes=[pltpu.VMEM((2 * nhp, LANES, LANES), F32)],
        compiler_params=_cparams(("parallel", "arbitrary")),
        name="wkv",
    )(rb, rb, rb, rb, rb, lw, rb, rb, rb, rb, rb, lw)


def _post_kernel(yf_ref, yb_ref, bonus_ref, g_ref, lnw_ref, lnb_ref, mbd_ref, o_ref, *, nhp):
    mbd = mbd_ref[...]
    inv_n = 1.0 / HEAD_DIM
    for hp in range(nhp):
        y = yf_ref[0, hp] + yb_ref[0, hp]
        mu = _head_sums_split(y, mbd) * inv_n
        d = y - mu
        var = _head_sums_split(d * d, mbd) * inv_n
        yn = d * lax.rsqrt(var + GN_EPS) * lnw_ref[hp:hp + 1, :] + lnb_ref[hp:hp + 1, :]
        out = (yn + bonus_ref[0, hp].astype(F32)) * g_ref[0, hp].astype(F32)
        o_ref[0, :, hp * LANES:(hp + 1) * LANES] = out.astype(o_ref.dtype)


def _post_call(yf, yb, rb, lnw, lnb, mbd, nhp, tb=256):
    b, _, s, _ = yf.shape
    yblk = pl.BlockSpec((1, nhp, tb, LANES), lambda bi, i: (bi, 0, i, 0))
    rblk = lambda q: pl.BlockSpec((1, nhp, tb, LANES), lambda bi, i: (bi, q, i, 0))
    full = lambda a: pl.BlockSpec(a.shape, lambda bi, i: (0,) * a.ndim)
    return pl.pallas_call(
        functools.partial(_post_kernel, nhp=nhp),
        grid=(b, s // tb),
        in_specs=[yblk, yblk, rblk(RB_BONUS), rblk(RB_G), full(lnw), full(lnb), full(mbd)],
        out_specs=pl.BlockSpec((1, tb, nhp * LANES), lambda bi, i: (bi, i, 0)),
        out_shape=jax.ShapeDtypeStruct((b, s, nhp * LANES), BF16),
        compiler_params=_cparams(("parallel", "parallel")),
        name="post",
    )(yf, yb, rb, rb, lnw, lnb, mbd)


def _na_kernel(q_ref, k_ref, v_ref, bias_ref, qw_ref, kw_ref, mbd_ref, o_ref, kn_sc, *, rq, rows):
    rbi = pl.program_id(2)
    mbd = mbd_ref[...]
    inv_n = 1.0 / HEAD_DIM
    span = WIN_R * GRID_W
    s_len = kn_sc.shape[0]

    @pl.when(rbi == 0)
    def _():
        def body(c, carry):
            off = pl.multiple_of(c * span, span)
            k = k_ref[0, 0, pl.ds(off, span), :].astype(F32)
            ms = _head_sums_split(k * k, mbd) * inv_n
            kn_sc[pl.ds(off, span), :] = (k * lax.rsqrt(ms + NORM_EPS) * kw_ref[...]).astype(BF16)
            return carry
        lax.fori_loop(0, s_len // span, body, 0)

    ri = lax.broadcasted_iota(I32, (HEADS_PER_BLOCK * GRID_W, LANES), 0)
    ci = lax.broadcasted_iota(I32, (HEADS_PER_BLOCK * GRID_W, LANES), 1)
    head_match = (ri // GRID_W) == (ci // HEAD_DIM)
    lane = lax.broadcasted_iota(I32, (GRID_W, LANES), 1)
    scale = HEAD_DIM ** -0.5
    q_all = q_ref[0, 0].astype(F32)
    ms = _head_sums_split(q_all * q_all, mbd) * inv_n
    qn_all = q_all * lax.rsqrt(ms + NORM_EPS) * qw_ref[...] * scale
    starts, offs = [], []
    for qi in range(rq):
        i = rbi * rq + qi
        rs = jnp.clip(i - WIN_R // 2, 0, rows - WIN_R)
        offs.append(rs - i + WIN_R - 1)
        starts.append(pl.multiple_of(rs * GRID_W, GRID_W))
    qs_l = [jnp.where(head_match, jnp.concatenate([qn_all[qi * GRID_W:(qi + 1) * GRID_W]] * HEADS_PER_BLOCK,
                                                  axis=0), 0.0).astype(BF16) for qi in range(rq)]
    def bias(off):
        return jnp.concatenate([bias_ref[0, off + 2 * i] for i in range(WIN_R // 2)], axis=1)

    s_l = [lax.dot_general(qs_l[qi], kn_sc[pl.ds(starts[qi], span), :], NT_DIMS, preferred_element_type=F32)
           + bias(offs[qi]) for qi in range(rq)]
    p_l, l_l = [], []
    for s in s_l:
        p = jnp.exp(s - jnp.max(s, axis=-1, keepdims=True))
        p_l.append(p.astype(BF16))
        l_l.append(jnp.sum(p, axis=-1, keepdims=True))
    o_l = [jnp.dot(p_l[qi], v_ref[0, 0, pl.ds(starts[qi], span), :], preferred_element_type=F32) / l_l[qi]
           for qi in range(rq)]
    for qi, o in enumerate(o_l):
        out = o[0:GRID_W]
        for h in range(1, HEADS_PER_BLOCK):
            out = jnp.where(lane // HEAD_DIM == h, o[h * GRID_W:(h + 1) * GRID_W], out)
        o_ref[0, qi * GRID_W:(qi + 1) * GRID_W, :] = out.astype(o_ref.dtype)


def _na_call(p, bias, qw, kw, mbd, q_blk0, nhp, rq=32):
    b, _, s, _ = p.shape
    rows = s // GRID_W
    rq = min(rq, rows)
    full = lambda a: pl.BlockSpec(a.shape, lambda bi, hp, r: (0,) * a.ndim)
    return pl.pallas_call(
        functools.partial(_na_kernel, rq=rq, rows=rows),
        grid=(b, nhp, rows // rq),
        in_specs=[pl.BlockSpec((1, 1, rq * GRID_W, LANES), lambda bi, hp, r: (bi, q_blk0 + hp, r, 0)),
                  pl.BlockSpec((1, 1, s, LANES), lambda bi, hp, r: (bi, q_blk0 + nhp + hp, 0, 0)),
                  pl.BlockSpec((1, 1, s, LANES), lambda bi, hp, r: (bi, q_blk0 + 2 * nhp + hp, 0, 0)),
                  pl.BlockSpec((1,) + bias.shape[1:], lambda bi, hp, r: (hp, 0, 0, 0)),
                  full(qw), full(kw), full(mbd)],
        out_specs=pl.BlockSpec((1, rq * GRID_W, LANES), lambda bi, hp, r: (bi, r, hp)),
        out_shape=jax.ShapeDtypeStruct((b, s, nhp * LANES), BF16),
        scratch_shapes=[pltpu.VMEM((s, LANES), BF16)],
        compiler_params=_cparams(("parallel", "parallel", "arbitrary")),
        name="na",
    )(p, p, p, bias, qw, kw, mbd)


def _na_bias_table(rel_bias):
    h = rel_bias.shape[0]
    qc = np.arange(GRID_W)
    kc = np.arange(GRID_W)
    win0 = np.clip(qc - WIN_C // 2, 0, GRID_W - WIN_C)
    valid = (kc[None, :] >= win0[:, None]) & (kc[None, :] < win0[:, None] + WIN_C)
    coff = np.clip(kc[None, :] - qc[:, None] + WIN_C - 1, 0, 2 * WIN_C - 2)
    n_row, n_col = 2 * WIN_R - 1, 2 * WIN_C - 1
    n_ro = n_row - 1
    sel = np.zeros((n_ro, 2, n_row), np.float32)
    for ro in range(n_ro):
        for part in range(2):
            sel[ro, part, ro + part] = 1.0
    pick = np.zeros((2, n_col, GRID_W, 2 * GRID_W), np.float32)
    for part in range(2):
        pick[part, coff, qc[:, None], part * GRID_W + kc[None, :]] = 1.0
    a = jnp.einsum('hrc,apr->hapc', rel_bias.astype(F32), jnp.asarray(sel), precision=HI)
    a = a.reshape(h // HEADS_PER_BLOCK, HEADS_PER_BLOCK, n_ro, 2 * n_col).transpose(0, 2, 1, 3)
    t = jnp.einsum('ahbx,xql->ahbql', a, jnp.asarray(pick.reshape(2 * n_col, GRID_W, 2 * GRID_W)),
                   precision=HI)
    t = jnp.where(np.tile(valid, (1, 2)), t, NEG_BIG)
    return t.reshape(h // HEADS_PER_BLOCK, n_ro, HEADS_PER_BLOCK * GRID_W, 2 * GRID_W)


def _outproj_kernel(yr_ref, yn_ref, x_ref, mod_ref, nw_ref, w1_ref, w2_ref, wr_ref, x1_ref, h2_ref, h2row_ref,
                    lg_ref):
    tm = x_ref.shape[1]
    sub = LANES
    tiles = [slice(i * sub, (i + 1) * sub) for i in range(tm // sub)]
    accs = [jnp.dot(yr_ref[0, r], w1_ref[...], preferred_element_type=F32)
            + jnp.dot(yn_ref[0, r], w2_ref[...], preferred_element_type=F32) for r in tiles]
    h2s = []
    for r, acc in zip(tiles, accs):
        x1 = x_ref[0, r] + mod_ref[0, 2:3, :] * acc
        x1_ref[0, r] = x1
        ms = jnp.mean(x1 * x1, axis=-1, keepdims=True)
        y = x1 * lax.rsqrt(ms + NORM_EPS) * nw_ref[...]
        h2 = y * (1.0 + mod_ref[0, 4:5, :]) + mod_ref[0, 3:4, :]
        packed = _pack_halves(h2)
        h2row_ref[0, r] = packed
        for j in range(h2_ref.shape[2]):
            h2_ref[0, r, j, :] = packed[:, j * LANES:(j + 1) * LANES]
        h2s.append(h2)
    wr = wr_ref[...]
    wr_hi = wr.astype(BF16)
    wr_lo = (wr - wr_hi.astype(F32)).astype(BF16)
    for r, h2 in zip(tiles, h2s):
        h_hi = h2.astype(BF16)
        h_lo = (h2 - h_hi.astype(F32)).astype(BF16)
        lg_ref[:, r] = (lax.dot_general(wr_hi, h_hi, NT_DIMS, preferred_element_type=F32)
                        + lax.dot_general(wr_hi, h_lo, NT_DIMS, preferred_element_type=F32)
                        + lax.dot_general(wr_lo, h_hi, NT_DIMS, preferred_element_type=F32))


def _outproj_call(yr, yn, x, mod3, norm_w, w_o, wr_t, tm=512):
    b, s, d = x.shape
    dh = yr.shape[-1]
    assert w_o.shape == (2 * dh, d)
    ne = wr_t.shape[0]
    nt = s // tm
    full = lambda a: pl.BlockSpec(a.shape, lambda bi, i: (0,) * a.ndim)
    return pl.pallas_call(
        _outproj_kernel,
        grid=(b, nt),
        in_specs=[pl.BlockSpec((1, tm, dh), lambda bi, i: (bi, i, 0)),
                  pl.BlockSpec((1, tm, dh), lambda bi, i: (bi, i, 0)),
                  pl.BlockSpec((1, tm, d), lambda bi, i: (bi, i, 0)),
                  pl.BlockSpec((1, 6, d), lambda bi, i: (bi, 0, 0)),
                  full(norm_w),
                  pl.BlockSpec((dh, d), lambda bi, i: (0, 0)),
                  pl.BlockSpec((dh, d), lambda bi, i: (1, 0)),
                  full(wr_t)],
        out_specs=[pl.BlockSpec((1, tm, d), lambda bi, i: (bi, i, 0)),
                   pl.BlockSpec((1, tm, d // 2 // LANES, LANES), lambda bi, i: (bi, i, 0, 0)),
                   pl.BlockSpec((1, tm, d // 2), lambda bi, i: (bi, i, 0)),
                   pl.BlockSpec((ne, tm), lambda bi, i: (0, bi * nt + i))],
        out_shape=[jax.ShapeDtypeStruct((b, s, d), F32),
                   jax.ShapeDtypeStruct((b, s, d // 2 // LANES, LANES), U32),
                   jax.ShapeDtypeStruct((b, s, d // 2), U32),
                   jax.ShapeDtypeStruct((ne, b * s), F32)],
        compiler_params=_cparams(("parallel", "parallel")),
        name="outproj",
    )(yr, yn, x, mod3, norm_w, w_o, w_o, wr_t)


def _first_argmax(x, idx, n):
    m = jnp.max(x, axis=0, keepdims=True)
    a = jnp.min(jnp.where(x == m, idx, n), axis=0, keepdims=True)
    return m, a


def _route_kernel(lg_ref, bias_ref, idx_ref, gw_ref, pos_ref, cnt_ref, *, tr):
    @pl.when(pl.program_id(0) == 0)
    def _():
        cnt_ref[...] = jnp.zeros(cnt_ref.shape, F32)

    gsz = N_EXPERTS // N_GROUPS
    scores = _sigmoid(lg_ref[...])
    biased = scores + bias_ref[:, 0:1]
    ig = lax.broadcasted_iota(I32, (gsz, tr), 0)
    grp_rows = []
    for g in range(N_GROUPS):
        blk = biased[g * gsz:(g + 1) * gsz]
        m1, a1 = _first_argmax(blk, ig, gsz)
        m2 = jnp.max(jnp.where(ig == a1, -jnp.inf, blk), axis=0, keepdims=True)
        grp_rows.append(m1 + m2)
    grp = jnp.concatenate(grp_rows, axis=0)
    ign = lax.broadcasted_iota(I32, (N_GROUPS, tr), 0)
    sel = jnp.zeros((N_GROUPS, tr), jnp.bool_)
    for _ in range(TOPK_GROUPS):
        _, a = _first_argmax(grp, ign, N_GROUPS)
        hit = ign == a
        sel = jnp.logical_or(sel, hit)
        grp = jnp.where(hit, -jnp.inf, grp)
    masked = jnp.concatenate(
        [jnp.where(sel[g:g + 1], biased[g * gsz:(g + 1) * gsz], -jnp.inf) for g in range(N_GROUPS)], axis=0)
    ie = lax.broadcasted_iota(I32, (N_EXPERTS, tr), 0)
    picks, pick_scores = [], []
    onehot = jnp.zeros((N_EXPERTS, tr), F32)
    for _ in range(TOP_K):
        _, a = _first_argmax(masked, ie, N_EXPERTS)
        hit = ie == a
        picks.append(a)
        pick_scores.append(jnp.sum(jnp.where(hit, scores, 0.0), axis=0, keepdims=True))
        onehot = onehot + hit.astype(F32)
        masked = jnp.where(hit, -jnp.inf, masked)
    total = pick_scores[0]
    for sc in pick_scores[1:]:
        total = total + sc
    t0 = lax.broadcasted_iota(I32, (tr, tr), 0)
    t1 = lax.broadcasted_iota(I32, (tr, tr), 1)
    before = (t0 < t1).astype(BF16)
    rank = jnp.dot(onehot.astype(BF16), before, preferred_element_type=F32) + cnt_ref[:, 0:1]
    zero_i = jnp.zeros((1, tr), I32)
    zero_f = jnp.zeros((1, tr), F32)
    for k in range(8):
        if k < TOP_K:
            idx_ref[k:k + 1, :] = picks[k]
            gw_ref[k:k + 1, :] = pick_scores[k] / total * ROUTED_SCALE
            pos = jnp.sum(jnp.where(ie == picks[k], rank, 0.0), axis=0, keepdims=True)
            pos_ref[k:k + 1, :] = pos.astype(I32)
        else:
            idx_ref[k:k + 1, :] = zero_i
            gw_ref[k:k + 1, :] = zero_f
            pos_ref[k:k + 1, :] = zero_i
    cnt_ref[...] = cnt_ref[...] + jnp.sum(onehot, axis=1, keepdims=True)


def _route_call(lg_t, bias2, tr=512):
    ne, t = lg_t.shape
    tr = min(tr, t)
    tok = pl.BlockSpec((8, tr), lambda i: (0, i))
    return pl.pallas_call(
        functools.partial(_route_kernel, tr=tr),
        grid=(t // tr,),
        in_specs=[pl.BlockSpec((ne, tr), lambda i: (0, i)),
                  pl.BlockSpec(bias2.shape, lambda i: (0, 0))],
        out_specs=[tok, tok, tok, pl.BlockSpec((ne, LANES), lambda i: (0, 0))],
        out_shape=[jax.ShapeDtypeStruct((8, t), I32), jax.ShapeDtypeStruct((8, t), F32),
                   jax.ShapeDtypeStruct((8, t), I32), jax.ShapeDtypeStruct((ne, LANES), F32)],
        compiler_params=_cparams(("arbitrary",)),
        name="route",
    )(lg_t, bias2)


FILL_CHUNKS = (1, 8, 64)
ZERO_ROWS = FILL_CHUNKS[-1]


def _dispatch_kernel(slot_ref, fill_lo_ref, fill_hi_ref, h_ref, xs_ref, zrow, sem, *, td):
    def row_copy(t, slot):
        return pltpu.make_async_copy(h_ref.at[t], xs_ref.at[slot], sem)

    def start(t, carry):
        for k in range(TOP_K):
            row_copy(t, slot_ref[t * SLOT_STRIDE + k]).start(priority=k % 2)
        return carry

    lax.fori_loop(0, td, start, 0)
    for k in range(TOP_K):
        pltpu.make_async_copy(h_ref, xs_ref.at[pl.ds(0, td)], sem).wait()

    @pl.when(pl.program_id(0) == pl.num_programs(0) - 1)
    def _():
        zrow[...] = jnp.zeros(zrow.shape, zrow.dtype)

        def zero_copy(row, n):
            if n > 1:
                row = pl.multiple_of(row, FILL_CHUNKS[1])
            return pltpu.make_async_copy(zrow.at[pl.ds(0, n)], xs_ref.at[pl.ds(row, n)], sem)

        def for_each_chunk(r, fn):
            lo, hi = fill_lo_ref[r], fill_hi_ref[r]
            for ci, n in enumerate(FILL_CHUNKS):
                if ci + 1 < len(FILL_CHUNKS):
                    count = jnp.minimum(((-lo) % FILL_CHUNKS[ci + 1]) // n, (hi - lo) // n)
                else:
                    count = (hi - lo) // n
                fn(lo, n, count)
                lo = lo + count * n

        def start_range(r, carry):
            def go(lo, n, count):
                lax.fori_loop(0, count, lambda i, c: (zero_copy(lo + i * n, n).start(), c)[1], 0)
            for_each_chunk(r, go)
            return carry

        def wait_range(r, carry):
            def go(lo, n, count):
                lax.fori_loop(0, count, lambda i, c: (zero_copy(0, n).wait(), c)[1], 0)
            for_each_chunk(r, go)
            return carry

        n_ranges = fill_lo_ref.shape[0]
        lax.fori_loop(0, n_ranges, start_range, 0)
        lax.fori_loop(0, n_ranges, wait_range, 0)


def _dispatch_call(slots_flat, fill_lo, fill_hi, h2, n_rows, td=512):
    t, nj, _ = h2.shape
    smem = pl.BlockSpec(memory_space=pltpu.SMEM)
    return pl.pallas_call(
        functools.partial(_dispatch_kernel, td=td),
        grid=(t // td,),
        in_specs=[pl.BlockSpec((td * SLOT_STRIDE,), lambda i: (i,), memory_space=pltpu.SMEM),
                  smem, smem,
                  pl.BlockSpec((td, nj, LANES), lambda i: (i, 0, 0))],
        out_specs=pl.BlockSpec(memory_space=pl.ANY),
        out_shape=jax.ShapeDtypeStruct((n_rows, nj, LANES), h2.dtype),
        scratch_shapes=[pltpu.VMEM((ZERO_ROWS, nj, LANES), h2.dtype), pltpu.SemaphoreType.DMA(())],
        compiler_params=_cparams(("arbitrary",)),
        name="dispatch",
    )(slots_flat, fill_lo, fill_hi, h2)


def _experts_kernel(be_ref, first_ref, nxt_ref, par_ref, rows_ref, nv_ref, xs_hbm, wg_hbm, wu_hbm, wd_hbm, ys_ref,
                    wg_buf, wu_buf, wd_buf, wg_sc, wu_sc, wd_sc, xpack, sem, isem):
    nj = xs_hbm.shape[1]
    b = pl.program_id(0)
    m = xpack.shape[1]
    half = wg_sc.shape[0] // 2
    oslot = lax.rem(b, 2)

    def in_copies(slot, row0):
        return [pltpu.make_async_copy(xs_hbm.at[pl.ds(row0, m), j],
                                      xpack.at[slot, :, pl.ds(j * LANES, LANES)], isem.at[slot]) for j in range(nj)]

    @pl.when(b == 0)
    def _():
        for c in in_copies(0, 0):
            c.start()

    @pl.when(b + 1 < nv_ref[0])
    def _():
        for c in in_copies(1 - oslot, pl.multiple_of((b + 1) * m, m)):
            c.start()

    def weight_copies(e, slot):
        return [pltpu.make_async_copy(src.at[e], dst.at[slot], sem.at[slot, i])
                for i, (src, dst) in enumerate(((wg_hbm, wg_buf), (wu_hbm, wu_buf), (wd_hbm, wd_buf)))]

    def compute(nrows):
        x_lo, x_hi = _unpack_halves(xpack[oslot, 0:nrows])
        x_lo, x_hi = x_lo.astype(BF16), x_hi.astype(BF16)
        g = (jnp.dot(x_lo, wg_sc[0:half], preferred_element_type=F32)
             + jnp.dot(x_hi, wg_sc[half:], preferred_element_type=F32))
        u = (jnp.dot(x_lo, wu_sc[0:half], preferred_element_type=F32)
             + jnp.dot(x_hi, wu_sc[half:], preferred_element_type=F32))
        a = (_silu(g) * u).astype(BF16)
        ys_ref[0:nrows] = _pack_halves(jnp.dot(a, wd_sc[...], preferred_element_type=F32))

    @pl.when(b < nv_ref[0])
    def _():
        for c in in_copies(oslot, 0):
            c.wait()

        @pl.when(b == 0)
        def _():
            for c in weight_copies(be_ref[0], par_ref[0]):
                c.start()

        @pl.when(first_ref[b] == 1)
        def _():
            slot = par_ref[b]
            for c in weight_copies(be_ref[b], slot):
                c.wait()

            @pl.when(nxt_ref[b] >= 0)
            def _():
                for c in weight_copies(nxt_ref[b], 1 - slot):
                    c.start()

            wg_sc[...] = wg_buf[slot].astype(BF16)
            wu_sc[...] = wu_buf[slot].astype(BF16)
            wd_sc[...] = wd_buf[slot].astype(BF16)

        quarter = m // EXPERT_BLOCK_PARTS
        for part in range(1, EXPERT_BLOCK_PARTS + 1):
            nrows = part * quarter
            if part == 1:
                cond = rows_ref[b] <= nrows
            elif part == EXPERT_BLOCK_PARTS:
                cond = rows_ref[b] > nrows - quarter
            else:
                cond = jnp.logical_and(rows_ref[b] > nrows - quarter, rows_ref[b] <= nrows)

            @pl.when(cond)
            def _(nrows=nrows):
                compute(nrows)
                if nrows < m:
                    ys_ref[nrows:] = jnp.zeros((m - nrows, ys_ref.shape[1]), ys_ref.dtype)

    @pl.when(b >= nv_ref[0])
    def _():
        ys_ref[...] = jnp.zeros(ys_ref.shape, ys_ref.dtype)


def _experts_call(block_e, first, nxt, par, rows_valid, n_valid, xs, wg, wu, wd):
    total, nj, _ = xs.shape
    d = wg.shape[1]
    m = DISPATCH_BLOCK
    ff = wg.shape[-1]

    hbm = pl.BlockSpec(memory_space=pl.ANY)
    grid_spec = pltpu.PrefetchScalarGridSpec(
        num_scalar_prefetch=6,
        grid=(total // m,),
        in_specs=[hbm, hbm, hbm, hbm],
        out_specs=pl.BlockSpec((m, nj * LANES), lambda b, *s: (b, 0)),
        scratch_shapes=[pltpu.VMEM((2, d, ff), F32), pltpu.VMEM((2, d, ff), F32), pltpu.VMEM((2, ff, d), F32),
                        pltpu.VMEM((d, ff), BF16), pltpu.VMEM((d, ff), BF16), pltpu.VMEM((ff, d), BF16),
                        pltpu.VMEM((2, m, nj * LANES), U32),
                        pltpu.SemaphoreType.DMA((2, 3)), pltpu.SemaphoreType.DMA((2,))],
    )
    return pl.pallas_call(
        _experts_kernel,
        grid_spec=grid_spec,
        out_shape=jax.ShapeDtypeStruct((total, nj * LANES), U32),
        compiler_params=_cparams(("arbitrary",)),
        name="experts",
    )(block_e, first, nxt, par, rows_valid, n_valid, xs, wg, wu, wd)


COMBINE_INLINE_ROWS = 128


def _combine_kernel(slot_ref, x1_ref, h2_ref, gw_ref, mod_ref, wg_ref, wu_ref, wd_ref, ys_ref, o_ref,
                    buf, sem, *, tc):
    def row_copy(t, k, slot):
        return pltpu.make_async_copy(ys_ref.at[pl.ds(slot, 1)], buf.at[k, pl.ds(t, 1)], sem)

    def start(t, carry):
        for k in range(TOP_K):
            row_copy(t, k, slot_ref[t * SLOT_STRIDE + k]).start(priority=k % 2)
        return carry

    inline = min(COMBINE_INLINE_ROWS, tc)
    lax.fori_loop(0, tc - inline, start, 0)
    for t in range(tc - inline, tc):
        start(t, 0)
    half = wg_ref.shape[0] // 2
    h_lo, h_hi = _unpack_halves(h2_ref[0])
    h_lo, h_hi = h_lo.astype(BF16), h_hi.astype(BF16)
    g = (jnp.dot(h_lo, wg_ref[0:half], preferred_element_type=F32)
         + jnp.dot(h_hi, wg_ref[half:], preferred_element_type=F32))
    u = (jnp.dot(h_lo, wu_ref[0:half], preferred_element_type=F32)
         + jnp.dot(h_hi, wu_ref[half:], preferred_element_type=F32))
    acc = jnp.dot((_silu(g) * u).astype(BF16), wd_ref[...], preferred_element_type=F32)
    for k in range(TOP_K):
        pltpu.make_async_copy(ys_ref.at[pl.ds(0, tc)], buf.at[k], sem).wait()
    gw = gw_ref[...]
    acc_lo, acc_hi = acc[:, :half], acc[:, half:]
    for k in range(TOP_K):
        y_lo, y_hi = _unpack_halves(buf[k])
        acc_lo = acc_lo + gw[:, k:k + 1] * y_lo
        acc_hi = acc_hi + gw[:, k:k + 1] * y_hi
    gate = mod_ref[0, 5:6, :]
    o_ref[0, :, 0:half] = x1_ref[0, :, 0:half] + gate[:, :half] * acc_lo
    o_ref[0, :, half:] = x1_ref[0, :, half:] + gate[:, half:] * acc_hi


def _combine_call(slots_flat, x1, h2, gw_tok, mod3, wg, wu, wd, ys, tc=512):
    b, s, d = x1.shape
    nt = s // tc
    full = lambda a: pl.BlockSpec(a.shape, lambda bi, i: (0,) * a.ndim)
    return pl.pallas_call(
        functools.partial(_combine_kernel, tc=tc),
        grid=(b, nt),
        in_specs=[pl.BlockSpec((tc * SLOT_STRIDE,), lambda bi, i: (bi * nt + i,), memory_space=pltpu.SMEM),
                  pl.BlockSpec((1, tc, d), lambda bi, i: (bi, i, 0)),
                  pl.BlockSpec((1, tc, d // 2), lambda bi, i: (bi, i, 0)),
                  pl.BlockSpec((tc, 8), lambda bi, i: (bi * nt + i, 0)),
                  pl.BlockSpec((1, 6, d), lambda bi, i: (bi, 0, 0)),
                  full(wg), full(wu), full(wd),
                  pl.BlockSpec(memory_space=pl.ANY)],
        out_specs=pl.BlockSpec((1, tc, d), lambda bi, i: (bi, i, 0)),
        out_shape=jax.ShapeDtypeStruct((b, s, d), F32),
        scratch_shapes=[pltpu.VMEM((TOP_K, tc, d // 2), U32), pltpu.SemaphoreType.DMA(())],
        compiler_params=_cparams(("arbitrary", "arbitrary")),
        name="combine",
    )(slots_flat, x1, h2, gw_tok, mod3, wg, wu, wd, ys)


def _pad_rows(w, rows, at=0):
    out = jnp.zeros((rows, w.shape[1]), w.dtype)
    return out.at[at:at + w.shape[0]].set(w)


def _mixing_stage(p, rwkv_shift, w0_f, w_up_f, w0_b, w_up_b, a0_f, a_up_f, a0_b, a_up_b, g_up,
                  k_k, k_a, r_k, ln_x_w, ln_x_b, q_norm_w, k_norm_w, rel_bias):
    d_rwkv = w0_f.shape[0]
    nhp = d_rwkv // LANES
    mbd = _head_sum_matrix()
    n_shift = rwkv_shift.shape[1]
    sw = jnp.pad(rwkv_shift, ((0, 0), (0, N_SHIFT_BLOCKS * LANES - n_shift))).reshape(3, N_SHIFT_BLOCKS, LANES)
    wupf = _pad_rows(w_up_f, LANES, 0).astype(BF16)
    wupb = _pad_rows(w_up_b, LANES, DECAY_LORA).astype(BF16)
    aupf = _pad_rows(a_up_f, LANES, 0).astype(BF16)
    aupb = _pad_rows(a_up_b, LANES, AAA_LORA).astype(BF16)
    gup = _pad_rows(g_up, 2 * LANES, 0).astype(BF16)
    vecs = jnp.stack([w0_f, w0_b, a0_f, a0_b, k_k, k_a, r_k.reshape(-1)]).reshape(7, nhp, LANES)
    rb, lw = _prep_call(p, sw, wupf, wupb, aupf, aupb, gup, vecs, mbd, nhp)
    y_f, y_b = _wkv_call(rb, lw, nhp)
    y_rwkv = _post_call(y_f, y_b, rb, ln_x_w.reshape(nhp, LANES), ln_x_b.reshape(nhp, LANES), mbd, nhp)
    bias = _na_bias_table(rel_bias)
    qw = jnp.tile(q_norm_w, HEADS_PER_BLOCK).reshape(1, LANES)
    kw = jnp.tile(k_norm_w, HEADS_PER_BLOCK).reshape(1, LANES)
    y_na = _na_call(p, bias, qw, kw, mbd, N_SHIFT_BLOCKS, nhp)
    return y_rwkv, y_na


def _moe_stage(x1, h2_tiles, h2_rows, lg_t, mod3, router_bias, w_gate_e, w_up_e, w_down_e, w_gate_s, w_up_s,
               w_down_s):
    b, s, d = x1.shape
    t = b * s
    m = DISPATCH_BLOCK
    bias2 = jnp.broadcast_to(router_bias.astype(F32)[:, None], (N_EXPERTS, LANES))
    idx_t, gw_t, pos_t, cnt = _route_call(lg_t, bias2)
    counts = cnt[:, 0].astype(I32)
    padded = (counts + m - 1) // m * m
    pad_end = jnp.cumsum(padded)
    pad_start = pad_end - padded
    n_blocks = -(-(t * TOP_K + N_EXPERTS * m) // m)
    n_valid = (pad_end[-1] // m).astype(I32).reshape(1)
    block_row0 = jnp.arange(n_blocks, dtype=I32) * m
    block_e = jnp.minimum(jnp.sum((pad_end[None, :] <= block_row0[:, None]).astype(I32), axis=1),
                          N_EXPERTS - 1).astype(I32)
    assert m % ZERO_ROWS == 0
    expert_ids = jnp.arange(N_EXPERTS, dtype=I32)
    start_of = jnp.sum(jnp.where(idx_t[:, :, None] == expert_ids, pad_start, 0), axis=-1)
    slots = (start_of + pos_t).astype(I32)
    slots_flat = slots.T.reshape(-1)
    gw_tok = gw_t.T
    fill_lo = jnp.concatenate([pad_start + counts, pad_end[-1:]]).astype(I32)
    fill_hi = jnp.concatenate([pad_end, jnp.full((1,), n_blocks * m, I32)]).astype(I32)
    xs = _dispatch_call(slots_flat, fill_lo, fill_hi, h2_tiles.reshape(t, *h2_tiles.shape[2:]), n_blocks * m)
    row_end = (pad_start + counts).astype(I32)
    nonempty = counts > 0
    first = jnp.logical_and(block_row0 == pad_start[block_e], block_row0 < pad_end[-1]).astype(I32)
    ordinal = jnp.cumsum(nonempty.astype(I32)) - 1
    par = (ordinal[block_e] & 1).astype(I32)
    cand = jnp.where(nonempty, expert_ids, N_EXPERTS)
    later = jnp.concatenate([lax.cummin(cand[::-1])[::-1][1:], jnp.full((1,), N_EXPERTS, I32)])
    nxt = jnp.where(later < N_EXPERTS, later, -1)[block_e].astype(I32)
    rows_valid = jnp.clip(row_end[block_e] - block_row0, 0, m).astype(I32)
    ys = _experts_call(block_e, first, nxt, par, rows_valid, n_valid, xs, w_gate_e, w_up_e, w_down_e)
    return _combine_call(slots_flat, x1, h2_rows, gw_tok, mod3, w_gate_s.astype(BF16), w_up_s.astype(BF16),
                         w_down_s.astype(BF16), ys)


def kernel(x, c, w_ada, b_ada, norm1_w, w_in, rwkv_shift, w0_f, w_up_f, w0_b, w_up_b, a0_f, a_up_f, a0_b,
           a_up_b, g_up, k_k, k_a, r_k, ln_x_w, ln_x_b, q_norm_w, k_norm_w, rel_bias, w_out, norm2_w,
           w_router, router_bias, w_gate_e, w_up_e, w_down_e, w_gate_s, w_up_s, w_down_s):
    bn, sn, d = x.shape
    depth = w_ada.shape[0]
    for l in range(depth):
        c_pad = jnp.pad(c, ((0, 8 - bn % 8 if bn % 8 else 0), (0, 0)))
        mod = _mod_call(c_pad, w_ada[l], b_ada[l].reshape(1, -1))[:bn]
        mod3 = mod.reshape(bn, 6, d)
        rwkv_in = rwkv_shift.shape[-1]
        pad_cols = N_SHIFT_BLOCKS * LANES - rwkv_in
        w_in_r = _widen_call(w_in, l, rwkv_in, pad_cols)
        p = _proj_call(x, mod3, norm1_w[l].reshape(1, d), w_in_r)
        y_rwkv, y_na = _mixing_stage(p, rwkv_shift[l], w0_f[l], w_up_f[l], w0_b[l], w_up_b[l], a0_f[l],
                                     a_up_f[l], a0_b[l], a_up_b[l], g_up[l], k_k[l], k_a[l], r_k[l],
                                     ln_x_w[l], ln_x_b[l], q_norm_w[l], k_norm_w[l], rel_bias[l])
        x1, h2_tiles, h2_rows, lg_t = _outproj_call(y_rwkv, y_na, x, mod3, norm2_w[l].reshape(1, d),
                                                    w_out[l].astype(BF16), w_router[l].T)
        x = _moe_stage(x1, h2_tiles, h2_rows, lg_t, mod3, router_bias[l], w_gate_e[l], w_up_e[l], w_down_e[l],
                       w_gate_s[l], w_up_s[l], w_down_s[l])
    return x
```

```python
import functools

import jax
import jax.numpy as jnp
import numpy as np
from jax import lax
from jax.experimental import pallas as pl
from jax.experimental.pallas import tpu as pltpu

F32 = jnp.float32
BF16 = jnp.bfloat16
I32 = jnp.int32
HI = lax.Precision.HIGHEST

LANES = 128
HEAD_DIM = 64
HEADS_PER_BLOCK = LANES // HEAD_DIM
GRID_W = 64
WIN_R = 8
WIN_C = 16
DECAY_LORA = 64
AAA_LORA = 64
GATE_LORA = 160
N_EXPERTS = 64
TOP_K = 6
SLOT_STRIDE = 8
N_GROUPS = 8
TOPK_GROUPS = 4
ROUTED_SCALE = 2.5
DISPATCH_BLOCK = 512
EXPERT_BLOCK_PARTS = 4
NORM_EPS = 1e-6
GN_EPS = 64e-5
L2_EPS = 1e-12
NEG_BIG = -1e30
WKV_CHUNK = 64
VMEM_LIMIT = 56 * 1024 * 1024

NT_DIMS = (((1,), (1,)), ((), ()))
TN_DIMS = (((0,), (0,)), ((), ()))


def _cparams(sem):
    return pltpu.CompilerParams(dimension_semantics=sem, vmem_limit_bytes=VMEM_LIMIT)


def _sigmoid(x):
    return 1.0 / (1.0 + jnp.exp(-x))


def _silu(x):
    return x * _sigmoid(x)


U32 = jnp.uint32
HI_HALF_MASK = 0xFFFF0000


def _pack_halves(x):
    n = x.shape[-1] // 2
    bits = lax.bitcast_convert_type(x.astype(BF16).astype(F32), U32)
    return (bits[:, :n] >> 16) | bits[:, n:]


def _unpack_halves(w):
    lo = lax.bitcast_convert_type(w << 16, F32)
    hi = lax.bitcast_convert_type(w & U32(HI_HALF_MASK), F32)
    return lo, hi


def _head_sum_matrix():
    a = np.arange(LANES) // HEAD_DIM
    return jnp.asarray((a[:, None] == a[None, :]).astype(np.float32))


def _head_sums_split(x, mbd):
    m16 = mbd.astype(BF16)
    hi = x.astype(BF16)
    lo = (x - hi.astype(F32)).astype(BF16)
    return jnp.dot(hi, m16, preferred_element_type=F32) + jnp.dot(lo, m16, preferred_element_type=F32)


def _mod_kernel(c_ref, w_ref, b_ref, o_ref):
    c = c_ref[...]
    o_ref[...] = jnp.dot(_silu(c), w_ref[...], preferred_element_type=F32, precision=HI) + b_ref[...]


def _mod_call(c_pad, w_ada, b_ada):
    rows, d = c_pad.shape
    n = w_ada.shape[1]
    tn = 1024
    return pl.pallas_call(
        _mod_kernel,
        grid=(n // tn,),
        in_specs=[pl.BlockSpec((rows, d), lambda j: (0, 0)),
                  pl.BlockSpec((d, tn), lambda j: (0, j)),
                  pl.BlockSpec((1, tn), lambda j: (0, j))],
        out_specs=pl.BlockSpec((rows, tn), lambda j: (0, j)),
        out_shape=jax.ShapeDtypeStruct((rows, n), F32),
        compiler_params=_cparams(("parallel",)),
        name="mod",
    )(c_pad, w_ada, b_ada)


def _widen_kernel(w_ref, o_ref, *, split, pad):
    w = w_ref[...]
    o_ref[...] = jnp.concatenate([w[:, :split], jnp.zeros((w.shape[0], pad), w.dtype), w[:, split:]],
                                 axis=1).astype(o_ref.dtype)


def _widen_call(w_stack, layer, split, pad, tr=256):
    _, rows, cols = w_stack.shape
    return pl.pallas_call(
        functools.partial(_widen_kernel, split=split, pad=pad),
        grid=(rows // tr,),
        in_specs=[pl.BlockSpec((None, tr, cols), lambda i: (layer, i, 0))],
        out_specs=pl.BlockSpec((tr, cols + pad), lambda i: (i, 0)),
        out_shape=jax.ShapeDtypeStruct((rows, cols + pad), BF16),
        compiler_params=_cparams(("parallel",)),
        name="widen",
    )(w_stack)


def _proj_kernel(x_ref, mod_ref, nw_ref, w_ref, o_ref, h_sc, *, nq):
    @pl.when(pl.program_id(2) == 0)
    def _():
        x = x_ref[0]
        ms = jnp.mean(x * x, axis=-1, keepdims=True)
        y = x * lax.rsqrt(ms + NORM_EPS) * nw_ref[...]
        h_sc[...] = (y * (1.0 + mod_ref[0, 1:2, :]) + mod_ref[0, 0:1, :]).astype(BF16)

    acc = jnp.dot(h_sc[...], w_ref[...], preferred_element_type=F32)
    for q in range(nq):
        o_ref[0, q] = acc[:, q * LANES:(q + 1) * LANES].astype(o_ref.dtype)


def _proj_call(x, mod3, norm_w, w_in_r, tm=1024, tn=1664):
    b, s, d = x.shape
    n = w_in_r.shape[1]
    nq = tn // LANES
    return pl.pallas_call(
        functools.partial(_proj_kernel, nq=nq),
        grid=(b, s // tm, n // tn),
        in_specs=[pl.BlockSpec((1, tm, d), lambda bi, i, j: (bi, i, 0)),
                  pl.BlockSpec((1, 6, d), lambda bi, i, j: (bi, 0, 0)),
                  pl.BlockSpec((1, d), lambda bi, i, j: (0, 0)),
                  pl.BlockSpec((d, tn), lambda bi, i, j: (0, j))],
        out_specs=pl.BlockSpec((1, nq, tm, LANES), lambda bi, i, j: (bi, j, i, 0)),
        out_shape=jax.ShapeDtypeStruct((b, n // LANES, s, LANES), BF16),
        scratch_shapes=[pltpu.VMEM((tm, d), BF16)],
        compiler_params=_cparams(("parallel", "parallel", "arbitrary")),
        name="proj",
    )(x, mod3, norm_w, w_in_r)


RB_R, RB_V, RB_KK, RB_G, RB_KF, RB_KB, RB_AF, RB_AB, RB_BONUS = range(9)
VEC_W0F, VEC_W0B, VEC_A0F, VEC_A0B, VEC_KK, VEC_KA, VEC_RK = range(7)
N_RKV_BLOCKS = 24
N_SHIFT_BLOCKS = 28


def _softplus(u):
    return jnp.maximum(u, 0.0) + jnp.log(1.0 + jnp.exp(-jnp.abs(u)))


def _prep_kernel(p_ref, pp_ref, pn_ref, sw_ref, wupf_ref, wupb_ref, aupf_ref, aupb_ref, gup_ref,
                 vec_ref, mbd_ref, rb_ref, lw_ref, *, tb, n_t, nhp):
    i = pl.program_id(1)
    row = lax.broadcasted_iota(I32, (tb, LANES), 0)
    has_prev = i > 0
    has_next = i < n_t - 1
    halo = pp_ref.shape[2]

    def shifted(q):
        cur = p_ref[0, q].astype(F32)
        prev_row = jnp.where(has_prev, pp_ref[0, q, halo - 1:halo, :].astype(F32), 0.0)
        next_row = jnp.where(has_next, pn_ref[0, q, 0:1, :].astype(F32), 0.0)
        up = jnp.where(row == 0, prev_row, pltpu.roll(cur, 1, 0))
        dn = jnp.where(row == tb - 1, next_row, pltpu.roll(cur, tb - 1, 0))
        return sw_ref[0, q:q + 1, :] * up + sw_ref[1, q:q + 1, :] * cur + sw_ref[2, q:q + 1, :] * dn

    mbd = mbd_ref[...]
    t_wd = jnp.tanh(shifted(N_RKV_BLOCKS)).astype(BF16)
    z_ad = shifted(N_RKV_BLOCKS + 1).astype(BF16)
    s_gd = jnp.concatenate([_sigmoid(shifted(N_RKV_BLOCKS + 2)),
                            _sigmoid(shifted(N_RKV_BLOCKS + 3))], axis=1).astype(BF16)

    def vec(v, hp):
        return vec_ref[v, hp:hp + 1, :]

    for hp in range(nhp):
        sl = slice(hp * LANES, (hp + 1) * LANES)
        r = shifted(hp)
        k = shifted(nhp + hp)
        v = shifted(2 * nhp + hp)
        g = jnp.dot(s_gd, gup_ref[:, sl], preferred_element_type=F32)
        kk0 = k * vec(VEC_KK, hp)
        ss = _head_sums_split(kk0 * kk0, mbd)
        kk = kk0 * lax.rsqrt(jnp.maximum(ss, L2_EPS))
        kdirs = []
        for wup_ref, aup_ref, v_w0, v_a0, q_k, q_a, lw_slot in (
                (wupf_ref, aupf_ref, VEC_W0F, VEC_A0F, RB_KF, RB_AF, 0),
                (wupb_ref, aupb_ref, VEC_W0B, VEC_A0B, RB_KB, RB_AB, 1)):
            wl = vec(v_w0, hp) + jnp.dot(t_wd, wup_ref[:, sl], preferred_element_type=F32)
            w_log = -_softplus(-wl) - 0.5
            lw_ref[0, lw_slot * nhp + hp] = -jnp.exp(w_log)
            ag = _sigmoid(vec(v_a0, hp) + jnp.dot(z_ad, aup_ref[:, sl], preferred_element_type=F32))
            kd = k * (1.0 + (ag - 1.0) * vec(VEC_KA, hp))
            rb_ref[0, q_k * nhp + hp] = kd.astype(BF16)
            rb_ref[0, q_a * nhp + hp] = ag.astype(BF16)
            kdirs.append(kd)
        bonus = _head_sums_split(r * kdirs[0] * vec(VEC_RK, hp), mbd) * v
        rb_ref[0, RB_R * nhp + hp] = r.astype(BF16)
        rb_ref[0, RB_V * nhp + hp] = v.astype(BF16)
        rb_ref[0, RB_KK * nhp + hp] = kk.astype(BF16)
        rb_ref[0, RB_G * nhp + hp] = g.astype(BF16)
        rb_ref[0, RB_BONUS * nhp + hp] = bonus.astype(BF16)


def _prep_call(p, sw, wupf, wupb, aupf, aupb, gup, vecs, mbd, nhp, tb=256):
    b, _, s, _ = p.shape
    n_t = s // tb
    halo = 16
    hb = tb // halo
    full = lambda a: pl.BlockSpec(a.shape, lambda bi, i: (0,) * a.ndim)
    return pl.pallas_call(
        functools.partial(_prep_kernel, tb=tb, n_t=n_t, nhp=nhp),
        grid=(b, n_t),
        in_specs=[pl.BlockSpec((1, N_SHIFT_BLOCKS, tb, LANES), lambda bi, i: (bi, 0, i, 0)),
                  pl.BlockSpec((1, N_SHIFT_BLOCKS, halo, LANES),
                               lambda bi, i: (bi, 0, jnp.maximum(i * hb - 1, 0), 0)),
                  pl.BlockSpec((1, N_SHIFT_BLOCKS, halo, LANES),
                               lambda bi, i: (bi, 0, jnp.minimum((i + 1) * hb, s // halo - 1), 0)),
                  full(sw), full(wupf), full(wupb), full(aupf), full(aupb), full(gup), full(vecs), full(mbd)],
        out_specs=[pl.BlockSpec((1, 9 * nhp, tb, LANES), lambda bi, i: (bi, 0, i, 0)),
                   pl.BlockSpec((1, 2 * nhp, tb, LANES), lambda bi, i: (bi, 0, i, 0))],
        out_shape=[jax.ShapeDtypeStruct((b, 9 * nhp, s, LANES), BF16),
                   jax.ShapeDtypeStruct((b, 2 * nhp, s, LANES), F32)],
        compiler_params=_cparams(("parallel", "parallel")),
        name="prep",
    )(p, p, p, sw, wupf, wupb, aupf, aupb, gup, vecs, mbd)


def _wkv_kernel(r_ref, v_ref, kk_ref, kf_ref, af_ref, lf_ref, rr_ref, vr_ref, kkr_ref, kb_ref, ab_ref, lb_ref,
                yf_ref, yb_ref, s_sc, *, L, nhp):
    @pl.when(pl.program_id(1) == 0)
    def _():
        s_sc[...] = jnp.zeros(s_sc.shape, F32)

    assert HEADS_PER_BLOCK == 2 and L == HEAD_DIM
    n2 = HEADS_PER_BLOCK * L
    ri = lax.broadcasted_iota(I32, (n2, LANES), 0)
    ci = lax.broadcasted_iota(I32, (n2, LANES), 1)
    head_match = (ri // L) == (ci // HEAD_DIM)
    st = lax.broadcasted_iota(I32, (L, LANES), 0)
    ss = lax.broadcasted_iota(I32, (L, LANES), 1) % L
    first_head = lax.broadcasted_iota(I32, (L, LANES), 1) < HEAD_DIM
    eye = (st == ss).astype(F32)
    strict = {False: st > ss, True: st < ss}
    incl = {False: st >= ss, True: st < ss}

    def stack(x):
        return jnp.where(head_match, jnp.concatenate([x] * HEADS_PER_BLOCK, axis=0), 0.0)

    def bdiag(x):
        return jnp.concatenate([jnp.where(first_head, x, 0.0), jnp.where(first_head, 0.0, x)], axis=0)

    chains = [(False, hp, r_ref, v_ref, kk_ref, kf_ref, af_ref, lf_ref) for hp in range(nhp)]
    chains += [(True, hp, rr_ref, vr_ref, kkr_ref, kb_ref, ab_ref, lb_ref) for hp in range(nhp)]
    n = len(chains)
    lws = [ch[7][0, ch[1]] for ch in chains]
    t_row = lax.broadcasted_iota(I32, (L, LANES), 0)

    def cumsum_time(x, rev):
        sh = 1
        while sh < L:
            if rev:
                x = x + jnp.where(t_row < L - sh, pltpu.roll(x, L - sh, 0), 0.0)
            else:
                x = x + jnp.where(t_row >= sh, pltpu.roll(x, sh, 0), 0.0)
            sh *= 2
        return x

    st = [dict() for _ in range(n)]

    def prologue(i):
        rev, hp, rr, vr, kkr, kdr, agr, _ = chains[i]
        r = rr[0, hp].astype(F32)
        kk = kkr[0, hp].astype(F32)
        kd = kdr[0, hp].astype(F32)
        ag = agr[0, hp].astype(F32)
        v = vr[0, hp].astype(F32)
        c_in = cumsum_time(lws[i], rev)
        e_ex = jnp.exp(c_in - lws[i])
        e_inv = jnp.exp(-c_in)
        if rev:
            e_r = e_ex
            p_last = jnp.exp(c_in[0:1])
        else:
            e_r = jnp.exp(c_in)
            p_last = jnp.exp(c_in[L - 1:L])
        b_t = kk * ag * e_inv
        k_t = kd * e_inv
        st[i].update(
            lhs=jnp.concatenate([-kk * e_ex, r * e_r], axis=0).astype(BF16),
            rhs=jnp.concatenate([stack(b_t), stack(k_t)], axis=0).astype(BF16),
            bk=jnp.concatenate([b_t * p_last, k_t * p_last], axis=0).astype(BF16),
            v=v.astype(BF16), vbd=bdiag(v).astype(BF16), p_last=p_last)

    def stage_scores(i):
        c = st[i]
        c['g'] = lax.dot_general(c['lhs'], c['rhs'], NT_DIMS, preferred_element_type=F32)
        c['s_old'] = s_sc[i]
        c['ls'] = lax.dot_general(c['lhs'], c['s_old'].astype(BF16), NT_DIMS, preferred_element_type=F32)

    def stage_solve_setup(i):
        c = st[i]
        rev, g = chains[i][0], c.pop('g')
        a_ab = jnp.where(strict[rev], g[0:L, 0:n2], 0.0)
        a_ak = jnp.where(strict[rev], g[0:L, n2:2 * n2], 0.0).astype(BF16)
        c['a_r'] = jnp.concatenate([jnp.where(incl[rev], g[L:2 * L, 0:n2], 0.0),
                                    jnp.where(incl[rev], g[L:2 * L, n2:2 * n2], 0.0)], axis=1).astype(BF16)
        c['x'] = c['ls'][0:L] + jnp.dot(a_ak, c['vbd'], preferred_element_type=F32)
        c['pw'] = jnp.dot(a_ab.astype(BF16), bdiag(a_ab).astype(BF16), preferred_element_type=F32)
        c['t'] = eye + a_ab

    def stage_round(i):
        c = st[i]
        both = jnp.dot(jnp.concatenate([c['pw'], c['t']], axis=0).astype(BF16), bdiag(c['pw']).astype(BF16),
                       preferred_element_type=F32)
        c['pw'] = both[0:L]
        c['t'] = c['t'] + both[L:2 * L]

    def stage_last_round(i):
        c = st[i]
        c['t'] = c['t'] + jnp.dot(c['t'].astype(BF16), bdiag(c.pop('pw')).astype(BF16),
                                  preferred_element_type=F32)

    def stage_solve(i):
        c = st[i]
        c['u'] = jnp.dot(c.pop('t').astype(BF16), bdiag(c.pop('x')).astype(BF16), preferred_element_type=F32)

    def stage_out(i):
        c = st[i]
        rev, hp = chains[i][0], chains[i][1]
        u = c['u']
        y = c['ls'][L:2 * L] + jnp.dot(c['a_r'], jnp.concatenate([bdiag(u).astype(BF16), c['vbd']], axis=0),
                                       preferred_element_type=F32)
        upd = lax.dot_general(jnp.concatenate([u.astype(BF16), c['v']], axis=0), c['bk'], TN_DIMS,
                              preferred_element_type=F32)
        (yb_ref if rev else yf_ref)[0, hp] = y
        s_sc[i] = c['s_old'] * c['p_last'] + jnp.where(head_match, upd, 0.0)

    rounds = L.bit_length() - 1
    stages = ([stage_scores, stage_solve_setup] + [stage_round] * (rounds - 2)
              + [stage_last_round, stage_solve, stage_out])
    for i in range(n):
        prologue(i)
    for stage in stages:
        for i in range(n):
            stage(i)


def _wkv_call(rb, lw, nhp, L=WKV_CHUNK):
    b, _, s, _ = rb.shape
    nc = s // L
    fwd = lambda q: pl.BlockSpec((1, nhp, L, LANES), lambda bi, c: (bi, q, c, 0))
    bwd = lambda q: pl.BlockSpec((1, nhp, L, LANES), lambda bi, c: (bi, q, nc - 1 - c, 0))
    y_shape = jax.ShapeDtypeStruct((b, nhp, s, LANES), F32)
    return pl.pallas_call(
        functools.partial(_wkv_kernel, L=L, nhp=nhp),
        grid=(b, nc),
        in_specs=[fwd(RB_R), fwd(RB_V), fwd(RB_KK), fwd(RB_KF), fwd(RB_AF), fwd(0),
                  bwd(RB_R), bwd(RB_V), bwd(RB_KK), bwd(RB_KB), bwd(RB_AB), bwd(1)],
        out_specs=[fwd(0), bwd(0)],
        out_shape=[y_shape, y_shape],
        scratch_shapes=[pltpu.VMEM((2 * nhp, LANES, LANES), F32)],
        compiler_params=_cparams(("parallel", "arbitrary")),
        name="wkv",
    )(rb, rb, rb, rb, rb, lw, rb, rb, rb, rb, rb, lw)


def _post_kernel(yf_ref, yb_ref, bonus_ref, g_ref, lnw_ref, lnb_ref, mbd_ref, o_ref, *, nhp):
    mbd = mbd_ref[...]
    inv_n = 1.0 / HEAD_DIM
    for hp in range(nhp):
        y = yf_ref[0, hp] + yb_ref[0, hp]
        mu = _head_sums_split(y, mbd) * inv_n
        d = y - mu
        var = _head_sums_split(d * d, mbd) * inv_n
        yn = d * lax.rsqrt(var + GN_EPS) * lnw_ref[hp:hp + 1, :] + lnb_ref[hp:hp + 1, :]
        out = (yn + bonus_ref[0, hp].astype(F32)) * g_ref[0, hp].astype(F32)
        o_ref[0, :, hp * LANES:(hp + 1) * LANES] = out.astype(o_ref.dtype)


def _post_call(yf, yb, rb, lnw, lnb, mbd, nhp, tb=512):
    b, _, s, _ = yf.shape
    yblk = pl.BlockSpec((1, nhp, tb, LANES), lambda bi, i: (bi, 0, i, 0))
    rblk = lambda q: pl.BlockSpec((1, nhp, tb, LANES), lambda bi, i: (bi, q, i, 0))
    full = lambda a: pl.BlockSpec(a.shape, lambda bi, i: (0,) * a.ndim)
    return pl.pallas_call(
        functools.partial(_post_kernel, nhp=nhp),
        grid=(b, s // tb),
        in_specs=[yblk, yblk, rblk(RB_BONUS), rblk(RB_G), full(lnw), full(lnb), full(mbd)],
        out_specs=pl.BlockSpec((1, tb, nhp * LANES), lambda bi, i: (bi, i, 0)),
        out_shape=jax.ShapeDtypeStruct((b, s, nhp * LANES), BF16),
        compiler_params=_cparams(("parallel", "parallel")),
        name="post",
    )(yf, yb, rb, rb, lnw, lnb, mbd)


def _na_kernel(q_ref, k_ref, v_ref, bias_ref, qw_ref, kw_ref, mbd_ref, o_ref, kn_sc, *, rq, rows):
    rbi = pl.program_id(2)
    mbd = mbd_ref[...]
    inv_n = 1.0 / HEAD_DIM
    span = WIN_R * GRID_W
    s_len = kn_sc.shape[0]

    @pl.when(rbi == 0)
    def _():
        def body(c, carry):
            off = pl.multiple_of(c * span, span)
            k = k_ref[0, 0, pl.ds(off, span), :].astype(F32)
            ms = _head_sums_split(k * k, mbd) * inv_n
            kn_sc[pl.ds(off, span), :] = (k * lax.rsqrt(ms + NORM_EPS) * kw_ref[...]).astype(BF16)
            return carry
        lax.fori_loop(0, s_len // span, body, 0)

    ri = lax.broadcasted_iota(I32, (HEADS_PER_BLOCK * GRID_W, LANES), 0)
    ci = lax.broadcasted_iota(I32, (HEADS_PER_BLOCK * GRID_W, LANES), 1)
    head_match = (ri // GRID_W) == (ci // HEAD_DIM)
    lane = lax.broadcasted_iota(I32, (GRID_W, LANES), 1)
    scale = HEAD_DIM ** -0.5
    q_all = q_ref[0, 0].astype(F32)
    ms = _head_sums_split(q_all * q_all, mbd) * inv_n
    qn_all = q_all * lax.rsqrt(ms + NORM_EPS) * qw_ref[...] * scale
    starts, offs = [], []
    for qi in range(rq):
        i = rbi * rq + qi
        rs = jnp.clip(i - WIN_R // 2, 0, rows - WIN_R)
        offs.append(rs - i + WIN_R - 1)
        starts.append(pl.multiple_of(rs * GRID_W, GRID_W))
    qs_l = [jnp.where(head_match, jnp.concatenate([qn_all[qi * GRID_W:(qi + 1) * GRID_W]] * HEADS_PER_BLOCK,
                                                  axis=0), 0.0).astype(BF16) for qi in range(rq)]
    def bias(off):
        return jnp.concatenate([bias_ref[0, off + 2 * i] for i in range(WIN_R // 2)], axis=1)

    s_l = [lax.dot_general(qs_l[qi], kn_sc[pl.ds(starts[qi], span), :], NT_DIMS, preferred_element_type=F32)
           + bias(offs[qi]) for qi in range(rq)]
    p_l, l_l = [], []
    for s in s_l:
        p = jnp.exp(s - jnp.max(s, axis=-1, keepdims=True))
        p_l.append(p.astype(BF16))
        l_l.append(jnp.sum(p, axis=-1, keepdims=True))
    o_l = [jnp.dot(p_l[qi], v_ref[0, 0, pl.ds(starts[qi], span), :], preferred_element_type=F32) / l_l[qi]
           for qi in range(rq)]
    for qi, o in enumerate(o_l):
        out = o[0:GRID_W]
        for h in range(1, HEADS_PER_BLOCK):
            out = jnp.where(lane // HEAD_DIM == h, o[h * GRID_W:(h + 1) * GRID_W], out)
        o_ref[0, qi * GRID_W:(qi + 1) * GRID_W, :] = out.astype(o_ref.dtype)


def _na_call(p, bias, qw, kw, mbd, q_blk0, nhp, rq=64):
    b, _, s, _ = p.shape
    rows = s // GRID_W
    rq = min(rq, rows)
    full = lambda a: pl.BlockSpec(a.shape, lambda bi, hp, r: (0,) * a.ndim)
    return pl.pallas_call(
        functools.partial(_na_kernel, rq=rq, rows=rows),
        grid=(b, nhp, rows // rq),
        in_specs=[pl.BlockSpec((1, 1, rq * GRID_W, LANES), lambda bi, hp, r: (bi, q_blk0 + hp, r, 0)),
                  pl.BlockSpec((1, 1, s, LANES), lambda bi, hp, r: (bi, q_blk0 + nhp + hp, 0, 0)),
                  pl.BlockSpec((1, 1, s, LANES), lambda bi, hp, r: (bi, q_blk0 + 2 * nhp + hp, 0, 0)),
                  pl.BlockSpec((1,) + bias.shape[1:], lambda bi, hp, r: (hp, 0, 0, 0)),
                  full(qw), full(kw), full(mbd)],
        out_specs=pl.BlockSpec((1, rq * GRID_W, LANES), lambda bi, hp, r: (bi, r, hp)),
        out_shape=jax.ShapeDtypeStruct((b, s, nhp * LANES), BF16),
        scratch_shapes=[pltpu.VMEM((s, LANES), BF16)],
        compiler_params=_cparams(("parallel", "parallel", "arbitrary")),
        name="na",
    )(p, p, p, bias, qw, kw, mbd)


def _na_bias_table(rel_bias):
    h = rel_bias.shape[0]
    qc = np.arange(GRID_W)
    kc = np.arange(GRID_W)
    win0 = np.clip(qc - WIN_C // 2, 0, GRID_W - WIN_C)
    valid = (kc[None, :] >= win0[:, None]) & (kc[None, :] < win0[:, None] + WIN_C)
    coff = np.clip(kc[None, :] - qc[:, None] + WIN_C - 1, 0, 2 * WIN_C - 2)
    n_row, n_col = 2 * WIN_R - 1, 2 * WIN_C - 1
    n_ro = n_row - 1
    sel = np.zeros((n_ro, 2, n_row), np.float32)
    for ro in range(n_ro):
        for part in range(2):
            sel[ro, part, ro + part] = 1.0
    pick = np.zeros((2, n_col, GRID_W, 2 * GRID_W), np.float32)
    for part in range(2):
        pick[part, coff, qc[:, None], part * GRID_W + kc[None, :]] = 1.0
    a = jnp.einsum('hrc,apr->hapc', rel_bias.astype(F32), jnp.asarray(sel), precision=HI)
    a = a.reshape(h // HEADS_PER_BLOCK, HEADS_PER_BLOCK, n_ro, 2 * n_col).transpose(0, 2, 1, 3)
    t = jnp.einsum('ahbx,xql->ahbql', a, jnp.asarray(pick.reshape(2 * n_col, GRID_W, 2 * GRID_W)),
                   precision=HI)
    t = jnp.where(np.tile(valid, (1, 2)), t, NEG_BIG)
    return t.reshape(h // HEADS_PER_BLOCK, n_ro, HEADS_PER_BLOCK * GRID_W, 2 * GRID_W)


def _outproj_kernel(yr_ref, yn_ref, x_ref, mod_ref, nw_ref, w1_ref, w2_ref, wr_ref, x1_ref, h2_ref, h2row_ref,
                    lg_ref):
    tm = x_ref.shape[1]
    sub = LANES
    tiles = [slice(i * sub, (i + 1) * sub) for i in range(tm // sub)]
    accs = [jnp.dot(yr_ref[0, r], w1_ref[...], preferred_element_type=F32)
            + jnp.dot(yn_ref[0, r], w2_ref[...], preferred_element_type=F32) for r in tiles]
    h2s = []
    for r, acc in zip(tiles, accs):
        x1 = x_ref[0, r] + mod_ref[0, 2:3, :] * acc
        x1_ref[0, r] = x1
        ms = jnp.mean(x1 * x1, axis=-1, keepdims=True)
        y = x1 * lax.rsqrt(ms + NORM_EPS) * nw_ref[...]
        h2 = y * (1.0 + mod_ref[0, 4:5, :]) + mod_ref[0, 3:4, :]
        packed = _pack_halves(h2)
        h2row_ref[0, r] = packed
        for j in range(h2_ref.shape[2]):
            h2_ref[0, r, j, :] = packed[:, j * LANES:(j + 1) * LANES]
        h2s.append(h2)
    wr = wr_ref[...]
    wr_hi = wr.astype(BF16)
    wr_lo = (wr - wr_hi.astype(F32)).astype(BF16)
    for r, h2 in zip(tiles, h2s):
        h_hi = h2.astype(BF16)
        h_lo = (h2 - h_hi.astype(F32)).astype(BF16)
        lg_ref[:, r] = (lax.dot_general(wr_hi, h_hi, NT_DIMS, preferred_element_type=F32)
                        + lax.dot_general(wr_hi, h_lo, NT_DIMS, preferred_element_type=F32)
                        + lax.dot_general(wr_lo, h_hi, NT_DIMS, preferred_element_type=F32))


def _outproj_call(yr, yn, x, mod3, norm_w, w_o, wr_t, tm=512):
    b, s, d = x.shape
    dh = yr.shape[-1]
    assert w_o.shape == (2 * dh, d)
    ne = wr_t.shape[0]
    nt = s // tm
    full = lambda a: pl.BlockSpec(a.shape, lambda bi, i: (0,) * a.ndim)
    return pl.pallas_call(
        _outproj_kernel,
        grid=(b, nt),
        in_specs=[pl.BlockSpec((1, tm, dh), lambda bi, i: (bi, i, 0)),
                  pl.BlockSpec((1, tm, dh), lambda bi, i: (bi, i, 0)),
                  pl.BlockSpec((1, tm, d), lambda bi, i: (bi, i, 0)),
                  pl.BlockSpec((1, 6, d), lambda bi, i: (bi, 0, 0)),
                  full(norm_w),
                  pl.BlockSpec((dh, d), lambda bi, i: (0, 0)),
                  pl.BlockSpec((dh, d), lambda bi, i: (1, 0)),
                  full(wr_t)],
        out_specs=[pl.BlockSpec((1, tm, d), lambda bi, i: (bi, i, 0)),
                   pl.BlockSpec((1, tm, d // 2 // LANES, LANES), lambda bi, i: (bi, i, 0, 0)),
                   pl.BlockSpec((1, tm, d // 2), lambda bi, i: (bi, i, 0)),
                   pl.BlockSpec((ne, tm), lambda bi, i: (0, bi * nt + i))],
        out_shape=[jax.ShapeDtypeStruct((b, s, d), F32),
                   jax.ShapeDtypeStruct((b, s, d // 2 // LANES, LANES), U32),
                   jax.ShapeDtypeStruct((b, s, d // 2), U32),
                   jax.ShapeDtypeStruct((ne, b * s), F32)],
        compiler_params=_cparams(("parallel", "parallel")),
        name="outproj",
    )(yr, yn, x, mod3, norm_w, w_o, w_o, wr_t)


def _first_argmax(x, idx, n):
    m = jnp.max(x, axis=0, keepdims=True)
    a = jnp.min(jnp.where(x == m, idx, n), axis=0, keepdims=True)
    return m, a


def _route_kernel(lg_ref, bias_ref, idx_ref, gw_ref, pos_ref, cnt_ref, *, tr):
    @pl.when(pl.program_id(0) == 0)
    def _():
        cnt_ref[...] = jnp.zeros(cnt_ref.shape, F32)

    gsz = N_EXPERTS // N_GROUPS
    scores = _sigmoid(lg_ref[...])
    biased = scores + bias_ref[:, 0:1]
    ig = lax.broadcasted_iota(I32, (gsz, tr), 0)
    grp_rows = []
    for g in range(N_GROUPS):
        blk = biased[g * gsz:(g + 1) * gsz]
        m1, a1 = _first_argmax(blk, ig, gsz)
        m2 = jnp.max(jnp.where(ig == a1, -jnp.inf, blk), axis=0, keepdims=True)
        grp_rows.append(m1 + m2)
    grp = jnp.concatenate(grp_rows, axis=0)
    ign = lax.broadcasted_iota(I32, (N_GROUPS, tr), 0)
    sel = jnp.zeros((N_GROUPS, tr), jnp.bool_)
    for _ in range(TOPK_GROUPS):
        _, a = _first_argmax(grp, ign, N_GROUPS)
        hit = ign == a
        sel = jnp.logical_or(sel, hit)
        grp = jnp.where(hit, -jnp.inf, grp)
    masked = jnp.concatenate(
        [jnp.where(sel[g:g + 1], biased[g * gsz:(g + 1) * gsz], -jnp.inf) for g in range(N_GROUPS)], axis=0)
    ie = lax.broadcasted_iota(I32, (N_EXPERTS, tr), 0)
    picks, pick_scores = [], []
    onehot = jnp.zeros((N_EXPERTS, tr), F32)
    for _ in range(TOP_K):
        _, a = _first_argmax(masked, ie, N_EXPERTS)
        hit = ie == a
        picks.append(a)
        pick_scores.append(jnp.sum(jnp.where(hit, scores, 0.0), axis=0, keepdims=True))
        onehot = onehot + hit.astype(F32)
        masked = jnp.where(hit, -jnp.inf, masked)
    total = pick_scores[0]
    for sc in pick_scores[1:]:
        total = total + sc
    t0 = lax.broadcasted_iota(I32, (tr, tr), 0)
    t1 = lax.broadcasted_iota(I32, (tr, tr), 1)
    before = (t0 < t1).astype(BF16)
    rank = jnp.dot(onehot.astype(BF16), before, preferred_element_type=F32) + cnt_ref[:, 0:1]
    zero_i = jnp.zeros((1, tr), I32)
    zero_f = jnp.zeros((1, tr), F32)
    for k in range(8):
        if k < TOP_K:
            idx_ref[k:k + 1, :] = picks[k]
            gw_ref[k:k + 1, :] = pick_scores[k] / total * ROUTED_SCALE
            pos = jnp.sum(jnp.where(ie == picks[k], rank, 0.0), axis=0, keepdims=True)
            pos_ref[k:k + 1, :] = pos.astype(I32)
        else:
            idx_ref[k:k + 1, :] = zero_i
            gw_ref[k:k + 1, :] = zero_f
            pos_ref[k:k + 1, :] = zero_i
    cnt_ref[...] = cnt_ref[...] + jnp.sum(onehot, axis=1, keepdims=True)


def _route_call(lg_t, bias2, tr=512):
    ne, t = lg_t.shape
    tr = min(tr, t)
    tok = pl.BlockSpec((8, tr), lambda i: (0, i))
    return pl.pallas_call(
        functools.partial(_route_kernel, tr=tr),
        grid=(t // tr,),
        in_specs=[pl.BlockSpec((ne, tr), lambda i: (0, i)),
                  pl.BlockSpec(bias2.shape, lambda i: (0, 0))],
        out_specs=[tok, tok, tok, pl.BlockSpec((ne, LANES), lambda i: (0, 0))],
        out_shape=[jax.ShapeDtypeStruct((8, t), I32), jax.ShapeDtypeStruct((8, t), F32),
                   jax.ShapeDtypeStruct((8, t), I32), jax.ShapeDtypeStruct((ne, LANES), F32)],
        compiler_params=_cparams(("arbitrary",)),
        name="route",
    )(lg_t, bias2)


FILL_CHUNKS = (1, 8, 64)
ZERO_ROWS = FILL_CHUNKS[-1]


def _dispatch_kernel(slot_ref, fill_lo_ref, fill_hi_ref, h_ref, xs_ref, zrow, sem, *, td):
    def row_copy(t, slot):
        return pltpu.make_async_copy(h_ref.at[t], xs_ref.at[slot], sem)

    def start(t, carry):
        for k in range(TOP_K):
            row_copy(t, slot_ref[t * SLOT_STRIDE + k]).start(priority=k % 2)
        return carry

    lax.fori_loop(0, td, start, 0)
    for k in range(TOP_K):
        pltpu.make_async_copy(h_ref, xs_ref.at[pl.ds(0, td)], sem).wait()

    @pl.when(pl.program_id(0) == pl.num_programs(0) - 1)
    def _():
        zrow[...] = jnp.zeros(zrow.shape, zrow.dtype)

        def zero_copy(row, n):
            if n > 1:
                row = pl.multiple_of(row, FILL_CHUNKS[1])
            return pltpu.make_async_copy(zrow.at[pl.ds(0, n)], xs_ref.at[pl.ds(row, n)], sem)

        def for_each_chunk(r, fn):
            lo, hi = fill_lo_ref[r], fill_hi_ref[r]
            for ci, n in enumerate(FILL_CHUNKS):
                if ci + 1 < len(FILL_CHUNKS):
                    count = jnp.minimum(((-lo) % FILL_CHUNKS[ci + 1]) // n, (hi - lo) // n)
                else:
                    count = (hi - lo) // n
                fn(lo, n, count)
                lo = lo + count * n

        def start_range(r, carry):
            def go(lo, n, count):
                lax.fori_loop(0, count, lambda i, c: (zero_copy(lo + i * n, n).start(), c)[1], 0)
            for_each_chunk(r, go)
            return carry

        def wait_range(r, carry):
            def go(lo, n, count):
                lax.fori_loop(0, count, lambda i, c: (zero_copy(0, n).wait(), c)[1], 0)
            for_each_chunk(r, go)
            return carry

        n_ranges = fill_lo_ref.shape[0]
        lax.fori_loop(0, n_ranges, start_range, 0)
        lax.fori_loop(0, n_ranges, wait_range, 0)


def _dispatch_call(slots_flat, fill_lo, fill_hi, h2, n_rows, td=512):
    t, nj, _ = h2.shape
    smem = pl.BlockSpec(memory_space=pltpu.SMEM)
    return pl.pallas_call(
        functools.partial(_dispatch_kernel, td=td),
        grid=(t // td,),
        in_specs=[pl.BlockSpec((td * SLOT_STRIDE,), lambda i: (i,), memory_space=pltpu.SMEM),
                  smem, smem,
                  pl.BlockSpec((td, nj, LANES), lambda i: (i, 0, 0))],
        out_specs=pl.BlockSpec(memory_space=pl.ANY),
        out_shape=jax.ShapeDtypeStruct((n_rows, nj, LANES), h2.dtype),
        scratch_shapes=[pltpu.VMEM((ZERO_ROWS, nj, LANES), h2.dtype), pltpu.SemaphoreType.DMA(())],
        compiler_params=_cparams(("arbitrary",)),
        name="dispatch",
    )(slots_flat, fill_lo, fill_hi, h2)


def _experts_kernel(be_ref, first_ref, nxt_ref, par_ref, rows_ref, nv_ref, xs_hbm, wg_hbm, wu_hbm, wd_hbm, ys_ref,
                    wg_buf, wu_buf, wd_buf, wg_sc, wu_sc, wd_sc, xpack, sem, isem):
    nj = xs_hbm.shape[1]
    b = pl.program_id(0)
    m = xpack.shape[1]
    half = wg_sc.shape[0] // 2
    oslot = lax.rem(b, 2)

    def in_copies(slot, row0):
        return [pltpu.make_async_copy(xs_hbm.at[pl.ds(row0, m), j],
                                      xpack.at[slot, :, pl.ds(j * LANES, LANES)], isem.at[slot]) for j in range(nj)]

    @pl.when(b == 0)
    def _():
        for c in in_copies(0, 0):
            c.start()

    @pl.when(b + 1 < nv_ref[0])
    def _():
        for c in in_copies(1 - oslot, pl.multiple_of((b + 1) * m, m)):
            c.start()

    def weight_copies(e, slot):
        return [pltpu.make_async_copy(src.at[e], dst.at[slot], sem.at[slot, i])
                for i, (src, dst) in enumerate(((wg_hbm, wg_buf), (wu_hbm, wu_buf), (wd_hbm, wd_buf)))]

    def compute(nrows):
        x_lo, x_hi = _unpack_halves(xpack[oslot, 0:nrows])
        x_lo, x_hi = x_lo.astype(BF16), x_hi.astype(BF16)
        g = (jnp.dot(x_lo, wg_sc[0:half], preferred_element_type=F32)
             + jnp.dot(x_hi, wg_sc[half:], preferred_element_type=F32))
        u = (jnp.dot(x_lo, wu_sc[0:half], preferred_element_type=F32)
             + jnp.dot(x_hi, wu_sc[half:], preferred_element_type=F32))
        a = (_silu(g) * u).astype(BF16)
        ys_ref[0:nrows] = _pack_halves(jnp.dot(a, wd_sc[...], preferred_element_type=F32))

    @pl.when(b < nv_ref[0])
    def _():
        for c in in_copies(oslot, 0):
            c.wait()

        @pl.when(b == 0)
        def _():
            for c in weight_copies(be_ref[0], par_ref[0]):
                c.start()

        @pl.when(first_ref[b] == 1)
        def _():
            slot = par_ref[b]
            for c in weight_copies(be_ref[b], slot):
                c.wait()

            @pl.when(nxt_ref[b] >= 0)
            def _():
                for c in weight_copies(nxt_ref[b], 1 - slot):
                    c.start()

            wg_sc[...] = wg_buf[slot].astype(BF16)
            wu_sc[...] = wu_buf[slot].astype(BF16)
            wd_sc[...] = wd_buf[slot].astype(BF16)

        quarter = m // EXPERT_BLOCK_PARTS
        for part in range(1, EXPERT_BLOCK_PARTS + 1):
            nrows = part * quarter
            if part == 1:
                cond = rows_ref[b] <= nrows
            elif part == EXPERT_BLOCK_PARTS:
                cond = rows_ref[b] > nrows - quarter
            else:
                cond = jnp.logical_and(rows_ref[b] > nrows - quarter, rows_ref[b] <= nrows)

            @pl.when(cond)
            def _(nrows=nrows):
                compute(nrows)
                if nrows < m:
                    ys_ref[nrows:] = jnp.zeros((m - nrows, ys_ref.shape[1]), ys_ref.dtype)

    @pl.when(b >= nv_ref[0])
    def _():
        ys_ref[...] = jnp.zeros(ys_ref.shape, ys_ref.dtype)


def _experts_call(block_e, first, nxt, par, rows_valid, n_valid, xs, wg, wu, wd):
    total, nj, _ = xs.shape
    d = wg.shape[1]
    m = DISPATCH_BLOCK
    ff = wg.shape[-1]

    hbm = pl.BlockSpec(memory_space=pl.ANY)
    grid_spec = pltpu.PrefetchScalarGridSpec(
        num_scalar_prefetch=6,
        grid=(total // m,),
        in_specs=[hbm, hbm, hbm, hbm],
        out_specs=pl.BlockSpec((m, nj * LANES), lambda b, *s: (b, 0)),
        scratch_shapes=[pltpu.VMEM((2, d, ff), F32), pltpu.VMEM((2, d, ff), F32), pltpu.VMEM((2, ff, d), F32),
                        pltpu.VMEM((d, ff), BF16), pltpu.VMEM((d, ff), BF16), pltpu.VMEM((ff, d), BF16),
                        pltpu.VMEM((2, m, nj * LANES), U32),
                        pltpu.SemaphoreType.DMA((2, 3)), pltpu.SemaphoreType.DMA((2,))],
    )
    return pl.pallas_call(
        _experts_kernel,
        grid_spec=grid_spec,
        out_shape=jax.ShapeDtypeStruct((total, nj * LANES), U32),
        compiler_params=_cparams(("arbitrary",)),
        name="experts",
    )(block_e, first, nxt, par, rows_valid, n_valid, xs, wg, wu, wd)


COMBINE_INLINE_ROWS = 128


def _combine_kernel(slot_ref, x1_ref, h2_ref, gw_ref, mod_ref, wg_ref, wu_ref, wd_ref, ys_ref, o_ref,
                    buf, sem, *, tc):
    def row_copy(t, k, slot):
        return pltpu.make_async_copy(ys_ref.at[pl.ds(slot, 1)], buf.at[k, pl.ds(t, 1)], sem)

    def start(t, carry):
        for k in range(TOP_K):
            row_copy(t, k, slot_ref[t * SLOT_STRIDE + k]).start(priority=k % 2)
        return carry

    inline = min(COMBINE_INLINE_ROWS, tc)
    lax.fori_loop(0, tc - inline, start, 0)
    for t in range(tc - inline, tc):
        start(t, 0)
    half = wg_ref.shape[0] // 2
    h_lo, h_hi = _unpack_halves(h2_ref[0])
    h_lo, h_hi = h_lo.astype(BF16), h_hi.astype(BF16)
    g = (jnp.dot(h_lo, wg_ref[0:half], preferred_element_type=F32)
         + jnp.dot(h_hi, wg_ref[half:], preferred_element_type=F32))
    u = (jnp.dot(h_lo, wu_ref[0:half], preferred_element_type=F32)
         + jnp.dot(h_hi, wu_ref[half:], preferred_element_type=F32))
    acc = jnp.dot((_silu(g) * u).astype(BF16), wd_ref[...], preferred_element_type=F32)
    for k in range(TOP_K):
        pltpu.make_async_copy(ys_ref.at[pl.ds(0, tc)], buf.at[k], sem).wait()
    gw = gw_ref[...]
    acc_lo, acc_hi = acc[:, :half], acc[:, half:]
    for k in range(TOP_K):
        y_lo, y_hi = _unpack_halves(buf[k])
        acc_lo = acc_lo + gw[:, k:k + 1] * y_lo
        acc_hi = acc_hi + gw[:, k:k + 1] * y_hi
    gate = mod_ref[0, 5:6, :]
    o_ref[0, :, 0:half] = x1_ref[0, :, 0:half] + gate[:, :half] * acc_lo
    o_ref[0, :, half:] = x1_ref[0, :, half:] + gate[:, half:] * acc_hi


def _combine_call(slots_flat, x1, h2, gw_tok, mod3, wg, wu, wd, ys, tc=512):
    b, s, d = x1.shape
    nt = s // tc
    full = lambda a: pl.BlockSpec(a.shape, lambda bi, i: (0,) * a.ndim)
    return pl.pallas_call(
        functools.partial(_combine_kernel, tc=tc),
        grid=(b, nt),
        in_specs=[pl.BlockSpec((tc * SLOT_STRIDE,), lambda bi, i: (bi * nt + i,), memory_space=pltpu.SMEM),
                  pl.BlockSpec((1, tc, d), lambda bi, i: (bi, i, 0)),
                  pl.BlockSpec((1, tc, d // 2), lambda bi, i: (bi, i, 0)),
                  pl.BlockSpec((tc, 8), lambda bi, i: (bi * nt + i, 0)),
                  pl.BlockSpec((1, 6, d), lambda bi, i: (bi, 0, 0)),
                  full(wg), full(wu), full(wd),
                  pl.BlockSpec(memory_space=pl.ANY)],
        out_specs=pl.BlockSpec((1, tc, d), lambda bi, i: (bi, i, 0)),
        out_shape=jax.ShapeDtypeStruct((b, s, d), F32),
        scratch_shapes=[pltpu.VMEM((TOP_K, tc, d // 2), U32), pltpu.SemaphoreType.DMA(())],
        compiler_params=_cparams(("arbitrary", "arbitrary")),
        name="combine",
    )(slots_flat, x1, h2, gw_tok, mod3, wg, wu, wd, ys)


def _pad_rows(w, rows, at=0):
    out = jnp.zeros((rows, w.shape[1]), w.dtype)
    return out.at[at:at + w.shape[0]].set(w)


def _mixing_stage(p, rwkv_shift, w0_f, w_up_f, w0_b, w_up_b, a0_f, a_up_f, a0_b, a_up_b, g_up,
                  k_k, k_a, r_k, ln_x_w, ln_x_b, q_norm_w, k_norm_w, rel_bias):
    d_rwkv = w0_f.shape[0]
    nhp = d_rwkv // LANES
    mbd = _head_sum_matrix()
    n_shift = rwkv_shift.shape[1]
    sw = jnp.pad(rwkv_shift, ((0, 0), (0, N_SHIFT_BLOCKS * LANES - n_shift))).reshape(3, N_SHIFT_BLOCKS, LANES)
    wupf = _pad_rows(w_up_f, LANES, 0).astype(BF16)
    wupb = _pad_rows(w_up_b, LANES, DECAY_LORA).astype(BF16)
    aupf = _pad_rows(a_up_f, LANES, 0).astype(BF16)
    aupb = _pad_rows(a_up_b, LANES, AAA_LORA).astype(BF16)
    gup = _pad_rows(g_up, 2 * LANES, 0).astype(BF16)
    vecs = jnp.stack([w0_f, w0_b, a0_f, a0_b, k_k, k_a, r_k.reshape(-1)]).reshape(7, nhp, LANES)
    rb, lw = _prep_call(p, sw, wupf, wupb, aupf, aupb, gup, vecs, mbd, nhp)
    y_f, y_b = _wkv_call(rb, lw, nhp)
    y_rwkv = _post_call(y_f, y_b, rb, ln_x_w.reshape(nhp, LANES), ln_x_b.reshape(nhp, LANES), mbd, nhp)
    bias = _na_bias_table(rel_bias)
    qw = jnp.tile(q_norm_w, HEADS_PER_BLOCK).reshape(1, LANES)
    kw = jnp.tile(k_norm_w, HEADS_PER_BLOCK).reshape(1, LANES)
    y_na = _na_call(p, bias, qw, kw, mbd, N_SHIFT_BLOCKS, nhp)
    return y_rwkv, y_na


def _moe_stage(x1, h2_tiles, h2_rows, lg_t, mod3, router_bias, w_gate_e, w_up_e, w_down_e, w_gate_s, w_up_s,
               w_down_s):
    b, s, d = x1.shape
    t = b * s
    m = DISPATCH_BLOCK
    bias2 = jnp.broadcast_to(router_bias.astype(F32)[:, None], (N_EXPERTS, LANES))
    idx_t, gw_t, pos_t, cnt = _route_call(lg_t, bias2)
    counts = cnt[:, 0].astype(I32)
    padded = (counts + m - 1) // m * m
    pad_end = jnp.cumsum(padded)
    pad_start = pad_end - padded
    n_blocks = -(-(t * TOP_K + N_EXPERTS * m) // m)
    n_valid = (pad_end[-1] // m).astype(I32).reshape(1)
    block_row0 = jnp.arange(n_blocks, dtype=I32) * m
    block_e = jnp.minimum(jnp.sum((pad_end[None, :] <= block_row0[:, None]).astype(I32), axis=1),
                          N_EXPERTS - 1).astype(I32)
    assert m % ZERO_ROWS == 0
    expert_ids = jnp.arange(N_EXPERTS, dtype=I32)
    start_of = jnp.sum(jnp.where(idx_t[:, :, None] == expert_ids, pad_start, 0), axis=-1)
    slots = (start_of + pos_t).astype(I32)
    slots_flat = slots.T.reshape(-1)
    gw_tok = gw_t.T
    fill_lo = jnp.concatenate([pad_start + counts, pad_end[-1:]]).astype(I32)
    fill_hi = jnp.concatenate([pad_end, jnp.full((1,), n_blocks * m, I32)]).astype(I32)
    xs = _dispatch_call(slots_flat, fill_lo, fill_hi, h2_tiles.reshape(t, *h2_tiles.shape[2:]), n_blocks * m)
    row_end = (pad_start + counts).astype(I32)
    nonempty = counts > 0
    first = jnp.logical_and(block_row0 == pad_start[block_e], block_row0 < pad_end[-1]).astype(I32)
    ordinal = jnp.cumsum(nonempty.astype(I32)) - 1
    par = (ordinal[block_e] & 1).astype(I32)
    cand = jnp.where(nonempty, expert_ids, N_EXPERTS)
    later = jnp.concatenate([lax.cummin(cand[::-1])[::-1][1:], jnp.full((1,), N_EXPERTS, I32)])
    nxt = jnp.where(later < N_EXPERTS, later, -1)[block_e].astype(I32)
    rows_valid = jnp.clip(row_end[block_e] - block_row0, 0, m).astype(I32)
    ys = _experts_call(block_e, first, nxt, par, rows_valid, n_valid, xs, w_gate_e, w_up_e, w_down_e)
    return _combine_call(slots_flat, x1, h2_rows, gw_tok, mod3, w_gate_s.astype(BF16), w_up_s.astype(BF16),
                         w_down_s.astype(BF16), ys)


def kernel(x, c, w_ada, b_ada, norm1_w, w_in, rwkv_shift, w0_f, w_up_f, w0_b, w_up_b, a0_f, a_up_f, a0_b,
           a_up_b, g_up, k_k, k_a, r_k, ln_x_w, ln_x_b, q_norm_w, k_norm_w, rel_bias, w_out, norm2_w,
           w_router, router_bias, w_gate_e, w_up_e, w_down_e, w_gate_s, w_up_s, w_down_s):
    bn, sn, d = x.shape
    depth = w_ada.shape[0]
    for l in range(depth):
        c_pad = jnp.pad(c, ((0, 8 - bn % 8 if bn % 8 else 0), (0, 0)))
        mod = _mod_call(c_pad, w_ada[l], b_ada[l].reshape(1, -1))[:bn]
        mod3 = mod.reshape(bn, 6, d)
        rwkv_in = rwkv_shift.shape[-1]
        pad_cols = N_SHIFT_BLOCKS * LANES - rwkv_in
        w_in_r = _widen_call(w_in, l, rwkv_in, pad_cols)
        p = _proj_call(x, mod3, norm1_w[l].reshape(1, d), w_in_r)
        y_rwkv, y_na = _mixing_stage(p, rwkv_shift[l], w0_f[l], w_up_f[l], w0_b[l], w_up_b[l], a0_f[l],
                                     a_up_f[l], a0_b[l], a_up_b[l], g_up[l], k_k[l], k_a[l], r_k[l],
                                     ln_x_w[l], ln_x_b[l], q_norm_w[l], k_norm_w[l], rel_bias[l])
        x1, h2_tiles, h2_rows, lg_t = _outproj_call(y_rwkv, y_na, x, mod3, norm2_w[l].reshape(1, d),
                                                    w_out[l].astype(BF16), w_router[l].T)
        x = _moe_stage(x1, h2_tiles, h2_rows, lg_t, mod3, router_bias[l], w_gate_e[l], w_up_e[l], w_down_e[l],
                       w_gate_s[l], w_up_s[l], w_down_s[l])
    return x
```
